```python
import jax, jax.numpy as jnp
from jax import lax
import numpy as np

D_MODEL = 1024
BATCH = 8
SEQ = 8192
DEPTH = 2

N_BRANCH = 4
BRANCH_WIDTH = D_MODEL // 4
HEAD_DIM = 64
N_HEADS_SB = BRANCH_WIDTH // HEAD_DIM
N_HEADS_FOX = BRANCH_WIDTH // HEAD_DIM
D_CONV_A = BRANCH_WIDTH
D_SB = N_HEADS_SB * HEAD_DIM
D_FOX = N_HEADS_FOX * HEAD_DIM
D_CONF = BRANCH_WIDTH
CONV_A_WIDTH = 3
CONF_WIDTH = 31
Q_BLOCK = 128
D_IN_PROJ = 3 * D_CONV_A + 3 * D_SB + 3 * D_FOX + N_HEADS_FOX + 2 * D_CONF
N_GROUPS = 4
EXPERTS_PER_GROUP = 8
N_EXPERTS = N_GROUPS * EXPERTS_PER_GROUP
TOP_K_IN_GROUP = 2
D_EXPERT = D_MODEL // 4
EPS = 1e-6

kernel_name = "hybrid_gated_mixers_hmoe"


def rms_norm(x, g):
    xf = x.astype(jnp.float32)
    y = xf * lax.rsqrt(jnp.mean(xf * xf, axis=-1, keepdims=True) + EPS)
    return (y * g.astype(jnp.float32)).astype(x.dtype)


def layer_norm(x, g, b):
    xf = x.astype(jnp.float32)
    mu = jnp.mean(xf, axis=-1, keepdims=True)
    var = jnp.mean(jnp.square(xf - mu), axis=-1, keepdims=True)
    y = (xf - mu) * lax.rsqrt(var + EPS)
    return (y * g.astype(jnp.float32) + b.astype(jnp.float32)).astype(x.dtype)


def causal_depthwise_conv(x, w):
    k_width = w.shape[0]
    return lax.conv_general_dilated(
        x, w[:, None, :].astype(x.dtype), window_strides=(1,), padding=[(k_width - 1, 0)],
        dimension_numbers=("NWC", "WIO", "NWC"), feature_group_count=x.shape[-1])


def split_heads(t, n_heads):
    b, s, _ = t.shape
    return t.reshape(b, s, n_heads, HEAD_DIM).transpose(0, 2, 1, 3).astype(jnp.float32)


def to_query_blocks(q):
    b, h, s, d = q.shape
    return q.reshape(b, h, s // Q_BLOCK, Q_BLOCK, d).transpose(2, 0, 1, 3, 4)


def from_query_blocks(o):
    nb, b, h, qb, d = o.shape
    return o.transpose(1, 0, 3, 2, 4).reshape(b, nb * qb, h * d)


def stick_breaking_attention(q, k, v):
    s_len = k.shape[2]
    scale = HEAD_DIM ** -0.5
    kpos = jnp.arange(s_len)

    def block(args):
        qb, i = args
        qpos = i * Q_BLOCK + jnp.arange(Q_BLOCK)
        z = jnp.einsum("bhqd,bhkd->bhqk", qb, k) * scale
        mask = kpos[None, :] < qpos[:, None]
        log_not = jnp.where(mask, jax.nn.log_sigmoid(-z), 0.0)
        after = lax.cumsum(log_not, axis=3, reverse=True) - log_not
        w = jnp.where(mask, jnp.exp(jax.nn.log_sigmoid(z) + after), 0.0)
        return jnp.einsum("bhqk,bhkd->bhqd", w, v)

    out = lax.map(block, (to_query_blocks(q), jnp.arange(s_len // Q_BLOCK)))
    return from_query_blocks(out)


def forgetting_attention(q, k, v, dcum):
    b, h, s_len, _ = q.shape
    nb = s_len // Q_BLOCK
    scale = HEAD_DIM ** -0.5
    kpos = jnp.arange(s_len)

    def block(args):
        qb, dq, i = args
        qpos = i * Q_BLOCK + jnp.arange(Q_BLOCK)
        logits = jnp.einsum("bhqd,bhkd->bhqk", qb, k) * scale + dq[..., :, None] - dcum[..., None, :]
        logits = jnp.where(kpos[None, :] <= qpos[:, None], logits, -jnp.inf)
        p = jax.nn.softmax(logits, axis=-1)
        return jnp.einsum("bhqk,bhkd->bhqd", p, v)

    dq_blocks = dcum.reshape(b, h, nb, Q_BLOCK).transpose(2, 0, 1, 3)
    out = lax.map(block, (to_query_blocks(q), dq_blocks, jnp.arange(nb)))
    return from_query_blocks(out)


def hybrid_mixer(xn, w_in, b_forget, conv_a_w, conf_dw_w, conf_dw_b, conf_ln_g, conf_ln_b,
                 w_branch, w_gate, b_gate, w_out):
    sizes = (D_CONV_A, D_CONV_A, D_CONV_A, D_SB, D_SB, D_SB, D_FOX, D_FOX, D_FOX, N_HEADS_FOX, 2 * D_CONF)
    split_points = [int(p) for p in np.cumsum(sizes)[:-1]]
    proj = xn @ w_in
    (a_x, a_b, a_c, sb_q, sb_k, sb_v, fx_q, fx_k, fx_v, fx_f, conf_in) = jnp.split(proj, split_points, axis=-1)

    y_a = a_b * causal_depthwise_conv(a_c * a_x, conv_a_w)

    y_sb = stick_breaking_attention(split_heads(sb_q, N_HEADS_SB), split_heads(sb_k, N_HEADS_SB),
                                    split_heads(sb_v, N_HEADS_SB))

    log_f = jax.nn.log_sigmoid((fx_f + b_forget).astype(jnp.float32))
    dcum = lax.cumsum(log_f, axis=1).transpose(0, 2, 1)
    y_fox = forgetting_attention(split_heads(fx_q, N_HEADS_FOX), split_heads(fx_k, N_HEADS_FOX),
                                 split_heads(fx_v, N_HEADS_FOX), dcum)

    u, u_gate = jnp.split(conf_in, 2, axis=-1)
    u = u * jax.nn.sigmoid(u_gate)
    u = causal_depthwise_conv(u, conf_dw_w) + conf_dw_b
    y_conf = jax.nn.silu(layer_norm(u, conf_ln_g, conf_ln_b))

    branches = (y_a, y_sb, y_fox, y_conf)
    h = jnp.zeros_like(xn)
    for g in range(N_BRANCH):
        gate = jax.nn.sigmoid(xn @ w_gate[g] + b_gate[g])
        h = h + gate * (branches[g].astype(xn.dtype) @ w_branch[g])
    return h @ w_out


def hierarchical_moe(xn, w_router_group, w_router_expert, w_expert_gate, w_expert_up, w_expert_down):
    b, s, d = xn.shape
    t = xn.reshape(b * s, d)
    group_probs = jax.nn.softmax((t @ w_router_group).astype(jnp.float32), axis=-1)
    g_idx = jnp.argmax(group_probs, axis=-1)
    p_group = jnp.take_along_axis(group_probs, g_idx[:, None], axis=1)
    expert_logits = jnp.einsum("td,dge->tge", t, w_router_expert).astype(jnp.float32)
    in_group = jnp.take_along_axis(expert_logits, g_idx[:, None, None], axis=1)[:, 0]
    top_p, top_i = lax.top_k(jax.nn.softmax(in_group, axis=-1), TOP_K_IN_GROUP)
    top_p = top_p / jnp.sum(top_p, axis=-1, keepdims=True)
    expert_id = g_idx[:, None] * EXPERTS_PER_GROUP + top_i
    combine = jnp.sum(jax.nn.one_hot(expert_id, N_EXPERTS, dtype=jnp.float32)
                      * (p_group * top_p)[..., None], axis=1).astype(t.dtype)
    out = jnp.zeros_like(t)
    for e in range(N_EXPERTS):
        hid = jax.nn.silu(t @ w_expert_gate[e]) * (t @ w_expert_up[e])
        out = out + combine[:, e:e + 1] * (hid @ w_expert_down[e])
    return out.reshape(b, s, d)


def setup_inputs(seed: int = 0) -> dict:
    key = jax.random.key(seed)
    ks = jax.random.split(key, 24)
    f32 = jnp.float32

    def nrm(k, shape, fan_in):
        return jax.random.normal(k, shape, f32) * (fan_in ** -0.5)

    def gain(k, shape):
        return 1.0 + 0.02 * jax.random.normal(k, shape, f32)

    def small(k, shape):
        return 0.02 * jax.random.normal(k, shape, f32)

    return {
        "x": jax.random.normal(ks[0], (BATCH, SEQ, D_MODEL), f32),
        "mix_norm_g": gain(ks[1], (DEPTH, D_MODEL)),
        "w_in": nrm(ks[2], (DEPTH, D_MODEL, D_IN_PROJ), D_MODEL),
        "b_forget": jax.random.uniform(ks[3], (DEPTH, N_HEADS_FOX), f32, 1.0, 5.0),
        "conv_a_w": nrm(ks[4], (DEPTH, CONV_A_WIDTH, D_CONV_A), CONV_A_WIDTH),
        "conf_dw_w": nrm(ks[5], (DEPTH, CONF_WIDTH, D_CONF), CONF_WIDTH),
        "conf_dw_b": small(ks[6], (DEPTH, D_CONF)),
        "conf_ln_g": gain(ks[7], (DEPTH, D_CONF)),
        "conf_ln_b": small(ks[8], (DEPTH, D_CONF)),
        "w_branch": nrm(ks[9], (DEPTH, N_BRANCH, BRANCH_WIDTH, D_MODEL), BRANCH_WIDTH),
        "w_gate": nrm(ks[10], (DEPTH, N_BRANCH, D_MODEL, D_MODEL), D_MODEL),
        "b_gate": small(ks[11], (DEPTH, N_BRANCH, D_MODEL)),
        "w_out": nrm(ks[12], (DEPTH, D_MODEL, D_MODEL), D_MODEL),
        "ffn_norm_g": gain(ks[13], (DEPTH, D_MODEL)),
        "w_router_group": nrm(ks[14], (DEPTH, D_MODEL, N_GROUPS), D_MODEL),
        "w_router_expert": nrm(ks[15], (DEPTH, D_MODEL, N_GROUPS, EXPERTS_PER_GROUP), D_MODEL),
        "w_expert_gate": nrm(ks[16], (DEPTH, N_EXPERTS, D_MODEL, D_EXPERT), D_MODEL),
        "w_expert_up": nrm(ks[17], (DEPTH, N_EXPERTS, D_MODEL, D_EXPERT), D_MODEL),
        "w_expert_down": nrm(ks[18], (DEPTH, N_EXPERTS, D_EXPERT, D_MODEL), D_EXPERT),
        "final_norm_g": gain(ks[19], (D_MODEL,)),
    }


def reference(x, mix_norm_g, w_in, b_forget, conv_a_w, conf_dw_w, conf_dw_b, conf_ln_g, conf_ln_b,
              w_branch, w_gate, b_gate, w_out, ffn_norm_g, w_router_group, w_router_expert,
              w_expert_gate, w_expert_up, w_expert_down, final_norm_g):
    for layer in range(DEPTH):
        xn = rms_norm(x, mix_norm_g[layer])
        x = x + hybrid_mixer(xn, w_in[layer], b_forget[layer], conv_a_w[layer], conf_dw_w[layer],
                             conf_dw_b[layer], conf_ln_g[layer], conf_ln_b[layer], w_branch[layer],
                             w_gate[layer], b_gate[layer], w_out[layer])
        xn = rms_norm(x, ffn_norm_g[layer])
        x = x + hierarchical_moe(xn, w_router_group[layer], w_router_expert[layer], w_expert_gate[layer],
                                 w_expert_up[layer], w_expert_down[layer])
    return rms_norm(x, final_norm_g)
```

```python
import functools

import jax
import jax.numpy as jnp
from jax import lax
from jax.experimental import pallas as pl
from jax.experimental.pallas import tpu as pltpu
from jax.experimental.pallas import tpu_sc as plsc

F32 = jnp.float32
BF16 = jnp.bfloat16
I32 = jnp.int32

D_MODEL = 1024
BRANCH_WIDTH = 256
HEAD_DIM = 64
N_HEADS = 4
HEAD_SLAB = 128
CONV_A_WIDTH = 3
CONF_WIDTH = 31
N_GROUPS = 4
EXPERTS_PER_GROUP = 8
N_EXPERTS = 32
D_EXPERT = 256
EPS = 1e-6
LANES = 128
HALF_D = D_MODEL // 2

TOKEN_TILE = 512
ATTN_TILE = 256
HEADS_PER_STEP = 2
GROUP_TILE = 256
CONV_ROW_CHUNK = 64
SC_WINDOW = 128
SC_SPLIT = 2
VMEM_LIMIT = 56 * 1024 * 1024

_C_AX, _C_AB, _C_AC = 0, 256, 512
_C_Q = 768
_C_V = 1280
_C_CU, _C_CG = 1792, 2048
_C_F = 2304
_N_MAIN = 2432


def _rms(x, g):
    return x * lax.rsqrt(jnp.mean(x * x, axis=-1, keepdims=True) + EPS) * g


def _softplus(z):
    return jnp.maximum(z, 0.0) + jnp.log1p(jnp.exp(-jnp.abs(z)))


def _sigmoid(z):
    return 1.0 / (1.0 + jnp.exp(-z))


def _split3(v):
    hi = v.astype(BF16)
    r = v - hi.astype(F32)
    mid = r.astype(BF16)
    lo = (r - mid.astype(F32)).astype(BF16)
    return hi, mid, lo


def _pack_rows(v):
    lo = pltpu.bitcast(v[:, :HALF_D].astype(BF16).astype(F32), jnp.uint32)
    hi = pltpu.bitcast(v[:, HALF_D:].astype(BF16).astype(F32), jnp.uint32)
    return pltpu.bitcast((lo >> 16) | hi, I32)


def _unpack_rows(w):
    u = pltpu.bitcast(w, jnp.uint32)
    lo = pltpu.bitcast(u << 16, F32)
    hi = pltpu.bitcast(u & jnp.uint32(0xFFFF0000), F32)
    return lo, hi


def _mixer_in_kernel(combine, *refs):
    if combine:
        (x_ref, y0_ref, y1_ref, rt_ref, *refs) = refs
    else:
        (x_ref, *refs) = refs
    (g_ref, wm_ref, wt_ref, bf_ref, caw_ref, cdw_ref, cdb_ref, lng_ref, lnb_ref, tril_ref,
     pq_ref, qc_ref, pk_ref, kc_ref, vc_ref, *refs) = refs
    if combine:
        (xo_ref, *refs) = refs
    (ya_ref, yc_ref, q_ref, kt_ref, v_ref, bufa, bufc, dcarry) = refs

    tm = x_ref.shape[1]
    tk = kt_ref.shape[-1]

    @pl.when(pl.program_id(1) == 0)
    def _():
        bufa[0:8, :] = jnp.zeros((8, BRANCH_WIDTH), F32)
        bufc[0:32, :] = jnp.zeros((32, BRANCH_WIDTH), F32)
        dcarry[...] = jnp.zeros_like(dcarry)

    x = x_ref[0]
    if combine:
        rt = rt_ref[0]
        y0l, y0h = _unpack_rows(y0_ref[0])
        y1l, y1h = _unpack_rows(y1_ref[0])
        w0 = rt[:, 2:3]
        w1 = rt[:, 3:4]
        x = x + jnp.concatenate([w0 * y0l + w1 * y1l, w0 * y0h + w1 * y1h], axis=1)
        xo_ref[0] = x
    xb = _rms(x, g_ref[...]).astype(BF16)

    p = jnp.dot(xb, wm_ref[...], preferred_element_type=F32)
    pt = lax.dot_general(wt_ref[...], xb, (((1,), (1,)), ((), ())),
                         preferred_element_type=F32)

    ca = p[:, _C_AC:_C_AC + 256] * p[:, _C_AX:_C_AX + 256]
    bufa[8:8 + tm, :] = ca
    caw = caw_ref[...]
    conv = caw[0:1] * bufa[6:6 + tm, :] + caw[1:2] * bufa[7:7 + tm, :] + caw[2:3] * ca
    ya_ref[0] = (p[:, _C_AB:_C_AB + 256] * conv).astype(BF16)
    bufa[0:8, :] = ca[tm - 8:tm]

    u = p[:, _C_CU:_C_CU + 256] * _sigmoid(p[:, _C_CG:_C_CG + 256])
    bufc[32:32 + tm, :] = u
    cdw = cdw_ref[...]
    cdb = cdb_ref[...]
    lng = lng_ref[...]
    lnb = lnb_ref[...]
    for r in range(tm // CONV_ROW_CHUNK):
        base = r * CONV_ROW_CHUNK + 32 - (CONF_WIDTH - 1)
        acc = jnp.broadcast_to(cdb, (CONV_ROW_CHUNK, BRANCH_WIDTH))
        for k in range(CONF_WIDTH):
            acc = acc + cdw[k:k + 1] * bufc[base + k:base + k + CONV_ROW_CHUNK, :]
        mu = jnp.mean(acc, axis=-1, keepdims=True)
        cen = acc - mu
        var = jnp.mean(cen * cen, axis=-1, keepdims=True)
        yn = cen * lax.rsqrt(var + EPS) * lng + lnb
        yc_ref[0, r * CONV_ROW_CHUNK:(r + 1) * CONV_ROW_CHUNK, :] = (yn * _sigmoid(yn)).astype(BF16)
    bufc[0:32, :] = u[tm - 32:tm]

    logf = -_softplus(-(p[:, _C_F:_C_F + LANES] + bf_ref[...]))
    dcum = jnp.dot(tril_ref[...], logf, precision=lax.Precision.HIGHEST,
                   preferred_element_type=F32) + dcarry[0:1, :]
    dcarry[0:1, :] = dcum[tm - 1:tm, :]
    dcum_t = dcum.T
    qh, qm, ql = _split3(dcum)
    kh, km, kl = _split3(-dcum_t)
    q_extra = (jnp.dot(qh, pq_ref[0], preferred_element_type=F32)
               + jnp.dot(qm, pq_ref[1], preferred_element_type=F32)
               + jnp.dot(ql, pq_ref[2], preferred_element_type=F32) + qc_ref[...])
    k_extra = (jnp.dot(pk_ref[0], kh, preferred_element_type=F32)
               + jnp.dot(pk_ref[1], km, preferred_element_type=F32)
               + jnp.dot(pk_ref[2], kl, preferred_element_type=F32)
               + jnp.concatenate([kc_ref[...]] * (tm // LANES), axis=1))

    lane = lax.broadcasted_iota(I32, (tm, HEAD_SLAB), 1)
    low = lane < HEAD_DIM
    vc = vc_ref[...]
    for hd in range(2 * N_HEADS):
        is_fox = hd >= N_HEADS
        pair = (hd // 2) * HEAD_SLAB
        qs = p[:, _C_Q + pair:_C_Q + pair + HEAD_SLAB]
        vs = p[:, _C_V + pair:_C_V + pair + HEAD_SLAB]
        if hd % 2:
            qs = pltpu.roll(qs, HEAD_DIM, axis=1)
            vs = pltpu.roll(vs, HEAD_DIM, axis=1)
        if is_fox:
            hf = hd - N_HEADS
            qx = q_extra[:, hf * HEAD_SLAB:(hf + 1) * HEAD_SLAB]
            kx = k_extra[hf * HEAD_DIM:(hf + 1) * HEAD_DIM, :]
            vx = vc
        else:
            qx = 0.0
            kx = jnp.zeros((HEAD_DIM, tm), F32)
            vx = 0.0
        q_ref[0, hd] = jnp.where(low, qs, qx).astype(BF16)
        v_ref[0, hd] = jnp.where(low, vs, vx).astype(BF16)
        kfull = jnp.concatenate([pt[hd * HEAD_DIM:(hd + 1) * HEAD_DIM, :], kx], axis=0).astype(BF16)
        for c in range(tm // tk):
            kt_ref[0, hd, c] = kfull[:, c * tk:(c + 1) * tk]


def _mixer_in(x, comb, lw, consts):
    b, s, _ = x.shape
    tm = min(TOKEN_TILE, s)
    tk = min(ATTN_TILE, s)
    nk = s // tk
    nh2 = 2 * N_HEADS
    combine = comb is not None

    def full(a):
        return pl.BlockSpec(a.shape, lambda bi, si, _n=a.ndim: (0,) * _n)

    tok = lambda w: pl.BlockSpec((1, tm, w), lambda bi, si: (bi, si, 0))
    in_arrays = [x]
    in_specs = [tok(D_MODEL)]
    if combine:
        in_arrays += list(comb)
        in_specs += [tok(HALF_D), tok(HALF_D), tok(LANES)]
    weights = [lw["mix_g"], lw["w_main"], lw["w_t"], lw["bf"], lw["caw"], lw["cdw"], lw["cdb"], lw["lng"],
               lw["lnb"], consts["tril"], consts["pq"], consts["qc"], consts["pk"], consts["kc"], consts["vc"]]
    in_arrays += weights
    in_specs += [full(a) for a in weights]

    out_shape = []
    out_specs = []
    if combine:
        out_shape.append(jax.ShapeDtypeStruct((b, s, D_MODEL), F32))
        out_specs.append(tok(D_MODEL))
    out_shape += [
        jax.ShapeDtypeStruct((b, s, BRANCH_WIDTH), BF16),
        jax.ShapeDtypeStruct((b, s, BRANCH_WIDTH), BF16),
        jax.ShapeDtypeStruct((b, nh2, s, HEAD_SLAB), BF16),
        jax.ShapeDtypeStruct((b, nh2, nk, HEAD_SLAB, tk), BF16),
        jax.ShapeDtypeStruct((b, nh2, s, HEAD_SLAB), BF16),
    ]
    out_specs += [
        tok(BRANCH_WIDTH), tok(BRANCH_WIDTH),
        pl.BlockSpec((1, nh2, tm, HEAD_SLAB), lambda bi, si: (bi, 0, si, 0)),
        pl.BlockSpec((1, nh2, tm // tk, HEAD_SLAB, tk), lambda bi, si: (bi, 0, si, 0, 0)),
        pl.BlockSpec((1, nh2, tm, HEAD_SLAB), lambda bi, si: (bi, 0, si, 0)),
    ]
    return pl.pallas_call(
        functools.partial(_mixer_in_kernel, combine),
        grid=(b, s // tm),
        in_specs=in_specs,
        out_specs=out_specs,
        out_shape=out_shape,
        scratch_shapes=[pltpu.VMEM((8 + tm, BRANCH_WIDTH), F32),
                        pltpu.VMEM((32 + tm, BRANCH_WIDTH), F32),
                        pltpu.VMEM((8, LANES), F32)],
        compiler_params=pltpu.CompilerParams(dimension_semantics=("arbitrary", "arbitrary"),
                                             vmem_limit_bytes=VMEM_LIMIT),
        name="mixer_in",
    )(*in_arrays)


def _pair_out(accs):
    lane = lax.broadcasted_iota(I32, accs[0].shape, 1)
    return jnp.where(lane < HEAD_DIM, accs[0], pltpu.roll(accs[1], HEAD_DIM, axis=1))


def _sb_attn_kernel(q_ref, kt_ref, v_ref, u_ref, o_ref):
    tq = q_ref.shape[2]
    tk = kt_ref.shape[-1]
    i = pl.program_id(2)
    umat = u_ref[...]
    row = lax.broadcasted_iota(I32, (tq, tk), 0)
    col = lax.broadcasted_iota(I32, (tq, tk), 1)
    strict = col < row
    qs = [q_ref[0, h] for h in range(HEADS_PER_STEP)]

    accs, tails = [], []
    for h in range(HEADS_PER_STEP):
        z = jnp.dot(qs[h], kt_ref[0, h, i], preferred_element_type=F32)
        log_not = jnp.where(strict, -_softplus(z), 0.0)
        incl = jnp.dot(log_not.astype(BF16), umat, preferred_element_type=F32)
        w = jnp.where(strict, jnp.exp(z + incl), 0.0)
        start = pl.multiple_of(i * tk, tk)
        accs.append(jnp.dot(w.astype(BF16), v_ref[0, h, pl.ds(start, tk), :], preferred_element_type=F32))
        tails.append(incl[:, 0:1])

    def body(jj, carry):
        accs, tails = carry
        j = i - 1 - jj
        start = pl.multiple_of(j * tk, tk)
        new_accs, new_tails = [], []
        for h in range(HEADS_PER_STEP):
            z = jnp.dot(qs[h], kt_ref[0, h, j], preferred_element_type=F32)
            log_not = -_softplus(z)
            incl = jnp.dot(log_not.astype(BF16), umat, preferred_element_type=F32)
            w = jnp.exp(z + incl + tails[h])
            new_accs.append(accs[h] + jnp.dot(w.astype(BF16), v_ref[0, h, pl.ds(start, tk), :],
                                              preferred_element_type=F32))
            new_tails.append(tails[h] + incl[:, 0:1])
        return tuple(new_accs), tuple(new_tails)

    accs, _ = lax.fori_loop(0, i, body, (tuple(accs), tuple(tails)))
    o_ref[0] = _pair_out(accs).astype(BF16)


def _fox_attn_kernel(q_ref, kt_ref, v_ref, o_ref):
    tq = q_ref.shape[2]
    tk = kt_ref.shape[-1]
    i = pl.program_id(2)
    row = lax.broadcasted_iota(I32, (tq, tk), 0)
    col = lax.broadcasted_iota(I32, (tq, tk), 1)
    causal = col <= row
    qs = [q_ref[0, h] for h in range(HEADS_PER_STEP)]

    accs, maxes = [], []
    for h in range(HEADS_PER_STEP):
        s = jnp.where(causal, jnp.dot(qs[h], kt_ref[0, h, i], preferred_element_type=F32), -jnp.inf)
        m = jnp.max(s, axis=-1, keepdims=True)
        pr = jnp.exp(s - m)
        start = pl.multiple_of(i * tk, tk)
        accs.append(jnp.dot(pr.astype(BF16), v_ref[0, h, pl.ds(start, tk), :], preferred_element_type=F32))
        maxes.append(m)

    def body(j, carry):
        accs, maxes = carry
        start = pl.multiple_of(j * tk, tk)
        new_accs, new_maxes = [], []
        for h in range(HEADS_PER_STEP):
            s = jnp.dot(qs[h], kt_ref[0, h, j], preferred_element_type=F32)
            m = jnp.maximum(maxes[h], jnp.max(s, axis=-1, keepdims=True))
            pr = jnp.exp(s - m)
            new_accs.append(jnp.exp(maxes[h] - m) * accs[h]
                            + jnp.dot(pr.astype(BF16), v_ref[0, h, pl.ds(start, tk), :],
                                      preferred_element_type=F32))
            new_maxes.append(m)
        return tuple(new_accs), tuple(new_maxes)

    accs, _ = lax.fori_loop(0, i, body, (tuple(accs), tuple(maxes)))
    outs = [a / a[:, HEAD_DIM:HEAD_DIM + 1] for a in accs]
    o_ref[0] = _pair_out(outs).astype(BF16)


def _attention(kind, q, kt, v, consts):
    b, _, s, _ = q.shape
    nk, tk = kt.shape[2], kt.shape[4]
    tq = tk
    head0 = 0 if kind == "sb" else N_HEADS // HEADS_PER_STEP
    in_specs = [
        pl.BlockSpec((1, HEADS_PER_STEP, tq, HEAD_SLAB), lambda bi, hp, i: (bi, hp + head0, i, 0)),
        pl.BlockSpec((1, HEADS_PER_STEP, nk, HEAD_SLAB, tk), lambda bi, hp, i: (bi, hp + head0, 0, 0, 0)),
        pl.BlockSpec((1, HEADS_PER_STEP, s, HEAD_SLAB), lambda bi, hp, i: (bi, hp + head0, 0, 0)),
    ]
    args = [q, kt, v]
    if kind == "sb":
        in_specs.append(pl.BlockSpec((tk, tk), lambda bi, hp, i: (0, 0)))
        args.append(consts["u_incl"])
        body = _sb_attn_kernel
    else:
        body = _fox_attn_kernel
    return pl.pallas_call(
        body,
        grid=(b, N_HEADS // HEADS_PER_STEP, s // tq),
        in_specs=in_specs,
        out_specs=pl.BlockSpec((1, tq, HEADS_PER_STEP * HEAD_DIM), lambda bi, hp, i: (bi, i, hp)),
        out_shape=jax.ShapeDtypeStruct((b, s, BRANCH_WIDTH), BF16),
        compiler_params=pltpu.CompilerParams(dimension_semantics=("arbitrary", "arbitrary", "arbitrary"),
                                             vmem_limit_bytes=VMEM_LIMIT),
        name=kind + "_attn",
    )(*args)


def _mixer_out_kernel(x_ref, ya_ref, ysb_ref, yfx_ref, yc_ref, g_ref, wg_ref, bg_ref, wb_ref, wo_ref,
                      fg_ref, wr_ref, trs_ref, xo_ref, xp_ref, rt_ref, cnt_ref, carry):
    tm = x_ref.shape[0]

    @pl.when(pl.program_id(0) == 0)
    def _():
        carry[...] = jnp.zeros_like(carry)

    x = x_ref[...]
    xb = _rms(x, g_ref[...]).astype(BF16)
    h = None
    for g, y_ref in enumerate((ya_ref, ysb_ref, yfx_ref, yc_ref)):
        gate = _sigmoid(jnp.dot(xb, wg_ref[g], preferred_element_type=F32) + bg_ref[g])
        term = gate * jnp.dot(y_ref[...], wb_ref[g], preferred_element_type=F32)
        h = term if h is None else h + term
    xo = x + jnp.dot(h.astype(BF16), wo_ref[...], preferred_element_type=F32)
    xo_ref[...] = xo
    xn = _rms(xo, fg_ref[...])
    xp_ref[...] = _pack_rows(xn)

    logits = jnp.dot(xn, wr_ref[...], precision=lax.Precision.HIGHEST, preferred_element_type=F32)
    lane = lax.broadcasted_iota(I32, (tm, LANES), 1).astype(F32)
    ninf = -jnp.inf
    big = float(LANES)
    gl = jnp.where(lane < N_GROUPS, logits, ninf)
    gmax = jnp.max(gl, axis=-1, keepdims=True)
    gidx = jnp.min(jnp.where(gl == gmax, lane, big), axis=-1, keepdims=True)
    p_group = 1.0 / jnp.sum(jnp.exp(gl - gmax), axis=-1, keepdims=True)
    first = N_GROUPS + EXPERTS_PER_GROUP * gidx
    el = jnp.where((lane >= first) & (lane < first + EXPERTS_PER_GROUP), logits, ninf)
    m1 = jnp.max(el, axis=-1, keepdims=True)
    i1 = jnp.min(jnp.where(el == m1, lane, big), axis=-1, keepdims=True)
    el2 = jnp.where(lane == i1, ninf, el)
    m2 = jnp.max(el2, axis=-1, keepdims=True)
    i2 = jnp.min(jnp.where(el2 == m2, lane, big), axis=-1, keepdims=True)
    e2 = jnp.exp(m2 - m1)
    w1 = p_group / (1.0 + e2)
    w2 = w1 * e2

    sel1 = lane == i1
    sel2 = lane == i2
    onehot = jnp.where(sel1, 1.0, jnp.where(sel2, 1.0, 0.0))
    before = jnp.dot(trs_ref[...], onehot.astype(BF16), preferred_element_type=F32) + carry[0:1, :]
    r1 = jnp.sum(jnp.where(sel1, before, 0.0), axis=-1, keepdims=True)
    r2 = jnp.sum(jnp.where(sel2, before, 0.0), axis=-1, keepdims=True)
    total = before[tm - 1:tm, :] + onehot[tm - 1:tm, :]
    carry[0:1, :] = total
    cnt_ref[...] = jnp.broadcast_to(total, cnt_ref.shape)

    rt = jnp.where(lane == 0, i1 - N_GROUPS, 0.0)
    rt = jnp.where(lane == 1, i2 - N_GROUPS, rt)
    rt = jnp.where(lane == 2, w1, rt)
    rt = jnp.where(lane == 3, w2, rt)
    rt = jnp.where(lane == 4, r1, rt)
    rt = jnp.where(lane == 5, r2, rt)
    rt_ref[...] = rt


def _mixer_out(x2d, ya, ysb, yfx, yc, lw, consts):
    t = x2d.shape[0]
    tm = min(TOKEN_TILE, t)

    def full(a):
        return pl.BlockSpec(a.shape, lambda i, _n=a.ndim: (0,) * _n)

    tok = lambda w: pl.BlockSpec((tm, w), lambda i: (i, 0))
    weights = [lw["mix_g"], lw["w_gate"], lw["b_gate"], lw["w_branch"], lw["w_out"], lw["ffn_g"], lw["w_router"],
               consts["tril_strict"]]
    return pl.pallas_call(
        _mixer_out_kernel,
        grid=(t // tm,),
        in_specs=[tok(D_MODEL)] + [tok(BRANCH_WIDTH)] * 4 + [full(a) for a in weights],
        out_specs=[tok(D_MODEL), tok(HALF_D), tok(LANES), pl.BlockSpec((8, LANES), lambda i: (0, 0))],
        out_shape=[jax.ShapeDtypeStruct((t, D_MODEL), F32),
                   jax.ShapeDtypeStruct((t, HALF_D), I32),
                   jax.ShapeDtypeStruct((t, LANES), F32),
                   jax.ShapeDtypeStruct((8, LANES), F32)],
        scratch_shapes=[pltpu.VMEM((8, LANES), F32)],
        compiler_params=pltpu.CompilerParams(dimension_semantics=("arbitrary",),
                                             vmem_limit_bytes=VMEM_LIMIT),
        name="mixer_out",
    )(x2d, ya, ysb, yfx, yc, *weights)


def _sc_mesh():
    return plsc.VectorSubcoreMesh(core_axis_name="core", subcore_axis_name="subcore")


def _split_index(idx):
    sub = idx[:, None] * SC_SPLIT + jnp.arange(SC_SPLIT, dtype=I32)[None, :]
    return sub.reshape(1, -1)


def _dispatch_rows(rows, pos0, pos1, n_out):
    w_full = rows.shape[1]
    rows = rows.reshape(-1, w_full // SC_SPLIT)
    t, w = rows.shape
    idx0 = _split_index(pos0)
    idx1 = _split_index(pos1)
    n_out = n_out * SC_SPLIT

    @functools.partial(pl.kernel, out_type=jax.ShapeDtypeStruct((n_out, w), rows.dtype), mesh=_sc_mesh(),
                       scratch_types=[])
    def scatter_kernel(x_hbm, i0_hbm, i1_hbm, o_hbm):
        def body(x_vmem, i0_vmem, i1_vmem):
            pltpu.sync_copy(x_vmem, o_hbm.at[i0_vmem.at[0]])
            pltpu.sync_copy(x_vmem, o_hbm.at[i1_vmem.at[0]])

        pltpu.emit_pipeline(
            body,
            grid=(t // SC_WINDOW,),
            in_specs=[pl.BlockSpec((SC_WINDOW, w), lambda i: (i, 0)),
                      pl.BlockSpec((1, SC_WINDOW), lambda i: (0, i)),
                      pl.BlockSpec((1, SC_WINDOW), lambda i: (0, i))],
            out_specs=[],
            core_axis_name=("core", "subcore"),
            dimension_semantics=(pltpu.PARALLEL,),
        )(x_hbm, i0_hbm, i1_hbm)

    return scatter_kernel(rows, idx0, idx1).reshape(-1, w_full)


def _collect_rows(table, idx):
    w_full = table.shape[1]
    table = table.reshape(-1, w_full // SC_SPLIT)
    w = table.shape[1]
    idx2 = _split_index(idx)
    m = idx2.shape[1]

    @functools.partial(pl.kernel, out_type=jax.ShapeDtypeStruct((m, w), table.dtype), mesh=_sc_mesh(),
                       scratch_types=[])
    def gather_kernel(x_hbm, i_hbm, o_hbm):
        def body(i_vmem, o_vmem):
            pltpu.sync_copy(x_hbm.at[i_vmem.at[0]], o_vmem)

        pltpu.emit_pipeline(
            body,
            grid=(m // SC_WINDOW,),
            in_specs=[pl.BlockSpec((1, SC_WINDOW), lambda i: (0, i))],
            out_specs=[pl.BlockSpec((SC_WINDOW, w), lambda i: (i, 0))],
            core_axis_name=("core", "subcore"),
            dimension_semantics=(pltpu.PARALLEL,),
        )(i_hbm, o_hbm)

    return gather_kernel(table, idx2).reshape(-1, w_full)


def _moe_ffn_kernel(te_ref, nt_ref, xs_ref, wgu_ref, wd_ref, ys_ref):
    i = pl.program_id(0)

    @pl.when(i < nt_ref[0])
    def _():
        lo, hi = _unpack_rows(xs_ref[...])
        gu = (jnp.dot(lo.astype(BF16), wgu_ref[0, :HALF_D, :], preferred_element_type=F32)
              + jnp.dot(hi.astype(BF16), wgu_ref[0, HALF_D:, :], preferred_element_type=F32))
        gate = gu[:, :D_EXPERT]
        hid = gate * _sigmoid(gate) * gu[:, D_EXPERT:]
        ys_ref[...] = _pack_rows(jnp.dot(hid.astype(BF16), wd_ref[0], preferred_element_type=F32))

    @pl.when(i >= nt_ref[0])
    def _():
        ys_ref[...] = jnp.zeros_like(ys_ref)


def _moe_ffn(xs, tile_expert, n_tiles, lw):
    p = xs.shape[0]
    tg = GROUP_TILE
    grid_spec = pltpu.PrefetchScalarGridSpec(
        num_scalar_prefetch=2,
        grid=(p // tg,),
        in_specs=[pl.BlockSpec((tg, HALF_D), lambda i, te, nt: (i, 0)),
                  pl.BlockSpec((1, D_MODEL, 2 * D_EXPERT), lambda i, te, nt: (te[i], 0, 0)),
                  pl.BlockSpec((1, D_EXPERT, D_MODEL), lambda i, te, nt: (te[i], 0, 0))],
        out_specs=pl.BlockSpec((tg, HALF_D), lambda i, te, nt: (i, 0)),
    )
    return pl.pallas_call(
        _moe_ffn_kernel,
        grid_spec=grid_spec,
        out_shape=jax.ShapeDtypeStruct((p, HALF_D), I32),
        compiler_params=pltpu.CompilerParams(dimension_semantics=("arbitrary",),
                                             vmem_limit_bytes=VMEM_LIMIT),
        name="moe_ffn",
    )(tile_expert, n_tiles, xs, lw["w_gu"], lw["w_down"])


def _route_plan(route, cnt, t):
    tg = GROUP_TILE
    n_tiles_max = (2 * t) // tg + N_EXPERTS
    counts = cnt[0, N_GROUPS:N_GROUPS + N_EXPERTS].astype(I32)
    padded = ((counts + tg - 1) // tg) * tg
    ends = jnp.cumsum(padded)
    offs = ends - padded
    e0 = route[:, 0].astype(I32)
    e1 = route[:, 1].astype(I32)
    pos0 = offs[e0] + route[:, 4].astype(I32)
    pos1 = offs[e1] + route[:, 5].astype(I32)
    tile_start = jnp.arange(n_tiles_max, dtype=I32) * tg
    tile_expert = jnp.minimum(jnp.searchsorted(ends, tile_start, side="right"), N_EXPERTS - 1).astype(I32)
    n_tiles = (ends[-1] // tg).astype(I32).reshape(1)
    last = tile_expert[jnp.maximum(n_tiles[0] - 1, 0)]
    tile_expert = jnp.where(tile_start < ends[-1], tile_expert, last)
    return pos0, pos1, tile_expert, n_tiles, n_tiles_max * tg


def _moe(xp, route, cnt, lw):
    t = xp.shape[0]
    pos0, pos1, tile_expert, n_tiles, p_rows = _route_plan(route, cnt, t)
    xs = _dispatch_rows(xp, pos0, pos1, p_rows)
    ys = _moe_ffn(xs, tile_expert, n_tiles, lw)
    y01 = _collect_rows(ys, jnp.concatenate([pos0, pos1]))
    return y01[:t], y01[t:]


def _final_kernel(x_ref, y0_ref, y1_ref, rt_ref, g_ref, o_ref):
    rt = rt_ref[...]
    y0l, y0h = _unpack_rows(y0_ref[...])
    y1l, y1h = _unpack_rows(y1_ref[...])
    w0 = rt[:, 2:3]
    w1 = rt[:, 3:4]
    x = x_ref[...] + jnp.concatenate([w0 * y0l + w1 * y1l, w0 * y0h + w1 * y1h], axis=1)
    o_ref[...] = _rms(x, g_ref[...])


def _final(x2d, y0, y1, route, g):
    t = x2d.shape[0]
    tm = min(TOKEN_TILE, t)
    tok = lambda w: pl.BlockSpec((tm, w), lambda i: (i, 0))
    return pl.pallas_call(
        _final_kernel,
        grid=(t // tm,),
        in_specs=[tok(D_MODEL), tok(HALF_D), tok(HALF_D), tok(LANES), pl.BlockSpec((1, D_MODEL), lambda i: (0, 0))],
        out_specs=tok(D_MODEL),
        out_shape=jax.ShapeDtypeStruct((t, D_MODEL), F32),
        compiler_params=pltpu.CompilerParams(dimension_semantics=("arbitrary",),
                                             vmem_limit_bytes=VMEM_LIMIT),
        name="final_norm",
    )(x2d, y0, y1, route, g)


def _constants(tm, tk):
    r = jnp.arange(tm)
    tril = (r[None, :] <= r[:, None]).astype(F32)
    tril_strict = (r[None, :] < r[:, None]).astype(BF16)
    rk = jnp.arange(tk)
    u_incl = (rk[:, None] >= rk[None, :]).astype(BF16)
    nh = N_HEADS
    pq = jnp.zeros((3, LANES, nh * HEAD_SLAB), F32)
    pk = jnp.zeros((3, nh * HEAD_DIM, LANES), F32)
    qc = jnp.zeros((1, nh * HEAD_SLAB), F32)
    kc = jnp.zeros((nh * HEAD_DIM, LANES), F32)
    for part in range(3):
        for h in range(nh):
            pq = pq.at[part, h, h * HEAD_SLAB + HEAD_DIM + part].set(1.0)
            qc = qc.at[0, h * HEAD_SLAB + HEAD_DIM + 3 + part].set(1.0)
            kc = kc.at[h * HEAD_DIM + part, :].set(1.0)
            pk = pk.at[part, h * HEAD_DIM + 3 + part, h].set(1.0)
    vc = jnp.zeros((1, HEAD_SLAB), F32).at[0, HEAD_DIM].set(1.0)
    return {"tril": tril, "tril_strict": tril_strict, "u_incl": u_incl, "pq": pq.astype(BF16),
            "pk": pk.astype(BF16), "qc": qc, "kc": kc, "vc": vc}


def _layer_weights(layer, mix_norm_g, w_in, b_forget, conv_a_w, conf_dw_w, conf_dw_b, conf_ln_g, conf_ln_b,
                   w_branch, w_gate, b_gate, w_out, ffn_norm_g, w_router_group, w_router_expert,
                   w_expert_gate, w_expert_up, w_expert_down):
    w = w_in[layer]
    bw = BRANCH_WIDTH
    a_x, a_b, a_c, sb_q, sb_k, sb_v, fx_q, fx_k, fx_v = [w[:, i * bw:(i + 1) * bw] for i in range(9)]
    fx_f = w[:, 9 * bw:9 * bw + N_HEADS]
    conf = w[:, 9 * bw + N_HEADS:]
    scale = HEAD_DIM ** -0.5
    f_pad = jnp.pad(fx_f, ((0, 0), (0, LANES - N_HEADS)))
    w_main = jnp.concatenate([a_x, a_b, a_c, sb_q * scale, fx_q * scale, sb_v, fx_v, conf, f_pad], axis=1)
    w_t = jnp.concatenate([sb_k, fx_k], axis=1).T
    w_router = jnp.concatenate([w_router_group[layer], w_router_expert[layer].reshape(D_MODEL, N_EXPERTS)], axis=1)
    w_router = jnp.pad(w_router, ((0, 0), (0, LANES - N_GROUPS - N_EXPERTS)))
    return {
        "mix_g": mix_norm_g[layer].reshape(1, D_MODEL),
        "w_main": w_main.astype(BF16),
        "w_t": w_t.astype(BF16),
        "bf": jnp.pad(b_forget[layer], (0, LANES - N_HEADS)).reshape(1, LANES),
        "caw": conv_a_w[layer],
        "cdw": jnp.pad(conf_dw_w[layer], ((0, 1), (0, 0))),
        "cdb": conf_dw_b[layer].reshape(1, bw),
        "lng": conf_ln_g[layer].reshape(1, bw),
        "lnb": conf_ln_b[layer].reshape(1, bw),
        "w_gate": w_gate[layer].astype(BF16),
        "b_gate": b_gate[layer].reshape(4, 1, D_MODEL),
        "w_branch": w_branch[layer].astype(BF16),
        "w_out": w_out[layer].astype(BF16),
        "ffn_g": ffn_norm_g[layer].reshape(1, D_MODEL),
        "w_router": w_router,
        "w_gu": jnp.concatenate([w_expert_gate[layer], w_expert_up[layer]], axis=2).astype(BF16),
        "w_down": w_expert_down[layer].astype(BF16),
    }


def kernel(x, mix_norm_g, w_in, b_forget, conv_a_w, conf_dw_w, conf_dw_b, conf_ln_g, conf_ln_b, w_branch, w_gate,
           b_gate, w_out, ffn_norm_g, w_router_group, w_router_expert, w_expert_gate, w_expert_up, w_expert_down,
           final_norm_g):
    b, s, _ = x.shape
    t = b * s
    depth = w_in.shape[0]
    consts = _constants(min(TOKEN_TILE, s), min(ATTN_TILE, s))
    comb = None
    for layer in range(depth):
        lw = _layer_weights(layer, mix_norm_g, w_in, b_forget, conv_a_w, conf_dw_w, conf_dw_b, conf_ln_g,
                            conf_ln_b, w_branch, w_gate, b_gate, w_out, ffn_norm_g, w_router_group,
                            w_router_expert, w_expert_gate, w_expert_up, w_expert_down)
        outs = _mixer_in(x, comb, lw, consts)
        if comb is not None:
            x, *outs = outs
        ya, yc, q, kt, v = outs
        ysb = _attention("sb", q, kt, v, consts)
        yfx = _attention("fox", q, kt, v, consts)
        x2d, xp, route, cnt = _mixer_out(x.reshape(t, D_MODEL), ya.reshape(t, -1), ysb.reshape(t, -1),
                                         yfx.reshape(t, -1), yc.reshape(t, -1), lw, consts)
        y0, y1 = _moe(xp, route, cnt, lw)
        x = x2d.reshape(b, s, D_MODEL)
        comb = (y0.reshape(b, s, HALF_D), y1.reshape(b, s, HALF_D), route.reshape(b, s, LANES))
    out = _final(x.reshape(t, D_MODEL), comb[0].reshape(t, HALF_D), comb[1].reshape(t, HALF_D),
                 comb[2].reshape(t, LANES), final_norm_g.reshape(1, D_MODEL))
    return out.reshape(b, s, D_MODEL)
```

```python
import functools

import jax
import jax.numpy as jnp
from jax import lax
from jax.experimental import pallas as pl
from jax.experimental.pallas import tpu as pltpu
from jax.experimental.pallas import tpu_sc as plsc

F32 = jnp.float32
BF16 = jnp.bfloat16
I32 = jnp.int32

D_MODEL = 1024
BRANCH_WIDTH = 256
HEAD_DIM = 64
N_HEADS = 4
HEAD_SLAB = 128
CONV_A_WIDTH = 3
CONF_WIDTH = 31
N_GROUPS = 4
EXPERTS_PER_GROUP = 8
N_EXPERTS = 32
D_EXPERT = 256
EPS = 1e-6
LANES = 128
HALF_D = D_MODEL // 2
SC_SPLIT = 2
PLANE_W = HALF_D // SC_SPLIT

TOKEN_TILE = 512
ATTN_TILE = 256
SB_HEADS_PER_STEP = 4
FOX_HEADS_PER_STEP = 4
ATTN_TRIPS = (4, 2, 1)
GROUP_TILE = 256
CONV_ROW_CHUNK = 64
SC_WINDOW = 128
VMEM_LIMIT = 56 * 1024 * 1024

_C_AX, _C_AB, _C_AC = 0, 256, 512
_C_Q = 768
_C_V = 1280
_C_CU, _C_CG = 1792, 2048
_C_F = 2304
_N_MAIN = 2432


def _rms(x, g):
    return x * lax.rsqrt(jnp.mean(x * x, axis=-1, keepdims=True) + EPS) * g


def _softplus(z):
    return jnp.maximum(z, 0.0) + jnp.log1p(jnp.exp(-jnp.abs(z)))


def _sigmoid(z):
    return 1.0 / (1.0 + jnp.exp(-z))


def _split3(v):
    hi = v.astype(BF16)
    r = v - hi.astype(F32)
    mid = r.astype(BF16)
    lo = (r - mid.astype(F32)).astype(BF16)
    return hi, mid, lo


def _pack_rows(v):
    lo = pltpu.bitcast(v[:, :HALF_D].astype(BF16).astype(F32), jnp.uint32)
    hi = pltpu.bitcast(v[:, HALF_D:].astype(BF16).astype(F32), jnp.uint32)
    return pltpu.bitcast((lo >> 16) | hi, I32)


def _unpack_rows(w):
    u = pltpu.bitcast(w, jnp.uint32)
    lo = pltpu.bitcast(u << 16, F32)
    hi = pltpu.bitcast(u & jnp.uint32(0xFFFF0000), F32)
    return lo, hi


def _store_planes(ref, v, lead=()):
    packed = _pack_rows(v)
    for k in range(SC_SPLIT):
        ref[(k, *lead)] = packed[:, k * PLANE_W:(k + 1) * PLANE_W]


def _load_planes(ref, lead=()):
    los, his = zip(*[_unpack_rows(ref[(k, *lead)]) for k in range(SC_SPLIT)])
    return list(los) + list(his)


def _combine(x, y0_ref, y1_ref, rt, lead=()):
    w0 = rt[:, 2:3]
    w1 = rt[:, 3:4]
    parts = [w0 * a + w1 * b for a, b in zip(_load_planes(y0_ref, lead), _load_planes(y1_ref, lead))]
    return x + jnp.concatenate(parts, axis=1)


def _mixer_in_kernel(combine, *refs):
    if combine:
        (x_ref, y0_ref, y1_ref, rt_ref, *refs) = refs
    else:
        (x_ref, *refs) = refs
    (g_ref, wm_ref, wt_ref, bf_ref, caw_ref, cdw_ref, cdb_ref, lng_ref, lnb_ref, tril_ref,
     pq_ref, qc_ref, pk_ref, kc_ref, vc_ref, *refs) = refs
    if combine:
        (xo_ref, *refs) = refs
    (ya_ref, yc_ref, q_ref, kt_ref, v_ref, bufa, bufc, dcarry) = refs

    tm = x_ref.shape[1]
    tk = kt_ref.shape[-1]

    @pl.when(pl.program_id(1) == 0)
    def _():
        bufa[0:8, :] = jnp.zeros((8, BRANCH_WIDTH), F32)
        bufc[0:32, :] = jnp.zeros((32, BRANCH_WIDTH), F32)
        dcarry[...] = jnp.zeros_like(dcarry)

    x = x_ref[0]
    if combine:
        x = _combine(x, y0_ref, y1_ref, rt_ref[0], lead=(0, 0))
        xo_ref[0] = x
    xb = _rms(x, g_ref[...]).astype(BF16)

    p = jnp.dot(xb, wm_ref[...], preferred_element_type=F32)
    pt = lax.dot_general(wt_ref[...], xb, (((1,), (1,)), ((), ())),
                         preferred_element_type=F32)

    ca = p[:, _C_AC:_C_AC + 256] * p[:, _C_AX:_C_AX + 256]
    bufa[8:8 + tm, :] = ca
    caw = caw_ref[...]
    conv = caw[0:1] * bufa[6:6 + tm, :] + caw[1:2] * bufa[7:7 + tm, :] + caw[2:3] * ca
    ya_ref[0] = (p[:, _C_AB:_C_AB + 256] * conv).astype(BF16)
    bufa[0:8, :] = ca[tm - 8:tm]

    u = p[:, _C_CU:_C_CU + 256] * _sigmoid(p[:, _C_CG:_C_CG + 256])
    bufc[32:32 + tm, :] = u
    cdw = cdw_ref[...]
    cdb = cdb_ref[...]
    lng = lng_ref[...]
    lnb = lnb_ref[...]
    for r in range(tm // CONV_ROW_CHUNK):
        base = r * CONV_ROW_CHUNK + 32 - (CONF_WIDTH - 1)
        acc = jnp.broadcast_to(cdb, (CONV_ROW_CHUNK, BRANCH_WIDTH))
        for k in range(CONF_WIDTH):
            acc = acc + cdw[k:k + 1] * bufc[base + k:base + k + CONV_ROW_CHUNK, :]
        mu = jnp.mean(acc, axis=-1, keepdims=True)
        cen = acc - mu
        var = jnp.mean(cen * cen, axis=-1, keepdims=True)
        yn = cen * lax.rsqrt(var + EPS) * lng + lnb
        yc_ref[0, r * CONV_ROW_CHUNK:(r + 1) * CONV_ROW_CHUNK, :] = (yn * _sigmoid(yn)).astype(BF16)
    bufc[0:32, :] = u[tm - 32:tm]

    logf = -_softplus(-(p[:, _C_F:_C_F + LANES] + bf_ref[...]))
    dcum = jnp.dot(tril_ref[...], logf, precision=lax.Precision.HIGHEST,
                   preferred_element_type=F32) + dcarry[0:1, :]
    dcarry[0:1, :] = dcum[tm - 1:tm, :]
    dcum_t = dcum.T
    qh, qm, ql = _split3(dcum)
    kh, km, kl = _split3(-dcum_t)
    q_extra = (jnp.dot(qh, pq_ref[0], preferred_element_type=F32)
               + jnp.dot(qm, pq_ref[1], preferred_element_type=F32)
               + jnp.dot(ql, pq_ref[2], preferred_element_type=F32) + qc_ref[...])
    k_extra = (jnp.dot(pk_ref[0], kh, preferred_element_type=F32)
               + jnp.dot(pk_ref[1], km, preferred_element_type=F32)
               + jnp.dot(pk_ref[2], kl, preferred_element_type=F32)
               + jnp.concatenate([kc_ref[...]] * (tm // LANES), axis=1))

    lane = lax.broadcasted_iota(I32, (tm, HEAD_SLAB), 1)
    low = lane < HEAD_DIM
    vc = vc_ref[...]
    for hd in range(2 * N_HEADS):
        is_fox = hd >= N_HEADS
        pair = (hd // 2) * HEAD_SLAB
        qs = p[:, _C_Q + pair:_C_Q + pair + HEAD_SLAB]
        vs = p[:, _C_V + pair:_C_V + pair + HEAD_SLAB]
        if hd % 2:
            qs = pltpu.roll(qs, HEAD_DIM, axis=1)
            vs = pltpu.roll(vs, HEAD_DIM, axis=1)
        if is_fox:
            hf = hd - N_HEADS
            qx = q_extra[:, hf * HEAD_SLAB:(hf + 1) * HEAD_SLAB]
            kx = k_extra[hf * HEAD_DIM:(hf + 1) * HEAD_DIM, :]
            vx = vc
        else:
            qx = 0.0
            kx = jnp.zeros((HEAD_DIM, tm), F32)
            vx = 0.0
        q_ref[0, hd] = jnp.where(low, qs, qx).astype(BF16)
        v_ref[0, hd] = jnp.where(low, vs, vx).astype(BF16)
        kfull = jnp.concatenate([pt[hd * HEAD_DIM:(hd + 1) * HEAD_DIM, :], kx], axis=0).astype(BF16)
        for c in range(tm // tk):
            kt_ref[0, hd, c] = kfull[:, c * tk:(c + 1) * tk]


def _mixer_in(x, comb, lw, consts):
    b, s, _ = x.shape
    tm = min(TOKEN_TILE, s)
    tk = min(ATTN_TILE, s)
    nk = s // tk
    nh2 = 2 * N_HEADS
    combine = comb is not None

    def full(a):
        return pl.BlockSpec(a.shape, lambda bi, si, _n=a.ndim: (0,) * _n)

    tok = lambda w: pl.BlockSpec((1, tm, w), lambda bi, si: (bi, si, 0))
    in_arrays = [x]
    in_specs = [tok(D_MODEL)]
    if combine:
        y01, route = comb
        in_arrays += [y01, y01, route]
        in_specs += [pl.BlockSpec((SC_SPLIT, 1, 1, tm, PLANE_W), lambda bi, si, _c=c: (0, _c, bi, si, 0))
                     for c in range(2)] + [tok(LANES)]
    weights = [lw["mix_g"], lw["w_main"], lw["w_t"], lw["bf"], lw["caw"], lw["cdw"], lw["cdb"], lw["lng"],
               lw["lnb"], consts["tril"], consts["pq"], consts["qc"], consts["pk"], consts["kc"], consts["vc"]]
    in_arrays += weights
    in_specs += [full(a) for a in weights]

    out_shape = []
    out_specs = []
    if combine:
        out_shape.append(jax.ShapeDtypeStruct((b, s, D_MODEL), F32))
        out_specs.append(tok(D_MODEL))
    out_shape += [
        jax.ShapeDtypeStruct((b, s, BRANCH_WIDTH), BF16),
        jax.ShapeDtypeStruct((b, s, BRANCH_WIDTH), BF16),
        jax.ShapeDtypeStruct((b, nh2, s, HEAD_SLAB), BF16),
        jax.ShapeDtypeStruct((b, nh2, nk, HEAD_SLAB, tk), BF16),
        jax.ShapeDtypeStruct((b, nh2, s, HEAD_SLAB), BF16),
    ]
    out_specs += [
        tok(BRANCH_WIDTH), tok(BRANCH_WIDTH),
        pl.BlockSpec((1, nh2, tm, HEAD_SLAB), lambda bi, si: (bi, 0, si, 0)),
        pl.BlockSpec((1, nh2, tm // tk, HEAD_SLAB, tk), lambda bi, si: (bi, 0, si, 0, 0)),
        pl.BlockSpec((1, nh2, tm, HEAD_SLAB), lambda bi, si: (bi, 0, si, 0)),
    ]
    return pl.pallas_call(
        functools.partial(_mixer_in_kernel, combine),
        grid=(b, s // tm),
        in_specs=in_specs,
        out_specs=out_specs,
        out_shape=out_shape,
        scratch_shapes=[pltpu.VMEM((8 + tm, BRANCH_WIDTH), F32),
                        pltpu.VMEM((32 + tm, BRANCH_WIDTH), F32),
                        pltpu.VMEM((8, LANES), F32)],
        compiler_params=pltpu.CompilerParams(dimension_semantics=("arbitrary", "arbitrary"),
                                             vmem_limit_bytes=VMEM_LIMIT),
        name="mixer_in",
    )(*in_arrays)


def _pair_out(accs):
    lane = lax.broadcasted_iota(I32, accs[0].shape, 1)
    return jnp.where(lane < HEAD_DIM, accs[0], pltpu.roll(accs[1], HEAD_DIM, axis=1))


def _sb_attn_kernel(q_ref, kt_ref, v_ref, u_ref, o_ref):
    tq = q_ref.shape[2]
    tk = kt_ref.shape[-1]
    i = pl.program_id(2)
    umat = u_ref[...]
    row = lax.broadcasted_iota(I32, (tq, tk), 0)
    col = lax.broadcasted_iota(I32, (tq, tk), 1)
    strict = col < row
    nh = q_ref.shape[1]
    qs = [q_ref[0, h] for h in range(nh)]

    def block(h, j, tail, mask):
        z = jnp.dot(qs[h], kt_ref[0, h, j], preferred_element_type=F32)
        sp = jnp.maximum(z, 0.0) + jnp.log(1.0 + jnp.exp(-jnp.abs(z)))
        if mask is not None:
            sp = jnp.where(mask, sp, 0.0)
        later = jnp.dot(sp.astype(BF16), umat, preferred_element_type=F32)
        e = z - later if tail is None else z - later - tail
        w = jnp.exp(e)
        if mask is not None:
            w = jnp.where(mask, w, 0.0)
        start = pl.multiple_of(j * tk, tk)
        pv = jnp.dot(w.astype(BF16), v_ref[0, h, pl.ds(start, tk), :], preferred_element_type=F32)
        return pv, later[:, 0:1]

    accs, tails = [], []
    for h in range(nh):
        pv, tot = block(h, i, None, strict)
        accs.append(pv)
        tails.append(tot)

    def make_body(n_blocks, first):
        def body(jj, carry):
            accs, tails = carry
            new_accs, new_tails = [], []
            for h in range(nh):
                acc, tail = accs[h], tails[h]
                for u in range(n_blocks):
                    pv, tot = block(h, first - n_blocks * jj - u, tail, None)
                    acc = acc + pv
                    tail = tail + tot
                new_accs.append(acc)
                new_tails.append(tail)
            return tuple(new_accs), tuple(new_tails)
        return body

    carry = (tuple(accs), tuple(tails))
    done = 0
    for k, n_blocks in reversed(list(enumerate(ATTN_TRIPS))):
        left = i - done
        trips = left // n_blocks if k == 0 else lax.rem(left, ATTN_TRIPS[k - 1]) // n_blocks
        carry = lax.fori_loop(0, trips, make_body(n_blocks, i - 1 - done), carry)
        done = done + trips * n_blocks
    accs = carry[0]
    o_ref[0] = jnp.concatenate([_pair_out(accs[k:k + 2]) for k in range(0, nh, 2)], axis=1).astype(BF16)


def _fox_attn_kernel(q_ref, kt_ref, v_ref, o_ref):
    tq = q_ref.shape[2]
    tk = kt_ref.shape[-1]
    i = pl.program_id(2)
    row = lax.broadcasted_iota(I32, (tq, tk), 0)
    col = lax.broadcasted_iota(I32, (tq, tk), 1)
    causal = col <= row
    nh = q_ref.shape[1]
    qs = [q_ref[0, h] for h in range(nh)]

    def vblock(h, j):
        return v_ref[0, h, pl.ds(pl.multiple_of(j * tk, tk), tk), :]

    accs, maxes = [], []
    for h in range(nh):
        s = jnp.where(causal, jnp.dot(qs[h], kt_ref[0, h, i], preferred_element_type=F32), -jnp.inf)
        m = jnp.max(s, axis=-1, keepdims=True)
        accs.append(jnp.dot(jnp.exp(s - m).astype(BF16), vblock(h, i), preferred_element_type=F32))
        maxes.append(m)

    def make_body(n_blocks, first):
        def body(jj, carry):
            accs, maxes = carry
            new_accs, new_maxes = [], []
            for h in range(nh):
                js = [first + n_blocks * jj + u for u in range(n_blocks)]
                ss = [jnp.dot(qs[h], kt_ref[0, h, j], preferred_element_type=F32) for j in js]
                m = maxes[h]
                for s in ss:
                    m = jnp.maximum(m, jnp.max(s, axis=-1, keepdims=True))
                acc = jnp.exp(maxes[h] - m) * accs[h]
                for s, j in zip(ss, js):
                    acc = acc + jnp.dot(jnp.exp(s - m).astype(BF16), vblock(h, j), preferred_element_type=F32)
                new_accs.append(acc)
                new_maxes.append(m)
            return tuple(new_accs), tuple(new_maxes)
        return body

    carry = (tuple(accs), tuple(maxes))
    done = 0
    for n_blocks in ATTN_TRIPS:
        trips = (i - done) // n_blocks
        carry = lax.fori_loop(0, trips, make_body(n_blocks, done), carry)
        done = done + trips * n_blocks
    accs = carry[0]
    outs = [a / a[:, HEAD_DIM:HEAD_DIM + 1] for a in accs]
    o_ref[0] = jnp.concatenate([_pair_out(outs[k:k + 2]) for k in range(0, nh, 2)], axis=1).astype(BF16)


def _attention(kind, q, kt, v, consts):
    b, _, s, _ = q.shape
    nk, tk = kt.shape[2], kt.shape[4]
    tq = tk
    hps = SB_HEADS_PER_STEP if kind == "sb" else FOX_HEADS_PER_STEP
    head0 = 0 if kind == "sb" else N_HEADS // hps
    in_specs = [
        pl.BlockSpec((1, hps, tq, HEAD_SLAB), lambda bi, hp, i: (bi, hp + head0, i, 0)),
        pl.BlockSpec((1, hps, nk, HEAD_SLAB, tk), lambda bi, hp, i: (bi, hp + head0, 0, 0, 0)),
        pl.BlockSpec((1, hps, s, HEAD_SLAB), lambda bi, hp, i: (bi, hp + head0, 0, 0)),
    ]
    args = [q, kt, v]
    if kind == "sb":
        in_specs.append(pl.BlockSpec((tk, tk), lambda bi, hp, i: (0, 0)))
        args.append(consts["u_incl"])
        body = _sb_attn_kernel
    else:
        body = _fox_attn_kernel
    return pl.pallas_call(
        body,
        grid=(b, N_HEADS // hps, s // tq),
        in_specs=in_specs,
        out_specs=pl.BlockSpec((1, tq, hps * HEAD_DIM), lambda bi, hp, i: (bi, i, hp)),
        out_shape=jax.ShapeDtypeStruct((b, s, BRANCH_WIDTH), BF16),
        compiler_params=pltpu.CompilerParams(dimension_semantics=("arbitrary", "arbitrary", "arbitrary"),
                                             vmem_limit_bytes=VMEM_LIMIT),
        name=kind + "_attn",
    )(*args)


def _mixer_out_kernel(x_ref, ya_ref, ysb_ref, yfx_ref, yc_ref, g_ref, wg_ref, bg_ref, wb_ref, wo_ref,
                      fg_ref, wr_ref, trs_ref, xo_ref, xp_ref, rt_ref, rtt_ref, cnt_ref, carry):
    tm = x_ref.shape[0]

    @pl.when(pl.program_id(0) == 0)
    def _():
        carry[...] = jnp.zeros_like(carry)

    x = x_ref[...]
    xb = _rms(x, g_ref[...]).astype(BF16)
    h = None
    for g, y_ref in enumerate((ya_ref, ysb_ref, yfx_ref, yc_ref)):
        gate = _sigmoid(jnp.dot(xb, wg_ref[g], preferred_element_type=F32) + bg_ref[g])
        term = gate * jnp.dot(y_ref[...], wb_ref[g], preferred_element_type=F32)
        h = term if h is None else h + term
    xo = x + jnp.dot(h.astype(BF16), wo_ref[...], preferred_element_type=F32)
    xo_ref[...] = xo
    xn = _rms(xo, fg_ref[...])
    _store_planes(xp_ref, xn)

    logits = jnp.dot(xn, wr_ref[...], precision=lax.Precision.HIGHEST, preferred_element_type=F32)
    lane = lax.broadcasted_iota(I32, (tm, LANES), 1).astype(F32)
    ninf = -jnp.inf
    big = float(LANES)
    gl = jnp.where(lane < N_GROUPS, logits, ninf)
    gmax = jnp.max(gl, axis=-1, keepdims=True)
    gidx = jnp.min(jnp.where(gl == gmax, lane, big), axis=-1, keepdims=True)
    p_group = 1.0 / jnp.sum(jnp.exp(gl - gmax), axis=-1, keepdims=True)
    first = N_GROUPS + EXPERTS_PER_GROUP * gidx
    el = jnp.where((lane >= first) & (lane < first + EXPERTS_PER_GROUP), logits, ninf)
    m1 = jnp.max(el, axis=-1, keepdims=True)
    i1 = jnp.min(jnp.where(el == m1, lane, big), axis=-1, keepdims=True)
    el2 = jnp.where(lane == i1, ninf, el)
    m2 = jnp.max(el2, axis=-1, keepdims=True)
    i2 = jnp.min(jnp.where(el2 == m2, lane, big), axis=-1, keepdims=True)
    e2 = jnp.exp(m2 - m1)
    w1 = p_group / (1.0 + e2)
    w2 = w1 * e2

    sel1 = lane == i1
    sel2 = lane == i2
    onehot = jnp.where(sel1, 1.0, jnp.where(sel2, 1.0, 0.0))
    before = jnp.dot(trs_ref[...], onehot.astype(BF16), preferred_element_type=F32) + carry[0:1, :]
    r1 = jnp.sum(jnp.where(sel1, before, 0.0), axis=-1, keepdims=True)
    r2 = jnp.sum(jnp.where(sel2, before, 0.0), axis=-1, keepdims=True)
    total = before[tm - 1:tm, :] + onehot[tm - 1:tm, :]
    carry[0:1, :] = total
    cnt_ref[...] = jnp.broadcast_to(total, cnt_ref.shape)

    rt = jnp.where(lane == 0, i1 - N_GROUPS, 0.0)
    rt = jnp.where(lane == 1, i2 - N_GROUPS, rt)
    rt = jnp.where(lane == 2, w1, rt)
    rt = jnp.where(lane == 3, w2, rt)
    rt = jnp.where(lane == 4, r1, rt)
    rt = jnp.where(lane == 5, r2, rt)
    rt_ref[...] = rt
    rtt_ref[...] = rt.T[0:8, :]


def _mixer_out(x2d, ya, ysb, yfx, yc, lw, consts):
    t = x2d.shape[0]
    tm = min(TOKEN_TILE, t)

    def full(a):
        return pl.BlockSpec(a.shape, lambda i, _n=a.ndim: (0,) * _n)

    tok = lambda w: pl.BlockSpec((tm, w), lambda i: (i, 0))
    weights = [lw["mix_g"], lw["w_gate"], lw["b_gate"], lw["w_branch"], lw["w_out"], lw["ffn_g"], lw["w_router"],
               consts["tril_strict"]]
    return pl.pallas_call(
        _mixer_out_kernel,
        grid=(t // tm,),
        in_specs=[tok(D_MODEL)] + [tok(BRANCH_WIDTH)] * 4 + [full(a) for a in weights],
        out_specs=[tok(D_MODEL), pl.BlockSpec((SC_SPLIT, tm, PLANE_W), lambda i: (0, i, 0)), tok(LANES),
                   pl.BlockSpec((8, tm), lambda i: (0, i)), pl.BlockSpec((8, LANES), lambda i: (0, 0))],
        out_shape=[jax.ShapeDtypeStruct((t, D_MODEL), F32),
                   jax.ShapeDtypeStruct((SC_SPLIT, t, PLANE_W), I32),
                   jax.ShapeDtypeStruct((t, LANES), F32),
                   jax.ShapeDtypeStruct((8, t), F32),
                   jax.ShapeDtypeStruct((8, LANES), F32)],
        scratch_shapes=[pltpu.VMEM((8, LANES), F32)],
        compiler_params=pltpu.CompilerParams(dimension_semantics=("arbitrary",),
                                             vmem_limit_bytes=VMEM_LIMIT),
        name="mixer_out",
    )(x2d, ya, ysb, yfx, yc, *weights)


def _sc_mesh():
    return plsc.VectorSubcoreMesh(core_axis_name="core", subcore_axis_name="subcore")


def _plane_index(idx, rows_per_plane):
    return jnp.concatenate([idx + k * rows_per_plane for k in range(SC_SPLIT)]).reshape(1, -1)


def _dispatch_rows(planes, pos0, pos1, n_out):
    w = planes.shape[2]
    rows = planes.reshape(-1, w)
    t = rows.shape[0]
    idx0 = _plane_index(pos0, n_out)
    idx1 = _plane_index(pos1, n_out)

    @functools.partial(pl.kernel, out_type=jax.ShapeDtypeStruct((SC_SPLIT * n_out, w), rows.dtype),
                       mesh=_sc_mesh(), scratch_types=[])
    def scatter_kernel(x_hbm, i0_hbm, i1_hbm, o_hbm):
        def body(x_vmem, i0_vmem, i1_vmem):
            pltpu.sync_copy(x_vmem, o_hbm.at[i0_vmem.at[0]])
            pltpu.sync_copy(x_vmem, o_hbm.at[i1_vmem.at[0]])

        pltpu.emit_pipeline(
            body,
            grid=(t // SC_WINDOW,),
            in_specs=[pl.BlockSpec((SC_WINDOW, w), lambda i: (i, 0)),
                      pl.BlockSpec((1, SC_WINDOW), lambda i: (0, i)),
                      pl.BlockSpec((1, SC_WINDOW), lambda i: (0, i))],
            out_specs=[],
            core_axis_name=("core", "subcore"),
            dimension_semantics=(pltpu.PARALLEL,),
        )(x_hbm, i0_hbm, i1_hbm)

    return scatter_kernel(rows, idx0, idx1).reshape(SC_SPLIT, n_out, w)


def _collect_rows(planes, idx):
    n, w = planes.shape[1:]
    table = planes.reshape(-1, w)
    idx2 = _plane_index(idx, n)
    m = idx2.shape[1]

    @functools.partial(pl.kernel, out_type=jax.ShapeDtypeStruct((m, w), table.dtype), mesh=_sc_mesh(),
                       scratch_types=[])
    def gather_kernel(x_hbm, i_hbm, o_hbm):
        def body(i_vmem, o_vmem):
            pltpu.sync_copy(x_hbm.at[i_vmem.at[0]], o_vmem)

        pltpu.emit_pipeline(
            body,
            grid=(m // SC_WINDOW,),
            in_specs=[pl.BlockSpec((1, SC_WINDOW), lambda i: (0, i))],
            out_specs=[pl.BlockSpec((SC_WINDOW, w), lambda i: (i, 0))],
            core_axis_name=("core", "subcore"),
            dimension_semantics=(pltpu.PARALLEL,),
        )(i_hbm, o_hbm)

    return gather_kernel(table, idx2).reshape(SC_SPLIT, -1, w)


def _moe_ffn_kernel(te_ref, nt_ref, xs_ref, wgu_ref, wd_ref, ys_ref):
    i = pl.program_id(0)

    @pl.when(i < nt_ref[0])
    def _():
        gu = None
        for k, part in enumerate(_load_planes(xs_ref)):
            term = jnp.dot(part.astype(BF16), wgu_ref[0, k * PLANE_W:(k + 1) * PLANE_W, :],
                           preferred_element_type=F32)
            gu = term if gu is None else gu + term
        gate = gu[:, :D_EXPERT]
        hid = gate * _sigmoid(gate) * gu[:, D_EXPERT:]
        _store_planes(ys_ref, jnp.dot(hid.astype(BF16), wd_ref[0], preferred_element_type=F32))

    @pl.when(i >= nt_ref[0])
    def _():
        ys_ref[...] = jnp.zeros_like(ys_ref)


def _moe_ffn(xs, tile_expert, n_tiles, lw):
    p = xs.shape[1]
    tg = GROUP_TILE
    rows = pl.BlockSpec((SC_SPLIT, tg, PLANE_W), lambda i, te, nt: (0, i, 0))
    grid_spec = pltpu.PrefetchScalarGridSpec(
        num_scalar_prefetch=2,
        grid=(p // tg,),
        in_specs=[rows,
                  pl.BlockSpec((1, D_MODEL, 2 * D_EXPERT), lambda i, te, nt: (te[i], 0, 0)),
                  pl.BlockSpec((1, D_EXPERT, D_MODEL), lambda i, te, nt: (te[i], 0, 0))],
        out_specs=rows,
    )
    return pl.pallas_call(
        _moe_ffn_kernel,
        grid_spec=grid_spec,
        out_shape=jax.ShapeDtypeStruct((SC_SPLIT, p, PLANE_W), I32),
        compiler_params=pltpu.CompilerParams(dimension_semantics=("arbitrary",),
                                             vmem_limit_bytes=VMEM_LIMIT),
        name="moe_ffn",
    )(tile_expert, n_tiles, xs, lw["w_gu"], lw["w_down"])


def _route_plan(route_t, cnt, t):
    tg = GROUP_TILE
    n_tiles_max = (2 * t) // tg + N_EXPERTS
    counts = cnt[0, N_GROUPS:N_GROUPS + N_EXPERTS].astype(I32)
    padded = ((counts + tg - 1) // tg) * tg
    ends = jnp.cumsum(padded)
    offs = ends - padded
    experts = jnp.arange(N_EXPERTS, dtype=I32)[:, None]

    def first_row(e):
        return jnp.sum(jnp.where(e[None, :] == experts, offs[:, None], 0), axis=0)

    fields = route_t.astype(I32)
    pos0 = first_row(fields[0]) + fields[4]
    pos1 = first_row(fields[1]) + fields[5]
    tile_start = jnp.arange(n_tiles_max, dtype=I32) * tg
    n_tiles = ends[-1] // tg
    tile_clamped = jnp.minimum(tile_start, jnp.maximum(n_tiles - 1, 0) * tg)
    tile_expert = jnp.sum((ends[None, :] <= tile_clamped[:, None]).astype(I32), axis=1)
    tile_expert = jnp.minimum(tile_expert, N_EXPERTS - 1)
    return pos0, pos1, tile_expert, n_tiles.reshape(1), n_tiles_max * tg


def _moe(xp, route_t, cnt, lw):
    t = xp.shape[1]
    pos0, pos1, tile_expert, n_tiles, p_rows = _route_plan(route_t, cnt, t)
    xs = _dispatch_rows(xp, pos0, pos1, p_rows)
    ys = _moe_ffn(xs, tile_expert, n_tiles, lw)
    return _collect_rows(ys, jnp.concatenate([pos0, pos1])).reshape(SC_SPLIT, 2, t, PLANE_W)


def _final_kernel(x_ref, y0_ref, y1_ref, rt_ref, g_ref, o_ref):
    x = _combine(x_ref[...], y0_ref, y1_ref, rt_ref[...], lead=(0,))
    o_ref[...] = _rms(x, g_ref[...])


def _final(x2d, y01, route, g):
    t = x2d.shape[0]
    tm = min(TOKEN_TILE, t)
    tok = lambda w: pl.BlockSpec((tm, w), lambda i: (i, 0))
    choice = lambda c: pl.BlockSpec((SC_SPLIT, 1, tm, PLANE_W), lambda i: (0, c, i, 0))
    return pl.pallas_call(
        _final_kernel,
        grid=(t // tm,),
        in_specs=[tok(D_MODEL), choice(0), choice(1), tok(LANES), pl.BlockSpec((1, D_MODEL), lambda i: (0, 0))],
        out_specs=tok(D_MODEL),
        out_shape=jax.ShapeDtypeStruct((t, D_MODEL), F32),
        compiler_params=pltpu.CompilerParams(dimension_semantics=("arbitrary",),
                                             vmem_limit_bytes=VMEM_LIMIT),
        name="final_norm",
    )(x2d, y01, y01, route, g)


def _constants(tm, tk):
    r = jnp.arange(tm)
    tril = (r[None, :] <= r[:, None]).astype(F32)
    tril_strict = (r[None, :] < r[:, None]).astype(BF16)
    rk = jnp.arange(tk)
    u_incl = (rk[:, None] >= rk[None, :]).astype(BF16)
    nh = N_HEADS
    pq = jnp.zeros((3, LANES, nh * HEAD_SLAB), F32)
    pk = jnp.zeros((3, nh * HEAD_DIM, LANES), F32)
    qc = jnp.zeros((1, nh * HEAD_SLAB), F32)
    kc = jnp.zeros((nh * HEAD_DIM, LANES), F32)
    for part in range(3):
        for h in range(nh):
            pq = pq.at[part, h, h * HEAD_SLAB + HEAD_DIM + part].set(1.0)
            qc = qc.at[0, h * HEAD_SLAB + HEAD_DIM + 3 + part].set(1.0)
            kc = kc.at[h * HEAD_DIM + part, :].set(1.0)
            pk = pk.at[part, h * HEAD_DIM + 3 + part, h].set(1.0)
    vc = jnp.zeros((1, HEAD_SLAB), F32).at[0, HEAD_DIM].set(1.0)
    return {"tril": tril, "tril_strict": tril_strict, "u_incl": u_incl, "pq": pq.astype(BF16),
            "pk": pk.astype(BF16), "qc": qc, "kc": kc, "vc": vc}


def _layer_weights(layer, mix_norm_g, w_in, b_forget, conv_a_w, conf_dw_w, conf_dw_b, conf_ln_g, conf_ln_b,
                   w_branch, w_gate, b_gate, w_out, ffn_norm_g, w_router_group, w_router_expert,
                   w_expert_gate, w_expert_up, w_expert_down):
    w = w_in[layer]
    bw = BRANCH_WIDTH
    a_x, a_b, a_c, sb_q, sb_k, sb_v, fx_q, fx_k, fx_v = [w[:, i * bw:(i + 1) * bw] for i in range(9)]
    fx_f = w[:, 9 * bw:9 * bw + N_HEADS]
    conf = w[:, 9 * bw + N_HEADS:]
    scale = HEAD_DIM ** -0.5
    f_pad = jnp.pad(fx_f, ((0, 0), (0, LANES - N_HEADS)))
    w_main = jnp.concatenate([a_x, a_b, a_c, sb_q * scale, fx_q * scale, sb_v, fx_v, conf, f_pad], axis=1)
    w_t = jnp.concatenate([sb_k, fx_k], axis=1).T
    w_router = jnp.concatenate([w_router_group[layer], w_router_expert[layer].reshape(D_MODEL, N_EXPERTS)], axis=1)
    w_router = jnp.pad(w_router, ((0, 0), (0, LANES - N_GROUPS - N_EXPERTS)))
    return {
        "mix_g": mix_norm_g[layer].reshape(1, D_MODEL),
        "w_main": w_main.astype(BF16),
        "w_t": w_t.astype(BF16),
        "bf": jnp.pad(b_forget[layer], (0, LANES - N_HEADS)).reshape(1, LANES),
        "caw": conv_a_w[layer],
        "cdw": jnp.pad(conf_dw_w[layer], ((0, 1), (0, 0))),
        "cdb": conf_dw_b[layer].reshape(1, bw),
        "lng": conf_ln_g[layer].reshape(1, bw),
        "lnb": conf_ln_b[layer].reshape(1, bw),
        "w_gate": w_gate[layer].astype(BF16),
        "b_gate": b_gate[layer].reshape(4, 1, D_MODEL),
        "w_branch": w_branch[layer].astype(BF16),
        "w_out": w_out[layer].astype(BF16),
        "ffn_g": ffn_norm_g[layer].reshape(1, D_MODEL),
        "w_router": w_router,
        "w_gu": jnp.concatenate([w_expert_gate[layer], w_expert_up[layer]], axis=2).astype(BF16),
        "w_down": w_expert_down[layer].astype(BF16),
    }


def kernel(x, mix_norm_g, w_in, b_forget, conv_a_w, conf_dw_w, conf_dw_b, conf_ln_g, conf_ln_b, w_branch, w_gate,
           b_gate, w_out, ffn_norm_g, w_router_group, w_router_expert, w_expert_gate, w_expert_up, w_expert_down,
           final_norm_g):
    b, s, _ = x.shape
    t = b * s
    depth = w_in.shape[0]
    consts = _constants(min(TOKEN_TILE, s), min(ATTN_TILE, s))
    comb = None
    for layer in range(depth):
        lw = _layer_weights(layer, mix_norm_g, w_in, b_forget, conv_a_w, conf_dw_w, conf_dw_b, conf_ln_g,
                            conf_ln_b, w_branch, w_gate, b_gate, w_out, ffn_norm_g, w_router_group,
                            w_router_expert, w_expert_gate, w_expert_up, w_expert_down)
        outs = _mixer_in(x, comb, lw, consts)
        if comb is not None:
            x, *outs = outs
        ya, yc, q, kt, v = outs
        ysb = _attention("sb", q, kt, v, consts)
        yfx = _attention("fox", q, kt, v, consts)
        x2d, xp, route, route_t, cnt = _mixer_out(x.reshape(t, D_MODEL), ya.reshape(t, -1), ysb.reshape(t, -1),
                                                  yfx.reshape(t, -1), yc.reshape(t, -1), lw, consts)
        y01 = _moe(xp, route_t, cnt, lw)
        x = x2d.reshape(b, s, D_MODEL)
        comb = (y01.reshape(SC_SPLIT, 2, b, s, PLANE_W), route.reshape(b, s, LANES))
    out = _final(x.reshape(t, D_MODEL), comb[0].reshape(SC_SPLIT, 2, t, PLANE_W), comb[1].reshape(t, LANES),
                 final_norm_g.reshape(1, D_MODEL))
    return out.reshape(b, s, D_MODEL)
```

```python
import functools

import jax
import jax.numpy as jnp
from jax import lax
from jax.experimental import pallas as pl
from jax.experimental.pallas import tpu as pltpu
from jax.experimental.pallas import tpu_sc as plsc

F32 = jnp.float32
BF16 = jnp.bfloat16
I32 = jnp.int32

D_MODEL = 1024
BRANCH_WIDTH = 256
HEAD_DIM = 64
N_HEADS = 4
HEAD_SLAB = 128
CONV_A_WIDTH = 3
CONF_WIDTH = 31
N_GROUPS = 4
EXPERTS_PER_GROUP = 8
N_EXPERTS = 32
D_EXPERT = 256
EPS = 1e-6
LANES = 128
HALF_D = D_MODEL // 2
SC_SPLIT = 2
PLANE_W = HALF_D // SC_SPLIT

TOKEN_TILE = 512
ATTN_TILE = 256
SB_HEADS_PER_STEP = 4
FOX_HEADS_PER_STEP = 4
ATTN_TRIPS = (4, 2, 1)
SB_GROUP = 4
GROUP_TILE = 256
CONV_ROW_CHUNK = 64
SC_WINDOW = 128
VMEM_LIMIT = 56 * 1024 * 1024

_C_AX, _C_AB, _C_AC = 0, 256, 512
_C_Q = 768
_C_V = 1280
_C_CU, _C_CG = 1792, 2048
_C_F = 2304
_N_MAIN = 2432


def _rms(x, g):
    return x * lax.rsqrt(jnp.mean(x * x, axis=-1, keepdims=True) + EPS) * g


def _softplus(z):
    return jnp.maximum(z, 0.0) + jnp.log1p(jnp.exp(-jnp.abs(z)))


def _sigmoid(z):
    return 1.0 / (1.0 + jnp.exp(-z))


def _split3(v):
    hi = v.astype(BF16)
    r = v - hi.astype(F32)
    mid = r.astype(BF16)
    lo = (r - mid.astype(F32)).astype(BF16)
    return hi, mid, lo


def _pack_rows(v):
    lo = pltpu.bitcast(v[:, :HALF_D].astype(BF16).astype(F32), jnp.uint32)
    hi = pltpu.bitcast(v[:, HALF_D:].astype(BF16).astype(F32), jnp.uint32)
    return pltpu.bitcast((lo >> 16) | hi, I32)


def _unpack_rows(w):
    u = pltpu.bitcast(w, jnp.uint32)
    lo = pltpu.bitcast(u << 16, F32)
    hi = pltpu.bitcast(u & jnp.uint32(0xFFFF0000), F32)
    return lo, hi


def _store_planes(ref, v, lead=()):
    packed = _pack_rows(v)
    for k in range(SC_SPLIT):
        ref[(k, *lead)] = packed[:, k * PLANE_W:(k + 1) * PLANE_W]


def _load_planes(ref, lead=()):
    los, his = zip(*[_unpack_rows(ref[(k, *lead)]) for k in range(SC_SPLIT)])
    return list(los) + list(his)


def _combine(x, y0_ref, y1_ref, rt, lead=()):
    w0 = rt[:, 2:3]
    w1 = rt[:, 3:4]
    parts = [w0 * a + w1 * b for a, b in zip(_load_planes(y0_ref, lead), _load_planes(y1_ref, lead))]
    return x + jnp.concatenate(parts, axis=1)


def _mixer_in_kernel(combine, *refs):
    if combine:
        (x_ref, y0_ref, y1_ref, rt_ref, *refs) = refs
    else:
        (x_ref, *refs) = refs
    (g_ref, wm_ref, wt_ref, bf_ref, caw_ref, cdw_ref, cdb_ref, lng_ref, lnb_ref, tril_ref,
     pq_ref, qc_ref, pk_ref, kc_ref, vc_ref, *refs) = refs
    if combine:
        (xo_ref, *refs) = refs
    (ya_ref, yc_ref, q_ref, kt_ref, v_ref, bufa, bufc, dcarry) = refs

    tm = x_ref.shape[1]
    tk = kt_ref.shape[-1]

    @pl.when(pl.program_id(1) == 0)
    def _():
        bufa[0:8, :] = jnp.zeros((8, BRANCH_WIDTH), F32)
        bufc[0:32, :] = jnp.zeros((32, BRANCH_WIDTH), F32)
        dcarry[...] = jnp.zeros_like(dcarry)

    x = x_ref[0]
    if combine:
        x = _combine(x, y0_ref, y1_ref, rt_ref[0], lead=(0, 0))
        xo_ref[0] = x
    xb = _rms(x, g_ref[...]).astype(BF16)

    p = jnp.dot(xb, wm_ref[...], preferred_element_type=F32)
    pt = lax.dot_general(wt_ref[...], xb, (((1,), (1,)), ((), ())),
                         preferred_element_type=F32)

    ca = p[:, _C_AC:_C_AC + 256] * p[:, _C_AX:_C_AX + 256]
    bufa[8:8 + tm, :] = ca
    caw = caw_ref[...]
    conv = caw[0:1] * bufa[6:6 + tm, :] + caw[1:2] * bufa[7:7 + tm, :] + caw[2:3] * ca
    ya_ref[0] = (p[:, _C_AB:_C_AB + 256] * conv).astype(BF16)
    bufa[0:8, :] = ca[tm - 8:tm]

    u = p[:, _C_CU:_C_CU + 256] * _sigmoid(p[:, _C_CG:_C_CG + 256])
    bufc[32:32 + tm, :] = u
    cdw = cdw_ref[...]
    cdb = cdb_ref[...]
    lng = lng_ref[...]
    lnb = lnb_ref[...]
    for r in range(tm // CONV_ROW_CHUNK):
        base = r * CONV_ROW_CHUNK + 32 - (CONF_WIDTH - 1)
        acc = jnp.broadcast_to(cdb, (CONV_ROW_CHUNK, BRANCH_WIDTH))
        for k in range(CONF_WIDTH):
            acc = acc + cdw[k:k + 1] * bufc[base + k:base + k + CONV_ROW_CHUNK, :]
        mu = jnp.mean(acc, axis=-1, keepdims=True)
        cen = acc - mu
        var = jnp.mean(cen * cen, axis=-1, keepdims=True)
        yn = cen * lax.rsqrt(var + EPS) * lng + lnb
        yc_ref[0, r * CONV_ROW_CHUNK:(r + 1) * CONV_ROW_CHUNK, :] = (yn * _sigmoid(yn)).astype(BF16)
    bufc[0:32, :] = u[tm - 32:tm]

    logf = -_softplus(-(p[:, _C_F:_C_F + LANES] + bf_ref[...]))
    dcum = jnp.dot(tril_ref[...], logf, precision=lax.Precision.HIGHEST,
                   preferred_element_type=F32) + dcarry[0:1, :]
    dcarry[0:1, :] = dcum[tm - 1:tm, :]
    dcum_t = dcum.T
    qh, qm, ql = _split3(dcum)
    kh, km, kl = _split3(-dcum_t)
    q_extra = (jnp.dot(qh, pq_ref[0], preferred_element_type=F32)
               + jnp.dot(qm, pq_ref[1], preferred_element_type=F32)
               + jnp.dot(ql, pq_ref[2], preferred_element_type=F32) + qc_ref[...])
    k_extra = (jnp.dot(pk_ref[0], kh, preferred_element_type=F32)
               + jnp.dot(pk_ref[1], km, preferred_element_type=F32)
               + jnp.dot(pk_ref[2], kl, preferred_element_type=F32)
               + jnp.concatenate([kc_ref[...]] * (tm // LANES), axis=1))

    lane = lax.broadcasted_iota(I32, (tm, HEAD_SLAB), 1)
    low = lane < HEAD_DIM
    vc = vc_ref[...]
    for hd in range(2 * N_HEADS):
        is_fox = hd >= N_HEADS
        pair = (hd // 2) * HEAD_SLAB
        qs = p[:, _C_Q + pair:_C_Q + pair + HEAD_SLAB]
        vs = p[:, _C_V + pair:_C_V + pair + HEAD_SLAB]
        if hd % 2:
            qs = pltpu.roll(qs, HEAD_DIM, axis=1)
            vs = pltpu.roll(vs, HEAD_DIM, axis=1)
        if is_fox:
            hf = hd - N_HEADS
            qx = q_extra[:, hf * HEAD_SLAB:(hf + 1) * HEAD_SLAB]
            kx = k_extra[hf * HEAD_DIM:(hf + 1) * HEAD_DIM, :]
            vx = vc
        else:
            qx = 0.0
            kx = jnp.zeros((HEAD_DIM, tm), F32)
            vx = 0.0
        q_ref[0, hd] = jnp.where(low, qs, qx).astype(BF16)
        v_ref[0, hd] = jnp.where(low, vs, vx).astype(BF16)
        kfull = jnp.concatenate([pt[hd * HEAD_DIM:(hd + 1) * HEAD_DIM, :], kx], axis=0).astype(BF16)
        for c in range(tm // tk):
            kt_ref[0, hd, c] = kfull[:, c * tk:(c + 1) * tk]


def _mixer_in(x, comb, lw, consts):
    b, s, _ = x.shape
    tm = min(TOKEN_TILE, s)
    tk = min(ATTN_TILE, s)
    nk = s // tk
    nh2 = 2 * N_HEADS
    combine = comb is not None

    def full(a):
        return pl.BlockSpec(a.shape, lambda bi, si, _n=a.ndim: (0,) * _n)

    tok = lambda w: pl.BlockSpec((1, tm, w), lambda bi, si: (bi, si, 0))
    in_arrays = [x]
    in_specs = [tok(D_MODEL)]
    if combine:
        y01, route = comb
        in_arrays += [y01, y01, route]
        in_specs += [pl.BlockSpec((SC_SPLIT, 1, 1, tm, PLANE_W), lambda bi, si, _c=c: (0, _c, bi, si, 0))
                     for c in range(2)] + [tok(LANES)]
    weights = [lw["mix_g"], lw["w_main"], lw["w_t"], lw["bf"], lw["caw"], lw["cdw"], lw["cdb"], lw["lng"],
               lw["lnb"], consts["tril"], consts["pq"], consts["qc"], consts["pk"], consts["kc"], consts["vc"]]
    in_arrays += weights
    in_specs += [full(a) for a in weights]

    out_shape = []
    out_specs = []
    if combine:
        out_shape.append(jax.ShapeDtypeStruct((b, s, D_MODEL), F32))
        out_specs.append(tok(D_MODEL))
    out_shape += [
        jax.ShapeDtypeStruct((b, s, BRANCH_WIDTH), BF16),
        jax.ShapeDtypeStruct((b, s, BRANCH_WIDTH), BF16),
        jax.ShapeDtypeStruct((b, nh2, s, HEAD_SLAB), BF16),
        jax.ShapeDtypeStruct((b, nh2, nk, HEAD_SLAB, tk), BF16),
        jax.ShapeDtypeStruct((b, nh2, s, HEAD_SLAB), BF16),
    ]
    out_specs += [
        tok(BRANCH_WIDTH), tok(BRANCH_WIDTH),
        pl.BlockSpec((1, nh2, tm, HEAD_SLAB), lambda bi, si: (bi, 0, si, 0)),
        pl.BlockSpec((1, nh2, tm // tk, HEAD_SLAB, tk), lambda bi, si: (bi, 0, si, 0, 0)),
        pl.BlockSpec((1, nh2, tm, HEAD_SLAB), lambda bi, si: (bi, 0, si, 0)),
    ]
    return pl.pallas_call(
        functools.partial(_mixer_in_kernel, combine),
        grid=(b, s // tm),
        in_specs=in_specs,
        out_specs=out_specs,
        out_shape=out_shape,
        scratch_shapes=[pltpu.VMEM((8 + tm, BRANCH_WIDTH), F32),
                        pltpu.VMEM((32 + tm, BRANCH_WIDTH), F32),
                        pltpu.VMEM((8, LANES), F32)],
        compiler_params=pltpu.CompilerParams(dimension_semantics=("arbitrary", "arbitrary"),
                                             vmem_limit_bytes=VMEM_LIMIT),
        name="mixer_in",
    )(*in_arrays)


def _pair_out(accs):
    lane = lax.broadcasted_iota(I32, accs[0].shape, 1)
    return jnp.where(lane < HEAD_DIM, accs[0], pltpu.roll(accs[1], HEAD_DIM, axis=1))


def _sb_attn_kernel(q_ref, kt_ref, v_ref, u_ref, o_ref):
    tq = q_ref.shape[2]
    nk, tk = kt_ref.shape[2], kt_ref.shape[-1]
    i = pl.program_id(2)
    umat = u_ref[...]
    nh = q_ref.shape[1]
    nb = SB_GROUP
    qs = [q_ref[0, h] for h in range(nh)]
    row = lax.broadcasted_iota(I32, (tq, tk), 0)
    col = lax.broadcasted_iota(I32, (tq, tk), 1)

    def weights(g, tails, masked):
        ws, new_tails = [], []
        for h in range(nh):
            tail = tails[h]
            wh = []
            for u in reversed(range(nb)):
                jr = g * nb + u
                z = jnp.dot(qs[h], kt_ref[0, h, jnp.minimum(jr, nk - 1)], preferred_element_type=F32)
                sp = jnp.maximum(z, 0.0) + jnp.log(1.0 + jnp.exp(-jnp.abs(z)))
                if masked:
                    mask = col + (jr - i) * tk < row
                    sp = jnp.where(mask, sp, 0.0)
                later = jnp.dot(sp.astype(BF16), umat, preferred_element_type=F32)
                w = jnp.exp((z - later - tail).astype(BF16))
                if masked:
                    w = jnp.where(mask, w, jnp.zeros_like(w))
                wh.append(w)
                tail = tail + later[:, 0:1]
            ws.append(tuple(reversed(wh)))
            new_tails.append(tail)
        return tuple(ws), tuple(new_tails)

    def apply(g, accs, ws):
        out = []
        for h in range(nh):
            acc = accs[h]
            for u in range(nb):
                start = pl.multiple_of(jnp.minimum(g * nb + u, nk - 1) * tk, tk)
                acc = acc + jnp.dot(ws[h][u], v_ref[0, h, pl.ds(start, tk), :], preferred_element_type=F32)
            out.append(acc)
        return tuple(out)

    last = i // nb
    zero_t = tuple(jnp.zeros((tq, 1), F32) for _ in range(nh))
    accs = tuple(jnp.zeros((tq, HEAD_SLAB), F32) for _ in range(nh))
    ws, tails = weights(last, zero_t, True)

    def body(t, carry):
        accs, tails, ws = carry
        g = last - 1 - t
        accs = apply(g + 1, accs, ws)
        ws, tails = weights(g, tails, False)
        return accs, tails, ws

    accs, tails, ws = lax.fori_loop(0, last, body, (accs, tails, ws))
    accs = apply(0, accs, ws)
    o_ref[0] = jnp.concatenate([_pair_out(accs[k:k + 2]) for k in range(0, nh, 2)], axis=1).astype(BF16)


def _fox_attn_kernel(q_ref, kt_ref, v_ref, o_ref):
    tq = q_ref.shape[2]
    tk = kt_ref.shape[-1]
    i = pl.program_id(2)
    row = lax.broadcasted_iota(I32, (tq, tk), 0)
    col = lax.broadcasted_iota(I32, (tq, tk), 1)
    causal = col <= row
    nh = q_ref.shape[1]
    qs = [q_ref[0, h] for h in range(nh)]

    def vblock(h, j):
        return v_ref[0, h, pl.ds(pl.multiple_of(j * tk, tk), tk), :]

    accs, maxes = [], []
    for h in range(nh):
        s = jnp.where(causal, jnp.dot(qs[h], kt_ref[0, h, i], preferred_element_type=F32), -jnp.inf)
        m = jnp.max(s, axis=-1, keepdims=True)
        accs.append(jnp.dot(jnp.exp(s - m).astype(BF16), vblock(h, i), preferred_element_type=F32))
        maxes.append(m)

    def make_body(n_blocks, first):
        def body(jj, carry):
            accs, maxes = carry
            new_accs, new_maxes = [], []
            for h in range(nh):
                js = [first + n_blocks * jj + u for u in range(n_blocks)]
                ss = [jnp.dot(qs[h], kt_ref[0, h, j], preferred_element_type=F32) for j in js]
                m = maxes[h]
                for s in ss:
                    m = jnp.maximum(m, jnp.max(s, axis=-1, keepdims=True))
                acc = jnp.exp(maxes[h] - m) * accs[h]
                for s, j in zip(ss, js):
                    acc = acc + jnp.dot(jnp.exp(s - m).astype(BF16), vblock(h, j), preferred_element_type=F32)
                new_accs.append(acc)
                new_maxes.append(m)
            return tuple(new_accs), tuple(new_maxes)
        return body

    carry = (tuple(accs), tuple(maxes))
    done = 0
    for n_blocks in ATTN_TRIPS:
        trips = (i - done) // n_blocks
        carry = lax.fori_loop(0, trips, make_body(n_blocks, done), carry)
        done = done + trips * n_blocks
    accs = carry[0]
    outs = [a / a[:, HEAD_DIM:HEAD_DIM + 1] for a in accs]
    o_ref[0] = jnp.concatenate([_pair_out(outs[k:k + 2]) for k in range(0, nh, 2)], axis=1).astype(BF16)


def _attention(kind, q, kt, v, consts):
    b, _, s, _ = q.shape
    nk, tk = kt.shape[2], kt.shape[4]
    tq = tk
    hps = SB_HEADS_PER_STEP if kind == "sb" else FOX_HEADS_PER_STEP
    head0 = 0 if kind == "sb" else N_HEADS // hps
    in_specs = [
        pl.BlockSpec((1, hps, tq, HEAD_SLAB), lambda bi, hp, i: (bi, hp + head0, i, 0)),
        pl.BlockSpec((1, hps, nk, HEAD_SLAB, tk), lambda bi, hp, i: (bi, hp + head0, 0, 0, 0)),
        pl.BlockSpec((1, hps, s, HEAD_SLAB), lambda bi, hp, i: (bi, hp + head0, 0, 0)),
    ]
    args = [q, kt, v]
    if kind == "sb":
        in_specs.append(pl.BlockSpec((tk, tk), lambda bi, hp, i: (0, 0)))
        args.append(consts["u_incl"])
        body = _sb_attn_kernel
    else:
        body = _fox_attn_kernel
    return pl.pallas_call(
        body,
        grid=(b, N_HEADS // hps, s // tq),
        in_specs=in_specs,
        out_specs=pl.BlockSpec((1, tq, hps * HEAD_DIM), lambda bi, hp, i: (bi, i, hp)),
        out_shape=jax.ShapeDtypeStruct((b, s, BRANCH_WIDTH), BF16),
        compiler_params=pltpu.CompilerParams(dimension_semantics=("arbitrary", "arbitrary", "arbitrary"),
                                             vmem_limit_bytes=VMEM_LIMIT),
        name=kind + "_attn",
    )(*args)


def _mixer_out_kernel(x_ref, ya_ref, ysb_ref, yfx_ref, yc_ref, g_ref, wg_ref, bg_ref, wb_ref, wo_ref,
                      fg_ref, wr_ref, trs_ref, xo_ref, xp_ref, rt_ref, rtt_ref, cnt_ref, carry):
    tm = x_ref.shape[0]

    @pl.when(pl.program_id(0) == 0)
    def _():
        carry[...] = jnp.zeros_like(carry)

    x = x_ref[...]
    xb = _rms(x, g_ref[...]).astype(BF16)
    h = None
    for g, y_ref in enumerate((ya_ref, ysb_ref, yfx_ref, yc_ref)):
        gate = _sigmoid(jnp.dot(xb, wg_ref[g], preferred_element_type=F32) + bg_ref[g])
        term = gate * jnp.dot(y_ref[...], wb_ref[g], preferred_element_type=F32)
        h = term if h is None else h + term
    xo = x + jnp.dot(h.astype(BF16), wo_ref[...], preferred_element_type=F32)
    xo_ref[...] = xo
    xn = _rms(xo, fg_ref[...])
    _store_planes(xp_ref, xn)

    xh = xn.astype(BF16)
    xl = (xn - xh.astype(F32)).astype(BF16)
    logits = (jnp.dot(xh, wr_ref[0], preferred_element_type=F32) + jnp.dot(xl, wr_ref[0], preferred_element_type=F32)
              + jnp.dot(xh, wr_ref[1], preferred_element_type=F32))
    lane = lax.broadcasted_iota(I32, (tm, LANES), 1).astype(F32)
    ninf = -jnp.inf
    big = float(LANES)
    gl = jnp.where(lane < N_GROUPS, logits, ninf)
    gmax = jnp.max(gl, axis=-1, keepdims=True)
    gidx = jnp.min(jnp.where(gl == gmax, lane, big), axis=-1, keepdims=True)
    p_group = 1.0 / jnp.sum(jnp.exp(gl - gmax), axis=-1, keepdims=True)
    first = N_GROUPS + EXPERTS_PER_GROUP * gidx
    el = jnp.where((lane >= first) & (lane < first + EXPERTS_PER_GROUP), logits, ninf)
    m1 = jnp.max(el, axis=-1, keepdims=True)
    i1 = jnp.min(jnp.where(el == m1, lane, big), axis=-1, keepdims=True)
    el2 = jnp.where(lane == i1, ninf, el)
    m2 = jnp.max(el2, axis=-1, keepdims=True)
    i2 = jnp.min(jnp.where(el2 == m2, lane, big), axis=-1, keepdims=True)
    e2 = jnp.exp(m2 - m1)
    w1 = p_group / (1.0 + e2)
    w2 = w1 * e2

    sel1 = lane == i1
    sel2 = lane == i2
    onehot = jnp.where(sel1, 1.0, jnp.where(sel2, 1.0, 0.0))
    before = jnp.dot(trs_ref[...], onehot.astype(BF16), preferred_element_type=F32) + carry[0:1, :]
    r1 = jnp.sum(jnp.where(sel1, before, 0.0), axis=-1, keepdims=True)
    r2 = jnp.sum(jnp.where(sel2, before, 0.0), axis=-1, keepdims=True)
    total = before[tm - 1:tm, :] + onehot[tm - 1:tm, :]
    carry[0:1, :] = total
    cnt_ref[...] = jnp.broadcast_to(total, cnt_ref.shape)

    rt = jnp.where(lane == 0, i1 - N_GROUPS, 0.0)
    rt = jnp.where(lane == 1, i2 - N_GROUPS, rt)
    rt = jnp.where(lane == 2, w1, rt)
    rt = jnp.where(lane == 3, w2, rt)
    rt = jnp.where(lane == 4, r1, rt)
    rt = jnp.where(lane == 5, r2, rt)
    rt_ref[...] = rt
    rtt_ref[...] = rt.T[0:8, :]


def _mixer_out(x2d, ya, ysb, yfx, yc, lw, consts):
    t = x2d.shape[0]
    tm = min(TOKEN_TILE, t)

    def full(a):
        return pl.BlockSpec(a.shape, lambda i, _n=a.ndim: (0,) * _n)

    tok = lambda w: pl.BlockSpec((tm, w), lambda i: (i, 0))
    weights = [lw["mix_g"], lw["w_gate"], lw["b_gate"], lw["w_branch"], lw["w_out"], lw["ffn_g"], lw["w_router"],
               consts["tril_strict"]]
    return pl.pallas_call(
        _mixer_out_kernel,
        grid=(t // tm,),
        in_specs=[tok(D_MODEL)] + [tok(BRANCH_WIDTH)] * 4 + [full(a) for a in weights],
        out_specs=[tok(D_MODEL), pl.BlockSpec((SC_SPLIT, tm, PLANE_W), lambda i: (0, i, 0)), tok(LANES),
                   pl.BlockSpec((8, tm), lambda i: (0, i)), pl.BlockSpec((8, LANES), lambda i: (0, 0))],
        out_shape=[jax.ShapeDtypeStruct((t, D_MODEL), F32),
                   jax.ShapeDtypeStruct((SC_SPLIT, t, PLANE_W), I32),
                   jax.ShapeDtypeStruct((t, LANES), F32),
                   jax.ShapeDtypeStruct((8, t), F32),
                   jax.ShapeDtypeStruct((8, LANES), F32)],
        scratch_shapes=[pltpu.VMEM((8, LANES), F32)],
        compiler_params=pltpu.CompilerParams(dimension_semantics=("arbitrary",),
                                             vmem_limit_bytes=VMEM_LIMIT),
        name="mixer_out",
    )(x2d, ya, ysb, yfx, yc, *weights)


def _sc_mesh():
    return plsc.VectorSubcoreMesh(core_axis_name="core", subcore_axis_name="subcore")


def _plane_index(idx, rows_per_plane):
    return jnp.concatenate([idx + k * rows_per_plane for k in range(SC_SPLIT)]).reshape(1, -1)


def _dispatch_rows(planes, pos0, pos1, n_out):
    w = planes.shape[2]
    rows = planes.reshape(-1, w)
    t = rows.shape[0]
    idx0 = _plane_index(pos0, n_out)
    idx1 = _plane_index(pos1, n_out)

    @functools.partial(pl.kernel, out_type=jax.ShapeDtypeStruct((SC_SPLIT * n_out, w), rows.dtype),
                       mesh=_sc_mesh(), scratch_types=[])
    def scatter_kernel(x_hbm, i0_hbm, i1_hbm, o_hbm):
        def body(x_vmem, i0_vmem, i1_vmem):
            pltpu.sync_copy(x_vmem, o_hbm.at[i0_vmem.at[0]])
            pltpu.sync_copy(x_vmem, o_hbm.at[i1_vmem.at[0]])

        pltpu.emit_pipeline(
            body,
            grid=(t // SC_WINDOW,),
            in_specs=[pl.BlockSpec((SC_WINDOW, w), lambda i: (i, 0)),
                      pl.BlockSpec((1, SC_WINDOW), lambda i: (0, i)),
                      pl.BlockSpec((1, SC_WINDOW), lambda i: (0, i))],
            out_specs=[],
            core_axis_name=("core", "subcore"),
            dimension_semantics=(pltpu.PARALLEL,),
        )(x_hbm, i0_hbm, i1_hbm)

    return scatter_kernel(rows, idx0, idx1).reshape(SC_SPLIT, n_out, w)


def _collect_rows(planes, idx):
    n, w = planes.shape[1:]
    table = planes.reshape(-1, w)
    idx2 = _plane_index(idx, n)
    m = idx2.shape[1]

    @functools.partial(pl.kernel, out_type=jax.ShapeDtypeStruct((m, w), table.dtype), mesh=_sc_mesh(),
                       scratch_types=[])
    def gather_kernel(x_hbm, i_hbm, o_hbm):
        def body(i_vmem, o_vmem):
            pltpu.sync_copy(x_hbm.at[i_vmem.at[0]], o_vmem)

        pltpu.emit_pipeline(
            body,
            grid=(m // SC_WINDOW,),
            in_specs=[pl.BlockSpec((1, SC_WINDOW), lambda i: (0, i))],
            out_specs=[pl.BlockSpec((SC_WINDOW, w), lambda i: (i, 0))],
            core_axis_name=("core", "subcore"),
            dimension_semantics=(pltpu.PARALLEL,),
        )(i_hbm, o_hbm)

    return gather_kernel(table, idx2).reshape(SC_SPLIT, -1, w)


def _moe_ffn_kernel(te_ref, nt_ref, xs_ref, wgu_ref, wd_ref, ys_ref):
    i = pl.program_id(0)

    @pl.when(i < nt_ref[0])
    def _():
        gu = None
        for k, part in enumerate(_load_planes(xs_ref)):
            term = jnp.dot(part.astype(BF16), wgu_ref[0, k * PLANE_W:(k + 1) * PLANE_W, :],
                           preferred_element_type=F32)
            gu = term if gu is None else gu + term
        gate = gu[:, :D_EXPERT]
        hid = gate * _sigmoid(gate) * gu[:, D_EXPERT:]
        _store_planes(ys_ref, jnp.dot(hid.astype(BF16), wd_ref[0], preferred_element_type=F32))

    @pl.when(i >= nt_ref[0])
    def _():
        ys_ref[...] = jnp.zeros_like(ys_ref)


def _moe_ffn(xs, tile_expert, n_tiles, lw):
    p = xs.shape[1]
    tg = GROUP_TILE
    rows = pl.BlockSpec((SC_SPLIT, tg, PLANE_W), lambda i, te, nt: (0, i, 0))
    grid_spec = pltpu.PrefetchScalarGridSpec(
        num_scalar_prefetch=2,
        grid=(p // tg,),
        in_specs=[rows,
                  pl.BlockSpec((1, D_MODEL, 2 * D_EXPERT), lambda i, te, nt: (te[i], 0, 0)),
                  pl.BlockSpec((1, D_EXPERT, D_MODEL), lambda i, te, nt: (te[i], 0, 0))],
        out_specs=rows,
    )
    return pl.pallas_call(
        _moe_ffn_kernel,
        grid_spec=grid_spec,
        out_shape=jax.ShapeDtypeStruct((SC_SPLIT, p, PLANE_W), I32),
        compiler_params=pltpu.CompilerParams(dimension_semantics=("arbitrary",),
                                             vmem_limit_bytes=VMEM_LIMIT),
        name="moe_ffn",
    )(tile_expert, n_tiles, xs, lw["w_gu"], lw["w_down"])


def _route_plan(route_t, cnt, t):
    tg = GROUP_TILE
    n_tiles_max = (2 * t) // tg + N_EXPERTS
    counts = cnt[0, N_GROUPS:N_GROUPS + N_EXPERTS].astype(I32)
    padded = ((counts + tg - 1) // tg) * tg
    ends = jnp.cumsum(padded)
    offs = ends - padded
    experts = jnp.arange(N_EXPERTS, dtype=I32)[:, None]

    def first_row(e):
        return jnp.sum(jnp.where(e[None, :] == experts, offs[:, None], 0), axis=0)

    fields = route_t.astype(I32)
    pos0 = first_row(fields[0]) + fields[4]
    pos1 = first_row(fields[1]) + fields[5]
    tile_start = jnp.arange(n_tiles_max, dtype=I32) * tg
    n_tiles = ends[-1] // tg
    tile_clamped = jnp.minimum(tile_start, jnp.maximum(n_tiles - 1, 0) * tg)
    tile_expert = jnp.sum((ends[None, :] <= tile_clamped[:, None]).astype(I32), axis=1)
    tile_expert = jnp.minimum(tile_expert, N_EXPERTS - 1)
    return pos0, pos1, tile_expert, n_tiles.reshape(1), n_tiles_max * tg


def _moe(xp, route_t, cnt, lw):
    t = xp.shape[1]
    pos0, pos1, tile_expert, n_tiles, p_rows = _route_plan(route_t, cnt, t)
    xs = _dispatch_rows(xp, pos0, pos1, p_rows)
    ys = _moe_ffn(xs, tile_expert, n_tiles, lw)
    return _collect_rows(ys, jnp.concatenate([pos0, pos1])).reshape(SC_SPLIT, 2, t, PLANE_W)


def _final_kernel(x_ref, y0_ref, y1_ref, rt_ref, g_ref, o_ref):
    x = _combine(x_ref[...], y0_ref, y1_ref, rt_ref[...], lead=(0,))
    o_ref[...] = _rms(x, g_ref[...])


def _final(x2d, y01, route, g):
    t = x2d.shape[0]
    tm = min(TOKEN_TILE, t)
    tok = lambda w: pl.BlockSpec((tm, w), lambda i: (i, 0))
    choice = lambda c: pl.BlockSpec((SC_SPLIT, 1, tm, PLANE_W), lambda i: (0, c, i, 0))
    return pl.pallas_call(
        _final_kernel,
        grid=(t // tm,),
        in_specs=[tok(D_MODEL), choice(0), choice(1), tok(LANES), pl.BlockSpec((1, D_MODEL), lambda i: (0, 0))],
        out_specs=tok(D_MODEL),
        out_shape=jax.ShapeDtypeStruct((t, D_MODEL), F32),
        compiler_params=pltpu.CompilerParams(dimension_semantics=("arbitrary",),
                                             vmem_limit_bytes=VMEM_LIMIT),
        name="final_norm",
    )(x2d, y01, y01, route, g)


def _constants(tm, tk):
    r = jnp.arange(tm)
    tril = (r[None, :] <= r[:, None]).astype(F32)
    tril_strict = (r[None, :] < r[:, None]).astype(BF16)
    rk = jnp.arange(tk)
    u_incl = (rk[:, None] >= rk[None, :]).astype(BF16)
    nh = N_HEADS
    pq = jnp.zeros((3, LANES, nh * HEAD_SLAB), F32)
    pk = jnp.zeros((3, nh * HEAD_DIM, LANES), F32)
    qc = jnp.zeros((1, nh * HEAD_SLAB), F32)
    kc = jnp.zeros((nh * HEAD_DIM, LANES), F32)
    for part in range(3):
        for h in range(nh):
            pq = pq.at[part, h, h * HEAD_SLAB + HEAD_DIM + part].set(1.0)
            qc = qc.at[0, h * HEAD_SLAB + HEAD_DIM + 3 + part].set(1.0)
            kc = kc.at[h * HEAD_DIM + part, :].set(1.0)
            pk = pk.at[part, h * HEAD_DIM + 3 + part, h].set(1.0)
    vc = jnp.zeros((1, HEAD_SLAB), F32).at[0, HEAD_DIM].set(1.0)
    return {"tril": tril, "tril_strict": tril_strict, "u_incl": u_incl, "pq": pq.astype(BF16),
            "pk": pk.astype(BF16), "qc": qc, "kc": kc, "vc": vc}


def _layer_weights(layer, mix_norm_g, w_in, b_forget, conv_a_w, conf_dw_w, conf_dw_b, conf_ln_g, conf_ln_b,
                   w_branch, w_gate, b_gate, w_out, ffn_norm_g, w_router_group, w_router_expert,
                   w_expert_gate, w_expert_up, w_expert_down):
    w = w_in[layer]
    bw = BRANCH_WIDTH
    a_x, a_b, a_c, sb_q, sb_k, sb_v, fx_q, fx_k, fx_v = [w[:, i * bw:(i + 1) * bw] for i in range(9)]
    fx_f = w[:, 9 * bw:9 * bw + N_HEADS]
    conf = w[:, 9 * bw + N_HEADS:]
    scale = HEAD_DIM ** -0.5
    f_pad = jnp.pad(fx_f, ((0, 0), (0, LANES - N_HEADS)))
    w_main = jnp.concatenate([a_x, a_b, a_c, sb_q * scale, fx_q * scale, sb_v, fx_v, conf, f_pad], axis=1)
    w_t = jnp.concatenate([sb_k, fx_k], axis=1).T
    w_router = jnp.concatenate([w_router_group[layer], w_router_expert[layer].reshape(D_MODEL, N_EXPERTS)], axis=1)
    w_router = jnp.pad(w_router, ((0, 0), (0, LANES - N_GROUPS - N_EXPERTS)))
    return {
        "mix_g": mix_norm_g[layer].reshape(1, D_MODEL),
        "w_main": w_main.astype(BF16),
        "w_t": w_t.astype(BF16),
        "bf": jnp.pad(b_forget[layer], (0, LANES - N_HEADS)).reshape(1, LANES),
        "caw": conv_a_w[layer],
        "cdw": jnp.pad(conf_dw_w[layer], ((0, 1), (0, 0))),
        "cdb": conf_dw_b[layer].reshape(1, bw),
        "lng": conf_ln_g[layer].reshape(1, bw),
        "lnb": conf_ln_b[layer].reshape(1, bw),
        "w_gate": w_gate[layer].astype(BF16),
        "b_gate": b_gate[layer].reshape(4, 1, D_MODEL),
        "w_branch": w_branch[layer].astype(BF16),
        "w_out": w_out[layer].astype(BF16),
        "ffn_g": ffn_norm_g[layer].reshape(1, D_MODEL),
        "w_router": jnp.stack([w_router.astype(BF16), (w_router - w_router.astype(BF16).astype(F32)).astype(BF16)]),
        "w_gu": jnp.concatenate([w_expert_gate[layer], w_expert_up[layer]], axis=2).astype(BF16),
        "w_down": w_expert_down[layer].astype(BF16),
    }


def kernel(x, mix_norm_g, w_in, b_forget, conv_a_w, conf_dw_w, conf_dw_b, conf_ln_g, conf_ln_b, w_branch, w_gate,
           b_gate, w_out, ffn_norm_g, w_router_group, w_router_expert, w_expert_gate, w_expert_up, w_expert_down,
           final_norm_g):
    b, s, _ = x.shape
    t = b * s
    depth = w_in.shape[0]
    consts = _constants(min(TOKEN_TILE, s), min(ATTN_TILE, s))
    comb = None
    for layer in range(depth):
        lw = _layer_weights(layer, mix_norm_g, w_in, b_forget, conv_a_w, conf_dw_w, conf_dw_b, conf_ln_g,
                            conf_ln_b, w_branch, w_gate, b_gate, w_out, ffn_norm_g, w_router_group,
                            w_router_expert, w_expert_gate, w_expert_up, w_expert_down)
        outs = _mixer_in(x, comb, lw, consts)
        if comb is not None:
            x, *outs = outs
        ya, yc, q, kt, v = outs
        ysb = _attention("sb", q, kt, v, consts)
        yfx = _attention("fox", q, kt, v, consts)
        x2d, xp, route, route_t, cnt = _mixer_out(x.reshape(t, D_MODEL), ya.reshape(t, -1), ysb.reshape(t, -1),
                                                  yfx.reshape(t, -1), yc.reshape(t, -1), lw, consts)
        y01 = _moe(xp, route_t, cnt, lw)
        x = x2d.reshape(b, s, D_MODEL)
        comb = (y01.reshape(SC_SPLIT, 2, b, s, PLANE_W), route.reshape(b, s, LANES))
    out = _final(x.reshape(t, D_MODEL), comb[0].reshape(SC_SPLIT, 2, t, PLANE_W), comb[1].reshape(t, LANES),
                 final_norm_g.reshape(1, D_MODEL))
    return out.reshape(b, s, D_MODEL)
```

```python
import functools

import jax
import jax.numpy as jnp
from jax import lax
from jax.experimental import pallas as pl
from jax.experimental.pallas import tpu as pltpu
from jax.experimental.pallas import tpu_sc as plsc

F32 = jnp.float32
BF16 = jnp.bfloat16
I32 = jnp.int32

D_MODEL = 1024
BRANCH_WIDTH = 256
HEAD_DIM = 64
N_HEADS = 4
HEAD_SLAB = 128
CONV_A_WIDTH = 3
CONF_WIDTH = 31
N_GROUPS = 4
EXPERTS_PER_GROUP = 8
N_EXPERTS = 32
D_EXPERT = 256
EPS = 1e-6
LANES = 128
ROUTE_ROWS = 48
HALF_D = D_MODEL // 2
SC_SPLIT = 2
PLANE_W = HALF_D // SC_SPLIT

TOKEN_TILE = 512
ATTN_TILE = 256
SB_HEADS_PER_STEP = 4
FOX_HEADS_PER_STEP = 4
ATTN_TRIPS = (4, 2, 1)
SB_GROUP = 4
GROUP_TILE = 512
CONV_ROW_CHUNK = 64
SC_WINDOW = 128
VMEM_LIMIT = 56 * 1024 * 1024

_C_CU, _C_CG = 0, 256
_C_F = 512
_N_FIRST = 640
_C_Q = 640
_C_V = 1152
_C_AX, _C_AB, _C_AC = 1664, 1920, 2176
_N_MAIN = 2432


def _rms(x, g):
    return x * lax.rsqrt(jnp.mean(x * x, axis=-1, keepdims=True) + EPS) * g


def _softplus(z):
    return jnp.maximum(z, 0.0) + jnp.log1p(jnp.exp(-jnp.abs(z)))


def _sigmoid(z):
    return 1.0 / (1.0 + jnp.exp(-z))


def _split3(v):
    hi = v.astype(BF16)
    r = v - hi.astype(F32)
    mid = r.astype(BF16)
    lo = (r - mid.astype(F32)).astype(BF16)
    return hi, mid, lo


def _pack_rows(v):
    lo = pltpu.bitcast(v[:, :HALF_D].astype(BF16).astype(F32), jnp.uint32)
    hi = pltpu.bitcast(v[:, HALF_D:].astype(BF16).astype(F32), jnp.uint32)
    return pltpu.bitcast((lo >> 16) | hi, I32)


def _unpack_rows(w):
    u = pltpu.bitcast(w, jnp.uint32)
    lo = pltpu.bitcast(u << 16, F32)
    hi = pltpu.bitcast(u & jnp.uint32(0xFFFF0000), F32)
    return lo, hi


def _store_planes(ref, v, lead=()):
    packed = _pack_rows(v)
    for k in range(SC_SPLIT):
        ref[(k, *lead)] = packed[:, k * PLANE_W:(k + 1) * PLANE_W]


def _load_planes(ref, lead=()):
    los, his = zip(*[_unpack_rows(ref[(k, *lead)]) for k in range(SC_SPLIT)])
    return list(los) + list(his)


def _combine(x, y0_ref, y1_ref, rt, lead=()):
    w0 = rt[:, 2:3]
    w1 = rt[:, 3:4]
    parts = [w0 * a + w1 * b for a, b in zip(_load_planes(y0_ref, lead), _load_planes(y1_ref, lead))]
    return x + jnp.concatenate(parts, axis=1)


def _mixer_in_kernel(combine, *refs):
    if combine:
        (x_ref, y0_ref, y1_ref, rt_ref, *refs) = refs
    else:
        (x_ref, *refs) = refs
    (g_ref, wm_ref, wt_ref, bf_ref, caw_ref, cdw_ref, cdb_ref, lng_ref, lnb_ref, tril_ref,
     pq_ref, qc_ref, pk_ref, kc_ref, vc_ref, *refs) = refs
    if combine:
        (xo_ref, *refs) = refs
    (ya_ref, yc_ref, q_ref, kt_ref, v_ref, bufa, bufc, dcarry) = refs

    tm = x_ref.shape[1]
    tk = kt_ref.shape[-1]

    @pl.when(pl.program_id(1) == 0)
    def _():
        bufa[0:8, :] = jnp.zeros((8, BRANCH_WIDTH), F32)
        bufc[0, 0:32, :] = jnp.zeros((32, BRANCH_WIDTH), F32)
        dcarry[...] = jnp.zeros_like(dcarry)

    x = x_ref[0]
    if combine:
        x = _combine(x, y0_ref, y1_ref, rt_ref[0], lead=(0, 0))
        xo_ref[0] = x
    xb = _rms(x, g_ref[...]).astype(BF16)

    p = jnp.concatenate([jnp.dot(xb, wm_ref[:, :_N_FIRST], preferred_element_type=F32),
                         jnp.dot(xb, wm_ref[:, _N_FIRST:], preferred_element_type=F32)], axis=1)
    pt = lax.dot_general(wt_ref[...], xb, (((1,), (1,)), ((), ())),
                         preferred_element_type=F32)

    ca = p[:, _C_AC:_C_AC + 256] * p[:, _C_AX:_C_AX + 256]
    bufa[8:8 + tm, :] = ca
    caw = caw_ref[...]
    conv = caw[0:1] * bufa[6:6 + tm, :] + caw[1:2] * bufa[7:7 + tm, :] + caw[2:3] * ca
    ya_ref[0] = (p[:, _C_AB:_C_AB + 256] * conv).astype(BF16)
    bufa[0:8, :] = ca[tm - 8:tm]

    u = p[:, _C_CU:_C_CU + 256] * _sigmoid(p[:, _C_CG:_C_CG + 256])
    bufc[0, 32:32 + tm, :] = u
    for r in range(1, 8):
        bufc[r, 0:tm + 24, :] = bufc[0, r:r + tm + 24, :]
    cdw = cdw_ref[...]
    cdb = cdb_ref[...]
    lng = lng_ref[...]
    lnb = lnb_ref[...]
    for c in range(tm // CONV_ROW_CHUNK):
        row0 = c * CONV_ROW_CHUNK
        acc = jnp.broadcast_to(cdb, (CONV_ROW_CHUNK, BRANCH_WIDTH))
        for k in range(CONF_WIDTH):
            off = 32 - (CONF_WIDTH - 1) + k
            acc = acc + cdw[k:k + 1] * bufc[off % 8, row0 + off - off % 8:row0 + off - off % 8 + CONV_ROW_CHUNK, :]
        mu = jnp.mean(acc, axis=-1, keepdims=True)
        cen = acc - mu
        var = jnp.mean(cen * cen, axis=-1, keepdims=True)
        yn = cen * lax.rsqrt(var + EPS) * lng + lnb
        yc_ref[0, row0:row0 + CONV_ROW_CHUNK, :] = (yn * _sigmoid(yn)).astype(BF16)
    bufc[0, 0:32, :] = u[tm - 32:tm]

    logf = -_softplus(-(p[:, _C_F:_C_F + LANES] + bf_ref[...]))
    tril = tril_ref[...]
    dcum = dcarry[0:1, :]
    for part in _split3(logf):
        dcum = dcum + jnp.dot(tril, part, preferred_element_type=F32)
    dcarry[0:1, :] = dcum[tm - 1:tm, :]
    dcum_t = dcum.T
    qh, qm, ql = _split3(dcum)
    kh, km, kl = _split3(-dcum_t)
    q_extra = (jnp.dot(qh, pq_ref[0], preferred_element_type=F32)
               + jnp.dot(qm, pq_ref[1], preferred_element_type=F32)
               + jnp.dot(ql, pq_ref[2], preferred_element_type=F32) + qc_ref[...])
    k_extra = (jnp.dot(pk_ref[0], kh, preferred_element_type=F32)
               + jnp.dot(pk_ref[1], km, preferred_element_type=F32)
               + jnp.dot(pk_ref[2], kl, preferred_element_type=F32)
               + jnp.concatenate([kc_ref[...]] * (tm // LANES), axis=1))

    lane = lax.broadcasted_iota(I32, (tm, HEAD_SLAB), 1)
    low = lane < HEAD_DIM
    vc = vc_ref[...]
    for hd in range(2 * N_HEADS):
        is_fox = hd >= N_HEADS
        pair = (hd // 2) * HEAD_SLAB
        qs = p[:, _C_Q + pair:_C_Q + pair + HEAD_SLAB]
        vs = p[:, _C_V + pair:_C_V + pair + HEAD_SLAB]
        if hd % 2:
            qs = pltpu.roll(qs, HEAD_DIM, axis=1)
            vs = pltpu.roll(vs, HEAD_DIM, axis=1)
        if is_fox:
            hf = hd - N_HEADS
            qx = q_extra[:, hf * HEAD_SLAB:(hf + 1) * HEAD_SLAB]
            kx = k_extra[hf * HEAD_DIM:(hf + 1) * HEAD_DIM, :]
            vx = vc
        else:
            qx = 0.0
            kx = jnp.zeros((HEAD_DIM, tm), F32)
            vx = 0.0
        q_ref[0, hd] = jnp.where(low, qs, qx).astype(BF16)
        v_ref[0, hd] = jnp.where(low, vs, vx).astype(BF16)
        kfull = jnp.concatenate([pt[hd * HEAD_DIM:(hd + 1) * HEAD_DIM, :], kx], axis=0).astype(BF16)
        for c in range(tm // tk):
            kt_ref[0, hd, c] = kfull[:, c * tk:(c + 1) * tk]


def _mixer_in(x, comb, lw, consts):
    b, s, _ = x.shape
    tm = min(TOKEN_TILE, s)
    tk = min(ATTN_TILE, s)
    nk = s // tk
    nh2 = 2 * N_HEADS
    combine = comb is not None

    def full(a):
        return pl.BlockSpec(a.shape, lambda bi, si, _n=a.ndim: (0,) * _n)

    tok = lambda w: pl.BlockSpec((1, tm, w), lambda bi, si: (bi, si, 0))
    in_arrays = [x]
    in_specs = [tok(D_MODEL)]
    if combine:
        y01, route = comb
        in_arrays += [y01, y01, route]
        in_specs += [pl.BlockSpec((SC_SPLIT, 1, 1, tm, PLANE_W), lambda bi, si, _c=c: (0, _c, bi, si, 0))
                     for c in range(2)] + [tok(LANES)]
    weights = [lw["mix_g"], lw["w_main"], lw["w_t"], lw["bf"], lw["caw"], lw["cdw"], lw["cdb"], lw["lng"],
               lw["lnb"], consts["tril"], consts["pq"], consts["qc"], consts["pk"], consts["kc"], consts["vc"]]
    in_arrays += weights
    in_specs += [full(a) for a in weights]

    out_shape = []
    out_specs = []
    if combine:
        out_shape.append(jax.ShapeDtypeStruct((b, s, D_MODEL), F32))
        out_specs.append(tok(D_MODEL))
    out_shape += [
        jax.ShapeDtypeStruct((b, s, BRANCH_WIDTH), BF16),
        jax.ShapeDtypeStruct((b, s, BRANCH_WIDTH), BF16),
        jax.ShapeDtypeStruct((b, nh2, s, HEAD_SLAB), BF16),
        jax.ShapeDtypeStruct((b, nh2, nk, HEAD_SLAB, tk), BF16),
        jax.ShapeDtypeStruct((b, nh2, s, HEAD_SLAB), BF16),
    ]
    out_specs += [
        tok(BRANCH_WIDTH), tok(BRANCH_WIDTH),
        pl.BlockSpec((1, nh2, tm, HEAD_SLAB), lambda bi, si: (bi, 0, si, 0)),
        pl.BlockSpec((1, nh2, tm // tk, HEAD_SLAB, tk), lambda bi, si: (bi, 0, si, 0, 0)),
        pl.BlockSpec((1, nh2, tm, HEAD_SLAB), lambda bi, si: (bi, 0, si, 0)),
    ]
    return pl.pallas_call(
        functools.partial(_mixer_in_kernel, combine),
        grid=(b, s // tm),
        in_specs=in_specs,
        out_specs=out_specs,
        out_shape=out_shape,
        scratch_shapes=[pltpu.VMEM((8 + tm, BRANCH_WIDTH), F32),
                        pltpu.VMEM((8, 32 + tm, BRANCH_WIDTH), F32),
                        pltpu.VMEM((8, LANES), F32)],
        compiler_params=pltpu.CompilerParams(dimension_semantics=("arbitrary", "arbitrary"),
                                             vmem_limit_bytes=VMEM_LIMIT),
        name="mixer_in",
    )(*in_arrays)


def _pair_out(accs):
    lane = lax.broadcasted_iota(I32, accs[0].shape, 1)
    return jnp.where(lane < HEAD_DIM, accs[0], pltpu.roll(accs[1], HEAD_DIM, axis=1))


def _sb_attn_kernel(q_ref, kt_ref, v_ref, u_ref, o_ref):
    tq = q_ref.shape[2]
    nk, tk = kt_ref.shape[2], kt_ref.shape[-1]
    i = pl.program_id(2)
    umat = u_ref[...]
    nh = q_ref.shape[1]
    nb = SB_GROUP
    qs = [q_ref[0, h] for h in range(nh)]
    row = lax.broadcasted_iota(I32, (tq, tk), 0)
    col = lax.broadcasted_iota(I32, (tq, tk), 1)

    def weights(g, tails, masked):
        ws, new_tails = [], []
        for h in range(nh):
            tail = tails[h]
            wh = []
            for u in reversed(range(nb)):
                jr = g * nb + u
                z = jnp.dot(qs[h], kt_ref[0, h, jnp.minimum(jr, nk - 1)], preferred_element_type=F32)
                sp = jnp.maximum(z, 0.0) + jnp.log(1.0 + jnp.exp(-jnp.abs(z)))
                if masked:
                    mask = col + (jr - i) * tk < row
                    sp = jnp.where(mask, sp, 0.0)
                later = jnp.dot(sp.astype(BF16), umat, preferred_element_type=F32)
                w = jnp.exp((z - later - tail).astype(BF16))
                if masked:
                    w = jnp.where(mask, w, jnp.zeros_like(w))
                wh.append(w)
                tail = tail + later[:, 0:1]
            ws.append(tuple(reversed(wh)))
            new_tails.append(tail)
        return tuple(ws), tuple(new_tails)

    def apply(g, accs, ws):
        out = []
        for h in range(nh):
            acc = accs[h]
            for u in range(nb):
                start = pl.multiple_of(jnp.minimum(g * nb + u, nk - 1) * tk, tk)
                acc = acc + jnp.dot(ws[h][u], v_ref[0, h, pl.ds(start, tk), :], preferred_element_type=F32)
            out.append(acc)
        return tuple(out)

    last = i // nb
    zero_t = tuple(jnp.zeros((tq, 1), F32) for _ in range(nh))
    accs = tuple(jnp.zeros((tq, HEAD_SLAB), F32) for _ in range(nh))
    ws, tails = weights(last, zero_t, True)

    def body(t, carry):
        accs, tails, ws = carry
        g = last - 1 - t
        accs = apply(g + 1, accs, ws)
        ws, tails = weights(g, tails, False)
        return accs, tails, ws

    accs, tails, ws = lax.fori_loop(0, last, body, (accs, tails, ws))
    accs = apply(0, accs, ws)
    o_ref[0] = jnp.concatenate([_pair_out(accs[k:k + 2]) for k in range(0, nh, 2)], axis=1).astype(BF16)


def _fox_attn_kernel(q_ref, kt_ref, v_ref, o_ref):
    tq = q_ref.shape[2]
    tk = kt_ref.shape[-1]
    i = pl.program_id(2)
    row = lax.broadcasted_iota(I32, (tq, tk), 0)
    col = lax.broadcasted_iota(I32, (tq, tk), 1)
    causal = col <= row
    nh = q_ref.shape[1]
    qs = [q_ref[0, h] for h in range(nh)]

    def vblock(h, j):
        return v_ref[0, h, pl.ds(pl.multiple_of(j * tk, tk), tk), :]

    accs, maxes = [], []
    for h in range(nh):
        s = jnp.where(causal, jnp.dot(qs[h], kt_ref[0, h, i], preferred_element_type=F32), -jnp.inf)
        m = jnp.max(s, axis=-1, keepdims=True)
        accs.append(jnp.dot(jnp.exp(s - m).astype(BF16), vblock(h, i), preferred_element_type=F32))
        maxes.append(m)

    def make_body(n_blocks, first):
        def body(jj, carry):
            accs, maxes = carry
            new_accs, new_maxes = [], []
            for h in range(nh):
                js = [first + n_blocks * jj + u for u in range(n_blocks)]
                ss = [jnp.dot(qs[h], kt_ref[0, h, j], preferred_element_type=F32) for j in js]
                m = maxes[h]
                for s in ss:
                    m = jnp.maximum(m, jnp.max(s, axis=-1, keepdims=True))
                acc = jnp.exp(maxes[h] - m) * accs[h]
                for s, j in zip(ss, js):
                    acc = acc + jnp.dot(jnp.exp(s - m).astype(BF16), vblock(h, j), preferred_element_type=F32)
                new_accs.append(acc)
                new_maxes.append(m)
            return tuple(new_accs), tuple(new_maxes)
        return body

    carry = (tuple(accs), tuple(maxes))
    done = 0
    for n_blocks in ATTN_TRIPS:
        trips = (i - done) // n_blocks
        carry = lax.fori_loop(0, trips, make_body(n_blocks, done), carry)
        done = done + trips * n_blocks
    accs = carry[0]
    outs = [a / a[:, HEAD_DIM:HEAD_DIM + 1] for a in accs]
    o_ref[0] = jnp.concatenate([_pair_out(outs[k:k + 2]) for k in range(0, nh, 2)], axis=1).astype(BF16)


def _attention(kind, q, kt, v, consts):
    b, _, s, _ = q.shape
    nk, tk = kt.shape[2], kt.shape[4]
    tq = tk
    hps = SB_HEADS_PER_STEP if kind == "sb" else FOX_HEADS_PER_STEP
    head0 = 0 if kind == "sb" else N_HEADS // hps
    in_specs = [
        pl.BlockSpec((1, hps, tq, HEAD_SLAB), lambda bi, hp, i: (bi, hp + head0, i, 0)),
        pl.BlockSpec((1, hps, nk, HEAD_SLAB, tk), lambda bi, hp, i: (bi, hp + head0, 0, 0, 0)),
        pl.BlockSpec((1, hps, s, HEAD_SLAB), lambda bi, hp, i: (bi, hp + head0, 0, 0)),
    ]
    args = [q, kt, v]
    if kind == "sb":
        in_specs.append(pl.BlockSpec((tk, tk), lambda bi, hp, i: (0, 0)))
        args.append(consts["u_incl"])
        body = _sb_attn_kernel
    else:
        body = _fox_attn_kernel
    return pl.pallas_call(
        body,
        grid=(b, N_HEADS // hps, s // tq),
        in_specs=in_specs,
        out_specs=pl.BlockSpec((1, tq, hps * HEAD_DIM), lambda bi, hp, i: (bi, i, hp)),
        out_shape=jax.ShapeDtypeStruct((b, s, BRANCH_WIDTH), BF16),
        compiler_params=pltpu.CompilerParams(dimension_semantics=("arbitrary", "arbitrary", "arbitrary"),
                                             vmem_limit_bytes=VMEM_LIMIT),
        name=kind + "_attn",
    )(*args)


def _mixer_out_kernel(x_ref, ya_ref, ysb_ref, yfx_ref, yc_ref, g_ref, wg_ref, bg_ref, wb_ref, wo_ref,
                      fg_ref, wr_ref, tru_ref, xo_ref, xp_ref, rt_ref, rtt_ref, cnt_ref, carry):
    tm = x_ref.shape[0]

    @pl.when(pl.program_id(0) == 0)
    def _():
        carry[...] = jnp.zeros_like(carry)

    x = x_ref[...]
    xb = _rms(x, g_ref[...]).astype(BF16)
    h = None
    for g, y_ref in enumerate((ya_ref, ysb_ref, yfx_ref, yc_ref)):
        gate = _sigmoid(jnp.dot(xb, wg_ref[g], preferred_element_type=F32) + bg_ref[g])
        term = gate * jnp.dot(y_ref[...], wb_ref[g], preferred_element_type=F32)
        h = term if h is None else h + term
    xo = x + jnp.dot(h.astype(BF16), wo_ref[...], preferred_element_type=F32)
    xo_ref[...] = xo
    xn = _rms(xo, fg_ref[...])
    _store_planes(xp_ref, xn)

    xh = xn.astype(BF16)
    xl = (xn - xh.astype(F32)).astype(BF16)
    nt = (((1,), (1,)), ((), ()))
    logits = (lax.dot_general(wr_ref[0], xh, nt, preferred_element_type=F32)
              + lax.dot_general(wr_ref[0], xl, nt, preferred_element_type=F32)
              + lax.dot_general(wr_ref[1], xh, nt, preferred_element_type=F32))[0:ROUTE_ROWS, :]
    row = lax.broadcasted_iota(I32, (ROUTE_ROWS, tm), 0).astype(F32)
    ninf = -jnp.inf
    big = float(LANES)
    gl = jnp.where(row < N_GROUPS, logits, ninf)
    gmax = jnp.max(gl, axis=0, keepdims=True)
    gidx = jnp.min(jnp.where(gl == gmax, row, big), axis=0, keepdims=True)
    p_group = 1.0 / jnp.sum(jnp.exp(gl - gmax), axis=0, keepdims=True)
    first = N_GROUPS + EXPERTS_PER_GROUP * gidx
    el = jnp.where((row >= first) & (row < first + EXPERTS_PER_GROUP), logits, ninf)
    m1 = jnp.max(el, axis=0, keepdims=True)
    i1 = jnp.min(jnp.where(el == m1, row, big), axis=0, keepdims=True)
    el2 = jnp.where(row == i1, ninf, el)
    m2 = jnp.max(el2, axis=0, keepdims=True)
    i2 = jnp.min(jnp.where(el2 == m2, row, big), axis=0, keepdims=True)
    e2 = jnp.exp(m2 - m1)
    w1 = p_group / (1.0 + e2)
    w2 = w1 * e2

    sel1 = row == i1
    sel2 = row == i2
    onehot = jnp.where(sel1, 1.0, jnp.where(sel2, 1.0, 0.0))
    before = jnp.dot(onehot.astype(BF16), tru_ref[...], preferred_element_type=F32) + carry[:, 0:1]
    r1 = jnp.sum(jnp.where(sel1, before, 0.0), axis=0, keepdims=True)
    r2 = jnp.sum(jnp.where(sel2, before, 0.0), axis=0, keepdims=True)
    total = carry[...] + jnp.sum(onehot, axis=1, keepdims=True)
    carry[...] = total
    cnt_ref[...] = total

    row8 = lax.broadcasted_iota(I32, (8, tm), 0)
    fields = (i1 - N_GROUPS, i2 - N_GROUPS, w1, w2, r1, r2)
    rtt = jnp.zeros((8, tm), F32)
    for k, f in enumerate(fields):
        rtt = jnp.where(row8 == k, f, rtt)
    rtt_ref[...] = rtt
    rt_ref[...] = jnp.concatenate([rtt, jnp.zeros((LANES - 8, tm), F32)], axis=0).T


def _mixer_out(x2d, ya, ysb, yfx, yc, lw, consts):
    t = x2d.shape[0]
    tm = min(TOKEN_TILE, t)

    def full(a):
        return pl.BlockSpec(a.shape, lambda i, _n=a.ndim: (0,) * _n)

    tok = lambda w: pl.BlockSpec((tm, w), lambda i: (i, 0))
    weights = [lw["mix_g"], lw["w_gate"], lw["b_gate"], lw["w_branch"], lw["w_out"], lw["ffn_g"], lw["w_router"],
               consts["triu_strict"]]
    return pl.pallas_call(
        _mixer_out_kernel,
        grid=(t // tm,),
        in_specs=[tok(D_MODEL)] + [tok(BRANCH_WIDTH)] * 4 + [full(a) for a in weights],
        out_specs=[tok(D_MODEL), pl.BlockSpec((SC_SPLIT, tm, PLANE_W), lambda i: (0, i, 0)), tok(LANES),
                   pl.BlockSpec((8, tm), lambda i: (0, i)), pl.BlockSpec((ROUTE_ROWS, LANES), lambda i: (0, 0))],
        out_shape=[jax.ShapeDtypeStruct((t, D_MODEL), F32),
                   jax.ShapeDtypeStruct((SC_SPLIT, t, PLANE_W), I32),
                   jax.ShapeDtypeStruct((t, LANES), F32),
                   jax.ShapeDtypeStruct((8, t), F32),
                   jax.ShapeDtypeStruct((ROUTE_ROWS, LANES), F32)],
        scratch_shapes=[pltpu.VMEM((ROUTE_ROWS, LANES), F32)],
        compiler_params=pltpu.CompilerParams(dimension_semantics=("arbitrary",),
                                             vmem_limit_bytes=VMEM_LIMIT),
        name="mixer_out",
    )(x2d, ya, ysb, yfx, yc, *weights)


def _sc_mesh():
    return plsc.VectorSubcoreMesh(core_axis_name="core", subcore_axis_name="subcore")


def _plane_index(idx, rows_per_plane):
    return jnp.concatenate([idx + k * rows_per_plane for k in range(SC_SPLIT)]).reshape(1, -1)


def _dispatch_rows(planes, pos0, pos1, n_out):
    w = planes.shape[2]
    rows = planes.reshape(-1, w)
    t = rows.shape[0]
    idx0 = _plane_index(pos0, n_out)
    idx1 = _plane_index(pos1, n_out)

    @functools.partial(pl.kernel, out_type=jax.ShapeDtypeStruct((SC_SPLIT * n_out, w), rows.dtype),
                       mesh=_sc_mesh(), scratch_types=[])
    def scatter_kernel(x_hbm, i0_hbm, i1_hbm, o_hbm):
        def body(x_vmem, i0_vmem, i1_vmem):
            pltpu.sync_copy(x_vmem, o_hbm.at[i0_vmem.at[0]])
            pltpu.sync_copy(x_vmem, o_hbm.at[i1_vmem.at[0]])

        pltpu.emit_pipeline(
            body,
            grid=(t // SC_WINDOW,),
            in_specs=[pl.BlockSpec((SC_WINDOW, w), lambda i: (i, 0)),
                      pl.BlockSpec((1, SC_WINDOW), lambda i: (0, i)),
                      pl.BlockSpec((1, SC_WINDOW), lambda i: (0, i))],
            out_specs=[],
            core_axis_name=("core", "subcore"),
            dimension_semantics=(pltpu.PARALLEL,),
        )(x_hbm, i0_hbm, i1_hbm)

    return scatter_kernel(rows, idx0, idx1).reshape(SC_SPLIT, n_out, w)


def _collect_rows(planes, idx):
    n, w = planes.shape[1:]
    table = planes.reshape(-1, w)
    idx2 = _plane_index(idx, n)
    m = idx2.shape[1]

    @functools.partial(pl.kernel, out_type=jax.ShapeDtypeStruct((m, w), table.dtype), mesh=_sc_mesh(),
                       scratch_types=[])
    def gather_kernel(x_hbm, i_hbm, o_hbm):
        def body(i_vmem, o_vmem):
            pltpu.sync_copy(x_hbm.at[i_vmem.at[0]], o_vmem)

        pltpu.emit_pipeline(
            body,
            grid=(m // SC_WINDOW,),
            in_specs=[pl.BlockSpec((1, SC_WINDOW), lambda i: (0, i))],
            out_specs=[pl.BlockSpec((SC_WINDOW, w), lambda i: (i, 0))],
            core_axis_name=("core", "subcore"),
            dimension_semantics=(pltpu.PARALLEL,),
        )(i_hbm, o_hbm)

    return gather_kernel(table, idx2).reshape(SC_SPLIT, -1, w)


def _moe_ffn_kernel(te_ref, nt_ref, xs_ref, wgu_ref, wd_ref, ys_ref):
    i = pl.program_id(0)

    @pl.when(i < nt_ref[0])
    def _():
        gu = None
        for k, part in enumerate(_load_planes(xs_ref)):
            term = jnp.dot(part.astype(BF16), wgu_ref[0, k * PLANE_W:(k + 1) * PLANE_W, :],
                           preferred_element_type=F32)
            gu = term if gu is None else gu + term
        gate = gu[:, :D_EXPERT]
        hid = gate * _sigmoid(gate) * gu[:, D_EXPERT:]
        _store_planes(ys_ref, jnp.dot(hid.astype(BF16), wd_ref[0], preferred_element_type=F32))

    @pl.when(i >= nt_ref[0])
    def _():
        ys_ref[...] = jnp.zeros_like(ys_ref)


def _moe_ffn(xs, tile_expert, n_tiles, lw):
    p = xs.shape[1]
    tg = GROUP_TILE
    rows = pl.BlockSpec((SC_SPLIT, tg, PLANE_W), lambda i, te, nt: (0, i, 0))
    grid_spec = pltpu.PrefetchScalarGridSpec(
        num_scalar_prefetch=2,
        grid=(p // tg,),
        in_specs=[rows,
                  pl.BlockSpec((1, D_MODEL, 2 * D_EXPERT), lambda i, te, nt: (te[i], 0, 0)),
                  pl.BlockSpec((1, D_EXPERT, D_MODEL), lambda i, te, nt: (te[i], 0, 0))],
        out_specs=rows,
    )
    return pl.pallas_call(
        _moe_ffn_kernel,
        grid_spec=grid_spec,
        out_shape=jax.ShapeDtypeStruct((SC_SPLIT, p, PLANE_W), I32),
        compiler_params=pltpu.CompilerParams(dimension_semantics=("arbitrary",),
                                             vmem_limit_bytes=VMEM_LIMIT),
        name="moe_ffn",
    )(tile_expert, n_tiles, xs, lw["w_gu"], lw["w_down"])


def _route_plan(route_t, cnt, t):
    tg = GROUP_TILE
    n_tiles_max = (2 * t) // tg + N_EXPERTS
    counts = cnt[N_GROUPS:N_GROUPS + N_EXPERTS, 0].astype(I32)
    padded = ((counts + tg - 1) // tg) * tg
    ends = jnp.cumsum(padded)
    offs = ends - padded
    experts = jnp.arange(N_EXPERTS, dtype=I32)[:, None]

    def first_row(e):
        return jnp.sum(jnp.where(e[None, :] == experts, offs[:, None], 0), axis=0)

    fields = route_t.astype(I32)
    pos0 = first_row(fields[0]) + fields[4]
    pos1 = first_row(fields[1]) + fields[5]
    tile_start = jnp.arange(n_tiles_max, dtype=I32) * tg
    n_tiles = ends[-1] // tg
    tile_clamped = jnp.minimum(tile_start, jnp.maximum(n_tiles - 1, 0) * tg)
    tile_expert = jnp.sum((ends[None, :] <= tile_clamped[:, None]).astype(I32), axis=1)
    tile_expert = jnp.minimum(tile_expert, N_EXPERTS - 1)
    return pos0, pos1, tile_expert, n_tiles.reshape(1), n_tiles_max * tg


def _moe(xp, route_t, cnt, lw):
    t = xp.shape[1]
    pos0, pos1, tile_expert, n_tiles, p_rows = _route_plan(route_t, cnt, t)
    xs = _dispatch_rows(xp, pos0, pos1, p_rows)
    ys = _moe_ffn(xs, tile_expert, n_tiles, lw)
    return _collect_rows(ys, jnp.concatenate([pos0, pos1])).reshape(SC_SPLIT, 2, t, PLANE_W)


def _final_kernel(x_ref, y0_ref, y1_ref, rt_ref, g_ref, o_ref):
    x = _combine(x_ref[...], y0_ref, y1_ref, rt_ref[...], lead=(0,))
    o_ref[...] = _rms(x, g_ref[...])


def _final(x2d, y01, route, g):
    t = x2d.shape[0]
    tm = min(TOKEN_TILE, t)
    tok = lambda w: pl.BlockSpec((tm, w), lambda i: (i, 0))
    choice = lambda c: pl.BlockSpec((SC_SPLIT, 1, tm, PLANE_W), lambda i: (0, c, i, 0))
    return pl.pallas_call(
        _final_kernel,
        grid=(t // tm,),
        in_specs=[tok(D_MODEL), choice(0), choice(1), tok(LANES), pl.BlockSpec((1, D_MODEL), lambda i: (0, 0))],
        out_specs=tok(D_MODEL),
        out_shape=jax.ShapeDtypeStruct((t, D_MODEL), F32),
        compiler_params=pltpu.CompilerParams(dimension_semantics=("arbitrary",),
                                             vmem_limit_bytes=VMEM_LIMIT),
        name="final_norm",
    )(x2d, y01, y01, route, g)


def _constants(tm, tk):
    r = jnp.arange(tm)
    tril = (r[None, :] <= r[:, None]).astype(BF16)
    triu_strict = (r[:, None] < r[None, :]).astype(BF16)
    rk = jnp.arange(tk)
    u_incl = (rk[:, None] >= rk[None, :]).astype(BF16)
    nh = N_HEADS
    pq = jnp.zeros((3, LANES, nh * HEAD_SLAB), F32)
    pk = jnp.zeros((3, nh * HEAD_DIM, LANES), F32)
    qc = jnp.zeros((1, nh * HEAD_SLAB), F32)
    kc = jnp.zeros((nh * HEAD_DIM, LANES), F32)
    for part in range(3):
        for h in range(nh):
            pq = pq.at[part, h, h * HEAD_SLAB + HEAD_DIM + part].set(1.0)
            qc = qc.at[0, h * HEAD_SLAB + HEAD_DIM + 3 + part].set(1.0)
            kc = kc.at[h * HEAD_DIM + part, :].set(1.0)
            pk = pk.at[part, h * HEAD_DIM + 3 + part, h].set(1.0)
    vc = jnp.zeros((1, HEAD_SLAB), F32).at[0, HEAD_DIM].set(1.0)
    return {"tril": tril, "triu_strict": triu_strict, "u_incl": u_incl, "pq": pq.astype(BF16),
            "pk": pk.astype(BF16), "qc": qc, "kc": kc, "vc": vc}


def _layer_weights(layer, mix_norm_g, w_in, b_forget, conv_a_w, conf_dw_w, conf_dw_b, conf_ln_g, conf_ln_b,
                   w_branch, w_gate, b_gate, w_out, ffn_norm_g, w_router_group, w_router_expert,
                   w_expert_gate, w_expert_up, w_expert_down):
    w = w_in[layer]
    bw = BRANCH_WIDTH
    a_x, a_b, a_c, sb_q, sb_k, sb_v, fx_q, fx_k, fx_v = [w[:, i * bw:(i + 1) * bw] for i in range(9)]
    fx_f = w[:, 9 * bw:9 * bw + N_HEADS]
    conf = w[:, 9 * bw + N_HEADS:]
    scale = HEAD_DIM ** -0.5
    f_pad = jnp.pad(fx_f, ((0, 0), (0, LANES - N_HEADS)))
    w_main = jnp.concatenate([conf, f_pad, sb_q * scale, fx_q * scale, sb_v, fx_v, a_x, a_b, a_c], axis=1)
    w_t = jnp.concatenate([sb_k, fx_k], axis=1).T
    w_router = jnp.concatenate([w_router_group[layer], w_router_expert[layer].reshape(D_MODEL, N_EXPERTS)], axis=1)
    w_router = jnp.pad(w_router, ((0, 0), (0, LANES - N_GROUPS - N_EXPERTS)))
    return {
        "mix_g": mix_norm_g[layer].reshape(1, D_MODEL),
        "w_main": w_main.astype(BF16),
        "w_t": w_t.astype(BF16),
        "bf": jnp.pad(b_forget[layer], (0, LANES - N_HEADS)).reshape(1, LANES),
        "caw": conv_a_w[layer],
        "cdw": jnp.pad(conf_dw_w[layer], ((0, 1), (0, 0))),
        "cdb": conf_dw_b[layer].reshape(1, bw),
        "lng": conf_ln_g[layer].reshape(1, bw),
        "lnb": conf_ln_b[layer].reshape(1, bw),
        "w_gate": w_gate[layer].astype(BF16),
        "b_gate": b_gate[layer].reshape(4, 1, D_MODEL),
        "w_branch": w_branch[layer].astype(BF16),
        "w_out": w_out[layer].astype(BF16),
        "ffn_g": ffn_norm_g[layer].reshape(1, D_MODEL),
        "w_router": jnp.stack([w_router.T.astype(BF16), (w_router - w_router.astype(BF16).astype(F32)).T.astype(BF16)]),
        "w_gu": jnp.concatenate([w_expert_gate[layer], w_expert_up[layer]], axis=2).astype(BF16),
        "w_down": w_expert_down[layer].astype(BF16),
    }


def kernel(x, mix_norm_g, w_in, b_forget, conv_a_w, conf_dw_w, conf_dw_b, conf_ln_g, conf_ln_b, w_branch, w_gate,
           b_gate, w_out, ffn_norm_g, w_router_group, w_router_expert, w_expert_gate, w_expert_up, w_expert_down,
           final_norm_g):
    b, s, _ = x.shape
    t = b * s
    depth = w_in.shape[0]
    consts = _constants(min(TOKEN_TILE, s), min(ATTN_TILE, s))
    comb = None
    for layer in range(depth):
        lw = _layer_weights(layer, mix_norm_g, w_in, b_forget, conv_a_w, conf_dw_w, conf_dw_b, conf_ln_g,
                            conf_ln_b, w_branch, w_gate, b_gate, w_out, ffn_norm_g, w_router_group,
                            w_router_expert, w_expert_gate, w_expert_up, w_expert_down)
        outs = _mixer_in(x, comb, lw, consts)
        if comb is not None:
            x, *outs = outs
        ya, yc, q, kt, v = outs
        ysb = _attention("sb", q, kt, v, consts)
        yfx = _attention("fox", q, kt, v, consts)
        x2d, xp, route, route_t, cnt = _mixer_out(x.reshape(t, D_MODEL), ya.reshape(t, -1), ysb.reshape(t, -1),
                                                  yfx.reshape(t, -1), yc.reshape(t, -1), lw, consts)
        y01 = _moe(xp, route_t, cnt, lw)
        x = x2d.reshape(b, s, D_MODEL)
        comb = (y01.reshape(SC_SPLIT, 2, b, s, PLANE_W), route.reshape(b, s, LANES))
    out = _final(x.reshape(t, D_MODEL), comb[0].reshape(SC_SPLIT, 2, t, PLANE_W), comb[1].reshape(t, LANES),
                 final_norm_g.reshape(1, D_MODEL))
    return out.reshape(b, s, D_MODEL)
```

```python
import functools

import jax
import jax.numpy as jnp
from jax import lax
from jax.experimental import pallas as pl
from jax.experimental.pallas import tpu as pltpu
from jax.experimental.pallas import tpu_sc as plsc

F32 = jnp.float32
BF16 = jnp.bfloat16
I32 = jnp.int32

D_MODEL = 1024
BRANCH_WIDTH = 256
HEAD_DIM = 64
N_HEADS = 4
HEAD_SLAB = 128
CONV_A_WIDTH = 3
CONF_WIDTH = 31
N_GROUPS = 4
EXPERTS_PER_GROUP = 8
N_EXPERTS = 32
D_EXPERT = 256
EPS = 1e-6
LANES = 128
ROUTE_ROWS = 48
HALF_D = D_MODEL // 2
SC_SPLIT = 2
PLANE_W = HALF_D // SC_SPLIT

TOKEN_TILE = 512
ATTN_TILE = 256
FOX_Q_BLOCKS = 2
SB_HEADS_PER_STEP = 4
FOX_HEADS_PER_STEP = 4
ATTN_TRIPS = (4, 2, 1)
SB_GROUP = 4
GROUP_TILE = 512
CONV_ROW_CHUNK = 64
SC_WINDOW = 128
VMEM_LIMIT = 56 * 1024 * 1024

_C_CU, _C_CG = 0, 256
_C_F = 512
_N_FIRST = 640
_C_Q = 640
_C_V = 1152
_C_AX, _C_AB, _C_AC = 1664, 1920, 2176
_N_MAIN = 2432


def _rms(x, g):
    return x * lax.rsqrt(jnp.mean(x * x, axis=-1, keepdims=True) + EPS) * g


def _softplus(z):
    return jnp.maximum(z, 0.0) + jnp.log1p(jnp.exp(-jnp.abs(z)))


def _sigmoid(z):
    return 1.0 / (1.0 + jnp.exp(-z))


def _split3(v):
    hi = v.astype(BF16)
    r = v - hi.astype(F32)
    mid = r.astype(BF16)
    lo = (r - mid.astype(F32)).astype(BF16)
    return hi, mid, lo


def _pack_rows(v):
    lo = pltpu.bitcast(v[:, :HALF_D].astype(BF16).astype(F32), jnp.uint32)
    hi = pltpu.bitcast(v[:, HALF_D:].astype(BF16).astype(F32), jnp.uint32)
    return pltpu.bitcast((lo >> 16) | hi, I32)


def _unpack_rows(w):
    u = pltpu.bitcast(w, jnp.uint32)
    lo = pltpu.bitcast(u << 16, F32)
    hi = pltpu.bitcast(u & jnp.uint32(0xFFFF0000), F32)
    return lo, hi


def _store_planes(ref, v, lead=()):
    packed = _pack_rows(v)
    for k in range(SC_SPLIT):
        ref[(k, *lead)] = packed[:, k * PLANE_W:(k + 1) * PLANE_W]


def _load_planes(ref, lead=()):
    los, his = zip(*[_unpack_rows(ref[(k, *lead)]) for k in range(SC_SPLIT)])
    return list(los) + list(his)


def _combine(x, y0_ref, y1_ref, rt, lead=()):
    w0 = rt[:, 2:3]
    w1 = rt[:, 3:4]
    parts = [w0 * a + w1 * b for a, b in zip(_load_planes(y0_ref, lead), _load_planes(y1_ref, lead))]
    return x + jnp.concatenate(parts, axis=1)


def _mixer_in_kernel(combine, *refs):
    if combine:
        (x_ref, y0_ref, y1_ref, rt_ref, *refs) = refs
    else:
        (x_ref, *refs) = refs
    (g_ref, wm_ref, wt_ref, bf_ref, caw_ref, cdw_ref, cdb_ref, lng_ref, lnb_ref, tril_ref,
     pq_ref, qc_ref, pk_ref, kc_ref, vc_ref, *refs) = refs
    if combine:
        (xo_ref, *refs) = refs
    (ya_ref, yc_ref, q_ref, kt_ref, v_ref, bufa, bufc, dcarry) = refs

    tm = x_ref.shape[1]
    tk = kt_ref.shape[-1]

    @pl.when(pl.program_id(1) == 0)
    def _():
        bufa[0:8, :] = jnp.zeros((8, BRANCH_WIDTH), F32)
        bufc[0, 0:32, :] = jnp.zeros((32, BRANCH_WIDTH), F32)
        dcarry[...] = jnp.zeros_like(dcarry)

    x = x_ref[0]
    if combine:
        x = _combine(x, y0_ref, y1_ref, rt_ref[0], lead=(0, 0))
        xo_ref[0] = x
    xb = _rms(x, g_ref[...]).astype(BF16)

    p = jnp.concatenate([jnp.dot(xb, wm_ref[:, :_N_FIRST], preferred_element_type=F32),
                         jnp.dot(xb, wm_ref[:, _N_FIRST:], preferred_element_type=F32)], axis=1)
    pt = lax.dot_general(wt_ref[...], xb, (((1,), (1,)), ((), ())),
                         preferred_element_type=F32)

    ca = p[:, _C_AC:_C_AC + 256] * p[:, _C_AX:_C_AX + 256]
    bufa[8:8 + tm, :] = ca
    caw = caw_ref[...]
    conv = caw[0:1] * bufa[6:6 + tm, :] + caw[1:2] * bufa[7:7 + tm, :] + caw[2:3] * ca
    ya_ref[0] = (p[:, _C_AB:_C_AB + 256] * conv).astype(BF16)
    bufa[0:8, :] = ca[tm - 8:tm]

    u = p[:, _C_CU:_C_CU + 256] * _sigmoid(p[:, _C_CG:_C_CG + 256])
    bufc[0, 32:32 + tm, :] = u
    for r in range(1, 8):
        bufc[r, 0:tm + 24, :] = bufc[0, r:r + tm + 24, :]
    cdw = cdw_ref[...]
    cdb = cdb_ref[...]
    lng = lng_ref[...]
    lnb = lnb_ref[...]
    for c in range(tm // CONV_ROW_CHUNK):
        row0 = c * CONV_ROW_CHUNK
        acc = jnp.broadcast_to(cdb, (CONV_ROW_CHUNK, BRANCH_WIDTH))
        for k in range(CONF_WIDTH):
            off = 32 - (CONF_WIDTH - 1) + k
            acc = acc + cdw[k:k + 1] * bufc[off % 8, row0 + off - off % 8:row0 + off - off % 8 + CONV_ROW_CHUNK, :]
        mu = jnp.mean(acc, axis=-1, keepdims=True)
        cen = acc - mu
        var = jnp.mean(cen * cen, axis=-1, keepdims=True)
        yn = cen * lax.rsqrt(var + EPS) * lng + lnb
        yc_ref[0, row0:row0 + CONV_ROW_CHUNK, :] = (yn * _sigmoid(yn)).astype(BF16)
    bufc[0, 0:32, :] = u[tm - 32:tm]

    logf = -_softplus(-(p[:, _C_F:_C_F + LANES] + bf_ref[...]))
    tril = tril_ref[...]
    dcum = dcarry[0:1, :]
    for part in _split3(logf):
        dcum = dcum + jnp.dot(tril, part, preferred_element_type=F32)
    dcarry[0:1, :] = dcum[tm - 1:tm, :]
    dcum_t = dcum.T
    qh, qm, ql = _split3(dcum)
    kh, km, kl = _split3(-dcum_t)
    q_extra = (jnp.dot(qh, pq_ref[0], preferred_element_type=F32)
               + jnp.dot(qm, pq_ref[1], preferred_element_type=F32)
               + jnp.dot(ql, pq_ref[2], preferred_element_type=F32) + qc_ref[...])
    k_extra = (jnp.dot(pk_ref[0], kh, preferred_element_type=F32)
               + jnp.dot(pk_ref[1], km, preferred_element_type=F32)
               + jnp.dot(pk_ref[2], kl, preferred_element_type=F32)
               + jnp.concatenate([kc_ref[...]] * (tm // LANES), axis=1))

    lane = lax.broadcasted_iota(I32, (tm, HEAD_SLAB), 1)
    low = lane < HEAD_DIM
    vc = vc_ref[...]
    for hd in range(2 * N_HEADS):
        is_fox = hd >= N_HEADS
        pair = (hd // 2) * HEAD_SLAB
        qs = p[:, _C_Q + pair:_C_Q + pair + HEAD_SLAB]
        vs = p[:, _C_V + pair:_C_V + pair + HEAD_SLAB]
        if hd % 2:
            qs = pltpu.roll(qs, HEAD_DIM, axis=1)
            vs = pltpu.roll(vs, HEAD_DIM, axis=1)
        if is_fox:
            hf = hd - N_HEADS
            qx = q_extra[:, hf * HEAD_SLAB:(hf + 1) * HEAD_SLAB]
            kx = k_extra[hf * HEAD_DIM:(hf + 1) * HEAD_DIM, :]
            vx = vc
        else:
            qx = 0.0
            kx = jnp.zeros((HEAD_DIM, tm), F32)
            vx = 0.0
        q_ref[0, hd] = jnp.where(low, qs, qx).astype(BF16)
        v_ref[0, hd] = jnp.where(low, vs, vx).astype(BF16)
        kfull = jnp.concatenate([pt[hd * HEAD_DIM:(hd + 1) * HEAD_DIM, :], kx], axis=0).astype(BF16)
        for c in range(tm // tk):
            kt_ref[0, hd, c] = kfull[:, c * tk:(c + 1) * tk]


def _mixer_in(x, comb, lw, consts):
    b, s, _ = x.shape
    tm = min(TOKEN_TILE, s)
    tk = min(ATTN_TILE, s)
    nk = s // tk
    nh2 = 2 * N_HEADS
    combine = comb is not None

    def full(a):
        return pl.BlockSpec(a.shape, lambda bi, si, _n=a.ndim: (0,) * _n)

    tok = lambda w: pl.BlockSpec((1, tm, w), lambda bi, si: (bi, si, 0))
    in_arrays = [x]
    in_specs = [tok(D_MODEL)]
    if combine:
        y01, route = comb
        in_arrays += [y01, y01, route]
        in_specs += [pl.BlockSpec((SC_SPLIT, 1, 1, tm, PLANE_W), lambda bi, si, _c=c: (0, _c, bi, si, 0))
                     for c in range(2)] + [tok(LANES)]
    weights = [lw["mix_g"], lw["w_main"], lw["w_t"], lw["bf"], lw["caw"], lw["cdw"], lw["cdb"], lw["lng"],
               lw["lnb"], consts["tril"], consts["pq"], consts["qc"], consts["pk"], consts["kc"], consts["vc"]]
    in_arrays += weights
    in_specs += [full(a) for a in weights]

    out_shape = []
    out_specs = []
    if combine:
        out_shape.append(jax.ShapeDtypeStruct((b, s, D_MODEL), F32))
        out_specs.append(tok(D_MODEL))
    out_shape += [
        jax.ShapeDtypeStruct((b, s, BRANCH_WIDTH), BF16),
        jax.ShapeDtypeStruct((b, s, BRANCH_WIDTH), BF16),
        jax.ShapeDtypeStruct((b, nh2, s, HEAD_SLAB), BF16),
        jax.ShapeDtypeStruct((b, nh2, nk, HEAD_SLAB, tk), BF16),
        jax.ShapeDtypeStruct((b, nh2, s, HEAD_SLAB), BF16),
    ]
    out_specs += [
        tok(BRANCH_WIDTH), tok(BRANCH_WIDTH),
        pl.BlockSpec((1, nh2, tm, HEAD_SLAB), lambda bi, si: (bi, 0, si, 0)),
        pl.BlockSpec((1, nh2, tm // tk, HEAD_SLAB, tk), lambda bi, si: (bi, 0, si, 0, 0)),
        pl.BlockSpec((1, nh2, tm, HEAD_SLAB), lambda bi, si: (bi, 0, si, 0)),
    ]
    return pl.pallas_call(
        functools.partial(_mixer_in_kernel, combine),
        grid=(b, s // tm),
        in_specs=in_specs,
        out_specs=out_specs,
        out_shape=out_shape,
        scratch_shapes=[pltpu.VMEM((8 + tm, BRANCH_WIDTH), F32),
                        pltpu.VMEM((8, 32 + tm, BRANCH_WIDTH), F32),
                        pltpu.VMEM((8, LANES), F32)],
        compiler_params=pltpu.CompilerParams(dimension_semantics=("arbitrary", "arbitrary"),
                                             vmem_limit_bytes=VMEM_LIMIT),
        name="mixer_in",
    )(*in_arrays)


def _pair_out(accs):
    lane = lax.broadcasted_iota(I32, accs[0].shape, 1)
    return jnp.where(lane < HEAD_DIM, accs[0], pltpu.roll(accs[1], HEAD_DIM, axis=1))


def _sb_attn_kernel(q_ref, kt_ref, v_ref, u_ref, o_ref):
    tq = q_ref.shape[2]
    nk, tk = kt_ref.shape[2], kt_ref.shape[-1]
    i = pl.program_id(2)
    umat = u_ref[...]
    nh = q_ref.shape[1]
    nb = SB_GROUP
    qs = [q_ref[0, h] for h in range(nh)]
    row = lax.broadcasted_iota(I32, (tq, tk), 0)
    col = lax.broadcasted_iota(I32, (tq, tk), 1)

    def weights(g, tails, masked):
        ws, new_tails = [], []
        for h in range(nh):
            tail = tails[h]
            wh = []
            for u in reversed(range(nb)):
                jr = g * nb + u
                z = jnp.dot(qs[h], kt_ref[0, h, jnp.minimum(jr, nk - 1)], preferred_element_type=F32)
                sp = jnp.maximum(z, 0.0) + jnp.log(1.0 + jnp.exp(-jnp.abs(z)))
                if masked:
                    mask = col + (jr - i) * tk < row
                    sp = jnp.where(mask, sp, 0.0)
                later = jnp.dot(sp.astype(BF16), umat, preferred_element_type=F32)
                w = jnp.exp((z - later - tail).astype(BF16))
                if masked:
                    w = jnp.where(mask, w, jnp.zeros_like(w))
                wh.append(w)
                tail = tail + later[:, 0:1]
            ws.append(tuple(reversed(wh)))
            new_tails.append(tail)
        return tuple(ws), tuple(new_tails)

    def apply(g, accs, ws):
        out = []
        for h in range(nh):
            acc = accs[h]
            for u in range(nb):
                start = pl.multiple_of(jnp.minimum(g * nb + u, nk - 1) * tk, tk)
                acc = acc + jnp.dot(ws[h][u], v_ref[0, h, pl.ds(start, tk), :], preferred_element_type=F32)
            out.append(acc)
        return tuple(out)

    last = i // nb
    zero_t = tuple(jnp.zeros((tq, 1), F32) for _ in range(nh))
    accs = tuple(jnp.zeros((tq, HEAD_SLAB), F32) for _ in range(nh))
    ws, tails = weights(last, zero_t, True)

    def body(t, carry):
        accs, tails, ws = carry
        g = last - 1 - t
        accs = apply(g + 1, accs, ws)
        ws, tails = weights(g, tails, False)
        return accs, tails, ws

    accs, tails, ws = lax.fori_loop(0, last, body, (accs, tails, ws))
    accs = apply(0, accs, ws)
    o_ref[0] = jnp.concatenate([_pair_out(accs[k:k + 2]) for k in range(0, nh, 2)], axis=1).astype(BF16)


def _fox_attn_kernel(q_ref, kt_ref, v_ref, o_ref):
    tq = q_ref.shape[2]
    tk = kt_ref.shape[-1]
    i = pl.program_id(2)
    row = lax.broadcasted_iota(I32, (tq, tk), 0)
    col = lax.broadcasted_iota(I32, (tq, tk), 1)
    nh = q_ref.shape[1]
    qs = [q_ref[0, h] for h in range(nh)]

    def vblock(h, j):
        return v_ref[0, h, pl.ds(pl.multiple_of(j * tk, tk), tk), :]

    r = tq // tk
    accs, maxes = [], []
    for h in range(nh):
        ss = [jnp.where(col + d * tk <= row, jnp.dot(qs[h], kt_ref[0, h, i * r + d], preferred_element_type=F32),
                        -jnp.inf) for d in range(r)]
        m = jnp.max(ss[0], axis=-1, keepdims=True)
        for s in ss[1:]:
            m = jnp.maximum(m, jnp.max(s, axis=-1, keepdims=True))
        acc = None
        for d, s in enumerate(ss):
            term = jnp.dot(jnp.exp(s - m).astype(BF16), vblock(h, i * r + d), preferred_element_type=F32)
            acc = term if acc is None else acc + term
        accs.append(acc)
        maxes.append(m)

    def make_body(n_blocks, first):
        def body(jj, carry):
            accs, maxes = carry
            new_accs, new_maxes = [], []
            for h in range(nh):
                js = [first + n_blocks * jj + u for u in range(n_blocks)]
                ss = [jnp.dot(qs[h], kt_ref[0, h, j], preferred_element_type=F32) for j in js]
                m = maxes[h]
                for s in ss:
                    m = jnp.maximum(m, jnp.max(s, axis=-1, keepdims=True))
                acc = jnp.exp(maxes[h] - m) * accs[h]
                for s, j in zip(ss, js):
                    acc = acc + jnp.dot(jnp.exp(s - m).astype(BF16), vblock(h, j), preferred_element_type=F32)
                new_accs.append(acc)
                new_maxes.append(m)
            return tuple(new_accs), tuple(new_maxes)
        return body

    carry = (tuple(accs), tuple(maxes))
    done = 0
    for n_blocks in ATTN_TRIPS:
        trips = (i * r - done) // n_blocks
        carry = lax.fori_loop(0, trips, make_body(n_blocks, done), carry)
        done = done + trips * n_blocks
    accs = carry[0]
    outs = [a / a[:, HEAD_DIM:HEAD_DIM + 1] for a in accs]
    o_ref[0] = jnp.concatenate([_pair_out(outs[k:k + 2]) for k in range(0, nh, 2)], axis=1).astype(BF16)


def _attention(kind, q, kt, v, consts):
    b, _, s, _ = q.shape
    nk, tk = kt.shape[2], kt.shape[4]
    tq = tk if kind == "sb" else min(FOX_Q_BLOCKS * tk, s)
    hps = SB_HEADS_PER_STEP if kind == "sb" else FOX_HEADS_PER_STEP
    head0 = 0 if kind == "sb" else N_HEADS // hps
    in_specs = [
        pl.BlockSpec((1, hps, tq, HEAD_SLAB), lambda bi, hp, i: (bi, hp + head0, i, 0)),
        pl.BlockSpec((1, hps, nk, HEAD_SLAB, tk), lambda bi, hp, i: (bi, hp + head0, 0, 0, 0)),
        pl.BlockSpec((1, hps, s, HEAD_SLAB), lambda bi, hp, i: (bi, hp + head0, 0, 0)),
    ]
    args = [q, kt, v]
    if kind == "sb":
        in_specs.append(pl.BlockSpec((tk, tk), lambda bi, hp, i: (0, 0)))
        args.append(consts["u_incl"])
        body = _sb_attn_kernel
    else:
        body = _fox_attn_kernel
    return pl.pallas_call(
        body,
        grid=(b, N_HEADS // hps, s // tq),
        in_specs=in_specs,
        out_specs=pl.BlockSpec((1, tq, hps * HEAD_DIM), lambda bi, hp, i: (bi, i, hp)),
        out_shape=jax.ShapeDtypeStruct((b, s, BRANCH_WIDTH), BF16),
        compiler_params=pltpu.CompilerParams(dimension_semantics=("arbitrary", "arbitrary", "arbitrary"),
                                             vmem_limit_bytes=VMEM_LIMIT),
        name=kind + "_attn",
    )(*args)


def _mixer_out_kernel(x_ref, ya_ref, ysb_ref, yfx_ref, yc_ref, g_ref, wg_ref, bg_ref, wb_ref, wo_ref,
                      fg_ref, wr_ref, tru_ref, xo_ref, xp_ref, rt_ref, rtt_ref, cnt_ref, carry):
    tm = x_ref.shape[0]

    @pl.when(pl.program_id(0) == 0)
    def _():
        carry[...] = jnp.zeros_like(carry)

    x = x_ref[...]
    xb = _rms(x, g_ref[...]).astype(BF16)
    h = None
    for g, y_ref in enumerate((ya_ref, ysb_ref, yfx_ref, yc_ref)):
        gate = _sigmoid(jnp.dot(xb, wg_ref[g], preferred_element_type=F32) + bg_ref[g])
        term = gate * jnp.dot(y_ref[...], wb_ref[g], preferred_element_type=F32)
        h = term if h is None else h + term
    xo = x + jnp.dot(h.astype(BF16), wo_ref[...], preferred_element_type=F32)
    xo_ref[...] = xo
    xn = _rms(xo, fg_ref[...])
    _store_planes(xp_ref, xn)

    xh = xn.astype(BF16)
    xl = (xn - xh.astype(F32)).astype(BF16)
    nt = (((1,), (1,)), ((), ()))
    logits = (lax.dot_general(wr_ref[0], xh, nt, preferred_element_type=F32)
              + lax.dot_general(wr_ref[0], xl, nt, preferred_element_type=F32)
              + lax.dot_general(wr_ref[1], xh, nt, preferred_element_type=F32))[0:ROUTE_ROWS, :]
    row = lax.broadcasted_iota(I32, (ROUTE_ROWS, tm), 0).astype(F32)
    ninf = -jnp.inf
    big = float(LANES)
    gl = jnp.where(row < N_GROUPS, logits, ninf)
    gmax = jnp.max(gl, axis=0, keepdims=True)
    gidx = jnp.min(jnp.where(gl == gmax, row, big), axis=0, keepdims=True)
    p_group = 1.0 / jnp.sum(jnp.exp(gl - gmax), axis=0, keepdims=True)
    first = N_GROUPS + EXPERTS_PER_GROUP * gidx
    el = jnp.where((row >= first) & (row < first + EXPERTS_PER_GROUP), logits, ninf)
    m1 = jnp.max(el, axis=0, keepdims=True)
    i1 = jnp.min(jnp.where(el == m1, row, big), axis=0, keepdims=True)
    el2 = jnp.where(row == i1, ninf, el)
    m2 = jnp.max(el2, axis=0, keepdims=True)
    i2 = jnp.min(jnp.where(el2 == m2, row, big), axis=0, keepdims=True)
    e2 = jnp.exp(m2 - m1)
    w1 = p_group / (1.0 + e2)
    w2 = w1 * e2

    sel1 = row == i1
    sel2 = row == i2
    onehot = jnp.where(sel1, 1.0, jnp.where(sel2, 1.0, 0.0))
    before = jnp.dot(onehot.astype(BF16), tru_ref[...], preferred_element_type=F32) + carry[:, 0:1]
    r1 = jnp.sum(jnp.where(sel1, before, 0.0), axis=0, keepdims=True)
    r2 = jnp.sum(jnp.where(sel2, before, 0.0), axis=0, keepdims=True)
    total = carry[...] + jnp.sum(onehot, axis=1, keepdims=True)
    carry[...] = total
    cnt_ref[...] = total

    row8 = lax.broadcasted_iota(I32, (8, tm), 0)
    fields = (i1 - N_GROUPS, i2 - N_GROUPS, w1, w2, r1, r2)
    rtt = jnp.zeros((8, tm), F32)
    for k, f in enumerate(fields):
        rtt = jnp.where(row8 == k, f, rtt)
    rtt_ref[...] = rtt
    rt_ref[...] = jnp.concatenate([rtt, jnp.zeros((LANES - 8, tm), F32)], axis=0).T


def _mixer_out(x2d, ya, ysb, yfx, yc, lw, consts):
    t = x2d.shape[0]
    tm = min(TOKEN_TILE, t)

    def full(a):
        return pl.BlockSpec(a.shape, lambda i, _n=a.ndim: (0,) * _n)

    tok = lambda w: pl.BlockSpec((tm, w), lambda i: (i, 0))
    weights = [lw["mix_g"], lw["w_gate"], lw["b_gate"], lw["w_branch"], lw["w_out"], lw["ffn_g"], lw["w_router"],
               consts["triu_strict"]]
    return pl.pallas_call(
        _mixer_out_kernel,
        grid=(t // tm,),
        in_specs=[tok(D_MODEL)] + [tok(BRANCH_WIDTH)] * 4 + [full(a) for a in weights],
        out_specs=[tok(D_MODEL), pl.BlockSpec((SC_SPLIT, tm, PLANE_W), lambda i: (0, i, 0)), tok(LANES),
                   pl.BlockSpec((8, tm), lambda i: (0, i)), pl.BlockSpec((ROUTE_ROWS, LANES), lambda i: (0, 0))],
        out_shape=[jax.ShapeDtypeStruct((t, D_MODEL), F32),
                   jax.ShapeDtypeStruct((SC_SPLIT, t, PLANE_W), I32),
                   jax.ShapeDtypeStruct((t, LANES), F32),
                   jax.ShapeDtypeStruct((8, t), F32),
                   jax.ShapeDtypeStruct((ROUTE_ROWS, LANES), F32)],
        scratch_shapes=[pltpu.VMEM((ROUTE_ROWS, LANES), F32)],
        compiler_params=pltpu.CompilerParams(dimension_semantics=("arbitrary",),
                                             vmem_limit_bytes=VMEM_LIMIT),
        name="mixer_out",
    )(x2d, ya, ysb, yfx, yc, *weights)


def _sc_mesh():
    return plsc.VectorSubcoreMesh(core_axis_name="core", subcore_axis_name="subcore")


def _plane_index(idx, rows_per_plane):
    return jnp.concatenate([idx + k * rows_per_plane for k in range(SC_SPLIT)]).reshape(1, -1)


def _dispatch_rows(planes, pos0, pos1, n_out):
    w = planes.shape[2]
    rows = planes.reshape(-1, w)
    t = rows.shape[0]
    idx0 = _plane_index(pos0, n_out)
    idx1 = _plane_index(pos1, n_out)

    @functools.partial(pl.kernel, out_type=jax.ShapeDtypeStruct((SC_SPLIT * n_out, w), rows.dtype),
                       mesh=_sc_mesh(), scratch_types=[])
    def scatter_kernel(x_hbm, i0_hbm, i1_hbm, o_hbm):
        def body(x_vmem, i0_vmem, i1_vmem):
            pltpu.sync_copy(x_vmem, o_hbm.at[i0_vmem.at[0]])
            pltpu.sync_copy(x_vmem, o_hbm.at[i1_vmem.at[0]])

        pltpu.emit_pipeline(
            body,
            grid=(t // SC_WINDOW,),
            in_specs=[pl.BlockSpec((SC_WINDOW, w), lambda i: (i, 0)),
                      pl.BlockSpec((1, SC_WINDOW), lambda i: (0, i)),
                      pl.BlockSpec((1, SC_WINDOW), lambda i: (0, i))],
            out_specs=[],
            core_axis_name=("core", "subcore"),
            dimension_semantics=(pltpu.PARALLEL,),
        )(x_hbm, i0_hbm, i1_hbm)

    return scatter_kernel(rows, idx0, idx1).reshape(SC_SPLIT, n_out, w)


def _collect_rows(planes, idx):
    n, w = planes.shape[1:]
    table = planes.reshape(-1, w)
    idx2 = _plane_index(idx, n)
    m = idx2.shape[1]

    @functools.partial(pl.kernel, out_type=jax.ShapeDtypeStruct((m, w), table.dtype), mesh=_sc_mesh(),
                       scratch_types=[])
    def gather_kernel(x_hbm, i_hbm, o_hbm):
        def body(i_vmem, o_vmem):
            pltpu.sync_copy(x_hbm.at[i_vmem.at[0]], o_vmem)

        pltpu.emit_pipeline(
            body,
            grid=(m // SC_WINDOW,),
            in_specs=[pl.BlockSpec((1, SC_WINDOW), lambda i: (0, i))],
            out_specs=[pl.BlockSpec((SC_WINDOW, w), lambda i: (i, 0))],
            core_axis_name=("core", "subcore"),
            dimension_semantics=(pltpu.PARALLEL,),
        )(i_hbm, o_hbm)

    return gather_kernel(table, idx2).reshape(SC_SPLIT, -1, w)


def _moe_ffn_kernel(te_ref, nt_ref, xs_ref, wgu_ref, wd_ref, ys_ref):
    i = pl.program_id(0)

    @pl.when(i < nt_ref[0])
    def _():
        gu = None
        for k, part in enumerate(_load_planes(xs_ref)):
            term = jnp.dot(part.astype(BF16), wgu_ref[0, k * PLANE_W:(k + 1) * PLANE_W, :],
                           preferred_element_type=F32)
            gu = term if gu is None else gu + term
        gate = gu[:, :D_EXPERT]
        hid = gate * _sigmoid(gate) * gu[:, D_EXPERT:]
        _store_planes(ys_ref, jnp.dot(hid.astype(BF16), wd_ref[0], preferred_element_type=F32))

    @pl.when(i >= nt_ref[0])
    def _():
        ys_ref[...] = jnp.zeros_like(ys_ref)


def _moe_ffn(xs, tile_expert, n_tiles, lw):
    p = xs.shape[1]
    tg = GROUP_TILE
    rows = pl.BlockSpec((SC_SPLIT, tg, PLANE_W), lambda i, te, nt: (0, i, 0))
    grid_spec = pltpu.PrefetchScalarGridSpec(
        num_scalar_prefetch=2,
        grid=(p // tg,),
        in_specs=[rows,
                  pl.BlockSpec((1, D_MODEL, 2 * D_EXPERT), lambda i, te, nt: (te[i], 0, 0)),
                  pl.BlockSpec((1, D_EXPERT, D_MODEL), lambda i, te, nt: (te[i], 0, 0))],
        out_specs=rows,
    )
    return pl.pallas_call(
        _moe_ffn_kernel,
        grid_spec=grid_spec,
        out_shape=jax.ShapeDtypeStruct((SC_SPLIT, p, PLANE_W), I32),
        compiler_params=pltpu.CompilerParams(dimension_semantics=("arbitrary",),
                                             vmem_limit_bytes=VMEM_LIMIT),
        name="moe_ffn",
    )(tile_expert, n_tiles, xs, lw["w_gu"], lw["w_down"])


def _route_plan(route_t, cnt, t):
    tg = GROUP_TILE
    n_tiles_max = (2 * t) // tg + N_EXPERTS
    counts = cnt[N_GROUPS:N_GROUPS + N_EXPERTS, 0].astype(I32)
    padded = ((counts + tg - 1) // tg) * tg
    ends = jnp.cumsum(padded)
    offs = ends - padded
    experts = jnp.arange(N_EXPERTS, dtype=I32)[:, None]

    def first_row(e):
        return jnp.sum(jnp.where(e[None, :] == experts, offs[:, None], 0), axis=0)

    fields = route_t.astype(I32)
    pos0 = first_row(fields[0]) + fields[4]
    pos1 = first_row(fields[1]) + fields[5]
    tile_start = jnp.arange(n_tiles_max, dtype=I32) * tg
    n_tiles = ends[-1] // tg
    tile_clamped = jnp.minimum(tile_start, jnp.maximum(n_tiles - 1, 0) * tg)
    tile_expert = jnp.sum((ends[None, :] <= tile_clamped[:, None]).astype(I32), axis=1)
    tile_expert = jnp.minimum(tile_expert, N_EXPERTS - 1)
    return pos0, pos1, tile_expert, n_tiles.reshape(1), n_tiles_max * tg


def _moe(xp, route_t, cnt, lw):
    t = xp.shape[1]
    pos0, pos1, tile_expert, n_tiles, p_rows = _route_plan(route_t, cnt, t)
    xs = _dispatch_rows(xp, pos0, pos1, p_rows)
    ys = _moe_ffn(xs, tile_expert, n_tiles, lw)
    return _collect_rows(ys, jnp.concatenate([pos0, pos1])).reshape(SC_SPLIT, 2, t, PLANE_W)


def _final_kernel(x_ref, y0_ref, y1_ref, rt_ref, g_ref, o_ref):
    x = _combine(x_ref[...], y0_ref, y1_ref, rt_ref[...], lead=(0,))
    o_ref[...] = _rms(x, g_ref[...])


def _final(x2d, y01, route, g):
    t = x2d.shape[0]
    tm = min(TOKEN_TILE, t)
    tok = lambda w: pl.BlockSpec((tm, w), lambda i: (i, 0))
    choice = lambda c: pl.BlockSpec((SC_SPLIT, 1, tm, PLANE_W), lambda i: (0, c, i, 0))
    return pl.pallas_call(
        _final_kernel,
        grid=(t // tm,),
        in_specs=[tok(D_MODEL), choice(0), choice(1), tok(LANES), pl.BlockSpec((1, D_MODEL), lambda i: (0, 0))],
        out_specs=tok(D_MODEL),
        out_shape=jax.ShapeDtypeStruct((t, D_MODEL), F32),
        compiler_params=pltpu.CompilerParams(dimension_semantics=("arbitrary",),
                                             vmem_limit_bytes=VMEM_LIMIT),
        name="final_norm",
    )(x2d, y01, y01, route, g)


def _constants(tm, tk):
    r = jnp.arange(tm)
    tril = (r[None, :] <= r[:, None]).astype(BF16)
    triu_strict = (r[:, None] < r[None, :]).astype(BF16)
    rk = jnp.arange(tk)
    u_incl = (rk[:, None] >= rk[None, :]).astype(BF16)
    nh = N_HEADS
    pq = jnp.zeros((3, LANES, nh * HEAD_SLAB), F32)
    pk = jnp.zeros((3, nh * HEAD_DIM, LANES), F32)
    qc = jnp.zeros((1, nh * HEAD_SLAB), F32)
    kc = jnp.zeros((nh * HEAD_DIM, LANES), F32)
    for part in range(3):
        for h in range(nh):
            pq = pq.at[part, h, h * HEAD_SLAB + HEAD_DIM + part].set(1.0)
            qc = qc.at[0, h * HEAD_SLAB + HEAD_DIM + 3 + part].set(1.0)
            kc = kc.at[h * HEAD_DIM + part, :].set(1.0)
            pk = pk.at[part, h * HEAD_DIM + 3 + part, h].set(1.0)
    vc = jnp.zeros((1, HEAD_SLAB), F32).at[0, HEAD_DIM].set(1.0)
    return {"tril": tril, "triu_strict": triu_strict, "u_incl": u_incl, "pq": pq.astype(BF16),
            "pk": pk.astype(BF16), "qc": qc, "kc": kc, "vc": vc}


def _layer_weights(layer, mix_norm_g, w_in, b_forget, conv_a_w, conf_dw_w, conf_dw_b, conf_ln_g, conf_ln_b,
                   w_branch, w_gate, b_gate, w_out, ffn_norm_g, w_router_group, w_router_expert,
                   w_expert_gate, w_expert_up, w_expert_down):
    w = w_in[layer]
    bw = BRANCH_WIDTH
    a_x, a_b, a_c, sb_q, sb_k, sb_v, fx_q, fx_k, fx_v = [w[:, i * bw:(i + 1) * bw] for i in range(9)]
    fx_f = w[:, 9 * bw:9 * bw + N_HEADS]
    conf = w[:, 9 * bw + N_HEADS:]
    scale = HEAD_DIM ** -0.5
    f_pad = jnp.pad(fx_f, ((0, 0), (0, LANES - N_HEADS)))
    w_main = jnp.concatenate([conf, f_pad, sb_q * scale, fx_q * scale, sb_v, fx_v, a_x, a_b, a_c], axis=1)
    w_t = jnp.concatenate([sb_k, fx_k], axis=1).T
    w_router = jnp.concatenate([w_router_group[layer], w_router_expert[layer].reshape(D_MODEL, N_EXPERTS)], axis=1)
    w_router = jnp.pad(w_router, ((0, 0), (0, LANES - N_GROUPS - N_EXPERTS)))
    return {
        "mix_g": mix_norm_g[layer].reshape(1, D_MODEL),
        "w_main": w_main.astype(BF16),
        "w_t": w_t.astype(BF16),
        "bf": jnp.pad(b_forget[layer], (0, LANES - N_HEADS)).reshape(1, LANES),
        "caw": conv_a_w[layer],
        "cdw": jnp.pad(conf_dw_w[layer], ((0, 1), (0, 0))),
        "cdb": conf_dw_b[layer].reshape(1, bw),
        "lng": conf_ln_g[layer].reshape(1, bw),
        "lnb": conf_ln_b[layer].reshape(1, bw),
        "w_gate": w_gate[layer].astype(BF16),
        "b_gate": b_gate[layer].reshape(4, 1, D_MODEL),
        "w_branch": w_branch[layer].astype(BF16),
        "w_out": w_out[layer].astype(BF16),
        "ffn_g": ffn_norm_g[layer].reshape(1, D_MODEL),
        "w_router": jnp.stack([w_router.T.astype(BF16), (w_router - w_router.astype(BF16).astype(F32)).T.astype(BF16)]),
        "w_gu": jnp.concatenate([w_expert_gate[layer], w_expert_up[layer]], axis=2).astype(BF16),
        "w_down": w_expert_down[layer].astype(BF16),
    }


def kernel(x, mix_norm_g, w_in, b_forget, conv_a_w, conf_dw_w, conf_dw_b, conf_ln_g, conf_ln_b, w_branch, w_gate,
           b_gate, w_out, ffn_norm_g, w_router_group, w_router_expert, w_expert_gate, w_expert_up, w_expert_down,
           final_norm_g):
    b, s, _ = x.shape
    t = b * s
    depth = w_in.shape[0]
    consts = _constants(min(TOKEN_TILE, s), min(ATTN_TILE, s))
    comb = None
    for layer in range(depth):
        lw = _layer_weights(layer, mix_norm_g, w_in, b_forget, conv_a_w, conf_dw_w, conf_dw_b, conf_ln_g,
                            conf_ln_b, w_branch, w_gate, b_gate, w_out, ffn_norm_g, w_router_group,
                            w_router_expert, w_expert_gate, w_expert_up, w_expert_down)
        outs = _mixer_in(x, comb, lw, consts)
        if comb is not None:
            x, *outs = outs
        ya, yc, q, kt, v = outs
        ysb = _attention("sb", q, kt, v, consts)
        yfx = _attention("fox", q, kt, v, consts)
        x2d, xp, route, route_t, cnt = _mixer_out(x.reshape(t, D_MODEL), ya.reshape(t, -1), ysb.reshape(t, -1),
                                                  yfx.reshape(t, -1), yc.reshape(t, -1), lw, consts)
        y01 = _moe(xp, route_t, cnt, lw)
        x = x2d.reshape(b, s, D_MODEL)
        comb = (y01.reshape(SC_SPLIT, 2, b, s, PLANE_W), route.reshape(b, s, LANES))
    out = _final(x.reshape(t, D_MODEL), comb[0].reshape(SC_SPLIT, 2, t, PLANE_W), comb[1].reshape(t, LANES),
                 final_norm_g.reshape(1, D_MODEL))
    return out.reshape(b, s, D_MODEL)
```

```python
import functools

import jax
import jax.numpy as jnp
from jax import lax
from jax.experimental import pallas as pl
from jax.experimental.pallas import tpu as pltpu
from jax.experimental.pallas import tpu_sc as plsc

F32 = jnp.float32
BF16 = jnp.bfloat16
I32 = jnp.int32

D_MODEL = 1024
BRANCH_WIDTH = 256
HEAD_DIM = 64
N_HEADS = 4
HEAD_SLAB = 128
CONV_A_WIDTH = 3
CONF_WIDTH = 31
N_GROUPS = 4
EXPERTS_PER_GROUP = 8
N_EXPERTS = 32
D_EXPERT = 256
EPS = 1e-6
LOG2E = 1.4426950408889634
LANES = 128
ROUTE_ROWS = 48
HALF_D = D_MODEL // 2
SC_SPLIT = 2
PLANE_W = HALF_D // SC_SPLIT

TOKEN_TILE = 512
ATTN_TILE = 256
FOX_Q_BLOCKS = 2
SB_HEADS_PER_STEP = 4
FOX_HEADS_PER_STEP = 4
ATTN_TRIPS = (4, 2, 1)
SB_GROUP = 4
GROUP_TILE = 512
CONV_ROW_CHUNK = 64
SC_WINDOW = 128
VMEM_LIMIT = 56 * 1024 * 1024

_C_CU, _C_CG = 0, 256
_C_F = 512
_N_FIRST = 640
_C_Q = 640
_C_V = 1152
_C_AX, _C_AB, _C_AC = 1664, 1920, 2176
_N_MAIN = 2432


def _rms(x, g):
    return x * lax.rsqrt(jnp.mean(x * x, axis=-1, keepdims=True) + EPS) * g


def _softplus(z):
    return jnp.maximum(z, 0.0) + jnp.log1p(jnp.exp(-jnp.abs(z)))


def _sigmoid(z):
    return 1.0 / (1.0 + jnp.exp(-z))


def _split3(v):
    hi = v.astype(BF16)
    r = v - hi.astype(F32)
    mid = r.astype(BF16)
    lo = (r - mid.astype(F32)).astype(BF16)
    return hi, mid, lo


def _pack_rows(v):
    lo = pltpu.bitcast(v[:, :HALF_D].astype(BF16).astype(F32), jnp.uint32)
    hi = pltpu.bitcast(v[:, HALF_D:].astype(BF16).astype(F32), jnp.uint32)
    return pltpu.bitcast((lo >> 16) | hi, I32)


def _unpack_rows(w):
    u = pltpu.bitcast(w, jnp.uint32)
    lo = pltpu.bitcast(u << 16, F32)
    hi = pltpu.bitcast(u & jnp.uint32(0xFFFF0000), F32)
    return lo, hi


def _store_planes(ref, v, lead=()):
    packed = _pack_rows(v)
    for k in range(SC_SPLIT):
        ref[(k, *lead)] = packed[:, k * PLANE_W:(k + 1) * PLANE_W]


def _load_planes(ref, lead=()):
    los, his = zip(*[_unpack_rows(ref[(k, *lead)]) for k in range(SC_SPLIT)])
    return list(los) + list(his)


def _combine(x, y0_ref, y1_ref, rt, lead=()):
    w0 = rt[:, 2:3]
    w1 = rt[:, 3:4]
    parts = [w0 * a + w1 * b for a, b in zip(_load_planes(y0_ref, lead), _load_planes(y1_ref, lead))]
    return x + jnp.concatenate(parts, axis=1)


def _mixer_in_kernel(combine, *refs):
    if combine:
        (x_ref, y0_ref, y1_ref, rt_ref, *refs) = refs
    else:
        (x_ref, *refs) = refs
    (g_ref, wm_ref, wt_ref, bf_ref, caw_ref, cdw_ref, cdb_ref, lng_ref, lnb_ref, tril_ref,
     pq_ref, qc_ref, pk_ref, kc_ref, vc_ref, *refs) = refs
    if combine:
        (xo_ref, *refs) = refs
    (ya_ref, yc_ref, q_ref, kt_ref, v_ref, bufa, bufc, dcarry) = refs

    tm = x_ref.shape[1]
    tk = kt_ref.shape[-1]

    @pl.when(pl.program_id(1) == 0)
    def _():
        bufa[0:8, :] = jnp.zeros((8, BRANCH_WIDTH), F32)
        bufc[0, 0:32, :] = jnp.zeros((32, BRANCH_WIDTH), F32)
        dcarry[...] = jnp.zeros_like(dcarry)

    x = x_ref[0]
    if combine:
        x = _combine(x, y0_ref, y1_ref, rt_ref[0], lead=(0, 0))
        xo_ref[0] = x
    xb = _rms(x, g_ref[...]).astype(BF16)

    p = jnp.concatenate([jnp.dot(xb, wm_ref[:, :_N_FIRST], preferred_element_type=F32),
                         jnp.dot(xb, wm_ref[:, _N_FIRST:], preferred_element_type=F32)], axis=1)
    pt = lax.dot_general(wt_ref[...], xb, (((1,), (1,)), ((), ())),
                         preferred_element_type=F32)

    ca = p[:, _C_AC:_C_AC + 256] * p[:, _C_AX:_C_AX + 256]
    bufa[8:8 + tm, :] = ca
    caw = caw_ref[...]
    conv = caw[0:1] * bufa[6:6 + tm, :] + caw[1:2] * bufa[7:7 + tm, :] + caw[2:3] * ca
    ya_ref[0] = (p[:, _C_AB:_C_AB + 256] * conv).astype(BF16)
    bufa[0:8, :] = ca[tm - 8:tm]

    u = p[:, _C_CU:_C_CU + 256] * _sigmoid(p[:, _C_CG:_C_CG + 256])
    bufc[0, 32:32 + tm, :] = u
    for r in range(1, 8):
        bufc[r, 0:tm + 24, :] = bufc[0, r:r + tm + 24, :]
    cdw = cdw_ref[...]
    cdb = cdb_ref[...]
    lng = lng_ref[...]
    lnb = lnb_ref[...]
    for c in range(tm // CONV_ROW_CHUNK):
        row0 = c * CONV_ROW_CHUNK
        acc = jnp.broadcast_to(cdb, (CONV_ROW_CHUNK, BRANCH_WIDTH))
        for k in range(CONF_WIDTH):
            off = 32 - (CONF_WIDTH - 1) + k
            acc = acc + cdw[k:k + 1] * bufc[off % 8, row0 + off - off % 8:row0 + off - off % 8 + CONV_ROW_CHUNK, :]
        mu = jnp.mean(acc, axis=-1, keepdims=True)
        cen = acc - mu
        var = jnp.mean(cen * cen, axis=-1, keepdims=True)
        yn = cen * lax.rsqrt(var + EPS) * lng + lnb
        yc_ref[0, row0:row0 + CONV_ROW_CHUNK, :] = (yn * _sigmoid(yn)).astype(BF16)
    bufc[0, 0:32, :] = u[tm - 32:tm]

    logf = -_softplus(-(p[:, _C_F:_C_F + LANES] + bf_ref[...]))
    tril = tril_ref[...]
    dcum = dcarry[0:1, :]
    for part in _split3(logf):
        dcum = dcum + jnp.dot(tril, part, preferred_element_type=F32)
    dcarry[0:1, :] = dcum[tm - 1:tm, :]
    dcum_t = dcum.T
    qh, qm, ql = _split3(dcum)
    kh, km, kl = _split3(-dcum_t)
    q_extra = (jnp.dot(qh, pq_ref[0], preferred_element_type=F32)
               + jnp.dot(qm, pq_ref[1], preferred_element_type=F32)
               + jnp.dot(ql, pq_ref[2], preferred_element_type=F32) + qc_ref[...])
    k_extra = (jnp.dot(pk_ref[0], kh, preferred_element_type=F32)
               + jnp.dot(pk_ref[1], km, preferred_element_type=F32)
               + jnp.dot(pk_ref[2], kl, preferred_element_type=F32)
               + jnp.concatenate([kc_ref[...]] * (tm // LANES), axis=1))

    lane = lax.broadcasted_iota(I32, (tm, HEAD_SLAB), 1)
    low = lane < HEAD_DIM
    vc = vc_ref[...]
    for hd in range(2 * N_HEADS):
        is_fox = hd >= N_HEADS
        pair = (hd // 2) * HEAD_SLAB
        qs = p[:, _C_Q + pair:_C_Q + pair + HEAD_SLAB]
        vs = p[:, _C_V + pair:_C_V + pair + HEAD_SLAB]
        if hd % 2:
            qs = pltpu.roll(qs, HEAD_DIM, axis=1)
            vs = pltpu.roll(vs, HEAD_DIM, axis=1)
        if is_fox:
            hf = hd - N_HEADS
            qx = q_extra[:, hf * HEAD_SLAB:(hf + 1) * HEAD_SLAB]
            kx = k_extra[hf * HEAD_DIM:(hf + 1) * HEAD_DIM, :]
            vx = vc
        else:
            qx = 0.0
            kx = jnp.zeros((HEAD_DIM, tm), F32)
            vx = 0.0
        q_ref[0, hd] = jnp.where(low, qs, qx).astype(BF16)
        v_ref[0, hd] = jnp.where(low, vs, vx).astype(BF16)
        kfull = jnp.concatenate([pt[hd * HEAD_DIM:(hd + 1) * HEAD_DIM, :], kx], axis=0).astype(BF16)
        for c in range(tm // tk):
            kt_ref[0, hd, c] = kfull[:, c * tk:(c + 1) * tk]


def _mixer_in(x, comb, lw, consts):
    b, s, _ = x.shape
    tm = min(TOKEN_TILE, s)
    tk = min(ATTN_TILE, s)
    nk = s // tk
    nh2 = 2 * N_HEADS
    combine = comb is not None

    def full(a):
        return pl.BlockSpec(a.shape, lambda bi, si, _n=a.ndim: (0,) * _n)

    tok = lambda w: pl.BlockSpec((1, tm, w), lambda bi, si: (bi, si, 0))
    in_arrays = [x]
    in_specs = [tok(D_MODEL)]
    if combine:
        y01, route = comb
        in_arrays += [y01, y01, route]
        in_specs += [pl.BlockSpec((SC_SPLIT, 1, 1, tm, PLANE_W), lambda bi, si, _c=c: (0, _c, bi, si, 0))
                     for c in range(2)] + [tok(LANES)]
    weights = [lw["mix_g"], lw["w_main"], lw["w_t"], lw["bf"], lw["caw"], lw["cdw"], lw["cdb"], lw["lng"],
               lw["lnb"], consts["tril"], consts["pq"], consts["qc"], consts["pk"], consts["kc"], consts["vc"]]
    in_arrays += weights
    in_specs += [full(a) for a in weights]

    out_shape = []
    out_specs = []
    if combine:
        out_shape.append(jax.ShapeDtypeStruct((b, s, D_MODEL), F32))
        out_specs.append(tok(D_MODEL))
    out_shape += [
        jax.ShapeDtypeStruct((b, s, BRANCH_WIDTH), BF16),
        jax.ShapeDtypeStruct((b, s, BRANCH_WIDTH), BF16),
        jax.ShapeDtypeStruct((b, nh2, s, HEAD_SLAB), BF16),
        jax.ShapeDtypeStruct((b, nh2, nk, HEAD_SLAB, tk), BF16),
        jax.ShapeDtypeStruct((b, nh2, s, HEAD_SLAB), BF16),
    ]
    out_specs += [
        tok(BRANCH_WIDTH), tok(BRANCH_WIDTH),
        pl.BlockSpec((1, nh2, tm, HEAD_SLAB), lambda bi, si: (bi, 0, si, 0)),
        pl.BlockSpec((1, nh2, tm // tk, HEAD_SLAB, tk), lambda bi, si: (bi, 0, si, 0, 0)),
        pl.BlockSpec((1, nh2, tm, HEAD_SLAB), lambda bi, si: (bi, 0, si, 0)),
    ]
    return pl.pallas_call(
        functools.partial(_mixer_in_kernel, combine),
        grid=(b, s // tm),
        in_specs=in_specs,
        out_specs=out_specs,
        out_shape=out_shape,
        scratch_shapes=[pltpu.VMEM((8 + tm, BRANCH_WIDTH), F32),
                        pltpu.VMEM((8, 32 + tm, BRANCH_WIDTH), F32),
                        pltpu.VMEM((8, LANES), F32)],
        compiler_params=pltpu.CompilerParams(dimension_semantics=("arbitrary", "arbitrary"),
                                             vmem_limit_bytes=VMEM_LIMIT),
        name="mixer_in",
    )(*in_arrays)


def _pair_out(accs):
    lane = lax.broadcasted_iota(I32, accs[0].shape, 1)
    return jnp.where(lane < HEAD_DIM, accs[0], pltpu.roll(accs[1], HEAD_DIM, axis=1))


def _sb_attn_kernel(q_ref, kt_ref, v_ref, u_ref, o_ref):
    tq = q_ref.shape[2]
    nk, tk = kt_ref.shape[2], kt_ref.shape[-1]
    i = pl.program_id(2)
    umat = u_ref[...]
    nh = q_ref.shape[1]
    nb = SB_GROUP
    qs = [q_ref[0, h] for h in range(nh)]
    row = lax.broadcasted_iota(I32, (tq, tk), 0)
    col = lax.broadcasted_iota(I32, (tq, tk), 1)

    def weights(g, tails, masked):
        ws, new_tails = [], []
        for h in range(nh):
            tail = tails[h]
            wh = []
            for u in reversed(range(nb)):
                jr = g * nb + u
                z = jnp.dot(qs[h], kt_ref[0, h, jnp.minimum(jr, nk - 1)], preferred_element_type=F32)
                sp = jnp.maximum(z, 0.0) + jnp.log(1.0 + jnp.exp2(jnp.abs(z) * -LOG2E))
                if masked:
                    mask = col + (jr - i) * tk < row
                    sp = jnp.where(mask, sp, 0.0)
                later = jnp.dot(sp.astype(BF16), umat, preferred_element_type=F32)
                w = jnp.exp((z - later - tail).astype(BF16))
                if masked:
                    w = jnp.where(mask, w, jnp.zeros_like(w))
                wh.append(w)
                tail = tail + later[:, 0:1]
            ws.append(tuple(reversed(wh)))
            new_tails.append(tail)
        return tuple(ws), tuple(new_tails)

    def apply(g, accs, ws):
        out = []
        for h in range(nh):
            acc = accs[h]
            for u in range(nb):
                start = pl.multiple_of(jnp.minimum(g * nb + u, nk - 1) * tk, tk)
                acc = acc + jnp.dot(ws[h][u], v_ref[0, h, pl.ds(start, tk), :], preferred_element_type=F32)
            out.append(acc)
        return tuple(out)

    last = i // nb
    zero_t = tuple(jnp.zeros((tq, 1), F32) for _ in range(nh))
    accs = tuple(jnp.zeros((tq, HEAD_SLAB), F32) for _ in range(nh))
    ws, tails = weights(last, zero_t, True)

    def body(t, carry):
        accs, tails, ws = carry
        g = last - 1 - t
        accs = apply(g + 1, accs, ws)
        ws, tails = weights(g, tails, False)
        return accs, tails, ws

    accs, tails, ws = lax.fori_loop(0, last, body, (accs, tails, ws))
    accs = apply(0, accs, ws)
    o_ref[0] = jnp.concatenate([_pair_out(accs[k:k + 2]) for k in range(0, nh, 2)], axis=1).astype(BF16)


def _fox_attn_kernel(q_ref, kt_ref, v_ref, o_ref):
    tq = q_ref.shape[2]
    tk = kt_ref.shape[-1]
    i = pl.program_id(2)
    row = lax.broadcasted_iota(I32, (tq, tk), 0)
    col = lax.broadcasted_iota(I32, (tq, tk), 1)
    nh = q_ref.shape[1]
    qs = [q_ref[0, h] for h in range(nh)]

    def vblock(h, j):
        return v_ref[0, h, pl.ds(pl.multiple_of(j * tk, tk), tk), :]

    r = tq // tk
    accs, maxes = [], []
    for h in range(nh):
        ss = [jnp.where(col + d * tk <= row, jnp.dot(qs[h], kt_ref[0, h, i * r + d], preferred_element_type=F32),
                        -jnp.inf) for d in range(r)]
        m = jnp.max(ss[0], axis=-1, keepdims=True)
        for s in ss[1:]:
            m = jnp.maximum(m, jnp.max(s, axis=-1, keepdims=True))
        acc = None
        for d, s in enumerate(ss):
            term = jnp.dot(jnp.exp(s - m).astype(BF16), vblock(h, i * r + d), preferred_element_type=F32)
            acc = term if acc is None else acc + term
        accs.append(acc)
        maxes.append(m)

    def make_body(n_blocks, first):
        def body(jj, carry):
            accs, maxes = carry
            new_accs, new_maxes = [], []
            for h in range(nh):
                js = [first + n_blocks * jj + u for u in range(n_blocks)]
                ss = [jnp.dot(qs[h], kt_ref[0, h, j], preferred_element_type=F32) for j in js]
                m = maxes[h]
                for s in ss:
                    m = jnp.maximum(m, jnp.max(s, axis=-1, keepdims=True))
                acc = jnp.exp(maxes[h] - m) * accs[h]
                for s, j in zip(ss, js):
                    acc = acc + jnp.dot(jnp.exp(s - m).astype(BF16), vblock(h, j), preferred_element_type=F32)
                new_accs.append(acc)
                new_maxes.append(m)
            return tuple(new_accs), tuple(new_maxes)
        return body

    carry = (tuple(accs), tuple(maxes))
    done = 0
    for n_blocks in ATTN_TRIPS:
        trips = (i * r - done) // n_blocks
        carry = lax.fori_loop(0, trips, make_body(n_blocks, done), carry)
        done = done + trips * n_blocks
    accs = carry[0]
    outs = [a / a[:, HEAD_DIM:HEAD_DIM + 1] for a in accs]
    o_ref[0] = jnp.concatenate([_pair_out(outs[k:k + 2]) for k in range(0, nh, 2)], axis=1).astype(BF16)


def _attention(kind, q, kt, v, consts):
    b, _, s, _ = q.shape
    nk, tk = kt.shape[2], kt.shape[4]
    tq = tk if kind == "sb" else min(FOX_Q_BLOCKS * tk, s)
    hps = SB_HEADS_PER_STEP if kind == "sb" else FOX_HEADS_PER_STEP
    head0 = 0 if kind == "sb" else N_HEADS // hps
    in_specs = [
        pl.BlockSpec((1, hps, tq, HEAD_SLAB), lambda bi, hp, i: (bi, hp + head0, i, 0)),
        pl.BlockSpec((1, hps, nk, HEAD_SLAB, tk), lambda bi, hp, i: (bi, hp + head0, 0, 0, 0)),
        pl.BlockSpec((1, hps, s, HEAD_SLAB), lambda bi, hp, i: (bi, hp + head0, 0, 0)),
    ]
    args = [q, kt, v]
    if kind == "sb":
        in_specs.append(pl.BlockSpec((tk, tk), lambda bi, hp, i: (0, 0)))
        args.append(consts["u_incl"])
        body = _sb_attn_kernel
    else:
        body = _fox_attn_kernel
    return pl.pallas_call(
        body,
        grid=(b, N_HEADS // hps, s // tq),
        in_specs=in_specs,
        out_specs=pl.BlockSpec((1, tq, hps * HEAD_DIM), lambda bi, hp, i: (bi, i, hp)),
        out_shape=jax.ShapeDtypeStruct((b, s, BRANCH_WIDTH), BF16),
        compiler_params=pltpu.CompilerParams(dimension_semantics=("arbitrary", "arbitrary", "arbitrary"),
                                             vmem_limit_bytes=VMEM_LIMIT),
        name=kind + "_attn",
    )(*args)


def _mixer_out_kernel(x_ref, ya_ref, ysb_ref, yfx_ref, yc_ref, g_ref, wg_ref, bg_ref, wb_ref, wo_ref,
                      fg_ref, wr_ref, tru_ref, xo_ref, xp_ref, rt_ref, rtt_ref, cnt_ref, carry):
    tm = x_ref.shape[0]

    @pl.when(pl.program_id(0) == 0)
    def _():
        carry[...] = jnp.zeros_like(carry)

    x = x_ref[...]
    xb = _rms(x, g_ref[...]).astype(BF16)
    h = None
    for g, y_ref in enumerate((ya_ref, ysb_ref, yfx_ref, yc_ref)):
        gate = _sigmoid(jnp.dot(xb, wg_ref[g], preferred_element_type=F32) + bg_ref[g])
        term = gate * jnp.dot(y_ref[...], wb_ref[g], preferred_element_type=F32)
        h = term if h is None else h + term
    xo = x + jnp.dot(h.astype(BF16), wo_ref[...], preferred_element_type=F32)
    xo_ref[...] = xo
    xn = _rms(xo, fg_ref[...])
    _store_planes(xp_ref, xn)

    xh = xn.astype(BF16)
    xl = (xn - xh.astype(F32)).astype(BF16)
    nt = (((1,), (1,)), ((), ()))
    logits = (lax.dot_general(wr_ref[0], xh, nt, preferred_element_type=F32)
              + lax.dot_general(wr_ref[0], xl, nt, preferred_element_type=F32)
              + lax.dot_general(wr_ref[1], xh, nt, preferred_element_type=F32))[0:ROUTE_ROWS, :]
    row = lax.broadcasted_iota(I32, (ROUTE_ROWS, tm), 0).astype(F32)
    ninf = -jnp.inf
    big = float(LANES)
    gl = jnp.where(row < N_GROUPS, logits, ninf)
    gmax = jnp.max(gl, axis=0, keepdims=True)
    gidx = jnp.min(jnp.where(gl == gmax, row, big), axis=0, keepdims=True)
    p_group = 1.0 / jnp.sum(jnp.exp(gl - gmax), axis=0, keepdims=True)
    first = N_GROUPS + EXPERTS_PER_GROUP * gidx
    el = jnp.where((row >= first) & (row < first + EXPERTS_PER_GROUP), logits, ninf)
    m1 = jnp.max(el, axis=0, keepdims=True)
    i1 = jnp.min(jnp.where(el == m1, row, big), axis=0, keepdims=True)
    el2 = jnp.where(row == i1, ninf, el)
    m2 = jnp.max(el2, axis=0, keepdims=True)
    i2 = jnp.min(jnp.where(el2 == m2, row, big), axis=0, keepdims=True)
    e2 = jnp.exp(m2 - m1)
    w1 = p_group / (1.0 + e2)
    w2 = w1 * e2

    sel1 = row == i1
    sel2 = row == i2
    onehot = jnp.where(sel1, 1.0, jnp.where(sel2, 1.0, 0.0))
    before = jnp.dot(onehot.astype(BF16), tru_ref[...], preferred_element_type=F32) + carry[:, 0:1]
    r1 = jnp.sum(jnp.where(sel1, before, 0.0), axis=0, keepdims=True)
    r2 = jnp.sum(jnp.where(sel2, before, 0.0), axis=0, keepdims=True)
    total = carry[...] + jnp.sum(onehot, axis=1, keepdims=True)
    carry[...] = total
    cnt_ref[...] = total

    row8 = lax.broadcasted_iota(I32, (8, tm), 0)
    fields = (i1 - N_GROUPS, i2 - N_GROUPS, w1, w2, r1, r2)
    rtt = jnp.zeros((8, tm), F32)
    for k, f in enumerate(fields):
        rtt = jnp.where(row8 == k, f, rtt)
    rtt_ref[...] = rtt
    rt_ref[...] = jnp.concatenate([rtt, jnp.zeros((LANES - 8, tm), F32)], axis=0).T


def _mixer_out(x2d, ya, ysb, yfx, yc, lw, consts):
    t = x2d.shape[0]
    tm = min(TOKEN_TILE, t)

    def full(a):
        return pl.BlockSpec(a.shape, lambda i, _n=a.ndim: (0,) * _n)

    tok = lambda w: pl.BlockSpec((tm, w), lambda i: (i, 0))
    weights = [lw["mix_g"], lw["w_gate"], lw["b_gate"], lw["w_branch"], lw["w_out"], lw["ffn_g"], lw["w_router"],
               consts["triu_strict"]]
    return pl.pallas_call(
        _mixer_out_kernel,
        grid=(t // tm,),
        in_specs=[tok(D_MODEL)] + [tok(BRANCH_WIDTH)] * 4 + [full(a) for a in weights],
        out_specs=[tok(D_MODEL), pl.BlockSpec((SC_SPLIT, tm, PLANE_W), lambda i: (0, i, 0)), tok(LANES),
                   pl.BlockSpec((8, tm), lambda i: (0, i)), pl.BlockSpec((ROUTE_ROWS, LANES), lambda i: (0, 0))],
        out_shape=[jax.ShapeDtypeStruct((t, D_MODEL), F32),
                   jax.ShapeDtypeStruct((SC_SPLIT, t, PLANE_W), I32),
                   jax.ShapeDtypeStruct((t, LANES), F32),
                   jax.ShapeDtypeStruct((8, t), F32),
                   jax.ShapeDtypeStruct((ROUTE_ROWS, LANES), F32)],
        scratch_shapes=[pltpu.VMEM((ROUTE_ROWS, LANES), F32)],
        compiler_params=pltpu.CompilerParams(dimension_semantics=("arbitrary",),
                                             vmem_limit_bytes=VMEM_LIMIT),
        name="mixer_out",
    )(x2d, ya, ysb, yfx, yc, *weights)


def _sc_mesh():
    return plsc.VectorSubcoreMesh(core_axis_name="core", subcore_axis_name="subcore")


def _plane_index(idx, rows_per_plane):
    return jnp.concatenate([idx + k * rows_per_plane for k in range(SC_SPLIT)]).reshape(1, -1)


def _dispatch_rows(planes, pos0, pos1, n_out):
    w = planes.shape[2]
    rows = planes.reshape(-1, w)
    t = rows.shape[0]
    idx0 = _plane_index(pos0, n_out)
    idx1 = _plane_index(pos1, n_out)

    @functools.partial(pl.kernel, out_type=jax.ShapeDtypeStruct((SC_SPLIT * n_out, w), rows.dtype),
                       mesh=_sc_mesh(), scratch_types=[])
    def scatter_kernel(x_hbm, i0_hbm, i1_hbm, o_hbm):
        def body(x_vmem, i0_vmem, i1_vmem):
            pltpu.sync_copy(x_vmem, o_hbm.at[i0_vmem.at[0]])
            pltpu.sync_copy(x_vmem, o_hbm.at[i1_vmem.at[0]])

        pltpu.emit_pipeline(
            body,
            grid=(t // SC_WINDOW,),
            in_specs=[pl.BlockSpec((SC_WINDOW, w), lambda i: (i, 0)),
                      pl.BlockSpec((1, SC_WINDOW), lambda i: (0, i)),
                      pl.BlockSpec((1, SC_WINDOW), lambda i: (0, i))],
            out_specs=[],
            core_axis_name=("core", "subcore"),
            dimension_semantics=(pltpu.PARALLEL,),
        )(x_hbm, i0_hbm, i1_hbm)

    return scatter_kernel(rows, idx0, idx1).reshape(SC_SPLIT, n_out, w)


def _collect_rows(planes, idx):
    n, w = planes.shape[1:]
    table = planes.reshape(-1, w)
    idx2 = _plane_index(idx, n)
    m = idx2.shape[1]

    @functools.partial(pl.kernel, out_type=jax.ShapeDtypeStruct((m, w), table.dtype), mesh=_sc_mesh(),
                       scratch_types=[])
    def gather_kernel(x_hbm, i_hbm, o_hbm):
        def body(i_vmem, o_vmem):
            pltpu.sync_copy(x_hbm.at[i_vmem.at[0]], o_vmem)

        pltpu.emit_pipeline(
            body,
            grid=(m // SC_WINDOW,),
            in_specs=[pl.BlockSpec((1, SC_WINDOW), lambda i: (0, i))],
            out_specs=[pl.BlockSpec((SC_WINDOW, w), lambda i: (i, 0))],
            core_axis_name=("core", "subcore"),
            dimension_semantics=(pltpu.PARALLEL,),
        )(i_hbm, o_hbm)

    return gather_kernel(table, idx2).reshape(SC_SPLIT, -1, w)


def _moe_ffn_kernel(te_ref, nt_ref, xs_ref, wgu_ref, wd_ref, ys_ref):
    i = pl.program_id(0)

    @pl.when(i < nt_ref[0])
    def _():
        gu = None
        for k, part in enumerate(_load_planes(xs_ref)):
            term = jnp.dot(part.astype(BF16), wgu_ref[0, k * PLANE_W:(k + 1) * PLANE_W, :],
                           preferred_element_type=F32)
            gu = term if gu is None else gu + term
        gate = gu[:, :D_EXPERT]
        hid = gate * _sigmoid(gate) * gu[:, D_EXPERT:]
        _store_planes(ys_ref, jnp.dot(hid.astype(BF16), wd_ref[0], preferred_element_type=F32))

    @pl.when(i >= nt_ref[0])
    def _():
        ys_ref[...] = jnp.zeros_like(ys_ref)


def _moe_ffn(xs, tile_expert, n_tiles, lw):
    p = xs.shape[1]
    tg = GROUP_TILE
    rows = pl.BlockSpec((SC_SPLIT, tg, PLANE_W), lambda i, te, nt: (0, i, 0))
    grid_spec = pltpu.PrefetchScalarGridSpec(
        num_scalar_prefetch=2,
        grid=(p // tg,),
        in_specs=[rows,
                  pl.BlockSpec((1, D_MODEL, 2 * D_EXPERT), lambda i, te, nt: (te[i], 0, 0)),
                  pl.BlockSpec((1, D_EXPERT, D_MODEL), lambda i, te, nt: (te[i], 0, 0))],
        out_specs=rows,
    )
    return pl.pallas_call(
        _moe_ffn_kernel,
        grid_spec=grid_spec,
        out_shape=jax.ShapeDtypeStruct((SC_SPLIT, p, PLANE_W), I32),
        compiler_params=pltpu.CompilerParams(dimension_semantics=("arbitrary",),
                                             vmem_limit_bytes=VMEM_LIMIT),
        name="moe_ffn",
    )(tile_expert, n_tiles, xs, lw["w_gu"], lw["w_down"])


def _route_plan(route_t, cnt, t):
    tg = GROUP_TILE
    n_tiles_max = (2 * t) // tg + N_EXPERTS
    counts = cnt[N_GROUPS:N_GROUPS + N_EXPERTS, 0].astype(I32)
    padded = ((counts + tg - 1) // tg) * tg
    ends = jnp.cumsum(padded)
    offs = ends - padded
    experts = jnp.arange(N_EXPERTS, dtype=I32)[:, None]

    def first_row(e):
        return jnp.sum(jnp.where(e[None, :] == experts, offs[:, None], 0), axis=0)

    fields = route_t.astype(I32)
    pos0 = first_row(fields[0]) + fields[4]
    pos1 = first_row(fields[1]) + fields[5]
    tile_start = jnp.arange(n_tiles_max, dtype=I32) * tg
    n_tiles = ends[-1] // tg
    tile_clamped = jnp.minimum(tile_start, jnp.maximum(n_tiles - 1, 0) * tg)
    tile_expert = jnp.sum((ends[None, :] <= tile_clamped[:, None]).astype(I32), axis=1)
    tile_expert = jnp.minimum(tile_expert, N_EXPERTS - 1)
    return pos0, pos1, tile_expert, n_tiles.reshape(1), n_tiles_max * tg


def _moe(xp, route_t, cnt, lw):
    t = xp.shape[1]
    pos0, pos1, tile_expert, n_tiles, p_rows = _route_plan(route_t, cnt, t)
    xs = _dispatch_rows(xp, pos0, pos1, p_rows)
    ys = _moe_ffn(xs, tile_expert, n_tiles, lw)
    return _collect_rows(ys, jnp.concatenate([pos0, pos1])).reshape(SC_SPLIT, 2, t, PLANE_W)


def _final_kernel(x_ref, y0_ref, y1_ref, rt_ref, g_ref, o_ref):
    x = _combine(x_ref[...], y0_ref, y1_ref, rt_ref[...], lead=(0,))
    o_ref[...] = _rms(x, g_ref[...])


def _final(x2d, y01, route, g):
    t = x2d.shape[0]
    tm = min(TOKEN_TILE, t)
    tok = lambda w: pl.BlockSpec((tm, w), lambda i: (i, 0))
    choice = lambda c: pl.BlockSpec((SC_SPLIT, 1, tm, PLANE_W), lambda i: (0, c, i, 0))
    return pl.pallas_call(
        _final_kernel,
        grid=(t // tm,),
        in_specs=[tok(D_MODEL), choice(0), choice(1), tok(LANES), pl.BlockSpec((1, D_MODEL), lambda i: (0, 0))],
        out_specs=tok(D_MODEL),
        out_shape=jax.ShapeDtypeStruct((t, D_MODEL), F32),
        compiler_params=pltpu.CompilerParams(dimension_semantics=("arbitrary",),
                                             vmem_limit_bytes=VMEM_LIMIT),
        name="final_norm",
    )(x2d, y01, y01, route, g)


def _constants(tm, tk):
    r = jnp.arange(tm)
    tril = (r[None, :] <= r[:, None]).astype(BF16)
    triu_strict = (r[:, None] < r[None, :]).astype(BF16)
    rk = jnp.arange(tk)
    u_incl = (rk[:, None] >= rk[None, :]).astype(BF16)
    nh = N_HEADS
    pq = jnp.zeros((3, LANES, nh * HEAD_SLAB), F32)
    pk = jnp.zeros((3, nh * HEAD_DIM, LANES), F32)
    qc = jnp.zeros((1, nh * HEAD_SLAB), F32)
    kc = jnp.zeros((nh * HEAD_DIM, LANES), F32)
    for part in range(3):
        for h in range(nh):
            pq = pq.at[part, h, h * HEAD_SLAB + HEAD_DIM + part].set(1.0)
            qc = qc.at[0, h * HEAD_SLAB + HEAD_DIM + 3 + part].set(1.0)
            kc = kc.at[h * HEAD_DIM + part, :].set(1.0)
            pk = pk.at[part, h * HEAD_DIM + 3 + part, h].set(1.0)
    vc = jnp.zeros((1, HEAD_SLAB), F32).at[0, HEAD_DIM].set(1.0)
    return {"tril": tril, "triu_strict": triu_strict, "u_incl": u_incl, "pq": pq.astype(BF16),
            "pk": pk.astype(BF16), "qc": qc, "kc": kc, "vc": vc}


def _layer_weights(layer, mix_norm_g, w_in, b_forget, conv_a_w, conf_dw_w, conf_dw_b, conf_ln_g, conf_ln_b,
                   w_branch, w_gate, b_gate, w_out, ffn_norm_g, w_router_group, w_router_expert,
                   w_expert_gate, w_expert_up, w_expert_down):
    w = w_in[layer]
    bw = BRANCH_WIDTH
    a_x, a_b, a_c, sb_q, sb_k, sb_v, fx_q, fx_k, fx_v = [w[:, i * bw:(i + 1) * bw] for i in range(9)]
    fx_f = w[:, 9 * bw:9 * bw + N_HEADS]
    conf = w[:, 9 * bw + N_HEADS:]
    scale = HEAD_DIM ** -0.5
    f_pad = jnp.pad(fx_f, ((0, 0), (0, LANES - N_HEADS)))
    w_main = jnp.concatenate([conf, f_pad, sb_q * scale, fx_q * scale, sb_v, fx_v, a_x, a_b, a_c], axis=1)
    w_t = jnp.concatenate([sb_k, fx_k], axis=1).T
    w_router = jnp.concatenate([w_router_group[layer], w_router_expert[layer].reshape(D_MODEL, N_EXPERTS)], axis=1)
    w_router = jnp.pad(w_router, ((0, 0), (0, LANES - N_GROUPS - N_EXPERTS)))
    return {
        "mix_g": mix_norm_g[layer].reshape(1, D_MODEL),
        "w_main": w_main.astype(BF16),
        "w_t": w_t.astype(BF16),
        "bf": jnp.pad(b_forget[layer], (0, LANES - N_HEADS)).reshape(1, LANES),
        "caw": conv_a_w[layer],
        "cdw": jnp.pad(conf_dw_w[layer], ((0, 1), (0, 0))),
        "cdb": conf_dw_b[layer].reshape(1, bw),
        "lng": conf_ln_g[layer].reshape(1, bw),
        "lnb": conf_ln_b[layer].reshape(1, bw),
        "w_gate": w_gate[layer].astype(BF16),
        "b_gate": b_gate[layer].reshape(4, 1, D_MODEL),
        "w_branch": w_branch[layer].astype(BF16),
        "w_out": w_out[layer].astype(BF16),
        "ffn_g": ffn_norm_g[layer].reshape(1, D_MODEL),
        "w_router": jnp.stack([w_router.T.astype(BF16), (w_router - w_router.astype(BF16).astype(F32)).T.astype(BF16)]),
        "w_gu": jnp.concatenate([w_expert_gate[layer], w_expert_up[layer]], axis=2).astype(BF16),
        "w_down": w_expert_down[layer].astype(BF16),
    }


def kernel(x, mix_norm_g, w_in, b_forget, conv_a_w, conf_dw_w, conf_dw_b, conf_ln_g, conf_ln_b, w_branch, w_gate,
           b_gate, w_out, ffn_norm_g, w_router_group, w_router_expert, w_expert_gate, w_expert_up, w_expert_down,
           final_norm_g):
    b, s, _ = x.shape
    t = b * s
    depth = w_in.shape[0]
    consts = _constants(min(TOKEN_TILE, s), min(ATTN_TILE, s))
    comb = None
    for layer in range(depth):
        lw = _layer_weights(layer, mix_norm_g, w_in, b_forget, conv_a_w, conf_dw_w, conf_dw_b, conf_ln_g,
                            conf_ln_b, w_branch, w_gate, b_gate, w_out, ffn_norm_g, w_router_group,
                            w_router_expert, w_expert_gate, w_expert_up, w_expert_down)
        outs = _mixer_in(x, comb, lw, consts)
        if comb is not None:
            x, *outs = outs
        ya, yc, q, kt, v = outs
        ysb = _attention("sb", q, kt, v, consts)
        yfx = _attention("fox", q, kt, v, consts)
        x2d, xp, route, route_t, cnt = _mixer_out(x.reshape(t, D_MODEL), ya.reshape(t, -1), ysb.reshape(t, -1),
                                                  yfx.reshape(t, -1), yc.reshape(t, -1), lw, consts)
        y01 = _moe(xp, route_t, cnt, lw)
        x = x2d.reshape(b, s, D_MODEL)
        comb = (y01.reshape(SC_SPLIT, 2, b, s, PLANE_W), route.reshape(b, s, LANES))
    out = _final(x.reshape(t, D_MODEL), comb[0].reshape(SC_SPLIT, 2, t, PLANE_W), comb[1].reshape(t, LANES),
                 final_norm_g.reshape(1, D_MODEL))
    return out.reshape(b, s, D_MODEL)
```

```python
import functools

import jax
import jax.numpy as jnp
from jax import lax
from jax.experimental import pallas as pl
from jax.experimental.pallas import tpu as pltpu
from jax.experimental.pallas import tpu_sc as plsc

F32 = jnp.float32
BF16 = jnp.bfloat16
I32 = jnp.int32

D_MODEL = 1024
BRANCH_WIDTH = 256
HEAD_DIM = 64
N_HEADS = 4
HEAD_SLAB = 128
CONV_A_WIDTH = 3
CONF_WIDTH = 31
N_GROUPS = 4
EXPERTS_PER_GROUP = 8
N_EXPERTS = 32
D_EXPERT = 256
EPS = 1e-6
LOG2E = 1.4426950408889634
LANES = 128
ROUTE_ROWS = 48
HALF_D = D_MODEL // 2
SC_SPLIT = 2
PLANE_W = HALF_D // SC_SPLIT

TOKEN_TILE = 512
ATTN_TILE = 256
FOX_Q_BLOCKS = 2
SB_HEADS_PER_STEP = 4
FOX_HEADS_PER_STEP = 4
ATTN_TRIPS = (4, 2, 1)
SB_GROUP = 4
GROUP_TILE = 512
CONV_ROW_CHUNK = 64
SC_WINDOW = 128
BATCH_SPLIT = 2
VMEM_LIMIT = 56 * 1024 * 1024

_C_CU, _C_CG = 0, 256
_C_F = 512
_N_FIRST = 640
_C_Q = 640
_C_V = 1152
_C_AX, _C_AB, _C_AC = 1664, 1920, 2176
_N_MAIN = 2432


def _rms(x, g):
    return x * lax.rsqrt(jnp.mean(x * x, axis=-1, keepdims=True) + EPS) * g


def _softplus(z):
    return jnp.maximum(z, 0.0) + jnp.log1p(jnp.exp(-jnp.abs(z)))


def _sigmoid(z):
    return 1.0 / (1.0 + jnp.exp(-z))


def _split3(v):
    hi = v.astype(BF16)
    r = v - hi.astype(F32)
    mid = r.astype(BF16)
    lo = (r - mid.astype(F32)).astype(BF16)
    return hi, mid, lo


def _pack_rows(v):
    lo = pltpu.bitcast(v[:, :HALF_D].astype(BF16).astype(F32), jnp.uint32)
    hi = pltpu.bitcast(v[:, HALF_D:].astype(BF16).astype(F32), jnp.uint32)
    return pltpu.bitcast((lo >> 16) | hi, I32)


def _unpack_rows(w):
    u = pltpu.bitcast(w, jnp.uint32)
    lo = pltpu.bitcast(u << 16, F32)
    hi = pltpu.bitcast(u & jnp.uint32(0xFFFF0000), F32)
    return lo, hi


def _store_planes(ref, v, lead=()):
    packed = _pack_rows(v)
    for k in range(SC_SPLIT):
        ref[(k, *lead)] = packed[:, k * PLANE_W:(k + 1) * PLANE_W]


def _load_planes(ref, lead=()):
    los, his = zip(*[_unpack_rows(ref[(k, *lead)]) for k in range(SC_SPLIT)])
    return list(los) + list(his)


def _combine(x, y0_ref, y1_ref, rt, lead=()):
    w0 = rt[:, 2:3]
    w1 = rt[:, 3:4]
    parts = [w0 * a + w1 * b for a, b in zip(_load_planes(y0_ref, lead), _load_planes(y1_ref, lead))]
    return x + jnp.concatenate(parts, axis=1)


def _mixer_in_kernel(combine, *refs):
    if combine:
        (x_ref, y0_ref, y1_ref, rt_ref, *refs) = refs
    else:
        (x_ref, *refs) = refs
    (g_ref, wm_ref, wt_ref, bf_ref, caw_ref, cdw_ref, cdb_ref, lng_ref, lnb_ref, tril_ref,
     pq_ref, qc_ref, pk_ref, kc_ref, vc_ref, *refs) = refs
    if combine:
        (xo_ref, *refs) = refs
    (ya_ref, yc_ref, q_ref, kt_ref, v_ref, bufa, bufc, dcarry) = refs

    tm = x_ref.shape[1]
    tk = kt_ref.shape[-1]

    @pl.when(pl.program_id(1) == 0)
    def _():
        bufa[0:8, :] = jnp.zeros((8, BRANCH_WIDTH), F32)
        bufc[0, 0:32, :] = jnp.zeros((32, BRANCH_WIDTH), F32)
        dcarry[...] = jnp.zeros_like(dcarry)

    x = x_ref[0]
    if combine:
        x = _combine(x, y0_ref, y1_ref, rt_ref[0], lead=(0, 0))
        xo_ref[0] = x
    xb = _rms(x, g_ref[...]).astype(BF16)

    p = jnp.concatenate([jnp.dot(xb, wm_ref[:, :_N_FIRST], preferred_element_type=F32),
                         jnp.dot(xb, wm_ref[:, _N_FIRST:], preferred_element_type=F32)], axis=1)
    pt = lax.dot_general(wt_ref[...], xb, (((1,), (1,)), ((), ())),
                         preferred_element_type=F32)

    ca = p[:, _C_AC:_C_AC + 256] * p[:, _C_AX:_C_AX + 256]
    bufa[8:8 + tm, :] = ca
    caw = caw_ref[...]
    conv = caw[0:1] * bufa[6:6 + tm, :] + caw[1:2] * bufa[7:7 + tm, :] + caw[2:3] * ca
    ya_ref[0] = (p[:, _C_AB:_C_AB + 256] * conv).astype(BF16)
    bufa[0:8, :] = ca[tm - 8:tm]

    u = p[:, _C_CU:_C_CU + 256] * _sigmoid(p[:, _C_CG:_C_CG + 256])
    bufc[0, 32:32 + tm, :] = u
    for r in range(1, 8):
        bufc[r, 0:tm + 24, :] = bufc[0, r:r + tm + 24, :]
    cdw = cdw_ref[...]
    cdb = cdb_ref[...]
    lng = lng_ref[...]
    lnb = lnb_ref[...]
    for c in range(tm // CONV_ROW_CHUNK):
        row0 = c * CONV_ROW_CHUNK
        acc = jnp.broadcast_to(cdb, (CONV_ROW_CHUNK, BRANCH_WIDTH))
        for k in range(CONF_WIDTH):
            off = 32 - (CONF_WIDTH - 1) + k
            acc = acc + cdw[k:k + 1] * bufc[off % 8, row0 + off - off % 8:row0 + off - off % 8 + CONV_ROW_CHUNK, :]
        mu = jnp.mean(acc, axis=-1, keepdims=True)
        cen = acc - mu
        var = jnp.mean(cen * cen, axis=-1, keepdims=True)
        yn = cen * lax.rsqrt(var + EPS) * lng + lnb
        yc_ref[0, row0:row0 + CONV_ROW_CHUNK, :] = (yn * _sigmoid(yn)).astype(BF16)
    bufc[0, 0:32, :] = u[tm - 32:tm]

    logf = -_softplus(-(p[:, _C_F:_C_F + LANES] + bf_ref[...]))
    tril = tril_ref[...]
    dcum = dcarry[0:1, :]
    for part in _split3(logf):
        dcum = dcum + jnp.dot(tril, part, preferred_element_type=F32)
    dcarry[0:1, :] = dcum[tm - 1:tm, :]
    dcum_t = dcum.T
    qh, qm, ql = _split3(dcum)
    kh, km, kl = _split3(-dcum_t)
    q_extra = (jnp.dot(qh, pq_ref[0], preferred_element_type=F32)
               + jnp.dot(qm, pq_ref[1], preferred_element_type=F32)
               + jnp.dot(ql, pq_ref[2], preferred_element_type=F32) + qc_ref[...])
    k_extra = (jnp.dot(pk_ref[0], kh, preferred_element_type=F32)
               + jnp.dot(pk_ref[1], km, preferred_element_type=F32)
               + jnp.dot(pk_ref[2], kl, preferred_element_type=F32)
               + jnp.concatenate([kc_ref[...]] * (tm // LANES), axis=1))

    lane = lax.broadcasted_iota(I32, (tm, HEAD_SLAB), 1)
    low = lane < HEAD_DIM
    vc = vc_ref[...]
    for hd in range(2 * N_HEADS):
        is_fox = hd >= N_HEADS
        pair = (hd // 2) * HEAD_SLAB
        qs = p[:, _C_Q + pair:_C_Q + pair + HEAD_SLAB]
        vs = p[:, _C_V + pair:_C_V + pair + HEAD_SLAB]
        if hd % 2:
            qs = pltpu.roll(qs, HEAD_DIM, axis=1)
            vs = pltpu.roll(vs, HEAD_DIM, axis=1)
        if is_fox:
            hf = hd - N_HEADS
            qx = q_extra[:, hf * HEAD_SLAB:(hf + 1) * HEAD_SLAB]
            kx = k_extra[hf * HEAD_DIM:(hf + 1) * HEAD_DIM, :]
            vx = vc
        else:
            qx = 0.0
            kx = jnp.zeros((HEAD_DIM, tm), F32)
            vx = 0.0
        q_ref[0, hd] = jnp.where(low, qs, qx).astype(BF16)
        v_ref[0, hd] = jnp.where(low, vs, vx).astype(BF16)
        kfull = jnp.concatenate([pt[hd * HEAD_DIM:(hd + 1) * HEAD_DIM, :], kx], axis=0).astype(BF16)
        for c in range(tm // tk):
            kt_ref[0, hd, c] = kfull[:, c * tk:(c + 1) * tk]


def _mixer_in(x, comb, lw, consts, b0=0, b=None):
    s = x.shape[1]
    b = x.shape[0] if b is None else b
    tm = min(TOKEN_TILE, s)
    tk = min(ATTN_TILE, s)
    nk = s // tk
    nh2 = 2 * N_HEADS
    combine = comb is not None

    def full(a):
        return pl.BlockSpec(a.shape, lambda bi, si, _n=a.ndim: (0,) * _n)

    tok = lambda w: pl.BlockSpec((1, tm, w), lambda bi, si: (bi, si, 0))
    in_arrays = [x]
    in_specs = [pl.BlockSpec((1, tm, D_MODEL), lambda bi, si: (bi + b0, si, 0))]
    if combine:
        y01, route = comb
        in_arrays += [y01, y01, route]
        in_specs += [pl.BlockSpec((SC_SPLIT, 1, 1, tm, PLANE_W), lambda bi, si, _c=c: (0, _c, bi, si, 0))
                     for c in range(2)] + [tok(LANES)]
    weights = [lw["mix_g"], lw["w_main"], lw["w_t"], lw["bf"], lw["caw"], lw["cdw"], lw["cdb"], lw["lng"],
               lw["lnb"], consts["tril"], consts["pq"], consts["qc"], consts["pk"], consts["kc"], consts["vc"]]
    in_arrays += weights
    in_specs += [full(a) for a in weights]

    out_shape = []
    out_specs = []
    if combine:
        out_shape.append(jax.ShapeDtypeStruct((b, s, D_MODEL), F32))
        out_specs.append(tok(D_MODEL))
    out_shape += [
        jax.ShapeDtypeStruct((b, s, BRANCH_WIDTH), BF16),
        jax.ShapeDtypeStruct((b, s, BRANCH_WIDTH), BF16),
        jax.ShapeDtypeStruct((b, nh2, s, HEAD_SLAB), BF16),
        jax.ShapeDtypeStruct((b, nh2, nk, HEAD_SLAB, tk), BF16),
        jax.ShapeDtypeStruct((b, nh2, s, HEAD_SLAB), BF16),
    ]
    out_specs += [
        tok(BRANCH_WIDTH), tok(BRANCH_WIDTH),
        pl.BlockSpec((1, nh2, tm, HEAD_SLAB), lambda bi, si: (bi, 0, si, 0)),
        pl.BlockSpec((1, nh2, tm // tk, HEAD_SLAB, tk), lambda bi, si: (bi, 0, si, 0, 0)),
        pl.BlockSpec((1, nh2, tm, HEAD_SLAB), lambda bi, si: (bi, 0, si, 0)),
    ]
    return pl.pallas_call(
        functools.partial(_mixer_in_kernel, combine),
        grid=(b, s // tm),
        in_specs=in_specs,
        out_specs=out_specs,
        out_shape=out_shape,
        scratch_shapes=[pltpu.VMEM((8 + tm, BRANCH_WIDTH), F32),
                        pltpu.VMEM((8, 32 + tm, BRANCH_WIDTH), F32),
                        pltpu.VMEM((8, LANES), F32)],
        compiler_params=pltpu.CompilerParams(dimension_semantics=("arbitrary", "arbitrary"),
                                             vmem_limit_bytes=VMEM_LIMIT),
        name="mixer_in",
    )(*in_arrays)


def _pair_out(accs):
    lane = lax.broadcasted_iota(I32, accs[0].shape, 1)
    return jnp.where(lane < HEAD_DIM, accs[0], pltpu.roll(accs[1], HEAD_DIM, axis=1))


def _sb_attn_kernel(q_ref, kt_ref, v_ref, u_ref, o_ref):
    tq = q_ref.shape[2]
    nk, tk = kt_ref.shape[2], kt_ref.shape[-1]
    i = pl.program_id(2)
    umat = u_ref[...]
    nh = q_ref.shape[1]
    nb = SB_GROUP
    qs = [q_ref[0, h] for h in range(nh)]
    row = lax.broadcasted_iota(I32, (tq, tk), 0)
    col = lax.broadcasted_iota(I32, (tq, tk), 1)

    def weights(g, tails, masked):
        ws, new_tails = [], []
        for h in range(nh):
            tail = tails[h]
            wh = []
            for u in reversed(range(nb)):
                jr = g * nb + u
                z = jnp.dot(qs[h], kt_ref[0, h, jnp.minimum(jr, nk - 1)], preferred_element_type=F32)
                sp = jnp.maximum(z, 0.0) + jnp.log(1.0 + jnp.exp2(jnp.abs(z) * -LOG2E))
                if masked:
                    mask = col + (jr - i) * tk < row
                    sp = jnp.where(mask, sp, 0.0)
                later = jnp.dot(sp.astype(BF16), umat, preferred_element_type=F32)
                w = jnp.exp((z - later - tail).astype(BF16))
                if masked:
                    w = jnp.where(mask, w, jnp.zeros_like(w))
                wh.append(w)
                tail = tail + later[:, 0:1]
            ws.append(tuple(reversed(wh)))
            new_tails.append(tail)
        return tuple(ws), tuple(new_tails)

    def apply(g, accs, ws):
        out = []
        for h in range(nh):
            acc = accs[h]
            for u in range(nb):
                start = pl.multiple_of(jnp.minimum(g * nb + u, nk - 1) * tk, tk)
                acc = acc + jnp.dot(ws[h][u], v_ref[0, h, pl.ds(start, tk), :], preferred_element_type=F32)
            out.append(acc)
        return tuple(out)

    last = i // nb
    zero_t = tuple(jnp.zeros((tq, 1), F32) for _ in range(nh))
    accs = tuple(jnp.zeros((tq, HEAD_SLAB), F32) for _ in range(nh))
    ws, tails = weights(last, zero_t, True)

    def body(t, carry):
        accs, tails, ws = carry
        g = last - 1 - t
        accs = apply(g + 1, accs, ws)
        ws, tails = weights(g, tails, False)
        return accs, tails, ws

    accs, tails, ws = lax.fori_loop(0, last, body, (accs, tails, ws))
    accs = apply(0, accs, ws)
    o_ref[0] = jnp.concatenate([_pair_out(accs[k:k + 2]) for k in range(0, nh, 2)], axis=1).astype(BF16)


def _fox_attn_kernel(q_ref, kt_ref, v_ref, o_ref):
    tq = q_ref.shape[2]
    tk = kt_ref.shape[-1]
    i = pl.program_id(2)
    row = lax.broadcasted_iota(I32, (tq, tk), 0)
    col = lax.broadcasted_iota(I32, (tq, tk), 1)
    nh = q_ref.shape[1]
    qs = [q_ref[0, h] for h in range(nh)]

    def vblock(h, j):
        return v_ref[0, h, pl.ds(pl.multiple_of(j * tk, tk), tk), :]

    r = tq // tk
    accs, maxes = [], []
    for h in range(nh):
        ss = [jnp.where(col + d * tk <= row, jnp.dot(qs[h], kt_ref[0, h, i * r + d], preferred_element_type=F32),
                        -jnp.inf) for d in range(r)]
        m = jnp.max(ss[0], axis=-1, keepdims=True)
        for s in ss[1:]:
            m = jnp.maximum(m, jnp.max(s, axis=-1, keepdims=True))
        acc = None
        for d, s in enumerate(ss):
            term = jnp.dot(jnp.exp(s - m).astype(BF16), vblock(h, i * r + d), preferred_element_type=F32)
            acc = term if acc is None else acc + term
        accs.append(acc)
        maxes.append(m)

    def make_body(n_blocks, first):
        def body(jj, carry):
            accs, maxes = carry
            new_accs, new_maxes = [], []
            for h in range(nh):
                js = [first + n_blocks * jj + u for u in range(n_blocks)]
                ss = [jnp.dot(qs[h], kt_ref[0, h, j], preferred_element_type=F32) for j in js]
                m = maxes[h]
                for s in ss:
                    m = jnp.maximum(m, jnp.max(s, axis=-1, keepdims=True))
                acc = jnp.exp(maxes[h] - m) * accs[h]
                for s, j in zip(ss, js):
                    acc = acc + jnp.dot(jnp.exp(s - m).astype(BF16), vblock(h, j), preferred_element_type=F32)
                new_accs.append(acc)
                new_maxes.append(m)
            return tuple(new_accs), tuple(new_maxes)
        return body

    carry = (tuple(accs), tuple(maxes))
    done = 0
    for n_blocks in ATTN_TRIPS:
        trips = (i * r - done) // n_blocks
        carry = lax.fori_loop(0, trips, make_body(n_blocks, done), carry)
        done = done + trips * n_blocks
    accs = carry[0]
    outs = [a / a[:, HEAD_DIM:HEAD_DIM + 1] for a in accs]
    o_ref[0] = jnp.concatenate([_pair_out(outs[k:k + 2]) for k in range(0, nh, 2)], axis=1).astype(BF16)


def _attention(kind, q, kt, v, consts):
    b, _, s, _ = q.shape
    nk, tk = kt.shape[2], kt.shape[4]
    tq = tk if kind == "sb" else min(FOX_Q_BLOCKS * tk, s)
    hps = SB_HEADS_PER_STEP if kind == "sb" else FOX_HEADS_PER_STEP
    head0 = 0 if kind == "sb" else N_HEADS // hps
    in_specs = [
        pl.BlockSpec((1, hps, tq, HEAD_SLAB), lambda bi, hp, i: (bi, hp + head0, i, 0)),
        pl.BlockSpec((1, hps, nk, HEAD_SLAB, tk), lambda bi, hp, i: (bi, hp + head0, 0, 0, 0)),
        pl.BlockSpec((1, hps, s, HEAD_SLAB), lambda bi, hp, i: (bi, hp + head0, 0, 0)),
    ]
    args = [q, kt, v]
    if kind == "sb":
        in_specs.append(pl.BlockSpec((tk, tk), lambda bi, hp, i: (0, 0)))
        args.append(consts["u_incl"])
        body = _sb_attn_kernel
    else:
        body = _fox_attn_kernel
    return pl.pallas_call(
        body,
        grid=(b, N_HEADS // hps, s // tq),
        in_specs=in_specs,
        out_specs=pl.BlockSpec((1, tq, hps * HEAD_DIM), lambda bi, hp, i: (bi, i, hp)),
        out_shape=jax.ShapeDtypeStruct((b, s, BRANCH_WIDTH), BF16),
        compiler_params=pltpu.CompilerParams(dimension_semantics=("arbitrary", "arbitrary", "arbitrary"),
                                             vmem_limit_bytes=VMEM_LIMIT),
        name=kind + "_attn",
    )(*args)


def _mixer_out_kernel(x_ref, ya_ref, ysb_ref, yfx_ref, yc_ref, g_ref, wg_ref, bg_ref, wb_ref, wo_ref,
                      fg_ref, wr_ref, tru_ref, xo_ref, xp_ref, rt_ref, rtt_ref, cnt_ref, carry):
    tm = x_ref.shape[0]

    @pl.when(pl.program_id(0) == 0)
    def _():
        carry[...] = jnp.zeros_like(carry)

    x = x_ref[...]
    xb = _rms(x, g_ref[...]).astype(BF16)
    h = None
    for g, y_ref in enumerate((ya_ref, ysb_ref, yfx_ref, yc_ref)):
        gate = _sigmoid(jnp.dot(xb, wg_ref[g], preferred_element_type=F32) + bg_ref[g])
        term = gate * jnp.dot(y_ref[...], wb_ref[g], preferred_element_type=F32)
        h = term if h is None else h + term
    xo = x + jnp.dot(h.astype(BF16), wo_ref[...], preferred_element_type=F32)
    xo_ref[...] = xo
    xn = _rms(xo, fg_ref[...])
    _store_planes(xp_ref, xn)

    xh = xn.astype(BF16)
    xl = (xn - xh.astype(F32)).astype(BF16)
    nt = (((1,), (1,)), ((), ()))
    logits = (lax.dot_general(wr_ref[0], xh, nt, preferred_element_type=F32)
              + lax.dot_general(wr_ref[0], xl, nt, preferred_element_type=F32)
              + lax.dot_general(wr_ref[1], xh, nt, preferred_element_type=F32))[0:ROUTE_ROWS, :]
    row = lax.broadcasted_iota(I32, (ROUTE_ROWS, tm), 0).astype(F32)
    ninf = -jnp.inf
    big = float(LANES)
    gl = jnp.where(row < N_GROUPS, logits, ninf)
    gmax = jnp.max(gl, axis=0, keepdims=True)
    gidx = jnp.min(jnp.where(gl == gmax, row, big), axis=0, keepdims=True)
    p_group = 1.0 / jnp.sum(jnp.exp(gl - gmax), axis=0, keepdims=True)
    first = N_GROUPS + EXPERTS_PER_GROUP * gidx
    el = jnp.where((row >= first) & (row < first + EXPERTS_PER_GROUP), logits, ninf)
    m1 = jnp.max(el, axis=0, keepdims=True)
    i1 = jnp.min(jnp.where(el == m1, row, big), axis=0, keepdims=True)
    el2 = jnp.where(row == i1, ninf, el)
    m2 = jnp.max(el2, axis=0, keepdims=True)
    i2 = jnp.min(jnp.where(el2 == m2, row, big), axis=0, keepdims=True)
    e2 = jnp.exp(m2 - m1)
    w1 = p_group / (1.0 + e2)
    w2 = w1 * e2

    sel1 = row == i1
    sel2 = row == i2
    onehot = jnp.where(sel1, 1.0, jnp.where(sel2, 1.0, 0.0))
    before = jnp.dot(onehot.astype(BF16), tru_ref[...], preferred_element_type=F32) + carry[:, 0:1]
    r1 = jnp.sum(jnp.where(sel1, before, 0.0), axis=0, keepdims=True)
    r2 = jnp.sum(jnp.where(sel2, before, 0.0), axis=0, keepdims=True)
    total = carry[...] + jnp.sum(onehot, axis=1, keepdims=True)
    carry[...] = total
    cnt_ref[...] = total

    row8 = lax.broadcasted_iota(I32, (8, tm), 0)
    fields = (i1 - N_GROUPS, i2 - N_GROUPS, w1, w2, r1, r2)
    rtt = jnp.zeros((8, tm), F32)
    for k, f in enumerate(fields):
        rtt = jnp.where(row8 == k, f, rtt)
    rtt_ref[...] = rtt
    rt_ref[...] = jnp.concatenate([rtt, jnp.zeros((LANES - 8, tm), F32)], axis=0).T


def _mixer_out(x2d, ya, ysb, yfx, yc, lw, consts, row0=0):
    t = ya.shape[0]
    tm = min(TOKEN_TILE, t)

    def full(a):
        return pl.BlockSpec(a.shape, lambda i, _n=a.ndim: (0,) * _n)

    tok = lambda w: pl.BlockSpec((tm, w), lambda i: (i, 0))
    weights = [lw["mix_g"], lw["w_gate"], lw["b_gate"], lw["w_branch"], lw["w_out"], lw["ffn_g"], lw["w_router"],
               consts["triu_strict"]]
    return pl.pallas_call(
        _mixer_out_kernel,
        grid=(t // tm,),
        in_specs=[pl.BlockSpec((tm, D_MODEL), lambda i: (i + row0 // tm, 0))] + [tok(BRANCH_WIDTH)] * 4
        + [full(a) for a in weights],
        out_specs=[tok(D_MODEL), pl.BlockSpec((SC_SPLIT, tm, PLANE_W), lambda i: (0, i, 0)), tok(LANES),
                   pl.BlockSpec((8, tm), lambda i: (0, i)), pl.BlockSpec((ROUTE_ROWS, LANES), lambda i: (0, 0))],
        out_shape=[jax.ShapeDtypeStruct((t, D_MODEL), F32),
                   jax.ShapeDtypeStruct((SC_SPLIT, t, PLANE_W), I32),
                   jax.ShapeDtypeStruct((t, LANES), F32),
                   jax.ShapeDtypeStruct((8, t), F32),
                   jax.ShapeDtypeStruct((ROUTE_ROWS, LANES), F32)],
        scratch_shapes=[pltpu.VMEM((ROUTE_ROWS, LANES), F32)],
        compiler_params=pltpu.CompilerParams(dimension_semantics=("arbitrary",),
                                             vmem_limit_bytes=VMEM_LIMIT),
        name="mixer_out",
    )(x2d, ya, ysb, yfx, yc, *weights)


def _sc_mesh():
    return plsc.VectorSubcoreMesh(core_axis_name="core", subcore_axis_name="subcore")


def _plane_index(idx, rows_per_plane):
    return jnp.concatenate([idx + k * rows_per_plane for k in range(SC_SPLIT)]).reshape(1, -1)


def _dispatch_rows(planes, pos0, pos1, n_out):
    w = planes.shape[2]
    rows = planes.reshape(-1, w)
    t = rows.shape[0]
    idx0 = _plane_index(pos0, n_out)
    idx1 = _plane_index(pos1, n_out)

    @functools.partial(pl.kernel, out_type=jax.ShapeDtypeStruct((SC_SPLIT * n_out, w), rows.dtype),
                       mesh=_sc_mesh(), scratch_types=[])
    def scatter_kernel(x_hbm, i0_hbm, i1_hbm, o_hbm):
        def body(x_vmem, i0_vmem, i1_vmem):
            pltpu.sync_copy(x_vmem, o_hbm.at[i0_vmem.at[0]])
            pltpu.sync_copy(x_vmem, o_hbm.at[i1_vmem.at[0]])

        pltpu.emit_pipeline(
            body,
            grid=(t // SC_WINDOW,),
            in_specs=[pl.BlockSpec((SC_WINDOW, w), lambda i: (i, 0)),
                      pl.BlockSpec((1, SC_WINDOW), lambda i: (0, i)),
                      pl.BlockSpec((1, SC_WINDOW), lambda i: (0, i))],
            out_specs=[],
            core_axis_name=("core", "subcore"),
            dimension_semantics=(pltpu.PARALLEL,),
        )(x_hbm, i0_hbm, i1_hbm)

    return scatter_kernel(rows, idx0, idx1).reshape(SC_SPLIT, n_out, w)


def _collect_rows(planes, idx):
    n, w = planes.shape[1:]
    table = planes.reshape(-1, w)
    idx2 = _plane_index(idx, n)
    m = idx2.shape[1]

    @functools.partial(pl.kernel, out_type=jax.ShapeDtypeStruct((m, w), table.dtype), mesh=_sc_mesh(),
                       scratch_types=[])
    def gather_kernel(x_hbm, i_hbm, o_hbm):
        def body(i_vmem, o_vmem):
            pltpu.sync_copy(x_hbm.at[i_vmem.at[0]], o_vmem)

        pltpu.emit_pipeline(
            body,
            grid=(m // SC_WINDOW,),
            in_specs=[pl.BlockSpec((1, SC_WINDOW), lambda i: (0, i))],
            out_specs=[pl.BlockSpec((SC_WINDOW, w), lambda i: (i, 0))],
            core_axis_name=("core", "subcore"),
            dimension_semantics=(pltpu.PARALLEL,),
        )(i_hbm, o_hbm)

    return gather_kernel(table, idx2).reshape(SC_SPLIT, -1, w)


def _moe_ffn_kernel(te_ref, nt_ref, xs_ref, wgu_ref, wd_ref, ys_ref):
    i = pl.program_id(0)

    @pl.when(i < nt_ref[0])
    def _():
        gu = None
        for k, part in enumerate(_load_planes(xs_ref)):
            term = jnp.dot(part.astype(BF16), wgu_ref[0, k * PLANE_W:(k + 1) * PLANE_W, :],
                           preferred_element_type=F32)
            gu = term if gu is None else gu + term
        gate = gu[:, :D_EXPERT]
        hid = gate * _sigmoid(gate) * gu[:, D_EXPERT:]
        _store_planes(ys_ref, jnp.dot(hid.astype(BF16), wd_ref[0], preferred_element_type=F32))

    @pl.when(i >= nt_ref[0])
    def _():
        ys_ref[...] = jnp.zeros_like(ys_ref)


def _moe_ffn(xs, tile_expert, n_tiles, lw):
    p = xs.shape[1]
    tg = GROUP_TILE
    rows = pl.BlockSpec((SC_SPLIT, tg, PLANE_W), lambda i, te, nt: (0, i, 0))
    grid_spec = pltpu.PrefetchScalarGridSpec(
        num_scalar_prefetch=2,
        grid=(p // tg,),
        in_specs=[rows,
                  pl.BlockSpec((1, D_MODEL, 2 * D_EXPERT), lambda i, te, nt: (te[i], 0, 0)),
                  pl.BlockSpec((1, D_EXPERT, D_MODEL), lambda i, te, nt: (te[i], 0, 0))],
        out_specs=rows,
    )
    return pl.pallas_call(
        _moe_ffn_kernel,
        grid_spec=grid_spec,
        out_shape=jax.ShapeDtypeStruct((SC_SPLIT, p, PLANE_W), I32),
        compiler_params=pltpu.CompilerParams(dimension_semantics=("arbitrary",),
                                             vmem_limit_bytes=VMEM_LIMIT),
        name="moe_ffn",
    )(tile_expert, n_tiles, xs, lw["w_gu"], lw["w_down"])


def _route_plan(route_t, cnt, t):
    tg = GROUP_TILE
    n_tiles_max = (2 * t) // tg + N_EXPERTS
    counts = cnt[N_GROUPS:N_GROUPS + N_EXPERTS, 0].astype(I32)
    padded = ((counts + tg - 1) // tg) * tg
    ends = jnp.cumsum(padded)
    offs = ends - padded
    experts = jnp.arange(N_EXPERTS, dtype=I32)[:, None]

    def first_row(e):
        return jnp.sum(jnp.where(e[None, :] == experts, offs[:, None], 0), axis=0)

    fields = route_t.astype(I32)
    pos0 = first_row(fields[0]) + fields[4]
    pos1 = first_row(fields[1]) + fields[5]
    tile_start = jnp.arange(n_tiles_max, dtype=I32) * tg
    n_tiles = ends[-1] // tg
    tile_clamped = jnp.minimum(tile_start, jnp.maximum(n_tiles - 1, 0) * tg)
    tile_expert = jnp.sum((ends[None, :] <= tile_clamped[:, None]).astype(I32), axis=1)
    tile_expert = jnp.minimum(tile_expert, N_EXPERTS - 1)
    return pos0, pos1, tile_expert, n_tiles.reshape(1), n_tiles_max * tg


def _moe(xp, route_t, cnt, lw):
    t = xp.shape[1]
    pos0, pos1, tile_expert, n_tiles, p_rows = _route_plan(route_t, cnt, t)
    xs = _dispatch_rows(xp, pos0, pos1, p_rows)
    ys = _moe_ffn(xs, tile_expert, n_tiles, lw)
    return _collect_rows(ys, jnp.concatenate([pos0, pos1])).reshape(SC_SPLIT, 2, t, PLANE_W)


def _final_kernel(x_ref, y0_ref, y1_ref, rt_ref, g_ref, *rest):
    o_ref = rest[-1]
    x = _combine(x_ref[...], y0_ref, y1_ref, rt_ref[...], lead=(0,))
    o_ref[...] = _rms(x, g_ref[...])


def _final(x2d, y01, route, g, t_total, row0, out_prev):
    t = x2d.shape[0]
    tm = min(TOKEN_TILE, t)
    tok = lambda w: pl.BlockSpec((tm, w), lambda i: (i, 0))
    choice = lambda c: pl.BlockSpec((SC_SPLIT, 1, tm, PLANE_W), lambda i: (0, c, i, 0))
    in_specs = [tok(D_MODEL), choice(0), choice(1), tok(LANES), pl.BlockSpec((1, D_MODEL), lambda i: (0, 0))]
    args = [x2d, y01, y01, route, g]
    aliases = {}
    if out_prev is not None:
        in_specs.append(pl.BlockSpec(memory_space=pl.ANY))
        args.append(out_prev)
        aliases = {len(args) - 1: 0}
    return pl.pallas_call(
        _final_kernel,
        grid=(t // tm,),
        in_specs=in_specs,
        out_specs=pl.BlockSpec((tm, D_MODEL), lambda i: (i + row0 // tm, 0)),
        out_shape=jax.ShapeDtypeStruct((t_total, D_MODEL), F32),
        input_output_aliases=aliases,
        compiler_params=pltpu.CompilerParams(dimension_semantics=("arbitrary",),
                                             vmem_limit_bytes=VMEM_LIMIT),
        name="final_norm",
    )(*args)


def _constants(tm, tk):
    r = jnp.arange(tm)
    tril = (r[None, :] <= r[:, None]).astype(BF16)
    triu_strict = (r[:, None] < r[None, :]).astype(BF16)
    rk = jnp.arange(tk)
    u_incl = (rk[:, None] >= rk[None, :]).astype(BF16)
    nh = N_HEADS
    pq = jnp.zeros((3, LANES, nh * HEAD_SLAB), F32)
    pk = jnp.zeros((3, nh * HEAD_DIM, LANES), F32)
    qc = jnp.zeros((1, nh * HEAD_SLAB), F32)
    kc = jnp.zeros((nh * HEAD_DIM, LANES), F32)
    for part in range(3):
        for h in range(nh):
            pq = pq.at[part, h, h * HEAD_SLAB + HEAD_DIM + part].set(1.0)
            qc = qc.at[0, h * HEAD_SLAB + HEAD_DIM + 3 + part].set(1.0)
            kc = kc.at[h * HEAD_DIM + part, :].set(1.0)
            pk = pk.at[part, h * HEAD_DIM + 3 + part, h].set(1.0)
    vc = jnp.zeros((1, HEAD_SLAB), F32).at[0, HEAD_DIM].set(1.0)
    return {"tril": tril, "triu_strict": triu_strict, "u_incl": u_incl, "pq": pq.astype(BF16),
            "pk": pk.astype(BF16), "qc": qc, "kc": kc, "vc": vc}


def _layer_weights(layer, mix_norm_g, w_in, b_forget, conv_a_w, conf_dw_w, conf_dw_b, conf_ln_g, conf_ln_b,
                   w_branch, w_gate, b_gate, w_out, ffn_norm_g, w_router_group, w_router_expert,
                   w_expert_gate, w_expert_up, w_expert_down):
    w = w_in[layer]
    bw = BRANCH_WIDTH
    a_x, a_b, a_c, sb_q, sb_k, sb_v, fx_q, fx_k, fx_v = [w[:, i * bw:(i + 1) * bw] for i in range(9)]
    fx_f = w[:, 9 * bw:9 * bw + N_HEADS]
    conf = w[:, 9 * bw + N_HEADS:]
    scale = HEAD_DIM ** -0.5
    f_pad = jnp.pad(fx_f, ((0, 0), (0, LANES - N_HEADS)))
    w_main = jnp.concatenate([conf, f_pad, sb_q * scale, fx_q * scale, sb_v, fx_v, a_x, a_b, a_c], axis=1)
    w_t = jnp.concatenate([sb_k, fx_k], axis=1).T
    w_router = jnp.concatenate([w_router_group[layer], w_router_expert[layer].reshape(D_MODEL, N_EXPERTS)], axis=1)
    w_router = jnp.pad(w_router, ((0, 0), (0, LANES - N_GROUPS - N_EXPERTS)))
    return {
        "mix_g": mix_norm_g[layer].reshape(1, D_MODEL),
        "w_main": w_main.astype(BF16),
        "w_t": w_t.astype(BF16),
        "bf": jnp.pad(b_forget[layer], (0, LANES - N_HEADS)).reshape(1, LANES),
        "caw": conv_a_w[layer],
        "cdw": jnp.pad(conf_dw_w[layer], ((0, 1), (0, 0))),
        "cdb": conf_dw_b[layer].reshape(1, bw),
        "lng": conf_ln_g[layer].reshape(1, bw),
        "lnb": conf_ln_b[layer].reshape(1, bw),
        "w_gate": w_gate[layer].astype(BF16),
        "b_gate": b_gate[layer].reshape(4, 1, D_MODEL),
        "w_branch": w_branch[layer].astype(BF16),
        "w_out": w_out[layer].astype(BF16),
        "ffn_g": ffn_norm_g[layer].reshape(1, D_MODEL),
        "w_router": jnp.stack([w_router.T.astype(BF16), (w_router - w_router.astype(BF16).astype(F32)).T.astype(BF16)]),
        "w_gu": jnp.concatenate([w_expert_gate[layer], w_expert_up[layer]], axis=2).astype(BF16),
        "w_down": w_expert_down[layer].astype(BF16),
    }


def kernel(x, mix_norm_g, w_in, b_forget, conv_a_w, conf_dw_w, conf_dw_b, conf_ln_g, conf_ln_b, w_branch, w_gate,
           b_gate, w_out, ffn_norm_g, w_router_group, w_router_expert, w_expert_gate, w_expert_up, w_expert_down,
           final_norm_g):
    b, s, _ = x.shape
    depth = w_in.shape[0]
    consts = _constants(min(TOKEN_TILE, s), min(ATTN_TILE, s))
    parts = BATCH_SPLIT if b % BATCH_SPLIT == 0 else 1
    bp = b // parts
    tp = bp * s
    xs = [x] * parts
    combs = [None] * parts
    for layer in range(depth):
        lw = _layer_weights(layer, mix_norm_g, w_in, b_forget, conv_a_w, conf_dw_w, conf_dw_b, conf_ln_g,
                            conf_ln_b, w_branch, w_gate, b_gate, w_out, ffn_norm_g, w_router_group,
                            w_router_expert, w_expert_gate, w_expert_up, w_expert_down)
        routed = []
        for k in range(parts):
            outs = _mixer_in(xs[k], combs[k], lw, consts, b0=k * bp if layer == 0 else 0, b=bp)
            if combs[k] is not None:
                xk, *outs = outs
            else:
                xk = None
            ya, yc, q, kt, v = outs
            ysb = _attention("sb", q, kt, v, consts)
            yfx = _attention("fox", q, kt, v, consts)
            if xk is None:
                x2d_in, row0 = x.reshape(b * s, D_MODEL), k * tp
            else:
                x2d_in, row0 = xk.reshape(tp, D_MODEL), 0
            routed.append(_mixer_out(x2d_in, ya.reshape(tp, -1), ysb.reshape(tp, -1), yfx.reshape(tp, -1),
                                     yc.reshape(tp, -1), lw, consts, row0=row0))
        for k in range(parts):
            x2d, xp, route, route_t, cnt = routed[k]
            y01 = _moe(xp, route_t, cnt, lw)
            xs[k] = x2d.reshape(bp, s, D_MODEL)
            combs[k] = (y01.reshape(SC_SPLIT, 2, bp, s, PLANE_W), route.reshape(bp, s, LANES))
    out = None
    for k in range(parts):
        out = _final(xs[k].reshape(tp, D_MODEL), combs[k][0].reshape(SC_SPLIT, 2, tp, PLANE_W),
                     combs[k][1].reshape(tp, LANES), final_norm_g.reshape(1, D_MODEL), b * s, k * tp, out)
    return out.reshape(b, s, D_MODEL)
```

```python
import functools

import jax
import jax.numpy as jnp
from jax import lax
from jax.experimental import pallas as pl
from jax.experimental.pallas import tpu as pltpu
from jax.experimental.pallas import tpu_sc as plsc

F32 = jnp.float32
BF16 = jnp.bfloat16
I32 = jnp.int32

D_MODEL = 1024
BRANCH_WIDTH = 256
HEAD_DIM = 64
N_HEADS = 4
HEAD_SLAB = 128
CONV_A_WIDTH = 3
CONF_WIDTH = 31
N_GROUPS = 4
EXPERTS_PER_GROUP = 8
N_EXPERTS = 32
D_EXPERT = 256
EPS = 1e-6
LOG2E = 1.4426950408889634
LANES = 128
ROUTE_ROWS = 48
HALF_D = D_MODEL // 2
SC_SPLIT = 2
PLANE_W = HALF_D // SC_SPLIT

TOKEN_TILE = 512
ATTN_TILE = 256
FOX_Q_BLOCKS = 2
SB_HEADS_PER_STEP = 4
FOX_HEADS_PER_STEP = 4
ATTN_TRIPS = (4, 2, 1)
SB_GROUP = 4
GROUP_TILE = 512
CONV_ROW_CHUNK = 64
SC_WINDOW = 128
BATCH_SPLIT = 2
VMEM_LIMIT = 56 * 1024 * 1024

_C_CU, _C_CG = 0, 256
_C_F = 512
_N_FIRST = 640
_C_Q = 640
_C_V = 1152
_C_AX, _C_AB, _C_AC = 1664, 1920, 2176
_N_MAIN = 2432


def _rms(x, g):
    return x * lax.rsqrt(jnp.mean(x * x, axis=-1, keepdims=True) + EPS) * g


def _softplus(z):
    return jnp.maximum(z, 0.0) + jnp.log1p(jnp.exp(-jnp.abs(z)))


def _sigmoid(z):
    return 1.0 / (1.0 + jnp.exp(-z))


def _split3(v):
    hi = v.astype(BF16)
    r = v - hi.astype(F32)
    mid = r.astype(BF16)
    lo = (r - mid.astype(F32)).astype(BF16)
    return hi, mid, lo


def _pack_rows(v):
    lo = pltpu.bitcast(v[:, :HALF_D].astype(BF16).astype(F32), jnp.uint32)
    hi = pltpu.bitcast(v[:, HALF_D:].astype(BF16).astype(F32), jnp.uint32)
    return pltpu.bitcast((lo >> 16) | hi, I32)


def _unpack_rows(w):
    u = pltpu.bitcast(w, jnp.uint32)
    lo = pltpu.bitcast(u << 16, F32)
    hi = pltpu.bitcast(u & jnp.uint32(0xFFFF0000), F32)
    return lo, hi


def _store_planes(ref, v, lead=()):
    packed = _pack_rows(v)
    for k in range(SC_SPLIT):
        ref[(k, *lead)] = packed[:, k * PLANE_W:(k + 1) * PLANE_W]


def _load_planes(ref, lead=()):
    los, his = zip(*[_unpack_rows(ref[(k, *lead)]) for k in range(SC_SPLIT)])
    return list(los) + list(his)


def _combine(x, y0_ref, y1_ref, rt, lead=()):
    w0 = rt[:, 2:3]
    w1 = rt[:, 3:4]
    parts = [w0 * a + w1 * b for a, b in zip(_load_planes(y0_ref, lead), _load_planes(y1_ref, lead))]
    return x + jnp.concatenate(parts, axis=1)


def _mixer_in_kernel(combine, *refs):
    if combine:
        (x_ref, y0_ref, y1_ref, rt_ref, *refs) = refs
    else:
        (x_ref, *refs) = refs
    (g_ref, wm_ref, wt_ref, bf_ref, caw_ref, cdw_ref, cdb_ref, lng_ref, lnb_ref, tril_ref,
     pq_ref, qc_ref, pk_ref, kc_ref, vc_ref, *refs) = refs
    if combine:
        (xo_ref, *refs) = refs
    (ya_ref, yc_ref, q_ref, kt_ref, v_ref, bufa, bufc, dcarry) = refs

    tm = x_ref.shape[1]
    tk = kt_ref.shape[-1]

    @pl.when(pl.program_id(1) == 0)
    def _():
        bufa[0:8, :] = jnp.zeros((8, BRANCH_WIDTH), F32)
        bufc[0, 0:32, :] = jnp.zeros((32, BRANCH_WIDTH), F32)
        dcarry[...] = jnp.zeros_like(dcarry)

    x = x_ref[0]
    if combine:
        x = _combine(x, y0_ref, y1_ref, rt_ref[0], lead=(0, 0))
        xo_ref[0] = x
    xb = _rms(x, g_ref[...]).astype(BF16)

    p = jnp.concatenate([jnp.dot(xb, wm_ref[:, :_N_FIRST], preferred_element_type=F32),
                         jnp.dot(xb, wm_ref[:, _N_FIRST:], preferred_element_type=F32)], axis=1)
    pt = lax.dot_general(wt_ref[...], xb, (((1,), (1,)), ((), ())),
                         preferred_element_type=F32)

    ca = p[:, _C_AC:_C_AC + 256] * p[:, _C_AX:_C_AX + 256]
    bufa[8:8 + tm, :] = ca
    caw = caw_ref[...]
    conv = caw[0:1] * bufa[6:6 + tm, :] + caw[1:2] * bufa[7:7 + tm, :] + caw[2:3] * ca
    ya_ref[0] = (p[:, _C_AB:_C_AB + 256] * conv).astype(BF16)
    bufa[0:8, :] = ca[tm - 8:tm]

    u = p[:, _C_CU:_C_CU + 256] * _sigmoid(p[:, _C_CG:_C_CG + 256])
    bufc[0, 32:32 + tm, :] = u
    for r in range(1, 8):
        bufc[r, 0:tm + 24, :] = bufc[0, r:r + tm + 24, :]
    cdw = cdw_ref[...]
    cdb = cdb_ref[...]
    lng = lng_ref[...]
    lnb = lnb_ref[...]
    for c in range(tm // CONV_ROW_CHUNK):
        row0 = c * CONV_ROW_CHUNK
        acc = jnp.broadcast_to(cdb, (CONV_ROW_CHUNK, BRANCH_WIDTH))
        for k in range(CONF_WIDTH):
            off = 32 - (CONF_WIDTH - 1) + k
            acc = acc + cdw[k:k + 1] * bufc[off % 8, row0 + off - off % 8:row0 + off - off % 8 + CONV_ROW_CHUNK, :]
        mu = jnp.mean(acc, axis=-1, keepdims=True)
        cen = acc - mu
        var = jnp.mean(cen * cen, axis=-1, keepdims=True)
        yn = cen * lax.rsqrt(var + EPS) * lng + lnb
        yc_ref[0, row0:row0 + CONV_ROW_CHUNK, :] = (yn * _sigmoid(yn)).astype(BF16)
    bufc[0, 0:32, :] = u[tm - 32:tm]

    logf = -_softplus(-(p[:, _C_F:_C_F + LANES] + bf_ref[...]))
    tril = tril_ref[...]
    dcum = dcarry[0:1, :]
    for part in _split3(logf):
        dcum = dcum + jnp.dot(tril, part, preferred_element_type=F32)
    dcarry[0:1, :] = dcum[tm - 1:tm, :]
    dcum_t = dcum.T
    qh, qm, ql = _split3(dcum)
    kh, km, kl = _split3(-dcum_t)
    q_extra = (jnp.dot(qh, pq_ref[0], preferred_element_type=F32)
               + jnp.dot(qm, pq_ref[1], preferred_element_type=F32)
               + jnp.dot(ql, pq_ref[2], preferred_element_type=F32) + qc_ref[...])
    k_extra = (jnp.dot(pk_ref[0], kh, preferred_element_type=F32)
               + jnp.dot(pk_ref[1], km, preferred_element_type=F32)
               + jnp.dot(pk_ref[2], kl, preferred_element_type=F32)
               + jnp.concatenate([kc_ref[...]] * (tm // LANES), axis=1))

    lane = lax.broadcasted_iota(I32, (tm, HEAD_SLAB), 1)
    low = lane < HEAD_DIM
    vc = vc_ref[...]
    for hd in range(2 * N_HEADS):
        is_fox = hd >= N_HEADS
        pair = (hd // 2) * HEAD_SLAB
        qs = p[:, _C_Q + pair:_C_Q + pair + HEAD_SLAB]
        vs = p[:, _C_V + pair:_C_V + pair + HEAD_SLAB]
        if hd % 2:
            qs = pltpu.roll(qs, HEAD_DIM, axis=1)
            vs = pltpu.roll(vs, HEAD_DIM, axis=1)
        if is_fox:
            hf = hd - N_HEADS
            qx = q_extra[:, hf * HEAD_SLAB:(hf + 1) * HEAD_SLAB]
            kx = k_extra[hf * HEAD_DIM:(hf + 1) * HEAD_DIM, :]
            vx = vc
        else:
            qx = 0.0
            kx = jnp.zeros((HEAD_DIM, tm), F32)
            vx = 0.0
        q_ref[0, hd] = jnp.where(low, qs, qx).astype(BF16)
        v_ref[0, hd] = jnp.where(low, vs, vx).astype(BF16)
        kfull = jnp.concatenate([pt[hd * HEAD_DIM:(hd + 1) * HEAD_DIM, :], kx], axis=0).astype(BF16)
        for c in range(tm // tk):
            kt_ref[0, hd, c] = kfull[:, c * tk:(c + 1) * tk]


def _mixer_in(x, comb, lw, consts, b0=0, b=None):
    s = x.shape[1]
    b = x.shape[0] if b is None else b
    tm = min(TOKEN_TILE, s)
    tk = min(ATTN_TILE, s)
    nk = s // tk
    nh2 = 2 * N_HEADS
    combine = comb is not None

    def full(a):
        return pl.BlockSpec(a.shape, lambda bi, si, _n=a.ndim: (0,) * _n)

    tok = lambda w: pl.BlockSpec((1, tm, w), lambda bi, si: (bi, si, 0))
    in_arrays = [x]
    in_specs = [pl.BlockSpec((1, tm, D_MODEL), lambda bi, si: (bi + b0, si, 0))]
    if combine:
        y01, route = comb
        in_arrays += [y01, y01, route]
        in_specs += [pl.BlockSpec((SC_SPLIT, 1, 1, tm, PLANE_W), lambda bi, si, _c=c: (0, _c, bi, si, 0))
                     for c in range(2)] + [tok(LANES)]
    weights = [lw["mix_g"], lw["w_main"], lw["w_t"], lw["bf"], lw["caw"], lw["cdw"], lw["cdb"], lw["lng"],
               lw["lnb"], consts["tril"], consts["pq"], consts["qc"], consts["pk"], consts["kc"], consts["vc"]]
    in_arrays += weights
    in_specs += [full(a) for a in weights]

    out_shape = []
    out_specs = []
    if combine:
        out_shape.append(jax.ShapeDtypeStruct((b, s, D_MODEL), F32))
        out_specs.append(tok(D_MODEL))
    out_shape += [
        jax.ShapeDtypeStruct((b, s, BRANCH_WIDTH), BF16),
        jax.ShapeDtypeStruct((b, s, BRANCH_WIDTH), BF16),
        jax.ShapeDtypeStruct((b, nh2, s, HEAD_SLAB), BF16),
        jax.ShapeDtypeStruct((b, nh2, nk, HEAD_SLAB, tk), BF16),
        jax.ShapeDtypeStruct((b, nh2, s, HEAD_SLAB), BF16),
    ]
    out_specs += [
        tok(BRANCH_WIDTH), tok(BRANCH_WIDTH),
        pl.BlockSpec((1, nh2, tm, HEAD_SLAB), lambda bi, si: (bi, 0, si, 0)),
        pl.BlockSpec((1, nh2, tm // tk, HEAD_SLAB, tk), lambda bi, si: (bi, 0, si, 0, 0)),
        pl.BlockSpec((1, nh2, tm, HEAD_SLAB), lambda bi, si: (bi, 0, si, 0)),
    ]
    return pl.pallas_call(
        functools.partial(_mixer_in_kernel, combine),
        grid=(b, s // tm),
        in_specs=in_specs,
        out_specs=out_specs,
        out_shape=out_shape,
        scratch_shapes=[pltpu.VMEM((8 + tm, BRANCH_WIDTH), F32),
                        pltpu.VMEM((8, 32 + tm, BRANCH_WIDTH), F32),
                        pltpu.VMEM((8, LANES), F32)],
        compiler_params=pltpu.CompilerParams(dimension_semantics=("arbitrary", "arbitrary"),
                                             vmem_limit_bytes=VMEM_LIMIT),
        name="mixer_in",
    )(*in_arrays)


def _pair_out(accs):
    lane = lax.broadcasted_iota(I32, accs[0].shape, 1)
    return jnp.where(lane < HEAD_DIM, accs[0], pltpu.roll(accs[1], HEAD_DIM, axis=1))


def _sb_attn_kernel(q_ref, kt_ref, v_ref, u_ref, o_ref):
    tq = q_ref.shape[2]
    nk, tk = kt_ref.shape[2], kt_ref.shape[-1]
    i = pl.program_id(2)
    umat = u_ref[...]
    nh = q_ref.shape[1]
    nb = SB_GROUP
    qs = [q_ref[0, h] for h in range(nh)]
    strict = lax.broadcasted_iota(I32, (tq, tk), 1) < lax.broadcasted_iota(I32, (tq, tk), 0)

    def weights(g, tails, n_valid=None):
        ws, new_tails = [], []
        for h in range(nh):
            tail = tails[h]
            wh = [jnp.zeros((tq, tk), BF16)] * nb
            for u in reversed(range(nb if n_valid is None else n_valid)):
                z = jnp.dot(qs[h], kt_ref[0, h, g * nb + u], preferred_element_type=F32)
                sp = jnp.maximum(z, 0.0) + jnp.log(1.0 + jnp.exp2(jnp.abs(z) * -LOG2E))
                diagonal = n_valid is not None and u == n_valid - 1
                if diagonal:
                    sp = jnp.where(strict, sp, 0.0)
                later = jnp.dot(sp.astype(BF16), umat, preferred_element_type=F32)
                w = jnp.exp((z - later - tail).astype(BF16))
                if diagonal:
                    w = jnp.where(strict, w, jnp.zeros_like(w))
                wh[u] = w
                tail = tail + later[:, 0:1]
            ws.append(tuple(wh))
            new_tails.append(tail)
        return tuple(ws), tuple(new_tails)

    def apply(g, accs, ws):
        out = []
        for h in range(nh):
            acc = accs[h]
            for u in range(nb):
                start = pl.multiple_of(jnp.minimum(g * nb + u, nk - 1) * tk, tk)
                acc = acc + jnp.dot(ws[h][u], v_ref[0, h, pl.ds(start, tk), :], preferred_element_type=F32)
            out.append(acc)
        return tuple(out)

    last = i // nb
    zero_t = tuple(jnp.zeros((tq, 1), F32) for _ in range(nh))
    accs = tuple(jnp.zeros((tq, HEAD_SLAB), F32) for _ in range(nh))
    ws, tails = lax.switch(i % nb, [functools.partial(weights, last, n_valid=k + 1) for k in range(nb)], zero_t)

    def body(t, carry):
        accs, tails, ws = carry
        g = last - 1 - t
        accs = apply(g + 1, accs, ws)
        ws, tails = weights(g, tails)
        return accs, tails, ws

    accs, tails, ws = lax.fori_loop(0, last, body, (accs, tails, ws))
    accs = apply(0, accs, ws)
    o_ref[0] = jnp.concatenate([_pair_out(accs[k:k + 2]) for k in range(0, nh, 2)], axis=1).astype(BF16)


def _fox_attn_kernel(q_ref, kt_ref, v_ref, o_ref):
    tq = q_ref.shape[2]
    tk = kt_ref.shape[-1]
    i = pl.program_id(2)
    row = lax.broadcasted_iota(I32, (tq, tk), 0)
    col = lax.broadcasted_iota(I32, (tq, tk), 1)
    nh = q_ref.shape[1]
    qs = [q_ref[0, h] for h in range(nh)]

    def vblock(h, j):
        return v_ref[0, h, pl.ds(pl.multiple_of(j * tk, tk), tk), :]

    r = tq // tk
    accs, maxes = [], []
    for h in range(nh):
        ss = [jnp.where(col + d * tk <= row, jnp.dot(qs[h], kt_ref[0, h, i * r + d], preferred_element_type=F32),
                        -jnp.inf) for d in range(r)]
        m = jnp.max(ss[0], axis=-1, keepdims=True)
        for s in ss[1:]:
            m = jnp.maximum(m, jnp.max(s, axis=-1, keepdims=True))
        acc = None
        for d, s in enumerate(ss):
            term = jnp.dot(jnp.exp(s - m).astype(BF16), vblock(h, i * r + d), preferred_element_type=F32)
            acc = term if acc is None else acc + term
        accs.append(acc)
        maxes.append(m)

    def make_body(n_blocks, first):
        def body(jj, carry):
            accs, maxes = carry
            new_accs, new_maxes = [], []
            for h in range(nh):
                js = [first + n_blocks * jj + u for u in range(n_blocks)]
                ss = [jnp.dot(qs[h], kt_ref[0, h, j], preferred_element_type=F32) for j in js]
                m = maxes[h]
                for s in ss:
                    m = jnp.maximum(m, jnp.max(s, axis=-1, keepdims=True))
                acc = jnp.exp(maxes[h] - m) * accs[h]
                for s, j in zip(ss, js):
                    acc = acc + jnp.dot(jnp.exp(s - m).astype(BF16), vblock(h, j), preferred_element_type=F32)
                new_accs.append(acc)
                new_maxes.append(m)
            return tuple(new_accs), tuple(new_maxes)
        return body

    carry = (tuple(accs), tuple(maxes))
    done = 0
    for n_blocks in ATTN_TRIPS:
        trips = (i * r - done) // n_blocks
        carry = lax.fori_loop(0, trips, make_body(n_blocks, done), carry)
        done = done + trips * n_blocks
    accs = carry[0]
    outs = [a / a[:, HEAD_DIM:HEAD_DIM + 1] for a in accs]
    o_ref[0] = jnp.concatenate([_pair_out(outs[k:k + 2]) for k in range(0, nh, 2)], axis=1).astype(BF16)


def _attention(kind, q, kt, v, consts):
    b, _, s, _ = q.shape
    nk, tk = kt.shape[2], kt.shape[4]
    tq = tk if kind == "sb" else min(FOX_Q_BLOCKS * tk, s)
    hps = SB_HEADS_PER_STEP if kind == "sb" else FOX_HEADS_PER_STEP
    head0 = 0 if kind == "sb" else N_HEADS // hps
    in_specs = [
        pl.BlockSpec((1, hps, tq, HEAD_SLAB), lambda bi, hp, i: (bi, hp + head0, i, 0)),
        pl.BlockSpec((1, hps, nk, HEAD_SLAB, tk), lambda bi, hp, i: (bi, hp + head0, 0, 0, 0)),
        pl.BlockSpec((1, hps, s, HEAD_SLAB), lambda bi, hp, i: (bi, hp + head0, 0, 0)),
    ]
    args = [q, kt, v]
    if kind == "sb":
        in_specs.append(pl.BlockSpec((tk, tk), lambda bi, hp, i: (0, 0)))
        args.append(consts["u_incl"])
        body = _sb_attn_kernel
    else:
        body = _fox_attn_kernel
    return pl.pallas_call(
        body,
        grid=(b, N_HEADS // hps, s // tq),
        in_specs=in_specs,
        out_specs=pl.BlockSpec((1, tq, hps * HEAD_DIM), lambda bi, hp, i: (bi, i, hp)),
        out_shape=jax.ShapeDtypeStruct((b, s, BRANCH_WIDTH), BF16),
        compiler_params=pltpu.CompilerParams(dimension_semantics=("arbitrary", "arbitrary", "arbitrary"),
                                             vmem_limit_bytes=VMEM_LIMIT),
        name=kind + "_attn",
    )(*args)


def _mixer_out_kernel(x_ref, ya_ref, ysb_ref, yfx_ref, yc_ref, g_ref, wg_ref, bg_ref, wb_ref, wo_ref,
                      fg_ref, wr_ref, tru_ref, xo_ref, xp_ref, rt_ref, rtt_ref, cnt_ref, carry):
    tm = x_ref.shape[0]

    @pl.when(pl.program_id(0) == 0)
    def _():
        carry[...] = jnp.zeros_like(carry)

    x = x_ref[...]
    xb = _rms(x, g_ref[...]).astype(BF16)
    h = None
    for g, y_ref in enumerate((ya_ref, ysb_ref, yfx_ref, yc_ref)):
        gate = _sigmoid(jnp.dot(xb, wg_ref[g], preferred_element_type=F32) + bg_ref[g])
        term = gate * jnp.dot(y_ref[...], wb_ref[g], preferred_element_type=F32)
        h = term if h is None else h + term
    xo = x + jnp.dot(h.astype(BF16), wo_ref[...], preferred_element_type=F32)
    xo_ref[...] = xo
    xn = _rms(xo, fg_ref[...])
    _store_planes(xp_ref, xn)

    xh = xn.astype(BF16)
    xl = (xn - xh.astype(F32)).astype(BF16)
    nt = (((1,), (1,)), ((), ()))
    logits = (lax.dot_general(wr_ref[0], xh, nt, preferred_element_type=F32)
              + lax.dot_general(wr_ref[0], xl, nt, preferred_element_type=F32)
              + lax.dot_general(wr_ref[1], xh, nt, preferred_element_type=F32))[0:ROUTE_ROWS, :]
    row = lax.broadcasted_iota(I32, (ROUTE_ROWS, tm), 0).astype(F32)
    ninf = -jnp.inf
    big = float(LANES)
    gl = jnp.where(row < N_GROUPS, logits, ninf)
    gmax = jnp.max(gl, axis=0, keepdims=True)
    gidx = jnp.min(jnp.where(gl == gmax, row, big), axis=0, keepdims=True)
    p_group = 1.0 / jnp.sum(jnp.exp(gl - gmax), axis=0, keepdims=True)
    first = N_GROUPS + EXPERTS_PER_GROUP * gidx
    el = jnp.where((row >= first) & (row < first + EXPERTS_PER_GROUP), logits, ninf)
    m1 = jnp.max(el, axis=0, keepdims=True)
    i1 = jnp.min(jnp.where(el == m1, row, big), axis=0, keepdims=True)
    el2 = jnp.where(row == i1, ninf, el)
    m2 = jnp.max(el2, axis=0, keepdims=True)
    i2 = jnp.min(jnp.where(el2 == m2, row, big), axis=0, keepdims=True)
    e2 = jnp.exp(m2 - m1)
    w1 = p_group / (1.0 + e2)
    w2 = w1 * e2

    sel1 = row == i1
    sel2 = row == i2
    onehot = jnp.where(sel1, 1.0, jnp.where(sel2, 1.0, 0.0))
    before = jnp.dot(onehot.astype(BF16), tru_ref[...], preferred_element_type=F32) + carry[:, 0:1]
    r1 = jnp.sum(jnp.where(sel1, before, 0.0), axis=0, keepdims=True)
    r2 = jnp.sum(jnp.where(sel2, before, 0.0), axis=0, keepdims=True)
    total = carry[...] + jnp.sum(onehot, axis=1, keepdims=True)
    carry[...] = total
    cnt_ref[...] = total

    row8 = lax.broadcasted_iota(I32, (8, tm), 0)
    fields = (i1 - N_GROUPS, i2 - N_GROUPS, w1, w2, r1, r2)
    rtt = jnp.zeros((8, tm), F32)
    for k, f in enumerate(fields):
        rtt = jnp.where(row8 == k, f, rtt)
    rtt_ref[...] = rtt
    rt_ref[...] = jnp.concatenate([rtt, jnp.zeros((LANES - 8, tm), F32)], axis=0).T


def _mixer_out(x2d, ya, ysb, yfx, yc, lw, consts, row0=0):
    t = ya.shape[0]
    tm = min(TOKEN_TILE, t)

    def full(a):
        return pl.BlockSpec(a.shape, lambda i, _n=a.ndim: (0,) * _n)

    tok = lambda w: pl.BlockSpec((tm, w), lambda i: (i, 0))
    weights = [lw["mix_g"], lw["w_gate"], lw["b_gate"], lw["w_branch"], lw["w_out"], lw["ffn_g"], lw["w_router"],
               consts["triu_strict"]]
    return pl.pallas_call(
        _mixer_out_kernel,
        grid=(t // tm,),
        in_specs=[pl.BlockSpec((tm, D_MODEL), lambda i: (i + row0 // tm, 0))] + [tok(BRANCH_WIDTH)] * 4
        + [full(a) for a in weights],
        out_specs=[tok(D_MODEL), pl.BlockSpec((SC_SPLIT, tm, PLANE_W), lambda i: (0, i, 0)), tok(LANES),
                   pl.BlockSpec((8, tm), lambda i: (0, i)), pl.BlockSpec((ROUTE_ROWS, LANES), lambda i: (0, 0))],
        out_shape=[jax.ShapeDtypeStruct((t, D_MODEL), F32),
                   jax.ShapeDtypeStruct((SC_SPLIT, t, PLANE_W), I32),
                   jax.ShapeDtypeStruct((t, LANES), F32),
                   jax.ShapeDtypeStruct((8, t), F32),
                   jax.ShapeDtypeStruct((ROUTE_ROWS, LANES), F32)],
        scratch_shapes=[pltpu.VMEM((ROUTE_ROWS, LANES), F32)],
        compiler_params=pltpu.CompilerParams(dimension_semantics=("arbitrary",),
                                             vmem_limit_bytes=VMEM_LIMIT),
        name="mixer_out",
    )(x2d, ya, ysb, yfx, yc, *weights)


def _sc_mesh():
    return plsc.VectorSubcoreMesh(core_axis_name="core", subcore_axis_name="subcore")


def _plane_index(idx, rows_per_plane):
    return jnp.concatenate([idx + k * rows_per_plane for k in range(SC_SPLIT)]).reshape(1, -1)


def _dispatch_rows(planes, pos0, pos1, n_out):
    w = planes.shape[2]
    rows = planes.reshape(-1, w)
    t = rows.shape[0]
    idx0 = _plane_index(pos0, n_out)
    idx1 = _plane_index(pos1, n_out)

    @functools.partial(pl.kernel, out_type=jax.ShapeDtypeStruct((SC_SPLIT * n_out, w), rows.dtype),
                       mesh=_sc_mesh(), scratch_types=[])
    def scatter_kernel(x_hbm, i0_hbm, i1_hbm, o_hbm):
        def body(x_vmem, i0_vmem, i1_vmem):
            pltpu.sync_copy(x_vmem, o_hbm.at[i0_vmem.at[0]])
            pltpu.sync_copy(x_vmem, o_hbm.at[i1_vmem.at[0]])

        pltpu.emit_pipeline(
            body,
            grid=(t // SC_WINDOW,),
            in_specs=[pl.BlockSpec((SC_WINDOW, w), lambda i: (i, 0)),
                      pl.BlockSpec((1, SC_WINDOW), lambda i: (0, i)),
                      pl.BlockSpec((1, SC_WINDOW), lambda i: (0, i))],
            out_specs=[],
            core_axis_name=("core", "subcore"),
            dimension_semantics=(pltpu.PARALLEL,),
        )(x_hbm, i0_hbm, i1_hbm)

    return scatter_kernel(rows, idx0, idx1).reshape(SC_SPLIT, n_out, w)


def _collect_rows(planes, idx):
    n, w = planes.shape[1:]
    table = planes.reshape(-1, w)
    idx2 = _plane_index(idx, n)
    m = idx2.shape[1]

    @functools.partial(pl.kernel, out_type=jax.ShapeDtypeStruct((m, w), table.dtype), mesh=_sc_mesh(),
                       scratch_types=[])
    def gather_kernel(x_hbm, i_hbm, o_hbm):
        def body(i_vmem, o_vmem):
            pltpu.sync_copy(x_hbm.at[i_vmem.at[0]], o_vmem)

        pltpu.emit_pipeline(
            body,
            grid=(m // SC_WINDOW,),
            in_specs=[pl.BlockSpec((1, SC_WINDOW), lambda i: (0, i))],
            out_specs=[pl.BlockSpec((SC_WINDOW, w), lambda i: (i, 0))],
            core_axis_name=("core", "subcore"),
            dimension_semantics=(pltpu.PARALLEL,),
        )(i_hbm, o_hbm)

    return gather_kernel(table, idx2).reshape(SC_SPLIT, -1, w)


def _moe_ffn_kernel(te_ref, nt_ref, xs_ref, wgu_ref, wd_ref, ys_ref):
    i = pl.program_id(0)

    @pl.when(i < nt_ref[0])
    def _():
        gu = None
        for k, part in enumerate(_load_planes(xs_ref)):
            term = jnp.dot(part.astype(BF16), wgu_ref[0, k * PLANE_W:(k + 1) * PLANE_W, :],
                           preferred_element_type=F32)
            gu = term if gu is None else gu + term
        gate = gu[:, :D_EXPERT]
        hid = gate * _sigmoid(gate) * gu[:, D_EXPERT:]
        _store_planes(ys_ref, jnp.dot(hid.astype(BF16), wd_ref[0], preferred_element_type=F32))

    @pl.when(i >= nt_ref[0])
    def _():
        ys_ref[...] = jnp.zeros_like(ys_ref)


def _moe_ffn(xs, tile_expert, n_tiles, lw):
    p = xs.shape[1]
    tg = GROUP_TILE
    rows = pl.BlockSpec((SC_SPLIT, tg, PLANE_W), lambda i, te, nt: (0, i, 0))
    grid_spec = pltpu.PrefetchScalarGridSpec(
        num_scalar_prefetch=2,
        grid=(p // tg,),
        in_specs=[rows,
                  pl.BlockSpec((1, D_MODEL, 2 * D_EXPERT), lambda i, te, nt: (te[i], 0, 0)),
                  pl.BlockSpec((1, D_EXPERT, D_MODEL), lambda i, te, nt: (te[i], 0, 0))],
        out_specs=rows,
    )
    return pl.pallas_call(
        _moe_ffn_kernel,
        grid_spec=grid_spec,
        out_shape=jax.ShapeDtypeStruct((SC_SPLIT, p, PLANE_W), I32),
        compiler_params=pltpu.CompilerParams(dimension_semantics=("arbitrary",),
                                             vmem_limit_bytes=VMEM_LIMIT),
        name="moe_ffn",
    )(tile_expert, n_tiles, xs, lw["w_gu"], lw["w_down"])


def _route_plan(route_t, cnt, t):
    tg = GROUP_TILE
    n_tiles_max = (2 * t) // tg + N_EXPERTS
    counts = cnt[N_GROUPS:N_GROUPS + N_EXPERTS, 0].astype(I32)
    padded = ((counts + tg - 1) // tg) * tg
    ends = jnp.cumsum(padded)
    offs = ends - padded
    experts = jnp.arange(N_EXPERTS, dtype=I32)[:, None]

    def first_row(e):
        return jnp.sum(jnp.where(e[None, :] == experts, offs[:, None], 0), axis=0)

    fields = route_t.astype(I32)
    pos0 = first_row(fields[0]) + fields[4]
    pos1 = first_row(fields[1]) + fields[5]
    tile_start = jnp.arange(n_tiles_max, dtype=I32) * tg
    n_tiles = ends[-1] // tg
    tile_clamped = jnp.minimum(tile_start, jnp.maximum(n_tiles - 1, 0) * tg)
    tile_expert = jnp.sum((ends[None, :] <= tile_clamped[:, None]).astype(I32), axis=1)
    tile_expert = jnp.minimum(tile_expert, N_EXPERTS - 1)
    return pos0, pos1, tile_expert, n_tiles.reshape(1), n_tiles_max * tg


def _moe(xp, route_t, cnt, lw):
    t = xp.shape[1]
    pos0, pos1, tile_expert, n_tiles, p_rows = _route_plan(route_t, cnt, t)
    xs = _dispatch_rows(xp, pos0, pos1, p_rows)
    ys = _moe_ffn(xs, tile_expert, n_tiles, lw)
    return _collect_rows(ys, jnp.concatenate([pos0, pos1])).reshape(SC_SPLIT, 2, t, PLANE_W)


def _final_kernel(x_ref, y0_ref, y1_ref, rt_ref, g_ref, *rest):
    o_ref = rest[-1]
    x = _combine(x_ref[...], y0_ref, y1_ref, rt_ref[...], lead=(0,))
    o_ref[...] = _rms(x, g_ref[...])


def _final(x2d, y01, route, g, t_total, row0, out_prev):
    t = x2d.shape[0]
    tm = min(TOKEN_TILE, t)
    tok = lambda w: pl.BlockSpec((tm, w), lambda i: (i, 0))
    choice = lambda c: pl.BlockSpec((SC_SPLIT, 1, tm, PLANE_W), lambda i: (0, c, i, 0))
    in_specs = [tok(D_MODEL), choice(0), choice(1), tok(LANES), pl.BlockSpec((1, D_MODEL), lambda i: (0, 0))]
    args = [x2d, y01, y01, route, g]
    aliases = {}
    if out_prev is not None:
        in_specs.append(pl.BlockSpec(memory_space=pl.ANY))
        args.append(out_prev)
        aliases = {len(args) - 1: 0}
    return pl.pallas_call(
        _final_kernel,
        grid=(t // tm,),
        in_specs=in_specs,
        out_specs=pl.BlockSpec((tm, D_MODEL), lambda i: (i + row0 // tm, 0)),
        out_shape=jax.ShapeDtypeStruct((t_total, D_MODEL), F32),
        input_output_aliases=aliases,
        compiler_params=pltpu.CompilerParams(dimension_semantics=("arbitrary",),
                                             vmem_limit_bytes=VMEM_LIMIT),
        name="final_norm",
    )(*args)


def _constants(tm, tk):
    r = jnp.arange(tm)
    tril = (r[None, :] <= r[:, None]).astype(BF16)
    triu_strict = (r[:, None] < r[None, :]).astype(BF16)
    rk = jnp.arange(tk)
    u_incl = (rk[:, None] >= rk[None, :]).astype(BF16)
    nh = N_HEADS
    pq = jnp.zeros((3, LANES, nh * HEAD_SLAB), F32)
    pk = jnp.zeros((3, nh * HEAD_DIM, LANES), F32)
    qc = jnp.zeros((1, nh * HEAD_SLAB), F32)
    kc = jnp.zeros((nh * HEAD_DIM, LANES), F32)
    for part in range(3):
        for h in range(nh):
            pq = pq.at[part, h, h * HEAD_SLAB + HEAD_DIM + part].set(1.0)
            qc = qc.at[0, h * HEAD_SLAB + HEAD_DIM + 3 + part].set(1.0)
            kc = kc.at[h * HEAD_DIM + part, :].set(1.0)
            pk = pk.at[part, h * HEAD_DIM + 3 + part, h].set(1.0)
    vc = jnp.zeros((1, HEAD_SLAB), F32).at[0, HEAD_DIM].set(1.0)
    return {"tril": tril, "triu_strict": triu_strict, "u_incl": u_incl, "pq": pq.astype(BF16),
            "pk": pk.astype(BF16), "qc": qc, "kc": kc, "vc": vc}


def _layer_weights(layer, mix_norm_g, w_in, b_forget, conv_a_w, conf_dw_w, conf_dw_b, conf_ln_g, conf_ln_b,
                   w_branch, w_gate, b_gate, w_out, ffn_norm_g, w_router_group, w_router_expert,
                   w_expert_gate, w_expert_up, w_expert_down):
    w = w_in[layer]
    bw = BRANCH_WIDTH
    a_x, a_b, a_c, sb_q, sb_k, sb_v, fx_q, fx_k, fx_v = [w[:, i * bw:(i + 1) * bw] for i in range(9)]
    fx_f = w[:, 9 * bw:9 * bw + N_HEADS]
    conf = w[:, 9 * bw + N_HEADS:]
    scale = HEAD_DIM ** -0.5
    f_pad = jnp.pad(fx_f, ((0, 0), (0, LANES - N_HEADS)))
    w_main = jnp.concatenate([conf, f_pad, sb_q * scale, fx_q * scale, sb_v, fx_v, a_x, a_b, a_c], axis=1)
    w_t = jnp.concatenate([sb_k, fx_k], axis=1).T
    w_router = jnp.concatenate([w_router_group[layer], w_router_expert[layer].reshape(D_MODEL, N_EXPERTS)], axis=1)
    w_router = jnp.pad(w_router, ((0, 0), (0, LANES - N_GROUPS - N_EXPERTS)))
    return {
        "mix_g": mix_norm_g[layer].reshape(1, D_MODEL),
        "w_main": w_main.astype(BF16),
        "w_t": w_t.astype(BF16),
        "bf": jnp.pad(b_forget[layer], (0, LANES - N_HEADS)).reshape(1, LANES),
        "caw": conv_a_w[layer],
        "cdw": jnp.pad(conf_dw_w[layer], ((0, 1), (0, 0))),
        "cdb": conf_dw_b[layer].reshape(1, bw),
        "lng": conf_ln_g[layer].reshape(1, bw),
        "lnb": conf_ln_b[layer].reshape(1, bw),
        "w_gate": w_gate[layer].astype(BF16),
        "b_gate": b_gate[layer].reshape(4, 1, D_MODEL),
        "w_branch": w_branch[layer].astype(BF16),
        "w_out": w_out[layer].astype(BF16),
        "ffn_g": ffn_norm_g[layer].reshape(1, D_MODEL),
        "w_router": jnp.stack([w_router.T.astype(BF16), (w_router - w_router.astype(BF16).astype(F32)).T.astype(BF16)]),
        "w_gu": jnp.concatenate([w_expert_gate[layer], w_expert_up[layer]], axis=2).astype(BF16),
        "w_down": w_expert_down[layer].astype(BF16),
    }


def kernel(x, mix_norm_g, w_in, b_forget, conv_a_w, conf_dw_w, conf_dw_b, conf_ln_g, conf_ln_b, w_branch, w_gate,
           b_gate, w_out, ffn_norm_g, w_router_group, w_router_expert, w_expert_gate, w_expert_up, w_expert_down,
           final_norm_g):
    b, s, _ = x.shape
    depth = w_in.shape[0]
    consts = _constants(min(TOKEN_TILE, s), min(ATTN_TILE, s))
    parts = BATCH_SPLIT if b % BATCH_SPLIT == 0 else 1
    bp = b // parts
    tp = bp * s
    xs = [x] * parts
    combs = [None] * parts
    for layer in range(depth):
        lw = _layer_weights(layer, mix_norm_g, w_in, b_forget, conv_a_w, conf_dw_w, conf_dw_b, conf_ln_g,
                            conf_ln_b, w_branch, w_gate, b_gate, w_out, ffn_norm_g, w_router_group,
                            w_router_expert, w_expert_gate, w_expert_up, w_expert_down)
        routed = []
        for k in range(parts):
            outs = _mixer_in(xs[k], combs[k], lw, consts, b0=k * bp if layer == 0 else 0, b=bp)
            if combs[k] is not None:
                xk, *outs = outs
            else:
                xk = None
            ya, yc, q, kt, v = outs
            ysb = _attention("sb", q, kt, v, consts)
            yfx = _attention("fox", q, kt, v, consts)
            if xk is None:
                x2d_in, row0 = x.reshape(b * s, D_MODEL), k * tp
            else:
                x2d_in, row0 = xk.reshape(tp, D_MODEL), 0
            routed.append(_mixer_out(x2d_in, ya.reshape(tp, -1), ysb.reshape(tp, -1), yfx.reshape(tp, -1),
                                     yc.reshape(tp, -1), lw, consts, row0=row0))
        for k in range(parts):
            x2d, xp, route, route_t, cnt = routed[k]
            y01 = _moe(xp, route_t, cnt, lw)
            xs[k] = x2d.reshape(bp, s, D_MODEL)
            combs[k] = (y01.reshape(SC_SPLIT, 2, bp, s, PLANE_W), route.reshape(bp, s, LANES))
    out = None
    for k in range(parts):
        out = _final(xs[k].reshape(tp, D_MODEL), combs[k][0].reshape(SC_SPLIT, 2, tp, PLANE_W),
                     combs[k][1].reshape(tp, LANES), final_norm_g.reshape(1, D_MODEL), b * s, k * tp, out)
    return out.reshape(b, s, D_MODEL)
```

```python
import functools

import jax
import jax.numpy as jnp
from jax import lax
from jax.experimental import pallas as pl
from jax.experimental.pallas import tpu as pltpu
from jax.experimental.pallas import tpu_sc as plsc

F32 = jnp.float32
BF16 = jnp.bfloat16
I32 = jnp.int32

D_MODEL = 1024
BRANCH_WIDTH = 256
HEAD_DIM = 64
N_HEADS = 4
HEAD_SLAB = 128
CONV_A_WIDTH = 3
CONF_WIDTH = 31
N_GROUPS = 4
EXPERTS_PER_GROUP = 8
N_EXPERTS = 32
D_EXPERT = 256
EPS = 1e-6
LOG2E = 1.4426950408889634
LANES = 128
SUBLANES = 8
A_HALO = SUBLANES
CONF_HALO = 32
ROUTE_ROWS = 48
HALF_D = D_MODEL // 2
SC_SPLIT = 2
PLANE_W = HALF_D // SC_SPLIT

TOKEN_TILE = 512
ATTN_TILE = 256
FOX_Q_BLOCKS = 2
SB_HEADS_PER_STEP = 4
FOX_HEADS_PER_STEP = 4
ATTN_TRIPS = (4, 2, 1)
SB_GROUP = 4
GROUP_TILE = 512
CONV_ROW_CHUNK = 64
SC_WINDOW = 128
BATCH_SPLIT = 2
VMEM_LIMIT = 56 * 1024 * 1024

_C_CU, _C_CG = 0, 256
_C_F = 512
_N_FIRST = 640
_C_Q = 640
_C_V = 1152
_C_AX, _C_AB, _C_AC = 1664, 1920, 2176
_N_MAIN = 2432


def _rms(x, g):
    return x * lax.rsqrt(jnp.mean(x * x, axis=-1, keepdims=True) + EPS) * g


def _softplus(z):
    return jnp.maximum(z, 0.0) + jnp.log1p(jnp.exp(-jnp.abs(z)))


def _sigmoid(z):
    return 1.0 / (1.0 + jnp.exp(-z))


def _split3(v):
    hi = v.astype(BF16)
    r = v - hi.astype(F32)
    mid = r.astype(BF16)
    lo = (r - mid.astype(F32)).astype(BF16)
    return hi, mid, lo


def _pack_rows(v):
    lo = pltpu.bitcast(v[:, :HALF_D].astype(BF16).astype(F32), jnp.uint32)
    hi = pltpu.bitcast(v[:, HALF_D:].astype(BF16).astype(F32), jnp.uint32)
    return pltpu.bitcast((lo >> 16) | hi, I32)


def _unpack_rows(w):
    u = pltpu.bitcast(w, jnp.uint32)
    lo = pltpu.bitcast(u << 16, F32)
    hi = pltpu.bitcast(u & jnp.uint32(0xFFFF0000), F32)
    return lo, hi


def _store_planes(ref, v, lead=()):
    packed = _pack_rows(v)
    for k in range(SC_SPLIT):
        ref[(k, *lead)] = packed[:, k * PLANE_W:(k + 1) * PLANE_W]


def _load_planes(ref, lead=()):
    los, his = zip(*[_unpack_rows(ref[(k, *lead)]) for k in range(SC_SPLIT)])
    return list(los) + list(his)


def _combine(x, y0_ref, y1_ref, rt, lead=()):
    w0 = rt[:, 2:3]
    w1 = rt[:, 3:4]
    parts = [w0 * a + w1 * b for a, b in zip(_load_planes(y0_ref, lead), _load_planes(y1_ref, lead))]
    return x + jnp.concatenate(parts, axis=1)


def _mixer_in_kernel(combine, *refs):
    if combine:
        (x_ref, y0_ref, y1_ref, rt_ref, *refs) = refs
    else:
        (x_ref, *refs) = refs
    (g_ref, wm_ref, wt_ref, bf_ref, caw_ref, cdw_ref, cdb_ref, lng_ref, lnb_ref, tril_ref,
     pq_ref, qc_ref, pk_ref, kc_ref, vc_ref, *refs) = refs
    if combine:
        (xo_ref, *refs) = refs
    (ya_ref, yc_ref, q_ref, kt_ref, v_ref, bufa, bufc, dcarry) = refs

    tm = x_ref.shape[1]
    tk = kt_ref.shape[-1]

    @pl.when(pl.program_id(1) == 0)
    def _():
        bufa[0:A_HALO, :] = jnp.zeros((A_HALO, BRANCH_WIDTH), F32)
        bufc[0, 0:CONF_HALO, :] = jnp.zeros((CONF_HALO, BRANCH_WIDTH), F32)
        dcarry[...] = jnp.zeros_like(dcarry)

    x = x_ref[0]
    if combine:
        x = _combine(x, y0_ref, y1_ref, rt_ref[0], lead=(0, 0))
        xo_ref[0] = x
    xb = _rms(x, g_ref[...]).astype(BF16)

    p = jnp.concatenate([jnp.dot(xb, wm_ref[:, :_N_FIRST], preferred_element_type=F32),
                         jnp.dot(xb, wm_ref[:, _N_FIRST:], preferred_element_type=F32)], axis=1)
    pt = lax.dot_general(wt_ref[...], xb, (((1,), (1,)), ((), ())),
                         preferred_element_type=F32)

    ca = p[:, _C_AC:_C_AC + 256] * p[:, _C_AX:_C_AX + 256]
    bufa[A_HALO:A_HALO + tm, :] = ca
    caw = caw_ref[...]
    conv = caw[CONV_A_WIDTH - 1:CONV_A_WIDTH] * ca
    for k in range(CONV_A_WIDTH - 1):
        off = A_HALO - (CONV_A_WIDTH - 1) + k
        conv = conv + caw[k:k + 1] * bufa[off:off + tm, :]
    ya_ref[0] = (p[:, _C_AB:_C_AB + 256] * conv).astype(BF16)
    bufa[0:A_HALO, :] = ca[tm - A_HALO:tm]

    u = p[:, _C_CU:_C_CU + 256] * _sigmoid(p[:, _C_CG:_C_CG + 256])
    bufc[0, CONF_HALO:CONF_HALO + tm, :] = u
    for r in range(1, SUBLANES):
        bufc[r, 0:tm + CONF_HALO - SUBLANES, :] = bufc[0, r:r + tm + CONF_HALO - SUBLANES, :]
    cdw = cdw_ref[...]
    cdb = cdb_ref[...]
    lng = lng_ref[...]
    lnb = lnb_ref[...]
    for c in range(tm // CONV_ROW_CHUNK):
        row0 = c * CONV_ROW_CHUNK
        acc = jnp.broadcast_to(cdb, (CONV_ROW_CHUNK, BRANCH_WIDTH))
        for k in range(CONF_WIDTH):
            off = CONF_HALO - (CONF_WIDTH - 1) + k
            start = row0 + off - off % SUBLANES
            acc = acc + cdw[k:k + 1] * bufc[off % SUBLANES, start:start + CONV_ROW_CHUNK, :]
        mu = jnp.mean(acc, axis=-1, keepdims=True)
        cen = acc - mu
        var = jnp.mean(cen * cen, axis=-1, keepdims=True)
        yn = cen * lax.rsqrt(var + EPS) * lng + lnb
        yc_ref[0, row0:row0 + CONV_ROW_CHUNK, :] = (yn * _sigmoid(yn)).astype(BF16)
    bufc[0, 0:CONF_HALO, :] = u[tm - CONF_HALO:tm]

    logf = -_softplus(-(p[:, _C_F:_C_F + LANES] + bf_ref[...]))
    tril = tril_ref[...]
    dcum = dcarry[0:1, :]
    for part in _split3(logf):
        dcum = dcum + jnp.dot(tril, part, preferred_element_type=F32)
    dcarry[0:1, :] = dcum[tm - 1:tm, :]
    dcum_t = dcum.T
    qh, qm, ql = _split3(dcum)
    kh, km, kl = _split3(-dcum_t)
    q_extra = (jnp.dot(qh, pq_ref[0], preferred_element_type=F32)
               + jnp.dot(qm, pq_ref[1], preferred_element_type=F32)
               + jnp.dot(ql, pq_ref[2], preferred_element_type=F32) + qc_ref[...])
    k_extra = (jnp.dot(pk_ref[0], kh, preferred_element_type=F32)
               + jnp.dot(pk_ref[1], km, preferred_element_type=F32)
               + jnp.dot(pk_ref[2], kl, preferred_element_type=F32)
               + jnp.concatenate([kc_ref[...]] * (tm // LANES), axis=1))

    lane = lax.broadcasted_iota(I32, (tm, HEAD_SLAB), 1)
    low = lane < HEAD_DIM
    vc = vc_ref[...]
    for hd in range(2 * N_HEADS):
        is_fox = hd >= N_HEADS
        pair = (hd // 2) * HEAD_SLAB
        qs = p[:, _C_Q + pair:_C_Q + pair + HEAD_SLAB]
        vs = p[:, _C_V + pair:_C_V + pair + HEAD_SLAB]
        if hd % 2:
            qs = pltpu.roll(qs, HEAD_DIM, axis=1)
            vs = pltpu.roll(vs, HEAD_DIM, axis=1)
        if is_fox:
            hf = hd - N_HEADS
            qx = q_extra[:, hf * HEAD_SLAB:(hf + 1) * HEAD_SLAB]
            kx = k_extra[hf * HEAD_DIM:(hf + 1) * HEAD_DIM, :]
            vx = vc
        else:
            qx = 0.0
            kx = jnp.zeros((HEAD_DIM, tm), F32)
            vx = 0.0
        q_ref[0, hd] = jnp.where(low, qs, qx).astype(BF16)
        v_ref[0, hd] = jnp.where(low, vs, vx).astype(BF16)
        kfull = jnp.concatenate([pt[hd * HEAD_DIM:(hd + 1) * HEAD_DIM, :], kx], axis=0).astype(BF16)
        for c in range(tm // tk):
            kt_ref[0, hd, c] = kfull[:, c * tk:(c + 1) * tk]


def _mixer_in(x, comb, lw, consts, b0=0, b=None):
    s = x.shape[1]
    b = x.shape[0] if b is None else b
    tm = min(TOKEN_TILE, s)
    tk = min(ATTN_TILE, s)
    nk = s // tk
    nh2 = 2 * N_HEADS
    combine = comb is not None

    def full(a):
        return pl.BlockSpec(a.shape, lambda bi, si, _n=a.ndim: (0,) * _n)

    tok = lambda w: pl.BlockSpec((1, tm, w), lambda bi, si: (bi, si, 0))
    in_arrays = [x]
    in_specs = [pl.BlockSpec((1, tm, D_MODEL), lambda bi, si: (bi + b0, si, 0))]
    if combine:
        y01, route = comb
        in_arrays += [y01, y01, route]
        in_specs += [pl.BlockSpec((SC_SPLIT, 1, 1, tm, PLANE_W), lambda bi, si, _c=c: (0, _c, bi, si, 0))
                     for c in range(2)] + [tok(LANES)]
    weights = [lw["mix_g"], lw["w_main"], lw["w_t"], lw["bf"], lw["caw"], lw["cdw"], lw["cdb"], lw["lng"],
               lw["lnb"], consts["tril"], consts["pq"], consts["qc"], consts["pk"], consts["kc"], consts["vc"]]
    in_arrays += weights
    in_specs += [full(a) for a in weights]

    out_shape = []
    out_specs = []
    if combine:
        out_shape.append(jax.ShapeDtypeStruct((b, s, D_MODEL), F32))
        out_specs.append(tok(D_MODEL))
    out_shape += [
        jax.ShapeDtypeStruct((b, s, BRANCH_WIDTH), BF16),
        jax.ShapeDtypeStruct((b, s, BRANCH_WIDTH), BF16),
        jax.ShapeDtypeStruct((b, nh2, s, HEAD_SLAB), BF16),
        jax.ShapeDtypeStruct((b, nh2, nk, HEAD_SLAB, tk), BF16),
        jax.ShapeDtypeStruct((b, nh2, s, HEAD_SLAB), BF16),
    ]
    out_specs += [
        tok(BRANCH_WIDTH), tok(BRANCH_WIDTH),
        pl.BlockSpec((1, nh2, tm, HEAD_SLAB), lambda bi, si: (bi, 0, si, 0)),
        pl.BlockSpec((1, nh2, tm // tk, HEAD_SLAB, tk), lambda bi, si: (bi, 0, si, 0, 0)),
        pl.BlockSpec((1, nh2, tm, HEAD_SLAB), lambda bi, si: (bi, 0, si, 0)),
    ]
    return pl.pallas_call(
        functools.partial(_mixer_in_kernel, combine),
        grid=(b, s // tm),
        in_specs=in_specs,
        out_specs=out_specs,
        out_shape=out_shape,
        scratch_shapes=[pltpu.VMEM((A_HALO + tm, BRANCH_WIDTH), F32),
                        pltpu.VMEM((SUBLANES, CONF_HALO + tm, BRANCH_WIDTH), F32),
                        pltpu.VMEM((SUBLANES, LANES), F32)],
        compiler_params=pltpu.CompilerParams(dimension_semantics=("arbitrary", "arbitrary"),
                                             vmem_limit_bytes=VMEM_LIMIT),
        name="mixer_in",
    )(*in_arrays)


def _pair_out(accs):
    lane = lax.broadcasted_iota(I32, accs[0].shape, 1)
    return jnp.where(lane < HEAD_DIM, accs[0], pltpu.roll(accs[1], HEAD_DIM, axis=1))


def _sb_attn_kernel(q_ref, kt_ref, v_ref, u_ref, o_ref):
    tq = q_ref.shape[2]
    nk, tk = kt_ref.shape[2], kt_ref.shape[-1]
    i = pl.program_id(2)
    umat = u_ref[...]
    nh = q_ref.shape[1]
    nb = SB_GROUP
    qs = [q_ref[0, h] for h in range(nh)]
    row = lax.broadcasted_iota(I32, (tq, tk), 0)
    col = lax.broadcasted_iota(I32, (tq, tk), 1)

    def weights(g, tails, masked):
        ws, new_tails = [], []
        for h in range(nh):
            tail = tails[h]
            wh = []
            for u in reversed(range(nb)):
                jr = g * nb + u
                z = jnp.dot(qs[h], kt_ref[0, h, jnp.minimum(jr, nk - 1)], preferred_element_type=F32)
                sp = jnp.maximum(z, 0.0) + jnp.log(1.0 + jnp.exp2(jnp.abs(z) * -LOG2E))
                if masked:
                    mask = col + (jr - i) * tk < row
                    sp = jnp.where(mask, sp, 0.0)
                later = jnp.dot(sp.astype(BF16), umat, preferred_element_type=F32)
                w = jnp.exp((z - later - tail).astype(BF16))
                if masked:
                    w = jnp.where(mask, w, jnp.zeros_like(w))
                wh.append(w)
                tail = tail + later[:, 0:1]
            ws.append(tuple(reversed(wh)))
            new_tails.append(tail)
        return tuple(ws), tuple(new_tails)

    def apply(g, accs, ws):
        out = []
        for h in range(nh):
            acc = accs[h]
            for u in range(nb):
                start = pl.multiple_of(jnp.minimum(g * nb + u, nk - 1) * tk, tk)
                acc = acc + jnp.dot(ws[h][u], v_ref[0, h, pl.ds(start, tk), :], preferred_element_type=F32)
            out.append(acc)
        return tuple(out)

    last = i // nb
    zero_t = tuple(jnp.zeros((tq, 1), F32) for _ in range(nh))
    accs = tuple(jnp.zeros((tq, HEAD_SLAB), F32) for _ in range(nh))
    ws, tails = weights(last, zero_t, True)

    def body(t, carry):
        accs, tails, ws = carry
        g = last - 1 - t
        accs = apply(g + 1, accs, ws)
        ws, tails = weights(g, tails, False)
        return accs, tails, ws

    accs, tails, ws = lax.fori_loop(0, last, body, (accs, tails, ws))
    accs = apply(0, accs, ws)
    o_ref[0] = jnp.concatenate([_pair_out(accs[k:k + 2]) for k in range(0, nh, 2)], axis=1).astype(BF16)


def _fox_attn_kernel(q_ref, kt_ref, v_ref, o_ref):
    tq = q_ref.shape[2]
    tk = kt_ref.shape[-1]
    i = pl.program_id(2)
    row = lax.broadcasted_iota(I32, (tq, tk), 0)
    col = lax.broadcasted_iota(I32, (tq, tk), 1)
    nh = q_ref.shape[1]
    qs = [q_ref[0, h] for h in range(nh)]

    def vblock(h, j):
        return v_ref[0, h, pl.ds(pl.multiple_of(j * tk, tk), tk), :]

    r = tq // tk
    accs, maxes = [], []
    for h in range(nh):
        ss = [jnp.where(col + d * tk <= row, jnp.dot(qs[h], kt_ref[0, h, i * r + d], preferred_element_type=F32),
                        -jnp.inf) for d in range(r)]
        m = jnp.max(ss[0], axis=-1, keepdims=True)
        for s in ss[1:]:
            m = jnp.maximum(m, jnp.max(s, axis=-1, keepdims=True))
        acc = None
        for d, s in enumerate(ss):
            term = jnp.dot(jnp.exp(s - m).astype(BF16), vblock(h, i * r + d), preferred_element_type=F32)
            acc = term if acc is None else acc + term
        accs.append(acc)
        maxes.append(m)

    def make_body(n_blocks, first):
        def body(jj, carry):
            accs, maxes = carry
            new_accs, new_maxes = [], []
            for h in range(nh):
                js = [first + n_blocks * jj + u for u in range(n_blocks)]
                ss = [jnp.dot(qs[h], kt_ref[0, h, j], preferred_element_type=F32) for j in js]
                m = maxes[h]
                for s in ss:
                    m = jnp.maximum(m, jnp.max(s, axis=-1, keepdims=True))
                acc = jnp.exp(maxes[h] - m) * accs[h]
                for s, j in zip(ss, js):
                    acc = acc + jnp.dot(jnp.exp(s - m).astype(BF16), vblock(h, j), preferred_element_type=F32)
                new_accs.append(acc)
                new_maxes.append(m)
            return tuple(new_accs), tuple(new_maxes)
        return body

    carry = (tuple(accs), tuple(maxes))
    done = 0
    for n_blocks in ATTN_TRIPS:
        trips = (i * r - done) // n_blocks
        carry = lax.fori_loop(0, trips, make_body(n_blocks, done), carry)
        done = done + trips * n_blocks
    accs = carry[0]
    outs = [a / a[:, HEAD_DIM:HEAD_DIM + 1] for a in accs]
    o_ref[0] = jnp.concatenate([_pair_out(outs[k:k + 2]) for k in range(0, nh, 2)], axis=1).astype(BF16)


def _attention(kind, q, kt, v, consts):
    b, _, s, _ = q.shape
    nk, tk = kt.shape[2], kt.shape[4]
    tq = tk if kind == "sb" else min(FOX_Q_BLOCKS * tk, s)
    hps = SB_HEADS_PER_STEP if kind == "sb" else FOX_HEADS_PER_STEP
    head0 = 0 if kind == "sb" else N_HEADS // hps
    in_specs = [
        pl.BlockSpec((1, hps, tq, HEAD_SLAB), lambda bi, hp, i: (bi, hp + head0, i, 0)),
        pl.BlockSpec((1, hps, nk, HEAD_SLAB, tk), lambda bi, hp, i: (bi, hp + head0, 0, 0, 0)),
        pl.BlockSpec((1, hps, s, HEAD_SLAB), lambda bi, hp, i: (bi, hp + head0, 0, 0)),
    ]
    args = [q, kt, v]
    if kind == "sb":
        in_specs.append(pl.BlockSpec((tk, tk), lambda bi, hp, i: (0, 0)))
        args.append(consts["u_incl"])
        body = _sb_attn_kernel
    else:
        body = _fox_attn_kernel
    return pl.pallas_call(
        body,
        grid=(b, N_HEADS // hps, s // tq),
        in_specs=in_specs,
        out_specs=pl.BlockSpec((1, tq, hps * HEAD_DIM), lambda bi, hp, i: (bi, i, hp)),
        out_shape=jax.ShapeDtypeStruct((b, s, BRANCH_WIDTH), BF16),
        compiler_params=pltpu.CompilerParams(dimension_semantics=("arbitrary", "arbitrary", "arbitrary"),
                                             vmem_limit_bytes=VMEM_LIMIT),
        name=kind + "_attn",
    )(*args)


def _mixer_out_kernel(x_ref, ya_ref, ysb_ref, yfx_ref, yc_ref, g_ref, wg_ref, bg_ref, wb_ref, wo_ref,
                      fg_ref, wr_ref, tru_ref, xo_ref, xp_ref, rt_ref, rtt_ref, cnt_ref, carry):
    tm = x_ref.shape[0]

    @pl.when(pl.program_id(0) == 0)
    def _():
        carry[...] = jnp.zeros_like(carry)

    x = x_ref[...]
    xb = _rms(x, g_ref[...]).astype(BF16)
    h = None
    for g, y_ref in enumerate((ya_ref, ysb_ref, yfx_ref, yc_ref)):
        gate = _sigmoid(jnp.dot(xb, wg_ref[g], preferred_element_type=F32) + bg_ref[g])
        term = gate * jnp.dot(y_ref[...], wb_ref[g], preferred_element_type=F32)
        h = term if h is None else h + term
    xo = x + jnp.dot(h.astype(BF16), wo_ref[...], preferred_element_type=F32)
    xo_ref[...] = xo
    xn = _rms(xo, fg_ref[...])
    _store_planes(xp_ref, xn)

    xh = xn.astype(BF16)
    xl = (xn - xh.astype(F32)).astype(BF16)
    nt = (((1,), (1,)), ((), ()))
    logits = (lax.dot_general(wr_ref[0], xh, nt, preferred_element_type=F32)
              + lax.dot_general(wr_ref[0], xl, nt, preferred_element_type=F32)
              + lax.dot_general(wr_ref[1], xh, nt, preferred_element_type=F32))[0:ROUTE_ROWS, :]
    row = lax.broadcasted_iota(I32, (ROUTE_ROWS, tm), 0).astype(F32)
    ninf = -jnp.inf
    big = float(LANES)
    gl = jnp.where(row < N_GROUPS, logits, ninf)
    gmax = jnp.max(gl, axis=0, keepdims=True)
    gidx = jnp.min(jnp.where(gl == gmax, row, big), axis=0, keepdims=True)
    p_group = 1.0 / jnp.sum(jnp.exp(gl - gmax), axis=0, keepdims=True)
    first = N_GROUPS + EXPERTS_PER_GROUP * gidx
    el = jnp.where((row >= first) & (row < first + EXPERTS_PER_GROUP), logits, ninf)
    m1 = jnp.max(el, axis=0, keepdims=True)
    i1 = jnp.min(jnp.where(el == m1, row, big), axis=0, keepdims=True)
    el2 = jnp.where(row == i1, ninf, el)
    m2 = jnp.max(el2, axis=0, keepdims=True)
    i2 = jnp.min(jnp.where(el2 == m2, row, big), axis=0, keepdims=True)
    e2 = jnp.exp(m2 - m1)
    w1 = p_group / (1.0 + e2)
    w2 = w1 * e2

    sel1 = row == i1
    sel2 = row == i2
    onehot = jnp.where(sel1, 1.0, jnp.where(sel2, 1.0, 0.0))
    before = jnp.dot(onehot.astype(BF16), tru_ref[...], preferred_element_type=F32) + carry[:, 0:1]
    r1 = jnp.sum(jnp.where(sel1, before, 0.0), axis=0, keepdims=True)
    r2 = jnp.sum(jnp.where(sel2, before, 0.0), axis=0, keepdims=True)
    total = carry[...] + jnp.sum(onehot, axis=1, keepdims=True)
    carry[...] = total
    cnt_ref[...] = total

    row8 = lax.broadcasted_iota(I32, (8, tm), 0)
    fields = (i1 - N_GROUPS, i2 - N_GROUPS, w1, w2, r1, r2)
    rtt = jnp.zeros((8, tm), F32)
    for k, f in enumerate(fields):
        rtt = jnp.where(row8 == k, f, rtt)
    rtt_ref[...] = rtt
    rt_ref[...] = jnp.concatenate([rtt, jnp.zeros((LANES - 8, tm), F32)], axis=0).T


def _mixer_out(x2d, ya, ysb, yfx, yc, lw, consts, row0=0):
    t = ya.shape[0]
    tm = min(TOKEN_TILE, t)

    def full(a):
        return pl.BlockSpec(a.shape, lambda i, _n=a.ndim: (0,) * _n)

    tok = lambda w: pl.BlockSpec((tm, w), lambda i: (i, 0))
    weights = [lw["mix_g"], lw["w_gate"], lw["b_gate"], lw["w_branch"], lw["w_out"], lw["ffn_g"], lw["w_router"],
               consts["triu_strict"]]
    return pl.pallas_call(
        _mixer_out_kernel,
        grid=(t // tm,),
        in_specs=[pl.BlockSpec((tm, D_MODEL), lambda i: (i + row0 // tm, 0))] + [tok(BRANCH_WIDTH)] * 4
        + [full(a) for a in weights],
        out_specs=[tok(D_MODEL), pl.BlockSpec((SC_SPLIT, tm, PLANE_W), lambda i: (0, i, 0)), tok(LANES),
                   pl.BlockSpec((8, tm), lambda i: (0, i)), pl.BlockSpec((ROUTE_ROWS, LANES), lambda i: (0, 0))],
        out_shape=[jax.ShapeDtypeStruct((t, D_MODEL), F32),
                   jax.ShapeDtypeStruct((SC_SPLIT, t, PLANE_W), I32),
                   jax.ShapeDtypeStruct((t, LANES), F32),
                   jax.ShapeDtypeStruct((8, t), F32),
                   jax.ShapeDtypeStruct((ROUTE_ROWS, LANES), F32)],
        scratch_shapes=[pltpu.VMEM((ROUTE_ROWS, LANES), F32)],
        compiler_params=pltpu.CompilerParams(dimension_semantics=("arbitrary",),
                                             vmem_limit_bytes=VMEM_LIMIT),
        name="mixer_out",
    )(x2d, ya, ysb, yfx, yc, *weights)


def _sc_mesh():
    return plsc.VectorSubcoreMesh(core_axis_name="core", subcore_axis_name="subcore")


def _plane_index(idx, rows_per_plane):
    return jnp.concatenate([idx + k * rows_per_plane for k in range(SC_SPLIT)]).reshape(1, -1)


def _dispatch_rows(planes, pos0, pos1, n_out):
    w = planes.shape[2]
    rows = planes.reshape(-1, w)
    t = rows.shape[0]
    idx0 = _plane_index(pos0, n_out)
    idx1 = _plane_index(pos1, n_out)

    @functools.partial(pl.kernel, out_type=jax.ShapeDtypeStruct((SC_SPLIT * n_out, w), rows.dtype),
                       mesh=_sc_mesh(), scratch_types=[])
    def scatter_kernel(x_hbm, i0_hbm, i1_hbm, o_hbm):
        def body(x_vmem, i0_vmem, i1_vmem):
            pltpu.sync_copy(x_vmem, o_hbm.at[i0_vmem.at[0]])
            pltpu.sync_copy(x_vmem, o_hbm.at[i1_vmem.at[0]])

        pltpu.emit_pipeline(
            body,
            grid=(t // SC_WINDOW,),
            in_specs=[pl.BlockSpec((SC_WINDOW, w), lambda i: (i, 0)),
                      pl.BlockSpec((1, SC_WINDOW), lambda i: (0, i)),
                      pl.BlockSpec((1, SC_WINDOW), lambda i: (0, i))],
            out_specs=[],
            core_axis_name=("core", "subcore"),
            dimension_semantics=(pltpu.PARALLEL,),
        )(x_hbm, i0_hbm, i1_hbm)

    return scatter_kernel(rows, idx0, idx1).reshape(SC_SPLIT, n_out, w)


def _collect_rows(planes, idx):
    n, w = planes.shape[1:]
    table = planes.reshape(-1, w)
    idx2 = _plane_index(idx, n)
    m = idx2.shape[1]

    @functools.partial(pl.kernel, out_type=jax.ShapeDtypeStruct((m, w), table.dtype), mesh=_sc_mesh(),
                       scratch_types=[])
    def gather_kernel(x_hbm, i_hbm, o_hbm):
        def body(i_vmem, o_vmem):
            pltpu.sync_copy(x_hbm.at[i_vmem.at[0]], o_vmem)

        pltpu.emit_pipeline(
            body,
            grid=(m // SC_WINDOW,),
            in_specs=[pl.BlockSpec((1, SC_WINDOW), lambda i: (0, i))],
            out_specs=[pl.BlockSpec((SC_WINDOW, w), lambda i: (i, 0))],
            core_axis_name=("core", "subcore"),
            dimension_semantics=(pltpu.PARALLEL,),
        )(i_hbm, o_hbm)

    return gather_kernel(table, idx2).reshape(SC_SPLIT, -1, w)


def _moe_ffn_kernel(te_ref, nt_ref, xs_ref, wgu_ref, wd_ref, ys_ref):
    i = pl.program_id(0)

    @pl.when(i < nt_ref[0])
    def _():
        gu = None
        for k, part in enumerate(_load_planes(xs_ref)):
            term = jnp.dot(part.astype(BF16), wgu_ref[0, k * PLANE_W:(k + 1) * PLANE_W, :],
                           preferred_element_type=F32)
            gu = term if gu is None else gu + term
        gate = gu[:, :D_EXPERT]
        hid = gate * _sigmoid(gate) * gu[:, D_EXPERT:]
        _store_planes(ys_ref, jnp.dot(hid.astype(BF16), wd_ref[0], preferred_element_type=F32))

    @pl.when(i >= nt_ref[0])
    def _():
        ys_ref[...] = jnp.zeros_like(ys_ref)


def _moe_ffn(xs, tile_expert, n_tiles, lw):
    p = xs.shape[1]
    tg = GROUP_TILE
    rows = pl.BlockSpec((SC_SPLIT, tg, PLANE_W), lambda i, te, nt: (0, i, 0))
    grid_spec = pltpu.PrefetchScalarGridSpec(
        num_scalar_prefetch=2,
        grid=(p // tg,),
        in_specs=[rows,
                  pl.BlockSpec((1, D_MODEL, 2 * D_EXPERT), lambda i, te, nt: (te[i], 0, 0)),
                  pl.BlockSpec((1, D_EXPERT, D_MODEL), lambda i, te, nt: (te[i], 0, 0))],
        out_specs=rows,
    )
    return pl.pallas_call(
        _moe_ffn_kernel,
        grid_spec=grid_spec,
        out_shape=jax.ShapeDtypeStruct((SC_SPLIT, p, PLANE_W), I32),
        compiler_params=pltpu.CompilerParams(dimension_semantics=("arbitrary",),
                                             vmem_limit_bytes=VMEM_LIMIT),
        name="moe_ffn",
    )(tile_expert, n_tiles, xs, lw["w_gu"], lw["w_down"])


def _route_plan(route_t, cnt, t):
    tg = GROUP_TILE
    n_tiles_max = (2 * t) // tg + N_EXPERTS
    counts = cnt[N_GROUPS:N_GROUPS + N_EXPERTS, 0].astype(I32)
    padded = ((counts + tg - 1) // tg) * tg
    ends = jnp.cumsum(padded)
    offs = ends - padded
    experts = jnp.arange(N_EXPERTS, dtype=I32)[:, None]

    def first_row(e):
        return jnp.sum(jnp.where(e[None, :] == experts, offs[:, None], 0), axis=0)

    fields = route_t.astype(I32)
    pos0 = first_row(fields[0]) + fields[4]
    pos1 = first_row(fields[1]) + fields[5]
    tile_start = jnp.arange(n_tiles_max, dtype=I32) * tg
    n_tiles = ends[-1] // tg
    tile_clamped = jnp.minimum(tile_start, jnp.maximum(n_tiles - 1, 0) * tg)
    tile_expert = jnp.sum((ends[None, :] <= tile_clamped[:, None]).astype(I32), axis=1)
    tile_expert = jnp.minimum(tile_expert, N_EXPERTS - 1)
    return pos0, pos1, tile_expert, n_tiles.reshape(1), n_tiles_max * tg


def _moe(xp, route_t, cnt, lw):
    t = xp.shape[1]
    pos0, pos1, tile_expert, n_tiles, p_rows = _route_plan(route_t, cnt, t)
    xs = _dispatch_rows(xp, pos0, pos1, p_rows)
    ys = _moe_ffn(xs, tile_expert, n_tiles, lw)
    return _collect_rows(ys, jnp.concatenate([pos0, pos1])).reshape(SC_SPLIT, 2, t, PLANE_W)


def _final_kernel(x_ref, y0_ref, y1_ref, rt_ref, g_ref, *rest):
    o_ref = rest[-1]
    x = _combine(x_ref[...], y0_ref, y1_ref, rt_ref[...], lead=(0,))
    o_ref[...] = _rms(x, g_ref[...])


def _final(x2d, y01, route, g, t_total, row0, out_prev):
    t = x2d.shape[0]
    tm = min(TOKEN_TILE, t)
    tok = lambda w: pl.BlockSpec((tm, w), lambda i: (i, 0))
    choice = lambda c: pl.BlockSpec((SC_SPLIT, 1, tm, PLANE_W), lambda i: (0, c, i, 0))
    in_specs = [tok(D_MODEL), choice(0), choice(1), tok(LANES), pl.BlockSpec((1, D_MODEL), lambda i: (0, 0))]
    args = [x2d, y01, y01, route, g]
    aliases = {}
    if out_prev is not None:
        in_specs.append(pl.BlockSpec(memory_space=pl.ANY))
        args.append(out_prev)
        aliases = {len(args) - 1: 0}
    return pl.pallas_call(
        _final_kernel,
        grid=(t // tm,),
        in_specs=in_specs,
        out_specs=pl.BlockSpec((tm, D_MODEL), lambda i: (i + row0 // tm, 0)),
        out_shape=jax.ShapeDtypeStruct((t_total, D_MODEL), F32),
        input_output_aliases=aliases,
        compiler_params=pltpu.CompilerParams(dimension_semantics=("arbitrary",),
                                             vmem_limit_bytes=VMEM_LIMIT),
        name="final_norm",
    )(*args)


def _constants(tm, tk):
    r = jnp.arange(tm)
    tril = (r[None, :] <= r[:, None]).astype(BF16)
    triu_strict = (r[:, None] < r[None, :]).astype(BF16)
    rk = jnp.arange(tk)
    u_incl = (rk[:, None] >= rk[None, :]).astype(BF16)
    nh = N_HEADS
    pq = jnp.zeros((3, LANES, nh * HEAD_SLAB), F32)
    pk = jnp.zeros((3, nh * HEAD_DIM, LANES), F32)
    qc = jnp.zeros((1, nh * HEAD_SLAB), F32)
    kc = jnp.zeros((nh * HEAD_DIM, LANES), F32)
    for part in range(3):
        for h in range(nh):
            pq = pq.at[part, h, h * HEAD_SLAB + HEAD_DIM + part].set(1.0)
            qc = qc.at[0, h * HEAD_SLAB + HEAD_DIM + 3 + part].set(1.0)
            kc = kc.at[h * HEAD_DIM + part, :].set(1.0)
            pk = pk.at[part, h * HEAD_DIM + 3 + part, h].set(1.0)
    vc = jnp.zeros((1, HEAD_SLAB), F32).at[0, HEAD_DIM].set(1.0)
    return {"tril": tril, "triu_strict": triu_strict, "u_incl": u_incl, "pq": pq.astype(BF16),
            "pk": pk.astype(BF16), "qc": qc, "kc": kc, "vc": vc}


def _layer_weights(layer, mix_norm_g, w_in, b_forget, conv_a_w, conf_dw_w, conf_dw_b, conf_ln_g, conf_ln_b,
                   w_branch, w_gate, b_gate, w_out, ffn_norm_g, w_router_group, w_router_expert,
                   w_expert_gate, w_expert_up, w_expert_down):
    w = w_in[layer]
    bw = BRANCH_WIDTH
    a_x, a_b, a_c, sb_q, sb_k, sb_v, fx_q, fx_k, fx_v = [w[:, i * bw:(i + 1) * bw] for i in range(9)]
    fx_f = w[:, 9 * bw:9 * bw + N_HEADS]
    conf = w[:, 9 * bw + N_HEADS:]
    scale = HEAD_DIM ** -0.5
    f_pad = jnp.pad(fx_f, ((0, 0), (0, LANES - N_HEADS)))
    w_main = jnp.concatenate([conf, f_pad, sb_q * scale, fx_q * scale, sb_v, fx_v, a_x, a_b, a_c], axis=1)
    w_t = jnp.concatenate([sb_k, fx_k], axis=1).T
    w_router = jnp.concatenate([w_router_group[layer], w_router_expert[layer].reshape(D_MODEL, N_EXPERTS)], axis=1)
    w_router = jnp.pad(w_router, ((0, 0), (0, LANES - N_GROUPS - N_EXPERTS)))
    return {
        "mix_g": mix_norm_g[layer].reshape(1, D_MODEL),
        "w_main": w_main.astype(BF16),
        "w_t": w_t.astype(BF16),
        "bf": jnp.pad(b_forget[layer], (0, LANES - N_HEADS)).reshape(1, LANES),
        "caw": conv_a_w[layer],
        "cdw": jnp.pad(conf_dw_w[layer], ((0, 1), (0, 0))),
        "cdb": conf_dw_b[layer].reshape(1, bw),
        "lng": conf_ln_g[layer].reshape(1, bw),
        "lnb": conf_ln_b[layer].reshape(1, bw),
        "w_gate": w_gate[layer].astype(BF16),
        "b_gate": b_gate[layer].reshape(4, 1, D_MODEL),
        "w_branch": w_branch[layer].astype(BF16),
        "w_out": w_out[layer].astype(BF16),
        "ffn_g": ffn_norm_g[layer].reshape(1, D_MODEL),
        "w_router": jnp.stack([w_router.T.astype(BF16), (w_router - w_router.astype(BF16).astype(F32)).T.astype(BF16)]),
        "w_gu": jnp.concatenate([w_expert_gate[layer], w_expert_up[layer]], axis=2).astype(BF16),
        "w_down": w_expert_down[layer].astype(BF16),
    }


def kernel(x, mix_norm_g, w_in, b_forget, conv_a_w, conf_dw_w, conf_dw_b, conf_ln_g, conf_ln_b, w_branch, w_gate,
           b_gate, w_out, ffn_norm_g, w_router_group, w_router_expert, w_expert_gate, w_expert_up, w_expert_down,
           final_norm_g):
    b, s, _ = x.shape
    depth = w_in.shape[0]
    consts = _constants(min(TOKEN_TILE, s), min(ATTN_TILE, s))
    parts = BATCH_SPLIT if b % BATCH_SPLIT == 0 else 1
    bp = b // parts
    tp = bp * s
    xs = [x] * parts
    combs = [None] * parts
    for layer in range(depth):
        lw = _layer_weights(layer, mix_norm_g, w_in, b_forget, conv_a_w, conf_dw_w, conf_dw_b, conf_ln_g,
                            conf_ln_b, w_branch, w_gate, b_gate, w_out, ffn_norm_g, w_router_group,
                            w_router_expert, w_expert_gate, w_expert_up, w_expert_down)
        routed = []
        for k in range(parts):
            outs = _mixer_in(xs[k], combs[k], lw, consts, b0=k * bp if layer == 0 else 0, b=bp)
            if combs[k] is not None:
                xk, *outs = outs
            else:
                xk = None
            ya, yc, q, kt, v = outs
            ysb = _attention("sb", q, kt, v, consts)
            yfx = _attention("fox", q, kt, v, consts)
            if xk is None:
                x2d_in, row0 = x.reshape(b * s, D_MODEL), k * tp
            else:
                x2d_in, row0 = xk.reshape(tp, D_MODEL), 0
            routed.append(_mixer_out(x2d_in, ya.reshape(tp, -1), ysb.reshape(tp, -1), yfx.reshape(tp, -1),
                                     yc.reshape(tp, -1), lw, consts, row0=row0))
        for k in range(parts):
            x2d, xp, route, route_t, cnt = routed[k]
            y01 = _moe(xp, route_t, cnt, lw)
            xs[k] = x2d.reshape(bp, s, D_MODEL)
            combs[k] = (y01.reshape(SC_SPLIT, 2, bp, s, PLANE_W), route.reshape(bp, s, LANES))
    out = None
    for k in range(parts):
        out = _final(xs[k].reshape(tp, D_MODEL), combs[k][0].reshape(SC_SPLIT, 2, tp, PLANE_W),
                     combs[k][1].reshape(tp, LANES), final_norm_g.reshape(1, D_MODEL), b * s, k * tp, out)
    return out.reshape(b, s, D_MODEL)
```

```python
import functools

import jax
import jax.numpy as jnp
from jax import lax
from jax.experimental import pallas as pl
from jax.experimental.pallas import tpu as pltpu
from jax.experimental.pallas import tpu_sc as plsc

F32 = jnp.float32
BF16 = jnp.bfloat16
I32 = jnp.int32

D_MODEL = 1024
BRANCH_WIDTH = 256
HEAD_DIM = 64
N_HEADS = 4
HEAD_SLAB = 128
CONV_A_WIDTH = 3
CONF_WIDTH = 31
N_GROUPS = 4
EXPERTS_PER_GROUP = 8
N_EXPERTS = 32
D_EXPERT = 256
EPS = 1e-6
LOG2E = 1.4426950408889634
LANES = 128
SUBLANES = 8
A_HALO = SUBLANES
CONF_HALO = 32
ROUTE_ROWS = 48
HALF_D = D_MODEL // 2
SC_SPLIT = 2
PLANE_W = HALF_D // SC_SPLIT

TOKEN_TILE = 512
ATTN_TILE = 256
FOX_Q_BLOCKS = 2
SB_HEADS_PER_STEP = 4
FOX_HEADS_PER_STEP = 4
ATTN_TRIPS = (8, 4, 2, 1)
SB_GROUP = 4
GROUP_TILE = 512
CONV_ROW_CHUNK = 64
SC_WINDOW = 128
BATCH_SPLIT = 2
VMEM_LIMIT = 56 * 1024 * 1024

_C_CU, _C_CG = 0, 256
_C_F = 512
_N_FIRST = 640
_C_Q = 640
_C_V = 1152
_C_AX, _C_AB, _C_AC = 1664, 1920, 2176
_N_MAIN = 2432


def _rms(x, g):
    return x * lax.rsqrt(jnp.mean(x * x, axis=-1, keepdims=True) + EPS) * g


def _softplus(z):
    return jnp.maximum(z, 0.0) + jnp.log1p(jnp.exp(-jnp.abs(z)))


def _sigmoid(z):
    return 1.0 / (1.0 + jnp.exp(-z))


def _split3(v):
    hi = v.astype(BF16)
    r = v - hi.astype(F32)
    mid = r.astype(BF16)
    lo = (r - mid.astype(F32)).astype(BF16)
    return hi, mid, lo


def _pack_rows(v):
    lo = pltpu.bitcast(v[:, :HALF_D].astype(BF16).astype(F32), jnp.uint32)
    hi = pltpu.bitcast(v[:, HALF_D:].astype(BF16).astype(F32), jnp.uint32)
    return pltpu.bitcast((lo >> 16) | hi, I32)


def _unpack_rows(w):
    u = pltpu.bitcast(w, jnp.uint32)
    lo = pltpu.bitcast(u << 16, F32)
    hi = pltpu.bitcast(u & jnp.uint32(0xFFFF0000), F32)
    return lo, hi


def _store_planes(ref, v, lead=()):
    packed = _pack_rows(v)
    for k in range(SC_SPLIT):
        ref[(k, *lead)] = packed[:, k * PLANE_W:(k + 1) * PLANE_W]


def _load_planes(ref, lead=()):
    los, his = zip(*[_unpack_rows(ref[(k, *lead)]) for k in range(SC_SPLIT)])
    return list(los) + list(his)


def _combine(x, y0_ref, y1_ref, rt, lead=()):
    w0 = rt[:, 2:3]
    w1 = rt[:, 3:4]
    parts = [w0 * a + w1 * b for a, b in zip(_load_planes(y0_ref, lead), _load_planes(y1_ref, lead))]
    return x + jnp.concatenate(parts, axis=1)


def _mixer_in_kernel(combine, *refs):
    if combine:
        (x_ref, y0_ref, y1_ref, rt_ref, *refs) = refs
    else:
        (x_ref, *refs) = refs
    (g_ref, wm_ref, wt_ref, bf_ref, caw_ref, cdw_ref, cdb_ref, lng_ref, lnb_ref, tril_ref,
     pq_ref, qc_ref, pk_ref, kc_ref, vc_ref, *refs) = refs
    if combine:
        (xo_ref, *refs) = refs
    (ya_ref, yc_ref, q_ref, kt_ref, v_ref, bufa, bufc, dcarry) = refs

    tm = x_ref.shape[1]
    tk = kt_ref.shape[-1]

    @pl.when(pl.program_id(1) == 0)
    def _():
        bufa[0:A_HALO, :] = jnp.zeros((A_HALO, BRANCH_WIDTH), F32)
        bufc[0, 0:CONF_HALO, :] = jnp.zeros((CONF_HALO, BRANCH_WIDTH), F32)
        dcarry[...] = jnp.zeros_like(dcarry)

    x = x_ref[0]
    if combine:
        x = _combine(x, y0_ref, y1_ref, rt_ref[0], lead=(0, 0))
        xo_ref[0] = x
    xb = _rms(x, g_ref[...]).astype(BF16)

    p = jnp.concatenate([jnp.dot(xb, wm_ref[:, :_N_FIRST], preferred_element_type=F32),
                         jnp.dot(xb, wm_ref[:, _N_FIRST:], preferred_element_type=F32)], axis=1)
    pt = lax.dot_general(wt_ref[...], xb, (((1,), (1,)), ((), ())),
                         preferred_element_type=F32)

    ca = p[:, _C_AC:_C_AC + 256] * p[:, _C_AX:_C_AX + 256]
    bufa[A_HALO:A_HALO + tm, :] = ca
    caw = caw_ref[...]
    conv = caw[CONV_A_WIDTH - 1:CONV_A_WIDTH] * ca
    for k in range(CONV_A_WIDTH - 1):
        off = A_HALO - (CONV_A_WIDTH - 1) + k
        conv = conv + caw[k:k + 1] * bufa[off:off + tm, :]
    ya_ref[0] = (p[:, _C_AB:_C_AB + 256] * conv).astype(BF16)
    bufa[0:A_HALO, :] = ca[tm - A_HALO:tm]

    u = p[:, _C_CU:_C_CU + 256] * _sigmoid(p[:, _C_CG:_C_CG + 256])
    bufc[0, CONF_HALO:CONF_HALO + tm, :] = u
    for r in range(1, SUBLANES):
        bufc[r, 0:tm + CONF_HALO - SUBLANES, :] = bufc[0, r:r + tm + CONF_HALO - SUBLANES, :]
    cdw = cdw_ref[...]
    cdb = cdb_ref[...]
    lng = lng_ref[...]
    lnb = lnb_ref[...]
    for c in range(tm // CONV_ROW_CHUNK):
        row0 = c * CONV_ROW_CHUNK
        acc = jnp.broadcast_to(cdb, (CONV_ROW_CHUNK, BRANCH_WIDTH))
        for k in range(CONF_WIDTH):
            off = CONF_HALO - (CONF_WIDTH - 1) + k
            start = row0 + off - off % SUBLANES
            acc = acc + cdw[k:k + 1] * bufc[off % SUBLANES, start:start + CONV_ROW_CHUNK, :]
        mu = jnp.mean(acc, axis=-1, keepdims=True)
        cen = acc - mu
        var = jnp.mean(cen * cen, axis=-1, keepdims=True)
        yn = cen * lax.rsqrt(var + EPS) * lng + lnb
        yc_ref[0, row0:row0 + CONV_ROW_CHUNK, :] = (yn * _sigmoid(yn)).astype(BF16)
    bufc[0, 0:CONF_HALO, :] = u[tm - CONF_HALO:tm]

    logf = -_softplus(-(p[:, _C_F:_C_F + LANES] + bf_ref[...]))
    tril = tril_ref[...]
    dcum = dcarry[0:1, :]
    for part in _split3(logf):
        dcum = dcum + jnp.dot(tril, part, preferred_element_type=F32)
    dcarry[0:1, :] = dcum[tm - 1:tm, :]
    dcum_t = dcum.T
    qh, qm, ql = _split3(dcum)
    kh, km, kl = _split3(-dcum_t)
    q_extra = (jnp.dot(qh, pq_ref[0], preferred_element_type=F32)
               + jnp.dot(qm, pq_ref[1], preferred_element_type=F32)
               + jnp.dot(ql, pq_ref[2], preferred_element_type=F32) + qc_ref[...])
    k_extra = (jnp.dot(pk_ref[0], kh, preferred_element_type=F32)
               + jnp.dot(pk_ref[1], km, preferred_element_type=F32)
               + jnp.dot(pk_ref[2], kl, preferred_element_type=F32)
               + jnp.concatenate([kc_ref[...]] * (tm // LANES), axis=1))

    lane = lax.broadcasted_iota(I32, (tm, HEAD_SLAB), 1)
    low = lane < HEAD_DIM
    vc = vc_ref[...]
    for hd in range(2 * N_HEADS):
        is_fox = hd >= N_HEADS
        pair = (hd // 2) * HEAD_SLAB
        qs = p[:, _C_Q + pair:_C_Q + pair + HEAD_SLAB]
        vs = p[:, _C_V + pair:_C_V + pair + HEAD_SLAB]
        if hd % 2:
            qs = pltpu.roll(qs, HEAD_DIM, axis=1)
            vs = pltpu.roll(vs, HEAD_DIM, axis=1)
        if is_fox:
            hf = hd - N_HEADS
            qx = q_extra[:, hf * HEAD_SLAB:(hf + 1) * HEAD_SLAB]
            kx = k_extra[hf * HEAD_DIM:(hf + 1) * HEAD_DIM, :]
            vx = vc
        else:
            qx = 0.0
            kx = jnp.zeros((HEAD_DIM, tm), F32)
            vx = 0.0
        q_ref[0, hd] = jnp.where(low, qs, qx).astype(BF16)
        v_ref[0, hd] = jnp.where(low, vs, vx).astype(BF16)
        kfull = jnp.concatenate([pt[hd * HEAD_DIM:(hd + 1) * HEAD_DIM, :], kx], axis=0).astype(BF16)
        for c in range(tm // tk):
            kt_ref[0, hd, c] = kfull[:, c * tk:(c + 1) * tk]


def _mixer_in(x, comb, lw, consts, b0=0, b=None):
    s = x.shape[1]
    b = x.shape[0] if b is None else b
    tm = min(TOKEN_TILE, s)
    tk = min(ATTN_TILE, s)
    nk = s // tk
    nh2 = 2 * N_HEADS
    combine = comb is not None

    def full(a):
        return pl.BlockSpec(a.shape, lambda bi, si, _n=a.ndim: (0,) * _n)

    tok = lambda w: pl.BlockSpec((1, tm, w), lambda bi, si: (bi, si, 0))
    in_arrays = [x]
    in_specs = [pl.BlockSpec((1, tm, D_MODEL), lambda bi, si: (bi + b0, si, 0))]
    if combine:
        y01, route = comb
        in_arrays += [y01, y01, route]
        in_specs += [pl.BlockSpec((SC_SPLIT, 1, 1, tm, PLANE_W), lambda bi, si, _c=c: (0, _c, bi, si, 0))
                     for c in range(2)] + [tok(LANES)]
    weights = [lw["mix_g"], lw["w_main"], lw["w_t"], lw["bf"], lw["caw"], lw["cdw"], lw["cdb"], lw["lng"],
               lw["lnb"], consts["tril"], consts["pq"], consts["qc"], consts["pk"], consts["kc"], consts["vc"]]
    in_arrays += weights
    in_specs += [full(a) for a in weights]

    out_shape = []
    out_specs = []
    if combine:
        out_shape.append(jax.ShapeDtypeStruct((b, s, D_MODEL), F32))
        out_specs.append(tok(D_MODEL))
    out_shape += [
        jax.ShapeDtypeStruct((b, s, BRANCH_WIDTH), BF16),
        jax.ShapeDtypeStruct((b, s, BRANCH_WIDTH), BF16),
        jax.ShapeDtypeStruct((b, nh2, s, HEAD_SLAB), BF16),
        jax.ShapeDtypeStruct((b, nh2, nk, HEAD_SLAB, tk), BF16),
        jax.ShapeDtypeStruct((b, nh2, s, HEAD_SLAB), BF16),
    ]
    out_specs += [
        tok(BRANCH_WIDTH), tok(BRANCH_WIDTH),
        pl.BlockSpec((1, nh2, tm, HEAD_SLAB), lambda bi, si: (bi, 0, si, 0)),
        pl.BlockSpec((1, nh2, tm // tk, HEAD_SLAB, tk), lambda bi, si: (bi, 0, si, 0, 0)),
        pl.BlockSpec((1, nh2, tm, HEAD_SLAB), lambda bi, si: (bi, 0, si, 0)),
    ]
    return pl.pallas_call(
        functools.partial(_mixer_in_kernel, combine),
        grid=(b, s // tm),
        in_specs=in_specs,
        out_specs=out_specs,
        out_shape=out_shape,
        scratch_shapes=[pltpu.VMEM((A_HALO + tm, BRANCH_WIDTH), F32),
                        pltpu.VMEM((SUBLANES, CONF_HALO + tm, BRANCH_WIDTH), F32),
                        pltpu.VMEM((SUBLANES, LANES), F32)],
        compiler_params=pltpu.CompilerParams(dimension_semantics=("arbitrary", "arbitrary"),
                                             vmem_limit_bytes=VMEM_LIMIT),
        name="mixer_in",
    )(*in_arrays)


def _pair_out(accs):
    lane = lax.broadcasted_iota(I32, accs[0].shape, 1)
    return jnp.where(lane < HEAD_DIM, accs[0], pltpu.roll(accs[1], HEAD_DIM, axis=1))


def _sb_attn_kernel(q_ref, kt_ref, v_ref, u_ref, o_ref):
    tq = q_ref.shape[2]
    nk, tk = kt_ref.shape[2], kt_ref.shape[-1]
    i = pl.program_id(2)
    umat = u_ref[...]
    nh = q_ref.shape[1]
    nb = SB_GROUP
    qs = [q_ref[0, h] for h in range(nh)]
    row = lax.broadcasted_iota(I32, (tq, tk), 0)
    col = lax.broadcasted_iota(I32, (tq, tk), 1)

    def weights(g, tails, masked):
        ws, new_tails = [], []
        for h in range(nh):
            tail = tails[h]
            wh = []
            for u in reversed(range(nb)):
                jr = g * nb + u
                z = jnp.dot(qs[h], kt_ref[0, h, jnp.minimum(jr, nk - 1)], preferred_element_type=F32)
                sp = jnp.maximum(z, 0.0) + jnp.log(1.0 + jnp.exp2(jnp.abs(z) * -LOG2E))
                if masked:
                    mask = col + (jr - i) * tk < row
                    sp = jnp.where(mask, sp, 0.0)
                later = jnp.dot(sp.astype(BF16), umat, preferred_element_type=F32)
                w = jnp.exp((z - later - tail).astype(BF16))
                if masked:
                    w = jnp.where(mask, w, jnp.zeros_like(w))
                wh.append(w)
                tail = tail + later[:, 0:1]
            ws.append(tuple(reversed(wh)))
            new_tails.append(tail)
        return tuple(ws), tuple(new_tails)

    def apply(g, accs, ws):
        out = []
        for h in range(nh):
            acc = accs[h]
            for u in range(nb):
                start = pl.multiple_of(jnp.minimum(g * nb + u, nk - 1) * tk, tk)
                acc = acc + jnp.dot(ws[h][u], v_ref[0, h, pl.ds(start, tk), :], preferred_element_type=F32)
            out.append(acc)
        return tuple(out)

    last = i // nb
    zero_t = tuple(jnp.zeros((tq, 1), F32) for _ in range(nh))
    accs = tuple(jnp.zeros((tq, HEAD_SLAB), F32) for _ in range(nh))
    ws, tails = weights(last, zero_t, True)

    def body(t, carry):
        accs, tails, ws = carry
        g = last - 1 - t
        accs = apply(g + 1, accs, ws)
        ws, tails = weights(g, tails, False)
        return accs, tails, ws

    accs, tails, ws = lax.fori_loop(0, last, body, (accs, tails, ws))
    accs = apply(0, accs, ws)
    o_ref[0] = jnp.concatenate([_pair_out(accs[k:k + 2]) for k in range(0, nh, 2)], axis=1).astype(BF16)


def _fox_attn_kernel(q_ref, kt_ref, v_ref, o_ref):
    tq = q_ref.shape[2]
    tk = kt_ref.shape[-1]
    i = pl.program_id(2)
    row = lax.broadcasted_iota(I32, (tq, tk), 0)
    col = lax.broadcasted_iota(I32, (tq, tk), 1)
    nh = q_ref.shape[1]
    qs = [q_ref[0, h] for h in range(nh)]

    def vblock(h, j):
        return v_ref[0, h, pl.ds(pl.multiple_of(j * tk, tk), tk), :]

    r = tq // tk
    accs, maxes = [], []
    for h in range(nh):
        ss = [jnp.where(col + d * tk <= row, jnp.dot(qs[h], kt_ref[0, h, i * r + d], preferred_element_type=F32),
                        -jnp.inf) for d in range(r)]
        m = jnp.max(ss[0], axis=-1, keepdims=True)
        for s in ss[1:]:
            m = jnp.maximum(m, jnp.max(s, axis=-1, keepdims=True))
        acc = None
        for d, s in enumerate(ss):
            term = jnp.dot(jnp.exp(s - m).astype(BF16), vblock(h, i * r + d), preferred_element_type=F32)
            acc = term if acc is None else acc + term
        accs.append(acc)
        maxes.append(m)

    def make_body(n_blocks, first):
        def body(jj, carry):
            accs, maxes = carry
            new_accs, new_maxes = [], []
            for h in range(nh):
                js = [first + n_blocks * jj + u for u in range(n_blocks)]
                ss = [jnp.dot(qs[h], kt_ref[0, h, j], preferred_element_type=F32) for j in js]
                m = maxes[h]
                for s in ss:
                    m = jnp.maximum(m, jnp.max(s, axis=-1, keepdims=True))
                acc = jnp.exp(maxes[h] - m) * accs[h]
                for s, j in zip(ss, js):
                    acc = acc + jnp.dot(jnp.exp(s - m).astype(BF16), vblock(h, j), preferred_element_type=F32)
                new_accs.append(acc)
                new_maxes.append(m)
            return tuple(new_accs), tuple(new_maxes)
        return body

    carry = (tuple(accs), tuple(maxes))
    done = 0
    for n_blocks in ATTN_TRIPS:
        trips = (i * r - done) // n_blocks
        carry = lax.fori_loop(0, trips, make_body(n_blocks, done), carry)
        done = done + trips * n_blocks
    accs = carry[0]
    outs = [a / a[:, HEAD_DIM:HEAD_DIM + 1] for a in accs]
    o_ref[0] = jnp.concatenate([_pair_out(outs[k:k + 2]) for k in range(0, nh, 2)], axis=1).astype(BF16)


def _attention(kind, q, kt, v, consts):
    b, _, s, _ = q.shape
    nk, tk = kt.shape[2], kt.shape[4]
    tq = tk if kind == "sb" else min(FOX_Q_BLOCKS * tk, s)
    hps = SB_HEADS_PER_STEP if kind == "sb" else FOX_HEADS_PER_STEP
    head0 = 0 if kind == "sb" else N_HEADS // hps
    in_specs = [
        pl.BlockSpec((1, hps, tq, HEAD_SLAB), lambda bi, hp, i: (bi, hp + head0, i, 0)),
        pl.BlockSpec((1, hps, nk, HEAD_SLAB, tk), lambda bi, hp, i: (bi, hp + head0, 0, 0, 0)),
        pl.BlockSpec((1, hps, s, HEAD_SLAB), lambda bi, hp, i: (bi, hp + head0, 0, 0)),
    ]
    args = [q, kt, v]
    if kind == "sb":
        in_specs.append(pl.BlockSpec((tk, tk), lambda bi, hp, i: (0, 0)))
        args.append(consts["u_incl"])
        body = _sb_attn_kernel
    else:
        body = _fox_attn_kernel
    return pl.pallas_call(
        body,
        grid=(b, N_HEADS // hps, s // tq),
        in_specs=in_specs,
        out_specs=pl.BlockSpec((1, tq, hps * HEAD_DIM), lambda bi, hp, i: (bi, i, hp)),
        out_shape=jax.ShapeDtypeStruct((b, s, BRANCH_WIDTH), BF16),
        compiler_params=pltpu.CompilerParams(dimension_semantics=("arbitrary", "arbitrary", "arbitrary"),
                                             vmem_limit_bytes=VMEM_LIMIT),
        name=kind + "_attn",
    )(*args)


def _mixer_out_kernel(x_ref, ya_ref, ysb_ref, yfx_ref, yc_ref, g_ref, wg_ref, bg_ref, wb_ref, wo_ref,
                      fg_ref, wr_ref, tru_ref, xo_ref, xp_ref, rt_ref, rtt_ref, cnt_ref, carry):
    tm = x_ref.shape[0]

    @pl.when(pl.program_id(0) == 0)
    def _():
        carry[...] = jnp.zeros_like(carry)

    x = x_ref[...]
    xb = _rms(x, g_ref[...]).astype(BF16)
    h = None
    for g, y_ref in enumerate((ya_ref, ysb_ref, yfx_ref, yc_ref)):
        gate = _sigmoid(jnp.dot(xb, wg_ref[g], preferred_element_type=F32) + bg_ref[g])
        term = gate * jnp.dot(y_ref[...], wb_ref[g], preferred_element_type=F32)
        h = term if h is None else h + term
    xo = x + jnp.dot(h.astype(BF16), wo_ref[...], preferred_element_type=F32)
    xo_ref[...] = xo
    xn = _rms(xo, fg_ref[...])
    _store_planes(xp_ref, xn)

    xh = xn.astype(BF16)
    xl = (xn - xh.astype(F32)).astype(BF16)
    nt = (((1,), (1,)), ((), ()))
    logits = (lax.dot_general(wr_ref[0], xh, nt, preferred_element_type=F32)
              + lax.dot_general(wr_ref[0], xl, nt, preferred_element_type=F32)
              + lax.dot_general(wr_ref[1], xh, nt, preferred_element_type=F32))[0:ROUTE_ROWS, :]
    row = lax.broadcasted_iota(I32, (ROUTE_ROWS, tm), 0).astype(F32)
    ninf = -jnp.inf
    big = float(LANES)
    gl = jnp.where(row < N_GROUPS, logits, ninf)
    gmax = jnp.max(gl, axis=0, keepdims=True)
    gidx = jnp.min(jnp.where(gl == gmax, row, big), axis=0, keepdims=True)
    p_group = 1.0 / jnp.sum(jnp.exp(gl - gmax), axis=0, keepdims=True)
    first = N_GROUPS + EXPERTS_PER_GROUP * gidx
    el = jnp.where((row >= first) & (row < first + EXPERTS_PER_GROUP), logits, ninf)
    m1 = jnp.max(el, axis=0, keepdims=True)
    i1 = jnp.min(jnp.where(el == m1, row, big), axis=0, keepdims=True)
    el2 = jnp.where(row == i1, ninf, el)
    m2 = jnp.max(el2, axis=0, keepdims=True)
    i2 = jnp.min(jnp.where(el2 == m2, row, big), axis=0, keepdims=True)
    e2 = jnp.exp(m2 - m1)
    w1 = p_group / (1.0 + e2)
    w2 = w1 * e2

    sel1 = row == i1
    sel2 = row == i2
    onehot = jnp.where(sel1, 1.0, jnp.where(sel2, 1.0, 0.0))
    before = jnp.dot(onehot.astype(BF16), tru_ref[...], preferred_element_type=F32) + carry[:, 0:1]
    r1 = jnp.sum(jnp.where(sel1, before, 0.0), axis=0, keepdims=True)
    r2 = jnp.sum(jnp.where(sel2, before, 0.0), axis=0, keepdims=True)
    total = carry[...] + jnp.sum(onehot, axis=1, keepdims=True)
    carry[...] = total
    cnt_ref[...] = total

    row8 = lax.broadcasted_iota(I32, (8, tm), 0)
    fields = (i1 - N_GROUPS, i2 - N_GROUPS, w1, w2, r1, r2)
    rtt = jnp.zeros((8, tm), F32)
    for k, f in enumerate(fields):
        rtt = jnp.where(row8 == k, f, rtt)
    rtt_ref[...] = rtt
    rt_ref[...] = jnp.concatenate([rtt, jnp.zeros((LANES - 8, tm), F32)], axis=0).T


def _mixer_out(x2d, ya, ysb, yfx, yc, lw, consts, row0=0):
    t = ya.shape[0]
    tm = min(TOKEN_TILE, t)

    def full(a):
        return pl.BlockSpec(a.shape, lambda i, _n=a.ndim: (0,) * _n)

    tok = lambda w: pl.BlockSpec((tm, w), lambda i: (i, 0))
    weights = [lw["mix_g"], lw["w_gate"], lw["b_gate"], lw["w_branch"], lw["w_out"], lw["ffn_g"], lw["w_router"],
               consts["triu_strict"]]
    return pl.pallas_call(
        _mixer_out_kernel,
        grid=(t // tm,),
        in_specs=[pl.BlockSpec((tm, D_MODEL), lambda i: (i + row0 // tm, 0))] + [tok(BRANCH_WIDTH)] * 4
        + [full(a) for a in weights],
        out_specs=[tok(D_MODEL), pl.BlockSpec((SC_SPLIT, tm, PLANE_W), lambda i: (0, i, 0)), tok(LANES),
                   pl.BlockSpec((8, tm), lambda i: (0, i)), pl.BlockSpec((ROUTE_ROWS, LANES), lambda i: (0, 0))],
        out_shape=[jax.ShapeDtypeStruct((t, D_MODEL), F32),
                   jax.ShapeDtypeStruct((SC_SPLIT, t, PLANE_W), I32),
                   jax.ShapeDtypeStruct((t, LANES), F32),
                   jax.ShapeDtypeStruct((8, t), F32),
                   jax.ShapeDtypeStruct((ROUTE_ROWS, LANES), F32)],
        scratch_shapes=[pltpu.VMEM((ROUTE_ROWS, LANES), F32)],
        compiler_params=pltpu.CompilerParams(dimension_semantics=("arbitrary",),
                                             vmem_limit_bytes=VMEM_LIMIT),
        name="mixer_out",
    )(x2d, ya, ysb, yfx, yc, *weights)


def _sc_mesh():
    return plsc.VectorSubcoreMesh(core_axis_name="core", subcore_axis_name="subcore")


def _plane_index(idx, rows_per_plane):
    return jnp.concatenate([idx + k * rows_per_plane for k in range(SC_SPLIT)]).reshape(1, -1)


def _dispatch_rows(planes, pos0, pos1, n_out):
    w = planes.shape[2]
    rows = planes.reshape(-1, w)
    t = rows.shape[0]
    idx0 = _plane_index(pos0, n_out)
    idx1 = _plane_index(pos1, n_out)

    @functools.partial(pl.kernel, out_type=jax.ShapeDtypeStruct((SC_SPLIT * n_out, w), rows.dtype),
                       mesh=_sc_mesh(), scratch_types=[])
    def scatter_kernel(x_hbm, i0_hbm, i1_hbm, o_hbm):
        def body(x_vmem, i0_vmem, i1_vmem):
            pltpu.sync_copy(x_vmem, o_hbm.at[i0_vmem.at[0]])
            pltpu.sync_copy(x_vmem, o_hbm.at[i1_vmem.at[0]])

        pltpu.emit_pipeline(
            body,
            grid=(t // SC_WINDOW,),
            in_specs=[pl.BlockSpec((SC_WINDOW, w), lambda i: (i, 0)),
                      pl.BlockSpec((1, SC_WINDOW), lambda i: (0, i)),
                      pl.BlockSpec((1, SC_WINDOW), lambda i: (0, i))],
            out_specs=[],
            core_axis_name=("core", "subcore"),
            dimension_semantics=(pltpu.PARALLEL,),
        )(x_hbm, i0_hbm, i1_hbm)

    return scatter_kernel(rows, idx0, idx1).reshape(SC_SPLIT, n_out, w)


def _collect_rows(planes, idx):
    n, w = planes.shape[1:]
    table = planes.reshape(-1, w)
    idx2 = _plane_index(idx, n)
    m = idx2.shape[1]

    @functools.partial(pl.kernel, out_type=jax.ShapeDtypeStruct((m, w), table.dtype), mesh=_sc_mesh(),
                       scratch_types=[])
    def gather_kernel(x_hbm, i_hbm, o_hbm):
        def body(i_vmem, o_vmem):
            pltpu.sync_copy(x_hbm.at[i_vmem.at[0]], o_vmem)

        pltpu.emit_pipeline(
            body,
            grid=(m // SC_WINDOW,),
            in_specs=[pl.BlockSpec((1, SC_WINDOW), lambda i: (0, i))],
            out_specs=[pl.BlockSpec((SC_WINDOW, w), lambda i: (i, 0))],
            core_axis_name=("core", "subcore"),
            dimension_semantics=(pltpu.PARALLEL,),
        )(i_hbm, o_hbm)

    return gather_kernel(table, idx2).reshape(SC_SPLIT, -1, w)


def _moe_ffn_kernel(te_ref, nt_ref, xs_ref, wgu_ref, wd_ref, ys_ref):
    i = pl.program_id(0)

    @pl.when(i < nt_ref[0])
    def _():
        gu = None
        for k, part in enumerate(_load_planes(xs_ref)):
            term = jnp.dot(part.astype(BF16), wgu_ref[0, k * PLANE_W:(k + 1) * PLANE_W, :],
                           preferred_element_type=F32)
            gu = term if gu is None else gu + term
        gate = gu[:, :D_EXPERT]
        hid = gate * _sigmoid(gate) * gu[:, D_EXPERT:]
        _store_planes(ys_ref, jnp.dot(hid.astype(BF16), wd_ref[0], preferred_element_type=F32))

    @pl.when(i >= nt_ref[0])
    def _():
        ys_ref[...] = jnp.zeros_like(ys_ref)


def _moe_ffn(xs, tile_expert, n_tiles, lw):
    p = xs.shape[1]
    tg = GROUP_TILE
    rows = pl.BlockSpec((SC_SPLIT, tg, PLANE_W), lambda i, te, nt: (0, i, 0))
    grid_spec = pltpu.PrefetchScalarGridSpec(
        num_scalar_prefetch=2,
        grid=(p // tg,),
        in_specs=[rows,
                  pl.BlockSpec((1, D_MODEL, 2 * D_EXPERT), lambda i, te, nt: (te[i], 0, 0)),
                  pl.BlockSpec((1, D_EXPERT, D_MODEL), lambda i, te, nt: (te[i], 0, 0))],
        out_specs=rows,
    )
    return pl.pallas_call(
        _moe_ffn_kernel,
        grid_spec=grid_spec,
        out_shape=jax.ShapeDtypeStruct((SC_SPLIT, p, PLANE_W), I32),
        compiler_params=pltpu.CompilerParams(dimension_semantics=("arbitrary",),
                                             vmem_limit_bytes=VMEM_LIMIT),
        name="moe_ffn",
    )(tile_expert, n_tiles, xs, lw["w_gu"], lw["w_down"])


def _route_plan(route_t, cnt, t):
    tg = GROUP_TILE
    n_tiles_max = (2 * t) // tg + N_EXPERTS
    counts = cnt[N_GROUPS:N_GROUPS + N_EXPERTS, 0].astype(I32)
    padded = ((counts + tg - 1) // tg) * tg
    ends = jnp.cumsum(padded)
    offs = ends - padded
    experts = jnp.arange(N_EXPERTS, dtype=I32)[:, None]

    def first_row(e):
        return jnp.sum(jnp.where(e[None, :] == experts, offs[:, None], 0), axis=0)

    fields = route_t.astype(I32)
    pos0 = first_row(fields[0]) + fields[4]
    pos1 = first_row(fields[1]) + fields[5]
    tile_start = jnp.arange(n_tiles_max, dtype=I32) * tg
    n_tiles = ends[-1] // tg
    tile_clamped = jnp.minimum(tile_start, jnp.maximum(n_tiles - 1, 0) * tg)
    tile_expert = jnp.sum((ends[None, :] <= tile_clamped[:, None]).astype(I32), axis=1)
    tile_expert = jnp.minimum(tile_expert, N_EXPERTS - 1)
    return pos0, pos1, tile_expert, n_tiles.reshape(1), n_tiles_max * tg


def _moe(xp, route_t, cnt, lw):
    t = xp.shape[1]
    pos0, pos1, tile_expert, n_tiles, p_rows = _route_plan(route_t, cnt, t)
    xs = _dispatch_rows(xp, pos0, pos1, p_rows)
    ys = _moe_ffn(xs, tile_expert, n_tiles, lw)
    return _collect_rows(ys, jnp.concatenate([pos0, pos1])).reshape(SC_SPLIT, 2, t, PLANE_W)


def _final_kernel(x_ref, y0_ref, y1_ref, rt_ref, g_ref, *rest):
    o_ref = rest[-1]
    x = _combine(x_ref[...], y0_ref, y1_ref, rt_ref[...], lead=(0,))
    o_ref[...] = _rms(x, g_ref[...])


def _final(x2d, y01, route, g, t_total, row0, out_prev):
    t = x2d.shape[0]
    tm = min(TOKEN_TILE, t)
    tok = lambda w: pl.BlockSpec((tm, w), lambda i: (i, 0))
    choice = lambda c: pl.BlockSpec((SC_SPLIT, 1, tm, PLANE_W), lambda i: (0, c, i, 0))
    in_specs = [tok(D_MODEL), choice(0), choice(1), tok(LANES), pl.BlockSpec((1, D_MODEL), lambda i: (0, 0))]
    args = [x2d, y01, y01, route, g]
    aliases = {}
    if out_prev is not None:
        in_specs.append(pl.BlockSpec(memory_space=pl.ANY))
        args.append(out_prev)
        aliases = {len(args) - 1: 0}
    return pl.pallas_call(
        _final_kernel,
        grid=(t // tm,),
        in_specs=in_specs,
        out_specs=pl.BlockSpec((tm, D_MODEL), lambda i: (i + row0 // tm, 0)),
        out_shape=jax.ShapeDtypeStruct((t_total, D_MODEL), F32),
        input_output_aliases=aliases,
        compiler_params=pltpu.CompilerParams(dimension_semantics=("arbitrary",),
                                             vmem_limit_bytes=VMEM_LIMIT),
        name="final_norm",
    )(*args)


def _constants(tm, tk):
    r = jnp.arange(tm)
    tril = (r[None, :] <= r[:, None]).astype(BF16)
    triu_strict = (r[:, None] < r[None, :]).astype(BF16)
    rk = jnp.arange(tk)
    u_incl = (rk[:, None] >= rk[None, :]).astype(BF16)
    nh = N_HEADS
    pq = jnp.zeros((3, LANES, nh * HEAD_SLAB), F32)
    pk = jnp.zeros((3, nh * HEAD_DIM, LANES), F32)
    qc = jnp.zeros((1, nh * HEAD_SLAB), F32)
    kc = jnp.zeros((nh * HEAD_DIM, LANES), F32)
    for part in range(3):
        for h in range(nh):
            pq = pq.at[part, h, h * HEAD_SLAB + HEAD_DIM + part].set(1.0)
            qc = qc.at[0, h * HEAD_SLAB + HEAD_DIM + 3 + part].set(1.0)
            kc = kc.at[h * HEAD_DIM + part, :].set(1.0)
            pk = pk.at[part, h * HEAD_DIM + 3 + part, h].set(1.0)
    vc = jnp.zeros((1, HEAD_SLAB), F32).at[0, HEAD_DIM].set(1.0)
    return {"tril": tril, "triu_strict": triu_strict, "u_incl": u_incl, "pq": pq.astype(BF16),
            "pk": pk.astype(BF16), "qc": qc, "kc": kc, "vc": vc}


def _layer_weights(layer, mix_norm_g, w_in, b_forget, conv_a_w, conf_dw_w, conf_dw_b, conf_ln_g, conf_ln_b,
                   w_branch, w_gate, b_gate, w_out, ffn_norm_g, w_router_group, w_router_expert,
                   w_expert_gate, w_expert_up, w_expert_down):
    w = w_in[layer]
    bw = BRANCH_WIDTH
    a_x, a_b, a_c, sb_q, sb_k, sb_v, fx_q, fx_k, fx_v = [w[:, i * bw:(i + 1) * bw] for i in range(9)]
    fx_f = w[:, 9 * bw:9 * bw + N_HEADS]
    conf = w[:, 9 * bw + N_HEADS:]
    scale = HEAD_DIM ** -0.5
    f_pad = jnp.pad(fx_f, ((0, 0), (0, LANES - N_HEADS)))
    w_main = jnp.concatenate([conf, f_pad, sb_q * scale, fx_q * scale, sb_v, fx_v, a_x, a_b, a_c], axis=1)
    w_t = jnp.concatenate([sb_k, fx_k], axis=1).T
    w_router = jnp.concatenate([w_router_group[layer], w_router_expert[layer].reshape(D_MODEL, N_EXPERTS)], axis=1)
    w_router = jnp.pad(w_router, ((0, 0), (0, LANES - N_GROUPS - N_EXPERTS)))
    return {
        "mix_g": mix_norm_g[layer].reshape(1, D_MODEL),
        "w_main": w_main.astype(BF16),
        "w_t": w_t.astype(BF16),
        "bf": jnp.pad(b_forget[layer], (0, LANES - N_HEADS)).reshape(1, LANES),
        "caw": conv_a_w[layer],
        "cdw": jnp.pad(conf_dw_w[layer], ((0, 1), (0, 0))),
        "cdb": conf_dw_b[layer].reshape(1, bw),
        "lng": conf_ln_g[layer].reshape(1, bw),
        "lnb": conf_ln_b[layer].reshape(1, bw),
        "w_gate": w_gate[layer].astype(BF16),
        "b_gate": b_gate[layer].reshape(4, 1, D_MODEL),
        "w_branch": w_branch[layer].astype(BF16),
        "w_out": w_out[layer].astype(BF16),
        "ffn_g": ffn_norm_g[layer].reshape(1, D_MODEL),
        "w_router": jnp.stack([w_router.T.astype(BF16), (w_router - w_router.astype(BF16).astype(F32)).T.astype(BF16)]),
        "w_gu": jnp.concatenate([w_expert_gate[layer], w_expert_up[layer]], axis=2).astype(BF16),
        "w_down": w_expert_down[layer].astype(BF16),
    }


def kernel(x, mix_norm_g, w_in, b_forget, conv_a_w, conf_dw_w, conf_dw_b, conf_ln_g, conf_ln_b, w_branch, w_gate,
           b_gate, w_out, ffn_norm_g, w_router_group, w_router_expert, w_expert_gate, w_expert_up, w_expert_down,
           final_norm_g):
    b, s, _ = x.shape
    depth = w_in.shape[0]
    consts = _constants(min(TOKEN_TILE, s), min(ATTN_TILE, s))
    parts = BATCH_SPLIT if b % BATCH_SPLIT == 0 else 1
    bp = b // parts
    tp = bp * s
    xs = [x] * parts
    combs = [None] * parts
    for layer in range(depth):
        lw = _layer_weights(layer, mix_norm_g, w_in, b_forget, conv_a_w, conf_dw_w, conf_dw_b, conf_ln_g,
                            conf_ln_b, w_branch, w_gate, b_gate, w_out, ffn_norm_g, w_router_group,
                            w_router_expert, w_expert_gate, w_expert_up, w_expert_down)
        routed = []
        for k in range(parts):
            outs = _mixer_in(xs[k], combs[k], lw, consts, b0=k * bp if layer == 0 else 0, b=bp)
            if combs[k] is not None:
                xk, *outs = outs
            else:
                xk = None
            ya, yc, q, kt, v = outs
            ysb = _attention("sb", q, kt, v, consts)
            yfx = _attention("fox", q, kt, v, consts)
            if xk is None:
                x2d_in, row0 = x.reshape(b * s, D_MODEL), k * tp
            else:
                x2d_in, row0 = xk.reshape(tp, D_MODEL), 0
            routed.append(_mixer_out(x2d_in, ya.reshape(tp, -1), ysb.reshape(tp, -1), yfx.reshape(tp, -1),
                                     yc.reshape(tp, -1), lw, consts, row0=row0))
        for k in range(parts):
            x2d, xp, route, route_t, cnt = routed[k]
            y01 = _moe(xp, route_t, cnt, lw)
            xs[k] = x2d.reshape(bp, s, D_MODEL)
            combs[k] = (y01.reshape(SC_SPLIT, 2, bp, s, PLANE_W), route.reshape(bp, s, LANES))
    out = None
    for k in range(parts):
        out = _final(xs[k].reshape(tp, D_MODEL), combs[k][0].reshape(SC_SPLIT, 2, tp, PLANE_W),
                     combs[k][1].reshape(tp, LANES), final_norm_g.reshape(1, D_MODEL), b * s, k * tp, out)
    return out.reshape(b, s, D_MODEL)
```

```python
import functools

import jax
import jax.numpy as jnp
from jax import lax
from jax.experimental import pallas as pl
from jax.experimental.pallas import tpu as pltpu
from jax.experimental.pallas import tpu_sc as plsc

F32 = jnp.float32
BF16 = jnp.bfloat16
I32 = jnp.int32

D_MODEL = 1024
BRANCH_WIDTH = 256
HEAD_DIM = 64
N_HEADS = 4
HEAD_SLAB = 128
CONV_A_WIDTH = 3
CONF_WIDTH = 31
N_GROUPS = 4
EXPERTS_PER_GROUP = 8
N_EXPERTS = 32
D_EXPERT = 256
EPS = 1e-6
LOG2E = 1.4426950408889634
LANES = 128
SUBLANES = 8
A_HALO = SUBLANES
CONF_HALO = 32
ROUTE_ROWS = 48
HALF_D = D_MODEL // 2
SC_SPLIT = 2
PLANE_W = HALF_D // SC_SPLIT

TOKEN_TILE = 512
ATTN_TILE = 256
FOX_Q_BLOCKS = 2
SB_HEADS_PER_STEP = 4
FOX_HEADS_PER_STEP = 4
ATTN_TRIPS = (8, 4, 2, 1)
SB_GROUP = 4
GROUP_TILE = 512
CONV_ROW_CHUNK = 128
SC_WINDOW = 128
BATCH_SPLIT = 2
VMEM_LIMIT = 56 * 1024 * 1024

_C_CU, _C_CG = 0, 256
_C_F = 512
_N_FIRST = 640
_C_Q = 640
_C_V = 1152
_C_AX, _C_AB, _C_AC = 1664, 1920, 2176
_N_MAIN = 2432


def _rms(x, g):
    return x * lax.rsqrt(jnp.mean(x * x, axis=-1, keepdims=True) + EPS) * g


def _softplus(z):
    return jnp.maximum(z, 0.0) + jnp.log1p(jnp.exp(-jnp.abs(z)))


def _sigmoid(z):
    return 1.0 / (1.0 + jnp.exp(-z))


def _split3(v):
    hi = v.astype(BF16)
    r = v - hi.astype(F32)
    mid = r.astype(BF16)
    lo = (r - mid.astype(F32)).astype(BF16)
    return hi, mid, lo


def _pack_rows(v):
    lo = pltpu.bitcast(v[:, :HALF_D].astype(BF16).astype(F32), jnp.uint32)
    hi = pltpu.bitcast(v[:, HALF_D:].astype(BF16).astype(F32), jnp.uint32)
    return pltpu.bitcast((lo >> 16) | hi, I32)


def _unpack_rows(w):
    u = pltpu.bitcast(w, jnp.uint32)
    lo = pltpu.bitcast(u << 16, F32)
    hi = pltpu.bitcast(u & jnp.uint32(0xFFFF0000), F32)
    return lo, hi


def _store_planes(ref, v, lead=()):
    packed = _pack_rows(v)
    for k in range(SC_SPLIT):
        ref[(k, *lead)] = packed[:, k * PLANE_W:(k + 1) * PLANE_W]


def _load_planes(ref, lead=()):
    los, his = zip(*[_unpack_rows(ref[(k, *lead)]) for k in range(SC_SPLIT)])
    return list(los) + list(his)


def _combine(x, y0_ref, y1_ref, rt, lead=()):
    w0 = rt[:, 2:3]
    w1 = rt[:, 3:4]
    parts = [w0 * a + w1 * b for a, b in zip(_load_planes(y0_ref, lead), _load_planes(y1_ref, lead))]
    return x + jnp.concatenate(parts, axis=1)


def _mixer_in_kernel(combine, *refs):
    if combine:
        (x_ref, y0_ref, y1_ref, rt_ref, *refs) = refs
    else:
        (x_ref, *refs) = refs
    (g_ref, wm_ref, wt_ref, bf_ref, caw_ref, cdw_ref, cdb_ref, lng_ref, lnb_ref, tril_ref,
     pq_ref, qc_ref, pk_ref, kc_ref, vc_ref, *refs) = refs
    if combine:
        (xo_ref, *refs) = refs
    (ya_ref, yc_ref, q_ref, kt_ref, v_ref, bufa, bufc, dcarry) = refs

    tm = x_ref.shape[1]
    tk = kt_ref.shape[-1]

    @pl.when(pl.program_id(1) == 0)
    def _():
        bufa[0:A_HALO, :] = jnp.zeros((A_HALO, BRANCH_WIDTH), F32)
        bufc[0, 0:CONF_HALO, :] = jnp.zeros((CONF_HALO, BRANCH_WIDTH), F32)
        dcarry[...] = jnp.zeros_like(dcarry)

    x = x_ref[0]
    if combine:
        x = _combine(x, y0_ref, y1_ref, rt_ref[0], lead=(0, 0))
        xo_ref[0] = x
    xb = _rms(x, g_ref[...]).astype(BF16)

    p = jnp.concatenate([jnp.dot(xb, wm_ref[:, :_N_FIRST], preferred_element_type=F32),
                         jnp.dot(xb, wm_ref[:, _N_FIRST:], preferred_element_type=F32)], axis=1)
    pt = lax.dot_general(wt_ref[...], xb, (((1,), (1,)), ((), ())),
                         preferred_element_type=F32)

    ca = p[:, _C_AC:_C_AC + 256] * p[:, _C_AX:_C_AX + 256]
    bufa[A_HALO:A_HALO + tm, :] = ca
    caw = caw_ref[...]
    conv = caw[CONV_A_WIDTH - 1:CONV_A_WIDTH] * ca
    for k in range(CONV_A_WIDTH - 1):
        off = A_HALO - (CONV_A_WIDTH - 1) + k
        conv = conv + caw[k:k + 1] * bufa[off:off + tm, :]
    ya_ref[0] = (p[:, _C_AB:_C_AB + 256] * conv).astype(BF16)
    bufa[0:A_HALO, :] = ca[tm - A_HALO:tm]

    u = p[:, _C_CU:_C_CU + 256] * _sigmoid(p[:, _C_CG:_C_CG + 256])
    bufc[0, CONF_HALO:CONF_HALO + tm, :] = u
    for r in range(1, SUBLANES):
        bufc[r, 0:tm + CONF_HALO - SUBLANES, :] = bufc[0, r:r + tm + CONF_HALO - SUBLANES, :]
    cdw = cdw_ref[...]
    cdb = cdb_ref[...]
    lng = lng_ref[...]
    lnb = lnb_ref[...]
    for c in range(tm // CONV_ROW_CHUNK):
        row0 = c * CONV_ROW_CHUNK
        acc = jnp.broadcast_to(cdb, (CONV_ROW_CHUNK, BRANCH_WIDTH))
        for k in range(CONF_WIDTH):
            off = CONF_HALO - (CONF_WIDTH - 1) + k
            start = row0 + off - off % SUBLANES
            acc = acc + cdw[k:k + 1] * bufc[off % SUBLANES, start:start + CONV_ROW_CHUNK, :]
        mu = jnp.mean(acc, axis=-1, keepdims=True)
        cen = acc - mu
        var = jnp.mean(cen * cen, axis=-1, keepdims=True)
        yn = cen * lax.rsqrt(var + EPS) * lng + lnb
        yc_ref[0, row0:row0 + CONV_ROW_CHUNK, :] = (yn * _sigmoid(yn)).astype(BF16)
    bufc[0, 0:CONF_HALO, :] = u[tm - CONF_HALO:tm]

    logf = -_softplus(-(p[:, _C_F:_C_F + LANES] + bf_ref[...]))
    tril = tril_ref[...]
    dcum = dcarry[0:1, :]
    for part in _split3(logf):
        dcum = dcum + jnp.dot(tril, part, preferred_element_type=F32)
    dcarry[0:1, :] = dcum[tm - 1:tm, :]
    dcum_t = dcum.T
    qh, qm, ql = _split3(dcum)
    kh, km, kl = _split3(-dcum_t)
    q_extra = (jnp.dot(qh, pq_ref[0], preferred_element_type=F32)
               + jnp.dot(qm, pq_ref[1], preferred_element_type=F32)
               + jnp.dot(ql, pq_ref[2], preferred_element_type=F32) + qc_ref[...])
    k_extra = (jnp.dot(pk_ref[0], kh, preferred_element_type=F32)
               + jnp.dot(pk_ref[1], km, preferred_element_type=F32)
               + jnp.dot(pk_ref[2], kl, preferred_element_type=F32)
               + jnp.concatenate([kc_ref[...]] * (tm // LANES), axis=1))

    lane = lax.broadcasted_iota(I32, (tm, HEAD_SLAB), 1)
    low = lane < HEAD_DIM
    vc = vc_ref[...]
    for hd in range(2 * N_HEADS):
        is_fox = hd >= N_HEADS
        pair = (hd // 2) * HEAD_SLAB
        qs = p[:, _C_Q + pair:_C_Q + pair + HEAD_SLAB]
        vs = p[:, _C_V + pair:_C_V + pair + HEAD_SLAB]
        if hd % 2:
            qs = pltpu.roll(qs, HEAD_DIM, axis=1)
            vs = pltpu.roll(vs, HEAD_DIM, axis=1)
        if is_fox:
            hf = hd - N_HEADS
            qx = q_extra[:, hf * HEAD_SLAB:(hf + 1) * HEAD_SLAB]
            kx = k_extra[hf * HEAD_DIM:(hf + 1) * HEAD_DIM, :]
            vx = vc
        else:
            qx = 0.0
            kx = jnp.zeros((HEAD_DIM, tm), F32)
            vx = 0.0
        q_ref[0, hd] = jnp.where(low, qs, qx).astype(BF16)
        v_ref[0, hd] = jnp.where(low, vs, vx).astype(BF16)
        kfull = jnp.concatenate([pt[hd * HEAD_DIM:(hd + 1) * HEAD_DIM, :], kx], axis=0).astype(BF16)
        for c in range(tm // tk):
            kt_ref[0, hd, c] = kfull[:, c * tk:(c + 1) * tk]


def _mixer_in(x, comb, lw, consts, b0=0, b=None):
    s = x.shape[1]
    b = x.shape[0] if b is None else b
    tm = min(TOKEN_TILE, s)
    tk = min(ATTN_TILE, s)
    nk = s // tk
    nh2 = 2 * N_HEADS
    combine = comb is not None

    def full(a):
        return pl.BlockSpec(a.shape, lambda bi, si, _n=a.ndim: (0,) * _n)

    tok = lambda w: pl.BlockSpec((1, tm, w), lambda bi, si: (bi, si, 0))
    in_arrays = [x]
    in_specs = [pl.BlockSpec((1, tm, D_MODEL), lambda bi, si: (bi + b0, si, 0))]
    if combine:
        y01, route = comb
        in_arrays += [y01, y01, route]
        in_specs += [pl.BlockSpec((SC_SPLIT, 1, 1, tm, PLANE_W), lambda bi, si, _c=c: (0, _c, bi, si, 0))
                     for c in range(2)] + [tok(LANES)]
    weights = [lw["mix_g"], lw["w_main"], lw["w_t"], lw["bf"], lw["caw"], lw["cdw"], lw["cdb"], lw["lng"],
               lw["lnb"], consts["tril"], consts["pq"], consts["qc"], consts["pk"], consts["kc"], consts["vc"]]
    in_arrays += weights
    in_specs += [full(a) for a in weights]

    out_shape = []
    out_specs = []
    if combine:
        out_shape.append(jax.ShapeDtypeStruct((b, s, D_MODEL), F32))
        out_specs.append(tok(D_MODEL))
    out_shape += [
        jax.ShapeDtypeStruct((b, s, BRANCH_WIDTH), BF16),
        jax.ShapeDtypeStruct((b, s, BRANCH_WIDTH), BF16),
        jax.ShapeDtypeStruct((b, nh2, s, HEAD_SLAB), BF16),
        jax.ShapeDtypeStruct((b, nh2, nk, HEAD_SLAB, tk), BF16),
        jax.ShapeDtypeStruct((b, nh2, s, HEAD_SLAB), BF16),
    ]
    out_specs += [
        tok(BRANCH_WIDTH), tok(BRANCH_WIDTH),
        pl.BlockSpec((1, nh2, tm, HEAD_SLAB), lambda bi, si: (bi, 0, si, 0)),
        pl.BlockSpec((1, nh2, tm // tk, HEAD_SLAB, tk), lambda bi, si: (bi, 0, si, 0, 0)),
        pl.BlockSpec((1, nh2, tm, HEAD_SLAB), lambda bi, si: (bi, 0, si, 0)),
    ]
    return pl.pallas_call(
        functools.partial(_mixer_in_kernel, combine),
        grid=(b, s // tm),
        in_specs=in_specs,
        out_specs=out_specs,
        out_shape=out_shape,
        scratch_shapes=[pltpu.VMEM((A_HALO + tm, BRANCH_WIDTH), F32),
                        pltpu.VMEM((SUBLANES, CONF_HALO + tm, BRANCH_WIDTH), F32),
                        pltpu.VMEM((SUBLANES, LANES), F32)],
        compiler_params=pltpu.CompilerParams(dimension_semantics=("arbitrary", "arbitrary"),
                                             vmem_limit_bytes=VMEM_LIMIT),
        name="mixer_in",
    )(*in_arrays)


def _pair_out(accs):
    lane = lax.broadcasted_iota(I32, accs[0].shape, 1)
    return jnp.where(lane < HEAD_DIM, accs[0], pltpu.roll(accs[1], HEAD_DIM, axis=1))


def _sb_attn_kernel(q_ref, kt_ref, v_ref, u_ref, o_ref):
    tq = q_ref.shape[2]
    nk, tk = kt_ref.shape[2], kt_ref.shape[-1]
    i = pl.program_id(2)
    umat = u_ref[...]
    nh = q_ref.shape[1]
    nb = SB_GROUP
    qs = [q_ref[0, h] for h in range(nh)]
    row = lax.broadcasted_iota(I32, (tq, tk), 0)
    col = lax.broadcasted_iota(I32, (tq, tk), 1)

    def weights(g, tails, masked):
        ws, new_tails = [], []
        for h in range(nh):
            tail = tails[h]
            wh = []
            for u in reversed(range(nb)):
                jr = g * nb + u
                z = jnp.dot(qs[h], kt_ref[0, h, jnp.minimum(jr, nk - 1)], preferred_element_type=F32)
                sp = jnp.maximum(z, 0.0) + jnp.log(1.0 + jnp.exp2(jnp.abs(z) * -LOG2E))
                if masked:
                    mask = col + (jr - i) * tk < row
                    sp = jnp.where(mask, sp, 0.0)
                later = jnp.dot(sp.astype(BF16), umat, preferred_element_type=F32)
                w = jnp.exp((z - later - tail).astype(BF16))
                if masked:
                    w = jnp.where(mask, w, jnp.zeros_like(w))
                wh.append(w)
                tail = tail + later[:, 0:1]
            ws.append(tuple(reversed(wh)))
            new_tails.append(tail)
        return tuple(ws), tuple(new_tails)

    def apply(g, accs, ws):
        out = []
        for h in range(nh):
            acc = accs[h]
            for u in range(nb):
                start = pl.multiple_of(jnp.minimum(g * nb + u, nk - 1) * tk, tk)
                acc = acc + jnp.dot(ws[h][u], v_ref[0, h, pl.ds(start, tk), :], preferred_element_type=F32)
            out.append(acc)
        return tuple(out)

    last = i // nb
    zero_t = tuple(jnp.zeros((tq, 1), F32) for _ in range(nh))
    accs = tuple(jnp.zeros((tq, HEAD_SLAB), F32) for _ in range(nh))
    ws, tails = weights(last, zero_t, True)

    def body(t, carry):
        accs, tails, ws = carry
        g = last - 1 - t
        accs = apply(g + 1, accs, ws)
        ws, tails = weights(g, tails, False)
        return accs, tails, ws

    accs, tails, ws = lax.fori_loop(0, last, body, (accs, tails, ws))
    accs = apply(0, accs, ws)
    o_ref[0] = jnp.concatenate([_pair_out(accs[k:k + 2]) for k in range(0, nh, 2)], axis=1).astype(BF16)


def _fox_attn_kernel(q_ref, kt_ref, v_ref, o_ref):
    tq = q_ref.shape[2]
    tk = kt_ref.shape[-1]
    i = pl.program_id(2)
    row = lax.broadcasted_iota(I32, (tq, tk), 0)
    col = lax.broadcasted_iota(I32, (tq, tk), 1)
    nh = q_ref.shape[1]
    qs = [q_ref[0, h] for h in range(nh)]

    def vblock(h, j):
        return v_ref[0, h, pl.ds(pl.multiple_of(j * tk, tk), tk), :]

    r = tq // tk
    accs, maxes = [], []
    for h in range(nh):
        ss = [jnp.where(col + d * tk <= row, jnp.dot(qs[h], kt_ref[0, h, i * r + d], preferred_element_type=F32),
                        -jnp.inf) for d in range(r)]
        m = jnp.max(ss[0], axis=-1, keepdims=True)
        for s in ss[1:]:
            m = jnp.maximum(m, jnp.max(s, axis=-1, keepdims=True))
        acc = None
        for d, s in enumerate(ss):
            term = jnp.dot(jnp.exp(s - m).astype(BF16), vblock(h, i * r + d), preferred_element_type=F32)
            acc = term if acc is None else acc + term
        accs.append(acc)
        maxes.append(m)

    def make_body(n_blocks, first):
        def body(jj, carry):
            accs, maxes = carry
            new_accs, new_maxes = [], []
            for h in range(nh):
                js = [first + n_blocks * jj + u for u in range(n_blocks)]
                ss = [jnp.dot(qs[h], kt_ref[0, h, j], preferred_element_type=F32) for j in js]
                m = maxes[h]
                for s in ss:
                    m = jnp.maximum(m, jnp.max(s, axis=-1, keepdims=True))
                acc = jnp.exp(maxes[h] - m) * accs[h]
                for s, j in zip(ss, js):
                    acc = acc + jnp.dot(jnp.exp(s - m).astype(BF16), vblock(h, j), preferred_element_type=F32)
                new_accs.append(acc)
                new_maxes.append(m)
            return tuple(new_accs), tuple(new_maxes)
        return body

    carry = (tuple(accs), tuple(maxes))
    done = 0
    for n_blocks in ATTN_TRIPS:
        trips = (i * r - done) // n_blocks
        carry = lax.fori_loop(0, trips, make_body(n_blocks, done), carry)
        done = done + trips * n_blocks
    accs = carry[0]
    outs = [a / a[:, HEAD_DIM:HEAD_DIM + 1] for a in accs]
    o_ref[0] = jnp.concatenate([_pair_out(outs[k:k + 2]) for k in range(0, nh, 2)], axis=1).astype(BF16)


def _attention(kind, q, kt, v, consts):
    b, _, s, _ = q.shape
    nk, tk = kt.shape[2], kt.shape[4]
    tq = tk if kind == "sb" else min(FOX_Q_BLOCKS * tk, s)
    hps = SB_HEADS_PER_STEP if kind == "sb" else FOX_HEADS_PER_STEP
    head0 = 0 if kind == "sb" else N_HEADS // hps
    in_specs = [
        pl.BlockSpec((1, hps, tq, HEAD_SLAB), lambda bi, hp, i: (bi, hp + head0, i, 0)),
        pl.BlockSpec((1, hps, nk, HEAD_SLAB, tk), lambda bi, hp, i: (bi, hp + head0, 0, 0, 0)),
        pl.BlockSpec((1, hps, s, HEAD_SLAB), lambda bi, hp, i: (bi, hp + head0, 0, 0)),
    ]
    args = [q, kt, v]
    if kind == "sb":
        in_specs.append(pl.BlockSpec((tk, tk), lambda bi, hp, i: (0, 0)))
        args.append(consts["u_incl"])
        body = _sb_attn_kernel
    else:
        body = _fox_attn_kernel
    return pl.pallas_call(
        body,
        grid=(b, N_HEADS // hps, s // tq),
        in_specs=in_specs,
        out_specs=pl.BlockSpec((1, tq, hps * HEAD_DIM), lambda bi, hp, i: (bi, i, hp)),
        out_shape=jax.ShapeDtypeStruct((b, s, BRANCH_WIDTH), BF16),
        compiler_params=pltpu.CompilerParams(dimension_semantics=("arbitrary", "arbitrary", "arbitrary"),
                                             vmem_limit_bytes=VMEM_LIMIT),
        name=kind + "_attn",
    )(*args)


def _mixer_out_kernel(x_ref, ya_ref, ysb_ref, yfx_ref, yc_ref, g_ref, wg_ref, bg_ref, wb_ref, wo_ref,
                      fg_ref, wr_ref, tru_ref, xo_ref, xp_ref, rt_ref, rtt_ref, cnt_ref, carry):
    tm = x_ref.shape[0]

    @pl.when(pl.program_id(0) == 0)
    def _():
        carry[...] = jnp.zeros_like(carry)

    x = x_ref[...]
    xb = _rms(x, g_ref[...]).astype(BF16)
    h = None
    for g, y_ref in enumerate((ya_ref, ysb_ref, yfx_ref, yc_ref)):
        gate = _sigmoid(jnp.dot(xb, wg_ref[g], preferred_element_type=F32) + bg_ref[g])
        term = gate * jnp.dot(y_ref[...], wb_ref[g], preferred_element_type=F32)
        h = term if h is None else h + term
    xo = x + jnp.dot(h.astype(BF16), wo_ref[...], preferred_element_type=F32)
    xo_ref[...] = xo
    xn = _rms(xo, fg_ref[...])
    _store_planes(xp_ref, xn)

    xh = xn.astype(BF16)
    xl = (xn - xh.astype(F32)).astype(BF16)
    nt = (((1,), (1,)), ((), ()))
    logits = (lax.dot_general(wr_ref[0], xh, nt, preferred_element_type=F32)
              + lax.dot_general(wr_ref[0], xl, nt, preferred_element_type=F32)
              + lax.dot_general(wr_ref[1], xh, nt, preferred_element_type=F32))[0:ROUTE_ROWS, :]
    row = lax.broadcasted_iota(I32, (ROUTE_ROWS, tm), 0).astype(F32)
    ninf = -jnp.inf
    big = float(LANES)
    gl = jnp.where(row < N_GROUPS, logits, ninf)
    gmax = jnp.max(gl, axis=0, keepdims=True)
    gidx = jnp.min(jnp.where(gl == gmax, row, big), axis=0, keepdims=True)
    p_group = 1.0 / jnp.sum(jnp.exp(gl - gmax), axis=0, keepdims=True)
    first = N_GROUPS + EXPERTS_PER_GROUP * gidx
    el = jnp.where((row >= first) & (row < first + EXPERTS_PER_GROUP), logits, ninf)
    m1 = jnp.max(el, axis=0, keepdims=True)
    i1 = jnp.min(jnp.where(el == m1, row, big), axis=0, keepdims=True)
    el2 = jnp.where(row == i1, ninf, el)
    m2 = jnp.max(el2, axis=0, keepdims=True)
    i2 = jnp.min(jnp.where(el2 == m2, row, big), axis=0, keepdims=True)
    e2 = jnp.exp(m2 - m1)
    w1 = p_group / (1.0 + e2)
    w2 = w1 * e2

    sel1 = row == i1
    sel2 = row == i2
    onehot = jnp.where(sel1, 1.0, jnp.where(sel2, 1.0, 0.0))
    before = jnp.dot(onehot.astype(BF16), tru_ref[...], preferred_element_type=F32) + carry[:, 0:1]
    r1 = jnp.sum(jnp.where(sel1, before, 0.0), axis=0, keepdims=True)
    r2 = jnp.sum(jnp.where(sel2, before, 0.0), axis=0, keepdims=True)
    total = carry[...] + jnp.sum(onehot, axis=1, keepdims=True)
    carry[...] = total
    cnt_ref[...] = total

    row8 = lax.broadcasted_iota(I32, (8, tm), 0)
    fields = (i1 - N_GROUPS, i2 - N_GROUPS, w1, w2, r1, r2)
    rtt = jnp.zeros((8, tm), F32)
    for k, f in enumerate(fields):
        rtt = jnp.where(row8 == k, f, rtt)
    rtt_ref[...] = rtt
    rt_ref[...] = jnp.concatenate([rtt, jnp.zeros((LANES - 8, tm), F32)], axis=0).T


def _mixer_out(x2d, ya, ysb, yfx, yc, lw, consts, row0=0):
    t = ya.shape[0]
    tm = min(TOKEN_TILE, t)

    def full(a):
        return pl.BlockSpec(a.shape, lambda i, _n=a.ndim: (0,) * _n)

    tok = lambda w: pl.BlockSpec((tm, w), lambda i: (i, 0))
    weights = [lw["mix_g"], lw["w_gate"], lw["b_gate"], lw["w_branch"], lw["w_out"], lw["ffn_g"], lw["w_router"],
               consts["triu_strict"]]
    return pl.pallas_call(
        _mixer_out_kernel,
        grid=(t // tm,),
        in_specs=[pl.BlockSpec((tm, D_MODEL), lambda i: (i + row0 // tm, 0))] + [tok(BRANCH_WIDTH)] * 4
        + [full(a) for a in weights],
        out_specs=[tok(D_MODEL), pl.BlockSpec((SC_SPLIT, tm, PLANE_W), lambda i: (0, i, 0)), tok(LANES),
                   pl.BlockSpec((8, tm), lambda i: (0, i)), pl.BlockSpec((ROUTE_ROWS, LANES), lambda i: (0, 0))],
        out_shape=[jax.ShapeDtypeStruct((t, D_MODEL), F32),
                   jax.ShapeDtypeStruct((SC_SPLIT, t, PLANE_W), I32),
                   jax.ShapeDtypeStruct((t, LANES), F32),
                   jax.ShapeDtypeStruct((8, t), F32),
                   jax.ShapeDtypeStruct((ROUTE_ROWS, LANES), F32)],
        scratch_shapes=[pltpu.VMEM((ROUTE_ROWS, LANES), F32)],
        compiler_params=pltpu.CompilerParams(dimension_semantics=("arbitrary",),
                                             vmem_limit_bytes=VMEM_LIMIT),
        name="mixer_out",
    )(x2d, ya, ysb, yfx, yc, *weights)


def _sc_mesh():
    return plsc.VectorSubcoreMesh(core_axis_name="core", subcore_axis_name="subcore")


def _plane_index(idx, rows_per_plane):
    return jnp.concatenate([idx + k * rows_per_plane for k in range(SC_SPLIT)]).reshape(1, -1)


def _dispatch_rows(planes, pos0, pos1, n_out):
    w = planes.shape[2]
    rows = planes.reshape(-1, w)
    t = rows.shape[0]
    idx0 = _plane_index(pos0, n_out)
    idx1 = _plane_index(pos1, n_out)

    @functools.partial(pl.kernel, out_type=jax.ShapeDtypeStruct((SC_SPLIT * n_out, w), rows.dtype),
                       mesh=_sc_mesh(), scratch_types=[])
    def scatter_kernel(x_hbm, i0_hbm, i1_hbm, o_hbm):
        def body(x_vmem, i0_vmem, i1_vmem):
            pltpu.sync_copy(x_vmem, o_hbm.at[i0_vmem.at[0]])
            pltpu.sync_copy(x_vmem, o_hbm.at[i1_vmem.at[0]])

        pltpu.emit_pipeline(
            body,
            grid=(t // SC_WINDOW,),
            in_specs=[pl.BlockSpec((SC_WINDOW, w), lambda i: (i, 0)),
                      pl.BlockSpec((1, SC_WINDOW), lambda i: (0, i)),
                      pl.BlockSpec((1, SC_WINDOW), lambda i: (0, i))],
            out_specs=[],
            core_axis_name=("core", "subcore"),
            dimension_semantics=(pltpu.PARALLEL,),
        )(x_hbm, i0_hbm, i1_hbm)

    return scatter_kernel(rows, idx0, idx1).reshape(SC_SPLIT, n_out, w)


def _collect_rows(planes, idx):
    n, w = planes.shape[1:]
    table = planes.reshape(-1, w)
    idx2 = _plane_index(idx, n)
    m = idx2.shape[1]

    @functools.partial(pl.kernel, out_type=jax.ShapeDtypeStruct((m, w), table.dtype), mesh=_sc_mesh(),
                       scratch_types=[])
    def gather_kernel(x_hbm, i_hbm, o_hbm):
        def body(i_vmem, o_vmem):
            pltpu.sync_copy(x_hbm.at[i_vmem.at[0]], o_vmem)

        pltpu.emit_pipeline(
            body,
            grid=(m // SC_WINDOW,),
            in_specs=[pl.BlockSpec((1, SC_WINDOW), lambda i: (0, i))],
            out_specs=[pl.BlockSpec((SC_WINDOW, w), lambda i: (i, 0))],
            core_axis_name=("core", "subcore"),
            dimension_semantics=(pltpu.PARALLEL,),
        )(i_hbm, o_hbm)

    return gather_kernel(table, idx2).reshape(SC_SPLIT, -1, w)


def _moe_ffn_kernel(te_ref, nt_ref, xs_ref, wgu_ref, wd_ref, ys_ref):
    i = pl.program_id(0)

    @pl.when(i < nt_ref[0])
    def _():
        gu = None
        for k, part in enumerate(_load_planes(xs_ref)):
            term = jnp.dot(part.astype(BF16), wgu_ref[0, k * PLANE_W:(k + 1) * PLANE_W, :],
                           preferred_element_type=F32)
            gu = term if gu is None else gu + term
        gate = gu[:, :D_EXPERT]
        hid = gate * _sigmoid(gate) * gu[:, D_EXPERT:]
        _store_planes(ys_ref, jnp.dot(hid.astype(BF16), wd_ref[0], preferred_element_type=F32))

    @pl.when(i >= nt_ref[0])
    def _():
        ys_ref[...] = jnp.zeros_like(ys_ref)


def _moe_ffn(xs, tile_expert, n_tiles, lw):
    p = xs.shape[1]
    tg = GROUP_TILE
    rows = pl.BlockSpec((SC_SPLIT, tg, PLANE_W), lambda i, te, nt: (0, i, 0))
    grid_spec = pltpu.PrefetchScalarGridSpec(
        num_scalar_prefetch=2,
        grid=(p // tg,),
        in_specs=[rows,
                  pl.BlockSpec((1, D_MODEL, 2 * D_EXPERT), lambda i, te, nt: (te[i], 0, 0)),
                  pl.BlockSpec((1, D_EXPERT, D_MODEL), lambda i, te, nt: (te[i], 0, 0))],
        out_specs=rows,
    )
    return pl.pallas_call(
        _moe_ffn_kernel,
        grid_spec=grid_spec,
        out_shape=jax.ShapeDtypeStruct((SC_SPLIT, p, PLANE_W), I32),
        compiler_params=pltpu.CompilerParams(dimension_semantics=("arbitrary",),
                                             vmem_limit_bytes=VMEM_LIMIT),
        name="moe_ffn",
    )(tile_expert, n_tiles, xs, lw["w_gu"], lw["w_down"])


def _route_plan(route_t, cnt, t):
    tg = GROUP_TILE
    n_tiles_max = (2 * t) // tg + N_EXPERTS
    counts = cnt[N_GROUPS:N_GROUPS + N_EXPERTS, 0].astype(I32)
    padded = ((counts + tg - 1) // tg) * tg
    ends = jnp.cumsum(padded)
    offs = ends - padded
    experts = jnp.arange(N_EXPERTS, dtype=I32)[:, None]

    def first_row(e):
        return jnp.sum(jnp.where(e[None, :] == experts, offs[:, None], 0), axis=0)

    fields = route_t.astype(I32)
    pos0 = first_row(fields[0]) + fields[4]
    pos1 = first_row(fields[1]) + fields[5]
    tile_start = jnp.arange(n_tiles_max, dtype=I32) * tg
    n_tiles = ends[-1] // tg
    tile_clamped = jnp.minimum(tile_start, jnp.maximum(n_tiles - 1, 0) * tg)
    tile_expert = jnp.sum((ends[None, :] <= tile_clamped[:, None]).astype(I32), axis=1)
    tile_expert = jnp.minimum(tile_expert, N_EXPERTS - 1)
    return pos0, pos1, tile_expert, n_tiles.reshape(1), n_tiles_max * tg


def _moe(xp, route_t, cnt, lw):
    t = xp.shape[1]
    pos0, pos1, tile_expert, n_tiles, p_rows = _route_plan(route_t, cnt, t)
    xs = _dispatch_rows(xp, pos0, pos1, p_rows)
    ys = _moe_ffn(xs, tile_expert, n_tiles, lw)
    return _collect_rows(ys, jnp.concatenate([pos0, pos1])).reshape(SC_SPLIT, 2, t, PLANE_W)


def _final_kernel(x_ref, y0_ref, y1_ref, rt_ref, g_ref, *rest):
    o_ref = rest[-1]
    x = _combine(x_ref[...], y0_ref, y1_ref, rt_ref[...], lead=(0,))
    o_ref[...] = _rms(x, g_ref[...])


def _final(x2d, y01, route, g, t_total, row0, out_prev):
    t = x2d.shape[0]
    tm = min(TOKEN_TILE, t)
    tok = lambda w: pl.BlockSpec((tm, w), lambda i: (i, 0))
    choice = lambda c: pl.BlockSpec((SC_SPLIT, 1, tm, PLANE_W), lambda i: (0, c, i, 0))
    in_specs = [tok(D_MODEL), choice(0), choice(1), tok(LANES), pl.BlockSpec((1, D_MODEL), lambda i: (0, 0))]
    args = [x2d, y01, y01, route, g]
    aliases = {}
    if out_prev is not None:
        in_specs.append(pl.BlockSpec(memory_space=pl.ANY))
        args.append(out_prev)
        aliases = {len(args) - 1: 0}
    return pl.pallas_call(
        _final_kernel,
        grid=(t // tm,),
        in_specs=in_specs,
        out_specs=pl.BlockSpec((tm, D_MODEL), lambda i: (i + row0 // tm, 0)),
        out_shape=jax.ShapeDtypeStruct((t_total, D_MODEL), F32),
        input_output_aliases=aliases,
        compiler_params=pltpu.CompilerParams(dimension_semantics=("arbitrary",),
                                             vmem_limit_bytes=VMEM_LIMIT),
        name="final_norm",
    )(*args)


def _constants(tm, tk):
    r = jnp.arange(tm)
    tril = (r[None, :] <= r[:, None]).astype(BF16)
    triu_strict = (r[:, None] < r[None, :]).astype(BF16)
    rk = jnp.arange(tk)
    u_incl = (rk[:, None] >= rk[None, :]).astype(BF16)
    nh = N_HEADS
    pq = jnp.zeros((3, LANES, nh * HEAD_SLAB), F32)
    pk = jnp.zeros((3, nh * HEAD_DIM, LANES), F32)
    qc = jnp.zeros((1, nh * HEAD_SLAB), F32)
    kc = jnp.zeros((nh * HEAD_DIM, LANES), F32)
    for part in range(3):
        for h in range(nh):
            pq = pq.at[part, h, h * HEAD_SLAB + HEAD_DIM + part].set(1.0)
            qc = qc.at[0, h * HEAD_SLAB + HEAD_DIM + 3 + part].set(1.0)
            kc = kc.at[h * HEAD_DIM + part, :].set(1.0)
            pk = pk.at[part, h * HEAD_DIM + 3 + part, h].set(1.0)
    vc = jnp.zeros((1, HEAD_SLAB), F32).at[0, HEAD_DIM].set(1.0)
    return {"tril": tril, "triu_strict": triu_strict, "u_incl": u_incl, "pq": pq.astype(BF16),
            "pk": pk.astype(BF16), "qc": qc, "kc": kc, "vc": vc}


def _layer_weights(layer, mix_norm_g, w_in, b_forget, conv_a_w, conf_dw_w, conf_dw_b, conf_ln_g, conf_ln_b,
                   w_branch, w_gate, b_gate, w_out, ffn_norm_g, w_router_group, w_router_expert,
                   w_expert_gate, w_expert_up, w_expert_down):
    w = w_in[layer]
    bw = BRANCH_WIDTH
    a_x, a_b, a_c, sb_q, sb_k, sb_v, fx_q, fx_k, fx_v = [w[:, i * bw:(i + 1) * bw] for i in range(9)]
    fx_f = w[:, 9 * bw:9 * bw + N_HEADS]
    conf = w[:, 9 * bw + N_HEADS:]
    scale = HEAD_DIM ** -0.5
    f_pad = jnp.pad(fx_f, ((0, 0), (0, LANES - N_HEADS)))
    w_main = jnp.concatenate([conf, f_pad, sb_q * scale, fx_q * scale, sb_v, fx_v, a_x, a_b, a_c], axis=1)
    w_t = jnp.concatenate([sb_k, fx_k], axis=1).T
    w_router = jnp.concatenate([w_router_group[layer], w_router_expert[layer].reshape(D_MODEL, N_EXPERTS)], axis=1)
    w_router = jnp.pad(w_router, ((0, 0), (0, LANES - N_GROUPS - N_EXPERTS)))
    return {
        "mix_g": mix_norm_g[layer].reshape(1, D_MODEL),
        "w_main": w_main.astype(BF16),
        "w_t": w_t.astype(BF16),
        "bf": jnp.pad(b_forget[layer], (0, LANES - N_HEADS)).reshape(1, LANES),
        "caw": conv_a_w[layer],
        "cdw": jnp.pad(conf_dw_w[layer], ((0, 1), (0, 0))),
        "cdb": conf_dw_b[layer].reshape(1, bw),
        "lng": conf_ln_g[layer].reshape(1, bw),
        "lnb": conf_ln_b[layer].reshape(1, bw),
        "w_gate": w_gate[layer].astype(BF16),
        "b_gate": b_gate[layer].reshape(4, 1, D_MODEL),
        "w_branch": w_branch[layer].astype(BF16),
        "w_out": w_out[layer].astype(BF16),
        "ffn_g": ffn_norm_g[layer].reshape(1, D_MODEL),
        "w_router": jnp.stack([w_router.T.astype(BF16), (w_router - w_router.astype(BF16).astype(F32)).T.astype(BF16)]),
        "w_gu": jnp.concatenate([w_expert_gate[layer], w_expert_up[layer]], axis=2).astype(BF16),
        "w_down": w_expert_down[layer].astype(BF16),
    }


def kernel(x, mix_norm_g, w_in, b_forget, conv_a_w, conf_dw_w, conf_dw_b, conf_ln_g, conf_ln_b, w_branch, w_gate,
           b_gate, w_out, ffn_norm_g, w_router_group, w_router_expert, w_expert_gate, w_expert_up, w_expert_down,
           final_norm_g):
    b, s, _ = x.shape
    depth = w_in.shape[0]
    consts = _constants(min(TOKEN_TILE, s), min(ATTN_TILE, s))
    parts = BATCH_SPLIT if b % BATCH_SPLIT == 0 else 1
    bp = b // parts
    tp = bp * s
    xs = [x] * parts
    combs = [None] * parts
    for layer in range(depth):
        lw = _layer_weights(layer, mix_norm_g, w_in, b_forget, conv_a_w, conf_dw_w, conf_dw_b, conf_ln_g,
                            conf_ln_b, w_branch, w_gate, b_gate, w_out, ffn_norm_g, w_router_group,
                            w_router_expert, w_expert_gate, w_expert_up, w_expert_down)
        routed = []
        for k in range(parts):
            outs = _mixer_in(xs[k], combs[k], lw, consts, b0=k * bp if layer == 0 else 0, b=bp)
            if combs[k] is not None:
                xk, *outs = outs
            else:
                xk = None
            ya, yc, q, kt, v = outs
            ysb = _attention("sb", q, kt, v, consts)
            yfx = _attention("fox", q, kt, v, consts)
            if xk is None:
                x2d_in, row0 = x.reshape(b * s, D_MODEL), k * tp
            else:
                x2d_in, row0 = xk.reshape(tp, D_MODEL), 0
            routed.append(_mixer_out(x2d_in, ya.reshape(tp, -1), ysb.reshape(tp, -1), yfx.reshape(tp, -1),
                                     yc.reshape(tp, -1), lw, consts, row0=row0))
        for k in range(parts):
            x2d, xp, route, route_t, cnt = routed[k]
            y01 = _moe(xp, route_t, cnt, lw)
            xs[k] = x2d.reshape(bp, s, D_MODEL)
            combs[k] = (y01.reshape(SC_SPLIT, 2, bp, s, PLANE_W), route.reshape(bp, s, LANES))
    out = None
    for k in range(parts):
        out = _final(xs[k].reshape(tp, D_MODEL), combs[k][0].reshape(SC_SPLIT, 2, tp, PLANE_W),
                     combs[k][1].reshape(tp, LANES), final_norm_g.reshape(1, D_MODEL), b * s, k * tp, out)
    return out.reshape(b, s, D_MODEL)
```

```python
import functools

import jax
import jax.numpy as jnp
from jax import lax
from jax.experimental import pallas as pl
from jax.experimental.pallas import tpu as pltpu
from jax.experimental.pallas import tpu_sc as plsc

F32 = jnp.float32
BF16 = jnp.bfloat16
I32 = jnp.int32

D_MODEL = 1024
BRANCH_WIDTH = 256
HEAD_DIM = 64
N_HEADS = 4
HEAD_SLAB = 128
CONV_A_WIDTH = 3
CONF_WIDTH = 31
N_GROUPS = 4
EXPERTS_PER_GROUP = 8
N_EXPERTS = 32
D_EXPERT = 256
EPS = 1e-6
LOG2E = 1.4426950408889634
LANES = 128
SUBLANES = 8
A_HALO = SUBLANES
CONF_HALO = 32
ROUTE_ROWS = 48
HALF_D = D_MODEL // 2
SC_SPLIT = 2
PLANE_W = HALF_D // SC_SPLIT

TOKEN_TILE = 512
ATTN_TILE = 256
FOX_Q_BLOCKS = 2
SB_HEADS_PER_STEP = 4
FOX_HEADS_PER_STEP = 4
ATTN_TRIPS = (8, 4, 2, 1)
SB_GROUP = 4
GROUP_TILE = 512
CONV_ROW_CHUNK = 128
SC_WINDOW = 128
BATCH_SPLIT = 4
VMEM_LIMIT = 56 * 1024 * 1024

_C_CU, _C_CG = 0, 256
_C_F = 512
_N_FIRST = 640
_C_Q = 640
_C_V = 1152
_C_AX, _C_AB, _C_AC = 1664, 1920, 2176
_N_MAIN = 2432


def _rms(x, g):
    return x * lax.rsqrt(jnp.mean(x * x, axis=-1, keepdims=True) + EPS) * g


def _softplus(z):
    return jnp.maximum(z, 0.0) + jnp.log1p(jnp.exp(-jnp.abs(z)))


def _sigmoid(z):
    return 1.0 / (1.0 + jnp.exp(-z))


def _split3(v):
    hi = v.astype(BF16)
    r = v - hi.astype(F32)
    mid = r.astype(BF16)
    lo = (r - mid.astype(F32)).astype(BF16)
    return hi, mid, lo


def _pack_rows(v):
    lo = pltpu.bitcast(v[:, :HALF_D].astype(BF16).astype(F32), jnp.uint32)
    hi = pltpu.bitcast(v[:, HALF_D:].astype(BF16).astype(F32), jnp.uint32)
    return pltpu.bitcast((lo >> 16) | hi, I32)


def _unpack_rows(w):
    u = pltpu.bitcast(w, jnp.uint32)
    lo = pltpu.bitcast(u << 16, F32)
    hi = pltpu.bitcast(u & jnp.uint32(0xFFFF0000), F32)
    return lo, hi


def _store_planes(ref, v, lead=()):
    packed = _pack_rows(v)
    for k in range(SC_SPLIT):
        ref[(k, *lead)] = packed[:, k * PLANE_W:(k + 1) * PLANE_W]


def _load_planes(ref, lead=()):
    los, his = zip(*[_unpack_rows(ref[(k, *lead)]) for k in range(SC_SPLIT)])
    return list(los) + list(his)


def _combine(x, y0_ref, y1_ref, rt, lead=()):
    w0 = rt[:, 2:3]
    w1 = rt[:, 3:4]
    parts = [w0 * a + w1 * b for a, b in zip(_load_planes(y0_ref, lead), _load_planes(y1_ref, lead))]
    return x + jnp.concatenate(parts, axis=1)


def _mixer_in_kernel(combine, *refs):
    if combine:
        (x_ref, y0_ref, y1_ref, rt_ref, *refs) = refs
    else:
        (x_ref, *refs) = refs
    (g_ref, wm_ref, wt_ref, bf_ref, caw_ref, cdw_ref, cdb_ref, lng_ref, lnb_ref, tril_ref,
     pq_ref, qc_ref, pk_ref, kc_ref, vc_ref, *refs) = refs
    if combine:
        (xo_ref, *refs) = refs
    (ya_ref, yc_ref, q_ref, kt_ref, v_ref, bufa, bufc, dcarry) = refs

    tm = x_ref.shape[1]
    tk = kt_ref.shape[-1]

    @pl.when(pl.program_id(1) == 0)
    def _():
        bufa[0:A_HALO, :] = jnp.zeros((A_HALO, BRANCH_WIDTH), F32)
        bufc[0, 0:CONF_HALO, :] = jnp.zeros((CONF_HALO, BRANCH_WIDTH), F32)
        dcarry[...] = jnp.zeros_like(dcarry)

    x = x_ref[0]
    if combine:
        x = _combine(x, y0_ref, y1_ref, rt_ref[0], lead=(0, 0))
        xo_ref[0] = x
    xb = _rms(x, g_ref[...]).astype(BF16)

    p = jnp.concatenate([jnp.dot(xb, wm_ref[:, :_N_FIRST], preferred_element_type=F32),
                         jnp.dot(xb, wm_ref[:, _N_FIRST:], preferred_element_type=F32)], axis=1)
    pt = lax.dot_general(wt_ref[...], xb, (((1,), (1,)), ((), ())),
                         preferred_element_type=F32)

    ca = p[:, _C_AC:_C_AC + 256] * p[:, _C_AX:_C_AX + 256]
    bufa[A_HALO:A_HALO + tm, :] = ca
    caw = caw_ref[...]
    conv = caw[CONV_A_WIDTH - 1:CONV_A_WIDTH] * ca
    for k in range(CONV_A_WIDTH - 1):
        off = A_HALO - (CONV_A_WIDTH - 1) + k
        conv = conv + caw[k:k + 1] * bufa[off:off + tm, :]
    ya_ref[0] = (p[:, _C_AB:_C_AB + 256] * conv).astype(BF16)
    bufa[0:A_HALO, :] = ca[tm - A_HALO:tm]

    u = p[:, _C_CU:_C_CU + 256] * _sigmoid(p[:, _C_CG:_C_CG + 256])
    bufc[0, CONF_HALO:CONF_HALO + tm, :] = u
    for r in range(1, SUBLANES):
        bufc[r, 0:tm + CONF_HALO - SUBLANES, :] = bufc[0, r:r + tm + CONF_HALO - SUBLANES, :]
    cdw = cdw_ref[...]
    cdb = cdb_ref[...]
    lng = lng_ref[...]
    lnb = lnb_ref[...]
    for c in range(tm // CONV_ROW_CHUNK):
        row0 = c * CONV_ROW_CHUNK
        acc = jnp.broadcast_to(cdb, (CONV_ROW_CHUNK, BRANCH_WIDTH))
        for k in range(CONF_WIDTH):
            off = CONF_HALO - (CONF_WIDTH - 1) + k
            start = row0 + off - off % SUBLANES
            acc = acc + cdw[k:k + 1] * bufc[off % SUBLANES, start:start + CONV_ROW_CHUNK, :]
        mu = jnp.mean(acc, axis=-1, keepdims=True)
        cen = acc - mu
        var = jnp.mean(cen * cen, axis=-1, keepdims=True)
        yn = cen * lax.rsqrt(var + EPS) * lng + lnb
        yc_ref[0, row0:row0 + CONV_ROW_CHUNK, :] = (yn * _sigmoid(yn)).astype(BF16)
    bufc[0, 0:CONF_HALO, :] = u[tm - CONF_HALO:tm]

    logf = -_softplus(-(p[:, _C_F:_C_F + LANES] + bf_ref[...]))
    tril = tril_ref[...]
    dcum = dcarry[0:1, :]
    for part in _split3(logf):
        dcum = dcum + jnp.dot(tril, part, preferred_element_type=F32)
    dcarry[0:1, :] = dcum[tm - 1:tm, :]
    dcum_t = dcum.T
    qh, qm, ql = _split3(dcum)
    kh, km, kl = _split3(-dcum_t)
    q_extra = (jnp.dot(qh, pq_ref[0], preferred_element_type=F32)
               + jnp.dot(qm, pq_ref[1], preferred_element_type=F32)
               + jnp.dot(ql, pq_ref[2], preferred_element_type=F32) + qc_ref[...])
    k_extra = (jnp.dot(pk_ref[0], kh, preferred_element_type=F32)
               + jnp.dot(pk_ref[1], km, preferred_element_type=F32)
               + jnp.dot(pk_ref[2], kl, preferred_element_type=F32)
               + jnp.concatenate([kc_ref[...]] * (tm // LANES), axis=1))

    lane = lax.broadcasted_iota(I32, (tm, HEAD_SLAB), 1)
    low = lane < HEAD_DIM
    vc = vc_ref[...]
    for hd in range(2 * N_HEADS):
        is_fox = hd >= N_HEADS
        pair = (hd // 2) * HEAD_SLAB
        qs = p[:, _C_Q + pair:_C_Q + pair + HEAD_SLAB]
        vs = p[:, _C_V + pair:_C_V + pair + HEAD_SLAB]
        if hd % 2:
            qs = pltpu.roll(qs, HEAD_DIM, axis=1)
            vs = pltpu.roll(vs, HEAD_DIM, axis=1)
        if is_fox:
            hf = hd - N_HEADS
            qx = q_extra[:, hf * HEAD_SLAB:(hf + 1) * HEAD_SLAB]
            kx = k_extra[hf * HEAD_DIM:(hf + 1) * HEAD_DIM, :]
            vx = vc
        else:
            qx = 0.0
            kx = jnp.zeros((HEAD_DIM, tm), F32)
            vx = 0.0
        q_ref[0, hd] = jnp.where(low, qs, qx).astype(BF16)
        v_ref[0, hd] = jnp.where(low, vs, vx).astype(BF16)
        kfull = jnp.concatenate([pt[hd * HEAD_DIM:(hd + 1) * HEAD_DIM, :], kx], axis=0).astype(BF16)
        for c in range(tm // tk):
            kt_ref[0, hd, c] = kfull[:, c * tk:(c + 1) * tk]


def _mixer_in(x, comb, lw, consts, b0=0, b=None):
    s = x.shape[1]
    b = x.shape[0] if b is None else b
    tm = min(TOKEN_TILE, s)
    tk = min(ATTN_TILE, s)
    nk = s // tk
    nh2 = 2 * N_HEADS
    combine = comb is not None

    def full(a):
        return pl.BlockSpec(a.shape, lambda bi, si, _n=a.ndim: (0,) * _n)

    tok = lambda w: pl.BlockSpec((1, tm, w), lambda bi, si: (bi, si, 0))
    in_arrays = [x]
    in_specs = [pl.BlockSpec((1, tm, D_MODEL), lambda bi, si: (bi + b0, si, 0))]
    if combine:
        y01, route = comb
        in_arrays += [y01, y01, route]
        in_specs += [pl.BlockSpec((SC_SPLIT, 1, 1, tm, PLANE_W), lambda bi, si, _c=c: (0, _c, bi, si, 0))
                     for c in range(2)] + [tok(LANES)]
    weights = [lw["mix_g"], lw["w_main"], lw["w_t"], lw["bf"], lw["caw"], lw["cdw"], lw["cdb"], lw["lng"],
               lw["lnb"], consts["tril"], consts["pq"], consts["qc"], consts["pk"], consts["kc"], consts["vc"]]
    in_arrays += weights
    in_specs += [full(a) for a in weights]

    out_shape = []
    out_specs = []
    if combine:
        out_shape.append(jax.ShapeDtypeStruct((b, s, D_MODEL), F32))
        out_specs.append(tok(D_MODEL))
    out_shape += [
        jax.ShapeDtypeStruct((b, s, BRANCH_WIDTH), BF16),
        jax.ShapeDtypeStruct((b, s, BRANCH_WIDTH), BF16),
        jax.ShapeDtypeStruct((b, nh2, s, HEAD_SLAB), BF16),
        jax.ShapeDtypeStruct((b, nh2, nk, HEAD_SLAB, tk), BF16),
        jax.ShapeDtypeStruct((b, nh2, s, HEAD_SLAB), BF16),
    ]
    out_specs += [
        tok(BRANCH_WIDTH), tok(BRANCH_WIDTH),
        pl.BlockSpec((1, nh2, tm, HEAD_SLAB), lambda bi, si: (bi, 0, si, 0)),
        pl.BlockSpec((1, nh2, tm // tk, HEAD_SLAB, tk), lambda bi, si: (bi, 0, si, 0, 0)),
        pl.BlockSpec((1, nh2, tm, HEAD_SLAB), lambda bi, si: (bi, 0, si, 0)),
    ]
    return pl.pallas_call(
        functools.partial(_mixer_in_kernel, combine),
        grid=(b, s // tm),
        in_specs=in_specs,
        out_specs=out_specs,
        out_shape=out_shape,
        scratch_shapes=[pltpu.VMEM((A_HALO + tm, BRANCH_WIDTH), F32),
                        pltpu.VMEM((SUBLANES, CONF_HALO + tm, BRANCH_WIDTH), F32),
                        pltpu.VMEM((SUBLANES, LANES), F32)],
        compiler_params=pltpu.CompilerParams(dimension_semantics=("arbitrary", "arbitrary"),
                                             vmem_limit_bytes=VMEM_LIMIT),
        name="mixer_in",
    )(*in_arrays)


def _pair_out(accs):
    lane = lax.broadcasted_iota(I32, accs[0].shape, 1)
    return jnp.where(lane < HEAD_DIM, accs[0], pltpu.roll(accs[1], HEAD_DIM, axis=1))


def _sb_attn_kernel(q_ref, kt_ref, v_ref, u_ref, o_ref):
    tq = q_ref.shape[2]
    nk, tk = kt_ref.shape[2], kt_ref.shape[-1]
    i = pl.program_id(2)
    umat = u_ref[...]
    nh = q_ref.shape[1]
    nb = SB_GROUP
    qs = [q_ref[0, h] for h in range(nh)]
    row = lax.broadcasted_iota(I32, (tq, tk), 0)
    col = lax.broadcasted_iota(I32, (tq, tk), 1)

    def weights(g, tails, masked):
        ws, new_tails = [], []
        for h in range(nh):
            tail = tails[h]
            wh = []
            for u in reversed(range(nb)):
                jr = g * nb + u
                z = jnp.dot(qs[h], kt_ref[0, h, jnp.minimum(jr, nk - 1)], preferred_element_type=F32)
                sp = jnp.maximum(z, 0.0) + jnp.log(1.0 + jnp.exp2(jnp.abs(z) * -LOG2E))
                if masked:
                    mask = col + (jr - i) * tk < row
                    sp = jnp.where(mask, sp, 0.0)
                later = jnp.dot(sp.astype(BF16), umat, preferred_element_type=F32)
                w = jnp.exp((z - later - tail).astype(BF16))
                if masked:
                    w = jnp.where(mask, w, jnp.zeros_like(w))
                wh.append(w)
                tail = tail + later[:, 0:1]
            ws.append(tuple(reversed(wh)))
            new_tails.append(tail)
        return tuple(ws), tuple(new_tails)

    def apply(g, accs, ws):
        out = []
        for h in range(nh):
            acc = accs[h]
            for u in range(nb):
                start = pl.multiple_of(jnp.minimum(g * nb + u, nk - 1) * tk, tk)
                acc = acc + jnp.dot(ws[h][u], v_ref[0, h, pl.ds(start, tk), :], preferred_element_type=F32)
            out.append(acc)
        return tuple(out)

    last = i // nb
    zero_t = tuple(jnp.zeros((tq, 1), F32) for _ in range(nh))
    accs = tuple(jnp.zeros((tq, HEAD_SLAB), F32) for _ in range(nh))
    ws, tails = weights(last, zero_t, True)

    def body(t, carry):
        accs, tails, ws = carry
        g = last - 1 - t
        accs = apply(g + 1, accs, ws)
        ws, tails = weights(g, tails, False)
        return accs, tails, ws

    accs, tails, ws = lax.fori_loop(0, last, body, (accs, tails, ws))
    accs = apply(0, accs, ws)
    o_ref[0] = jnp.concatenate([_pair_out(accs[k:k + 2]) for k in range(0, nh, 2)], axis=1).astype(BF16)


def _fox_attn_kernel(q_ref, kt_ref, v_ref, o_ref):
    tq = q_ref.shape[2]
    tk = kt_ref.shape[-1]
    i = pl.program_id(2)
    row = lax.broadcasted_iota(I32, (tq, tk), 0)
    col = lax.broadcasted_iota(I32, (tq, tk), 1)
    nh = q_ref.shape[1]
    qs = [q_ref[0, h] for h in range(nh)]

    def vblock(h, j):
        return v_ref[0, h, pl.ds(pl.multiple_of(j * tk, tk), tk), :]

    r = tq // tk
    accs, maxes = [], []
    for h in range(nh):
        ss = [jnp.where(col + d * tk <= row, jnp.dot(qs[h], kt_ref[0, h, i * r + d], preferred_element_type=F32),
                        -jnp.inf) for d in range(r)]
        m = jnp.max(ss[0], axis=-1, keepdims=True)
        for s in ss[1:]:
            m = jnp.maximum(m, jnp.max(s, axis=-1, keepdims=True))
        acc = None
        for d, s in enumerate(ss):
            term = jnp.dot(jnp.exp(s - m).astype(BF16), vblock(h, i * r + d), preferred_element_type=F32)
            acc = term if acc is None else acc + term
        accs.append(acc)
        maxes.append(m)

    def make_body(n_blocks, first):
        def body(jj, carry):
            accs, maxes = carry
            new_accs, new_maxes = [], []
            for h in range(nh):
                js = [first + n_blocks * jj + u for u in range(n_blocks)]
                ss = [jnp.dot(qs[h], kt_ref[0, h, j], preferred_element_type=F32) for j in js]
                m = maxes[h]
                for s in ss:
                    m = jnp.maximum(m, jnp.max(s, axis=-1, keepdims=True))
                acc = jnp.exp(maxes[h] - m) * accs[h]
                for s, j in zip(ss, js):
                    acc = acc + jnp.dot(jnp.exp(s - m).astype(BF16), vblock(h, j), preferred_element_type=F32)
                new_accs.append(acc)
                new_maxes.append(m)
            return tuple(new_accs), tuple(new_maxes)
        return body

    carry = (tuple(accs), tuple(maxes))
    done = 0
    for n_blocks in ATTN_TRIPS:
        trips = (i * r - done) // n_blocks
        carry = lax.fori_loop(0, trips, make_body(n_blocks, done), carry)
        done = done + trips * n_blocks
    accs = carry[0]
    outs = [a / a[:, HEAD_DIM:HEAD_DIM + 1] for a in accs]
    o_ref[0] = jnp.concatenate([_pair_out(outs[k:k + 2]) for k in range(0, nh, 2)], axis=1).astype(BF16)


def _attention(kind, q, kt, v, consts):
    b, _, s, _ = q.shape
    nk, tk = kt.shape[2], kt.shape[4]
    tq = tk if kind == "sb" else min(FOX_Q_BLOCKS * tk, s)
    hps = SB_HEADS_PER_STEP if kind == "sb" else FOX_HEADS_PER_STEP
    head0 = 0 if kind == "sb" else N_HEADS // hps
    in_specs = [
        pl.BlockSpec((1, hps, tq, HEAD_SLAB), lambda bi, hp, i: (bi, hp + head0, i, 0)),
        pl.BlockSpec((1, hps, nk, HEAD_SLAB, tk), lambda bi, hp, i: (bi, hp + head0, 0, 0, 0)),
        pl.BlockSpec((1, hps, s, HEAD_SLAB), lambda bi, hp, i: (bi, hp + head0, 0, 0)),
    ]
    args = [q, kt, v]
    if kind == "sb":
        in_specs.append(pl.BlockSpec((tk, tk), lambda bi, hp, i: (0, 0)))
        args.append(consts["u_incl"])
        body = _sb_attn_kernel
    else:
        body = _fox_attn_kernel
    return pl.pallas_call(
        body,
        grid=(b, N_HEADS // hps, s // tq),
        in_specs=in_specs,
        out_specs=pl.BlockSpec((1, tq, hps * HEAD_DIM), lambda bi, hp, i: (bi, i, hp)),
        out_shape=jax.ShapeDtypeStruct((b, s, BRANCH_WIDTH), BF16),
        compiler_params=pltpu.CompilerParams(dimension_semantics=("arbitrary", "arbitrary", "arbitrary"),
                                             vmem_limit_bytes=VMEM_LIMIT),
        name=kind + "_attn",
    )(*args)


def _mixer_out_kernel(x_ref, ya_ref, ysb_ref, yfx_ref, yc_ref, g_ref, wg_ref, bg_ref, wb_ref, wo_ref,
                      fg_ref, wr_ref, tru_ref, xo_ref, xp_ref, rt_ref, rtt_ref, cnt_ref, carry):
    tm = x_ref.shape[0]

    @pl.when(pl.program_id(0) == 0)
    def _():
        carry[...] = jnp.zeros_like(carry)

    x = x_ref[...]
    xb = _rms(x, g_ref[...]).astype(BF16)
    h = None
    for g, y_ref in enumerate((ya_ref, ysb_ref, yfx_ref, yc_ref)):
        gate = _sigmoid(jnp.dot(xb, wg_ref[g], preferred_element_type=F32) + bg_ref[g])
        term = gate * jnp.dot(y_ref[...], wb_ref[g], preferred_element_type=F32)
        h = term if h is None else h + term
    xo = x + jnp.dot(h.astype(BF16), wo_ref[...], preferred_element_type=F32)
    xo_ref[...] = xo
    xn = _rms(xo, fg_ref[...])
    _store_planes(xp_ref, xn)

    xh = xn.astype(BF16)
    xl = (xn - xh.astype(F32)).astype(BF16)
    nt = (((1,), (1,)), ((), ()))
    logits = (lax.dot_general(wr_ref[0], xh, nt, preferred_element_type=F32)
              + lax.dot_general(wr_ref[0], xl, nt, preferred_element_type=F32)
              + lax.dot_general(wr_ref[1], xh, nt, preferred_element_type=F32))[0:ROUTE_ROWS, :]
    row = lax.broadcasted_iota(I32, (ROUTE_ROWS, tm), 0).astype(F32)
    ninf = -jnp.inf
    big = float(LANES)
    gl = jnp.where(row < N_GROUPS, logits, ninf)
    gmax = jnp.max(gl, axis=0, keepdims=True)
    gidx = jnp.min(jnp.where(gl == gmax, row, big), axis=0, keepdims=True)
    p_group = 1.0 / jnp.sum(jnp.exp(gl - gmax), axis=0, keepdims=True)
    first = N_GROUPS + EXPERTS_PER_GROUP * gidx
    el = jnp.where((row >= first) & (row < first + EXPERTS_PER_GROUP), logits, ninf)
    m1 = jnp.max(el, axis=0, keepdims=True)
    i1 = jnp.min(jnp.where(el == m1, row, big), axis=0, keepdims=True)
    el2 = jnp.where(row == i1, ninf, el)
    m2 = jnp.max(el2, axis=0, keepdims=True)
    i2 = jnp.min(jnp.where(el2 == m2, row, big), axis=0, keepdims=True)
    e2 = jnp.exp(m2 - m1)
    w1 = p_group / (1.0 + e2)
    w2 = w1 * e2

    sel1 = row == i1
    sel2 = row == i2
    onehot = jnp.where(sel1, 1.0, jnp.where(sel2, 1.0, 0.0))
    before = jnp.dot(onehot.astype(BF16), tru_ref[...], preferred_element_type=F32) + carry[:, 0:1]
    r1 = jnp.sum(jnp.where(sel1, before, 0.0), axis=0, keepdims=True)
    r2 = jnp.sum(jnp.where(sel2, before, 0.0), axis=0, keepdims=True)
    total = carry[...] + jnp.sum(onehot, axis=1, keepdims=True)
    carry[...] = total
    cnt_ref[...] = total

    row8 = lax.broadcasted_iota(I32, (8, tm), 0)
    fields = (i1 - N_GROUPS, i2 - N_GROUPS, w1, w2, r1, r2)
    rtt = jnp.zeros((8, tm), F32)
    for k, f in enumerate(fields):
        rtt = jnp.where(row8 == k, f, rtt)
    rtt_ref[...] = rtt
    rt_ref[...] = jnp.concatenate([rtt, jnp.zeros((LANES - 8, tm), F32)], axis=0).T


def _mixer_out(x2d, ya, ysb, yfx, yc, lw, consts, row0=0):
    t = ya.shape[0]
    tm = min(TOKEN_TILE, t)

    def full(a):
        return pl.BlockSpec(a.shape, lambda i, _n=a.ndim: (0,) * _n)

    tok = lambda w: pl.BlockSpec((tm, w), lambda i: (i, 0))
    weights = [lw["mix_g"], lw["w_gate"], lw["b_gate"], lw["w_branch"], lw["w_out"], lw["ffn_g"], lw["w_router"],
               consts["triu_strict"]]
    return pl.pallas_call(
        _mixer_out_kernel,
        grid=(t // tm,),
        in_specs=[pl.BlockSpec((tm, D_MODEL), lambda i: (i + row0 // tm, 0))] + [tok(BRANCH_WIDTH)] * 4
        + [full(a) for a in weights],
        out_specs=[tok(D_MODEL), pl.BlockSpec((SC_SPLIT, tm, PLANE_W), lambda i: (0, i, 0)), tok(LANES),
                   pl.BlockSpec((8, tm), lambda i: (0, i)), pl.BlockSpec((ROUTE_ROWS, LANES), lambda i: (0, 0))],
        out_shape=[jax.ShapeDtypeStruct((t, D_MODEL), F32),
                   jax.ShapeDtypeStruct((SC_SPLIT, t, PLANE_W), I32),
                   jax.ShapeDtypeStruct((t, LANES), F32),
                   jax.ShapeDtypeStruct((8, t), F32),
                   jax.ShapeDtypeStruct((ROUTE_ROWS, LANES), F32)],
        scratch_shapes=[pltpu.VMEM((ROUTE_ROWS, LANES), F32)],
        compiler_params=pltpu.CompilerParams(dimension_semantics=("arbitrary",),
                                             vmem_limit_bytes=VMEM_LIMIT),
        name="mixer_out",
    )(x2d, ya, ysb, yfx, yc, *weights)


def _sc_mesh():
    return plsc.VectorSubcoreMesh(core_axis_name="core", subcore_axis_name="subcore")


def _plane_index(idx, rows_per_plane):
    return jnp.concatenate([idx + k * rows_per_plane for k in range(SC_SPLIT)]).reshape(1, -1)


def _dispatch_rows(planes, pos0, pos1, n_out):
    w = planes.shape[2]
    rows = planes.reshape(-1, w)
    t = rows.shape[0]
    idx0 = _plane_index(pos0, n_out)
    idx1 = _plane_index(pos1, n_out)

    @functools.partial(pl.kernel, out_type=jax.ShapeDtypeStruct((SC_SPLIT * n_out, w), rows.dtype),
                       mesh=_sc_mesh(), scratch_types=[])
    def scatter_kernel(x_hbm, i0_hbm, i1_hbm, o_hbm):
        def body(x_vmem, i0_vmem, i1_vmem):
            pltpu.sync_copy(x_vmem, o_hbm.at[i0_vmem.at[0]])
            pltpu.sync_copy(x_vmem, o_hbm.at[i1_vmem.at[0]])

        pltpu.emit_pipeline(
            body,
            grid=(t // SC_WINDOW,),
            in_specs=[pl.BlockSpec((SC_WINDOW, w), lambda i: (i, 0)),
                      pl.BlockSpec((1, SC_WINDOW), lambda i: (0, i)),
                      pl.BlockSpec((1, SC_WINDOW), lambda i: (0, i))],
            out_specs=[],
            core_axis_name=("core", "subcore"),
            dimension_semantics=(pltpu.PARALLEL,),
        )(x_hbm, i0_hbm, i1_hbm)

    return scatter_kernel(rows, idx0, idx1).reshape(SC_SPLIT, n_out, w)


def _collect_rows(planes, idx):
    n, w = planes.shape[1:]
    table = planes.reshape(-1, w)
    idx2 = _plane_index(idx, n)
    m = idx2.shape[1]

    @functools.partial(pl.kernel, out_type=jax.ShapeDtypeStruct((m, w), table.dtype), mesh=_sc_mesh(),
                       scratch_types=[])
    def gather_kernel(x_hbm, i_hbm, o_hbm):
        def body(i_vmem, o_vmem):
            pltpu.sync_copy(x_hbm.at[i_vmem.at[0]], o_vmem)

        pltpu.emit_pipeline(
            body,
            grid=(m // SC_WINDOW,),
            in_specs=[pl.BlockSpec((1, SC_WINDOW), lambda i: (0, i))],
            out_specs=[pl.BlockSpec((SC_WINDOW, w), lambda i: (i, 0))],
            core_axis_name=("core", "subcore"),
            dimension_semantics=(pltpu.PARALLEL,),
        )(i_hbm, o_hbm)

    return gather_kernel(table, idx2).reshape(SC_SPLIT, -1, w)


def _moe_ffn_kernel(te_ref, nt_ref, xs_ref, wgu_ref, wd_ref, ys_ref):
    i = pl.program_id(0)

    @pl.when(i < nt_ref[0])
    def _():
        gu = None
        for k, part in enumerate(_load_planes(xs_ref)):
            term = jnp.dot(part.astype(BF16), wgu_ref[0, k * PLANE_W:(k + 1) * PLANE_W, :],
                           preferred_element_type=F32)
            gu = term if gu is None else gu + term
        gate = gu[:, :D_EXPERT]
        hid = gate * _sigmoid(gate) * gu[:, D_EXPERT:]
        _store_planes(ys_ref, jnp.dot(hid.astype(BF16), wd_ref[0], preferred_element_type=F32))

    @pl.when(i >= nt_ref[0])
    def _():
        ys_ref[...] = jnp.zeros_like(ys_ref)


def _moe_ffn(xs, tile_expert, n_tiles, lw):
    p = xs.shape[1]
    tg = GROUP_TILE
    rows = pl.BlockSpec((SC_SPLIT, tg, PLANE_W), lambda i, te, nt: (0, i, 0))
    grid_spec = pltpu.PrefetchScalarGridSpec(
        num_scalar_prefetch=2,
        grid=(p // tg,),
        in_specs=[rows,
                  pl.BlockSpec((1, D_MODEL, 2 * D_EXPERT), lambda i, te, nt: (te[i], 0, 0)),
                  pl.BlockSpec((1, D_EXPERT, D_MODEL), lambda i, te, nt: (te[i], 0, 0))],
        out_specs=rows,
    )
    return pl.pallas_call(
        _moe_ffn_kernel,
        grid_spec=grid_spec,
        out_shape=jax.ShapeDtypeStruct((SC_SPLIT, p, PLANE_W), I32),
        compiler_params=pltpu.CompilerParams(dimension_semantics=("arbitrary",),
                                             vmem_limit_bytes=VMEM_LIMIT),
        name="moe_ffn",
    )(tile_expert, n_tiles, xs, lw["w_gu"], lw["w_down"])


def _route_plan(route_t, cnt, t):
    tg = GROUP_TILE
    n_tiles_max = (2 * t) // tg + N_EXPERTS
    counts = cnt[N_GROUPS:N_GROUPS + N_EXPERTS, 0].astype(I32)
    padded = ((counts + tg - 1) // tg) * tg
    ends = jnp.cumsum(padded)
    offs = ends - padded
    experts = jnp.arange(N_EXPERTS, dtype=I32)[:, None]

    def first_row(e):
        return jnp.sum(jnp.where(e[None, :] == experts, offs[:, None], 0), axis=0)

    fields = route_t.astype(I32)
    pos0 = first_row(fields[0]) + fields[4]
    pos1 = first_row(fields[1]) + fields[5]
    tile_start = jnp.arange(n_tiles_max, dtype=I32) * tg
    n_tiles = ends[-1] // tg
    tile_clamped = jnp.minimum(tile_start, jnp.maximum(n_tiles - 1, 0) * tg)
    tile_expert = jnp.sum((ends[None, :] <= tile_clamped[:, None]).astype(I32), axis=1)
    tile_expert = jnp.minimum(tile_expert, N_EXPERTS - 1)
    return pos0, pos1, tile_expert, n_tiles.reshape(1), n_tiles_max * tg


def _moe(xp, route_t, cnt, lw):
    t = xp.shape[1]
    pos0, pos1, tile_expert, n_tiles, p_rows = _route_plan(route_t, cnt, t)
    xs = _dispatch_rows(xp, pos0, pos1, p_rows)
    ys = _moe_ffn(xs, tile_expert, n_tiles, lw)
    return _collect_rows(ys, jnp.concatenate([pos0, pos1])).reshape(SC_SPLIT, 2, t, PLANE_W)


def _final_kernel(x_ref, y0_ref, y1_ref, rt_ref, g_ref, *rest):
    o_ref = rest[-1]
    x = _combine(x_ref[...], y0_ref, y1_ref, rt_ref[...], lead=(0,))
    o_ref[...] = _rms(x, g_ref[...])


def _final(x2d, y01, route, g, t_total, row0, out_prev):
    t = x2d.shape[0]
    tm = min(TOKEN_TILE, t)
    tok = lambda w: pl.BlockSpec((tm, w), lambda i: (i, 0))
    choice = lambda c: pl.BlockSpec((SC_SPLIT, 1, tm, PLANE_W), lambda i: (0, c, i, 0))
    in_specs = [tok(D_MODEL), choice(0), choice(1), tok(LANES), pl.BlockSpec((1, D_MODEL), lambda i: (0, 0))]
    args = [x2d, y01, y01, route, g]
    aliases = {}
    if out_prev is not None:
        in_specs.append(pl.BlockSpec(memory_space=pl.ANY))
        args.append(out_prev)
        aliases = {len(args) - 1: 0}
    return pl.pallas_call(
        _final_kernel,
        grid=(t // tm,),
        in_specs=in_specs,
        out_specs=pl.BlockSpec((tm, D_MODEL), lambda i: (i + row0 // tm, 0)),
        out_shape=jax.ShapeDtypeStruct((t_total, D_MODEL), F32),
        input_output_aliases=aliases,
        compiler_params=pltpu.CompilerParams(dimension_semantics=("arbitrary",),
                                             vmem_limit_bytes=VMEM_LIMIT),
        name="final_norm",
    )(*args)


def _constants(tm, tk):
    r = jnp.arange(tm)
    tril = (r[None, :] <= r[:, None]).astype(BF16)
    triu_strict = (r[:, None] < r[None, :]).astype(BF16)
    rk = jnp.arange(tk)
    u_incl = (rk[:, None] >= rk[None, :]).astype(BF16)
    nh = N_HEADS
    pq = jnp.zeros((3, LANES, nh * HEAD_SLAB), F32)
    pk = jnp.zeros((3, nh * HEAD_DIM, LANES), F32)
    qc = jnp.zeros((1, nh * HEAD_SLAB), F32)
    kc = jnp.zeros((nh * HEAD_DIM, LANES), F32)
    for part in range(3):
        for h in range(nh):
            pq = pq.at[part, h, h * HEAD_SLAB + HEAD_DIM + part].set(1.0)
            qc = qc.at[0, h * HEAD_SLAB + HEAD_DIM + 3 + part].set(1.0)
            kc = kc.at[h * HEAD_DIM + part, :].set(1.0)
            pk = pk.at[part, h * HEAD_DIM + 3 + part, h].set(1.0)
    vc = jnp.zeros((1, HEAD_SLAB), F32).at[0, HEAD_DIM].set(1.0)
    return {"tril": tril, "triu_strict": triu_strict, "u_incl": u_incl, "pq": pq.astype(BF16),
            "pk": pk.astype(BF16), "qc": qc, "kc": kc, "vc": vc}


def _layer_weights(layer, mix_norm_g, w_in, b_forget, conv_a_w, conf_dw_w, conf_dw_b, conf_ln_g, conf_ln_b,
                   w_branch, w_gate, b_gate, w_out, ffn_norm_g, w_router_group, w_router_expert,
                   w_expert_gate, w_expert_up, w_expert_down):
    w = w_in[layer]
    bw = BRANCH_WIDTH
    a_x, a_b, a_c, sb_q, sb_k, sb_v, fx_q, fx_k, fx_v = [w[:, i * bw:(i + 1) * bw] for i in range(9)]
    fx_f = w[:, 9 * bw:9 * bw + N_HEADS]
    conf = w[:, 9 * bw + N_HEADS:]
    scale = HEAD_DIM ** -0.5
    f_pad = jnp.pad(fx_f, ((0, 0), (0, LANES - N_HEADS)))
    w_main = jnp.concatenate([conf, f_pad, sb_q * scale, fx_q * scale, sb_v, fx_v, a_x, a_b, a_c], axis=1)
    w_t = jnp.concatenate([sb_k, fx_k], axis=1).T
    w_router = jnp.concatenate([w_router_group[layer], w_router_expert[layer].reshape(D_MODEL, N_EXPERTS)], axis=1)
    w_router = jnp.pad(w_router, ((0, 0), (0, LANES - N_GROUPS - N_EXPERTS)))
    return {
        "mix_g": mix_norm_g[layer].reshape(1, D_MODEL),
        "w_main": w_main.astype(BF16),
        "w_t": w_t.astype(BF16),
        "bf": jnp.pad(b_forget[layer], (0, LANES - N_HEADS)).reshape(1, LANES),
        "caw": conv_a_w[layer],
        "cdw": jnp.pad(conf_dw_w[layer], ((0, 1), (0, 0))),
        "cdb": conf_dw_b[layer].reshape(1, bw),
        "lng": conf_ln_g[layer].reshape(1, bw),
        "lnb": conf_ln_b[layer].reshape(1, bw),
        "w_gate": w_gate[layer].astype(BF16),
        "b_gate": b_gate[layer].reshape(4, 1, D_MODEL),
        "w_branch": w_branch[layer].astype(BF16),
        "w_out": w_out[layer].astype(BF16),
        "ffn_g": ffn_norm_g[layer].reshape(1, D_MODEL),
        "w_router": jnp.stack([w_router.T.astype(BF16), (w_router - w_router.astype(BF16).astype(F32)).T.astype(BF16)]),
        "w_gu": jnp.concatenate([w_expert_gate[layer], w_expert_up[layer]], axis=2).astype(BF16),
        "w_down": w_expert_down[layer].astype(BF16),
    }


def kernel(x, mix_norm_g, w_in, b_forget, conv_a_w, conf_dw_w, conf_dw_b, conf_ln_g, conf_ln_b, w_branch, w_gate,
           b_gate, w_out, ffn_norm_g, w_router_group, w_router_expert, w_expert_gate, w_expert_up, w_expert_down,
           final_norm_g):
    b, s, _ = x.shape
    depth = w_in.shape[0]
    consts = _constants(min(TOKEN_TILE, s), min(ATTN_TILE, s))
    parts = BATCH_SPLIT if b % BATCH_SPLIT == 0 else 1
    bp = b // parts
    tp = bp * s
    xs = [x] * parts
    combs = [None] * parts
    for layer in range(depth):
        lw = _layer_weights(layer, mix_norm_g, w_in, b_forget, conv_a_w, conf_dw_w, conf_dw_b, conf_ln_g,
                            conf_ln_b, w_branch, w_gate, b_gate, w_out, ffn_norm_g, w_router_group,
                            w_router_expert, w_expert_gate, w_expert_up, w_expert_down)
        routed = []
        for k in range(parts):
            outs = _mixer_in(xs[k], combs[k], lw, consts, b0=k * bp if layer == 0 else 0, b=bp)
            if combs[k] is not None:
                xk, *outs = outs
            else:
                xk = None
            ya, yc, q, kt, v = outs
            ysb = _attention("sb", q, kt, v, consts)
            yfx = _attention("fox", q, kt, v, consts)
            if xk is None:
                x2d_in, row0 = x.reshape(b * s, D_MODEL), k * tp
            else:
                x2d_in, row0 = xk.reshape(tp, D_MODEL), 0
            routed.append(_mixer_out(x2d_in, ya.reshape(tp, -1), ysb.reshape(tp, -1), yfx.reshape(tp, -1),
                                     yc.reshape(tp, -1), lw, consts, row0=row0))
        for k in range(parts):
            x2d, xp, route, route_t, cnt = routed[k]
            y01 = _moe(xp, route_t, cnt, lw)
            xs[k] = x2d.reshape(bp, s, D_MODEL)
            combs[k] = (y01.reshape(SC_SPLIT, 2, bp, s, PLANE_W), route.reshape(bp, s, LANES))
    out = None
    for k in range(parts):
        out = _final(xs[k].reshape(tp, D_MODEL), combs[k][0].reshape(SC_SPLIT, 2, tp, PLANE_W),
                     combs[k][1].reshape(tp, LANES), final_norm_g.reshape(1, D_MODEL), b * s, k * tp, out)
    return out.reshape(b, s, D_MODEL)
```

```python
import functools

import jax
import jax.numpy as jnp
from jax import lax
from jax.experimental import pallas as pl
from jax.experimental.pallas import tpu as pltpu
from jax.experimental.pallas import tpu_sc as plsc

F32 = jnp.float32
BF16 = jnp.bfloat16
I32 = jnp.int32

D_MODEL = 1024
BRANCH_WIDTH = 256
HEAD_DIM = 64
N_HEADS = 4
HEAD_SLAB = 128
CONV_A_WIDTH = 3
CONF_WIDTH = 31
N_GROUPS = 4
EXPERTS_PER_GROUP = 8
N_EXPERTS = 32
D_EXPERT = 256
EPS = 1e-6
LOG2E = 1.4426950408889634
LANES = 128
SUBLANES = 8
A_HALO = SUBLANES
CONF_HALO = 32
ROUTE_ROWS = 48
HALF_D = D_MODEL // 2
SC_SPLIT = 2
PLANE_W = HALF_D // SC_SPLIT

TOKEN_TILE = 512
ATTN_TILE = 256
FOX_Q_BLOCKS = 2
SB_HEADS_PER_STEP = 4
FOX_HEADS_PER_STEP = 4
ATTN_TRIPS = (8, 4, 2, 1)
SB_GROUP = 4
GROUP_TILE = 256
CONV_ROW_CHUNK = 128
SC_WINDOW = 128
BATCH_SPLIT = 2
VMEM_LIMIT = 56 * 1024 * 1024

_C_CU, _C_CG = 0, 256
_C_F = 512
_N_FIRST = 640
_C_Q = 640
_C_V = 1152
_C_AX, _C_AB, _C_AC = 1664, 1920, 2176
_N_MAIN = 2432


def _rms(x, g):
    return x * lax.rsqrt(jnp.mean(x * x, axis=-1, keepdims=True) + EPS) * g


def _softplus(z):
    return jnp.maximum(z, 0.0) + jnp.log1p(jnp.exp(-jnp.abs(z)))


def _sigmoid(z):
    return 1.0 / (1.0 + jnp.exp(-z))


def _split3(v):
    hi = v.astype(BF16)
    r = v - hi.astype(F32)
    mid = r.astype(BF16)
    lo = (r - mid.astype(F32)).astype(BF16)
    return hi, mid, lo


def _pack_rows(v):
    lo = pltpu.bitcast(v[:, :HALF_D].astype(BF16).astype(F32), jnp.uint32)
    hi = pltpu.bitcast(v[:, HALF_D:].astype(BF16).astype(F32), jnp.uint32)
    return pltpu.bitcast((lo >> 16) | hi, I32)


def _unpack_rows(w):
    u = pltpu.bitcast(w, jnp.uint32)
    lo = pltpu.bitcast(u << 16, F32)
    hi = pltpu.bitcast(u & jnp.uint32(0xFFFF0000), F32)
    return lo, hi


def _store_planes(ref, v, lead=()):
    packed = _pack_rows(v)
    for k in range(SC_SPLIT):
        ref[(k, *lead)] = packed[:, k * PLANE_W:(k + 1) * PLANE_W]


def _load_planes(ref, lead=()):
    los, his = zip(*[_unpack_rows(ref[(k, *lead)]) for k in range(SC_SPLIT)])
    return list(los) + list(his)


def _combine(x, y0_ref, y1_ref, rt, lead=()):
    w0 = rt[:, 2:3]
    w1 = rt[:, 3:4]
    parts = [w0 * a + w1 * b for a, b in zip(_load_planes(y0_ref, lead), _load_planes(y1_ref, lead))]
    return x + jnp.concatenate(parts, axis=1)


def _mixer_in_kernel(combine, *refs):
    if combine:
        (x_ref, y0_ref, y1_ref, rt_ref, *refs) = refs
    else:
        (x_ref, *refs) = refs
    (g_ref, wm_ref, wt_ref, bf_ref, caw_ref, cdw_ref, cdb_ref, lng_ref, lnb_ref, tril_ref,
     pq_ref, qc_ref, pk_ref, kc_ref, vc_ref, *refs) = refs
    if combine:
        (xo_ref, *refs) = refs
    (ya_ref, yc_ref, q_ref, kt_ref, v_ref, bufa, bufc, dcarry) = refs

    tm = x_ref.shape[1]
    tk = kt_ref.shape[-1]

    @pl.when(pl.program_id(1) == 0)
    def _():
        bufa[0:A_HALO, :] = jnp.zeros((A_HALO, BRANCH_WIDTH), F32)
        bufc[0, 0:CONF_HALO, :] = jnp.zeros((CONF_HALO, BRANCH_WIDTH), F32)
        dcarry[...] = jnp.zeros_like(dcarry)

    x = x_ref[0]
    if combine:
        x = _combine(x, y0_ref, y1_ref, rt_ref[0], lead=(0, 0))
        xo_ref[0] = x
    xb = _rms(x, g_ref[...]).astype(BF16)

    p = jnp.concatenate([jnp.dot(xb, wm_ref[:, :_N_FIRST], preferred_element_type=F32),
                         jnp.dot(xb, wm_ref[:, _N_FIRST:], preferred_element_type=F32)], axis=1)
    pt = lax.dot_general(wt_ref[...], xb, (((1,), (1,)), ((), ())),
                         preferred_element_type=F32)

    ca = p[:, _C_AC:_C_AC + 256] * p[:, _C_AX:_C_AX + 256]
    bufa[A_HALO:A_HALO + tm, :] = ca
    caw = caw_ref[...]
    conv = caw[CONV_A_WIDTH - 1:CONV_A_WIDTH] * ca
    for k in range(CONV_A_WIDTH - 1):
        off = A_HALO - (CONV_A_WIDTH - 1) + k
        conv = conv + caw[k:k + 1] * bufa[off:off + tm, :]
    ya_ref[0] = (p[:, _C_AB:_C_AB + 256] * conv).astype(BF16)
    bufa[0:A_HALO, :] = ca[tm - A_HALO:tm]

    u = p[:, _C_CU:_C_CU + 256] * _sigmoid(p[:, _C_CG:_C_CG + 256])
    bufc[0, CONF_HALO:CONF_HALO + tm, :] = u
    for r in range(1, SUBLANES):
        bufc[r, 0:tm + CONF_HALO - SUBLANES, :] = bufc[0, r:r + tm + CONF_HALO - SUBLANES, :]
    cdw = cdw_ref[...]
    cdb = cdb_ref[...]
    lng = lng_ref[...]
    lnb = lnb_ref[...]
    for c in range(tm // CONV_ROW_CHUNK):
        row0 = c * CONV_ROW_CHUNK
        acc = jnp.broadcast_to(cdb, (CONV_ROW_CHUNK, BRANCH_WIDTH))
        for k in range(CONF_WIDTH):
            off = CONF_HALO - (CONF_WIDTH - 1) + k
            start = row0 + off - off % SUBLANES
            acc = acc + cdw[k:k + 1] * bufc[off % SUBLANES, start:start + CONV_ROW_CHUNK, :]
        mu = jnp.mean(acc, axis=-1, keepdims=True)
        cen = acc - mu
        var = jnp.mean(cen * cen, axis=-1, keepdims=True)
        yn = cen * lax.rsqrt(var + EPS) * lng + lnb
        yc_ref[0, row0:row0 + CONV_ROW_CHUNK, :] = (yn * _sigmoid(yn)).astype(BF16)
    bufc[0, 0:CONF_HALO, :] = u[tm - CONF_HALO:tm]

    logf = -_softplus(-(p[:, _C_F:_C_F + LANES] + bf_ref[...]))
    tril = tril_ref[...]
    dcum = dcarry[0:1, :]
    for part in _split3(logf):
        dcum = dcum + jnp.dot(tril, part, preferred_element_type=F32)
    dcarry[0:1, :] = dcum[tm - 1:tm, :]
    dcum_t = dcum.T
    qh, qm, ql = _split3(dcum)
    kh, km, kl = _split3(-dcum_t)
    q_extra = (jnp.dot(qh, pq_ref[0], preferred_element_type=F32)
               + jnp.dot(qm, pq_ref[1], preferred_element_type=F32)
               + jnp.dot(ql, pq_ref[2], preferred_element_type=F32) + qc_ref[...])
    k_extra = (jnp.dot(pk_ref[0], kh, preferred_element_type=F32)
               + jnp.dot(pk_ref[1], km, preferred_element_type=F32)
               + jnp.dot(pk_ref[2], kl, preferred_element_type=F32)
               + jnp.concatenate([kc_ref[...]] * (tm // LANES), axis=1))

    lane = lax.broadcasted_iota(I32, (tm, HEAD_SLAB), 1)
    low = lane < HEAD_DIM
    vc = vc_ref[...]
    for hd in range(2 * N_HEADS):
        is_fox = hd >= N_HEADS
        pair = (hd // 2) * HEAD_SLAB
        qs = p[:, _C_Q + pair:_C_Q + pair + HEAD_SLAB]
        vs = p[:, _C_V + pair:_C_V + pair + HEAD_SLAB]
        if hd % 2:
            qs = pltpu.roll(qs, HEAD_DIM, axis=1)
            vs = pltpu.roll(vs, HEAD_DIM, axis=1)
        if is_fox:
            hf = hd - N_HEADS
            qx = q_extra[:, hf * HEAD_SLAB:(hf + 1) * HEAD_SLAB]
            kx = k_extra[hf * HEAD_DIM:(hf + 1) * HEAD_DIM, :]
            vx = vc
        else:
            qx = 0.0
            kx = jnp.zeros((HEAD_DIM, tm), F32)
            vx = 0.0
        q_ref[0, hd] = jnp.where(low, qs, qx).astype(BF16)
        v_ref[0, hd] = jnp.where(low, vs, vx).astype(BF16)
        kfull = jnp.concatenate([pt[hd * HEAD_DIM:(hd + 1) * HEAD_DIM, :], kx], axis=0).astype(BF16)
        for c in range(tm // tk):
            kt_ref[0, hd, c] = kfull[:, c * tk:(c + 1) * tk]


def _mixer_in(x, comb, lw, consts, b0=0, b=None):
    s = x.shape[1]
    b = x.shape[0] if b is None else b
    tm = min(TOKEN_TILE, s)
    tk = min(ATTN_TILE, s)
    nk = s // tk
    nh2 = 2 * N_HEADS
    combine = comb is not None

    def full(a):
        return pl.BlockSpec(a.shape, lambda bi, si, _n=a.ndim: (0,) * _n)

    tok = lambda w: pl.BlockSpec((1, tm, w), lambda bi, si: (bi, si, 0))
    in_arrays = [x]
    in_specs = [pl.BlockSpec((1, tm, D_MODEL), lambda bi, si: (bi + b0, si, 0))]
    if combine:
        y01, route = comb
        in_arrays += [y01, y01, route]
        in_specs += [pl.BlockSpec((SC_SPLIT, 1, 1, tm, PLANE_W), lambda bi, si, _c=c: (0, _c, bi, si, 0))
                     for c in range(2)] + [tok(LANES)]
    weights = [lw["mix_g"], lw["w_main"], lw["w_t"], lw["bf"], lw["caw"], lw["cdw"], lw["cdb"], lw["lng"],
               lw["lnb"], consts["tril"], consts["pq"], consts["qc"], consts["pk"], consts["kc"], consts["vc"]]
    in_arrays += weights
    in_specs += [full(a) for a in weights]

    out_shape = []
    out_specs = []
    if combine:
        out_shape.append(jax.ShapeDtypeStruct((b, s, D_MODEL), F32))
        out_specs.append(tok(D_MODEL))
    out_shape += [
        jax.ShapeDtypeStruct((b, s, BRANCH_WIDTH), BF16),
        jax.ShapeDtypeStruct((b, s, BRANCH_WIDTH), BF16),
        jax.ShapeDtypeStruct((b, nh2, s, HEAD_SLAB), BF16),
        jax.ShapeDtypeStruct((b, nh2, nk, HEAD_SLAB, tk), BF16),
        jax.ShapeDtypeStruct((b, nh2, s, HEAD_SLAB), BF16),
    ]
    out_specs += [
        tok(BRANCH_WIDTH), tok(BRANCH_WIDTH),
        pl.BlockSpec((1, nh2, tm, HEAD_SLAB), lambda bi, si: (bi, 0, si, 0)),
        pl.BlockSpec((1, nh2, tm // tk, HEAD_SLAB, tk), lambda bi, si: (bi, 0, si, 0, 0)),
        pl.BlockSpec((1, nh2, tm, HEAD_SLAB), lambda bi, si: (bi, 0, si, 0)),
    ]
    return pl.pallas_call(
        functools.partial(_mixer_in_kernel, combine),
        grid=(b, s // tm),
        in_specs=in_specs,
        out_specs=out_specs,
        out_shape=out_shape,
        scratch_shapes=[pltpu.VMEM((A_HALO + tm, BRANCH_WIDTH), F32),
                        pltpu.VMEM((SUBLANES, CONF_HALO + tm, BRANCH_WIDTH), F32),
                        pltpu.VMEM((SUBLANES, LANES), F32)],
        compiler_params=pltpu.CompilerParams(dimension_semantics=("arbitrary", "arbitrary"),
                                             vmem_limit_bytes=VMEM_LIMIT),
        name="mixer_in",
    )(*in_arrays)


def _pair_out(accs):
    lane = lax.broadcasted_iota(I32, accs[0].shape, 1)
    return jnp.where(lane < HEAD_DIM, accs[0], pltpu.roll(accs[1], HEAD_DIM, axis=1))


def _sb_attn_kernel(q_ref, kt_ref, v_ref, u_ref, o_ref):
    tq = q_ref.shape[2]
    nk, tk = kt_ref.shape[2], kt_ref.shape[-1]
    i = pl.program_id(2)
    umat = u_ref[...]
    nh = q_ref.shape[1]
    nb = SB_GROUP
    qs = [q_ref[0, h] for h in range(nh)]
    row = lax.broadcasted_iota(I32, (tq, tk), 0)
    col = lax.broadcasted_iota(I32, (tq, tk), 1)

    def weights(g, tails, masked):
        ws, new_tails = [], []
        for h in range(nh):
            tail = tails[h]
            wh = []
            for u in reversed(range(nb)):
                jr = g * nb + u
                z = jnp.dot(qs[h], kt_ref[0, h, jnp.minimum(jr, nk - 1)], preferred_element_type=F32)
                sp = jnp.maximum(z, 0.0) + jnp.log(1.0 + jnp.exp2(jnp.abs(z) * -LOG2E))
                if masked:
                    mask = col + (jr - i) * tk < row
                    sp = jnp.where(mask, sp, 0.0)
                later = jnp.dot(sp.astype(BF16), umat, preferred_element_type=F32)
                w = jnp.exp((z - later - tail).astype(BF16))
                if masked:
                    w = jnp.where(mask, w, jnp.zeros_like(w))
                wh.append(w)
                tail = tail + later[:, 0:1]
            ws.append(tuple(reversed(wh)))
            new_tails.append(tail)
        return tuple(ws), tuple(new_tails)

    def apply(g, accs, ws):
        out = []
        for h in range(nh):
            acc = accs[h]
            for u in range(nb):
                start = pl.multiple_of(jnp.minimum(g * nb + u, nk - 1) * tk, tk)
                acc = acc + jnp.dot(ws[h][u], v_ref[0, h, pl.ds(start, tk), :], preferred_element_type=F32)
            out.append(acc)
        return tuple(out)

    last = i // nb
    zero_t = tuple(jnp.zeros((tq, 1), F32) for _ in range(nh))
    accs = tuple(jnp.zeros((tq, HEAD_SLAB), F32) for _ in range(nh))
    ws, tails = weights(last, zero_t, True)

    def body(t, carry):
        accs, tails, ws = carry
        g = last - 1 - t
        accs = apply(g + 1, accs, ws)
        ws, tails = weights(g, tails, False)
        return accs, tails, ws

    accs, tails, ws = lax.fori_loop(0, last, body, (accs, tails, ws))
    accs = apply(0, accs, ws)
    o_ref[0] = jnp.concatenate([_pair_out(accs[k:k + 2]) for k in range(0, nh, 2)], axis=1).astype(BF16)


def _fox_attn_kernel(q_ref, kt_ref, v_ref, o_ref):
    tq = q_ref.shape[2]
    tk = kt_ref.shape[-1]
    i = pl.program_id(2)
    row = lax.broadcasted_iota(I32, (tq, tk), 0)
    col = lax.broadcasted_iota(I32, (tq, tk), 1)
    nh = q_ref.shape[1]
    qs = [q_ref[0, h] for h in range(nh)]

    def vblock(h, j):
        return v_ref[0, h, pl.ds(pl.multiple_of(j * tk, tk), tk), :]

    r = tq // tk
    accs, maxes = [], []
    for h in range(nh):
        ss = [jnp.where(col + d * tk <= row, jnp.dot(qs[h], kt_ref[0, h, i * r + d], preferred_element_type=F32),
                        -jnp.inf) for d in range(r)]
        m = jnp.max(ss[0], axis=-1, keepdims=True)
        for s in ss[1:]:
            m = jnp.maximum(m, jnp.max(s, axis=-1, keepdims=True))
        acc = None
        for d, s in enumerate(ss):
            term = jnp.dot(jnp.exp(s - m).astype(BF16), vblock(h, i * r + d), preferred_element_type=F32)
            acc = term if acc is None else acc + term
        accs.append(acc)
        maxes.append(m)

    def make_body(n_blocks, first):
        def body(jj, carry):
            accs, maxes = carry
            new_accs, new_maxes = [], []
            for h in range(nh):
                js = [first + n_blocks * jj + u for u in range(n_blocks)]
                ss = [jnp.dot(qs[h], kt_ref[0, h, j], preferred_element_type=F32) for j in js]
                m = maxes[h]
                for s in ss:
                    m = jnp.maximum(m, jnp.max(s, axis=-1, keepdims=True))
                acc = jnp.exp(maxes[h] - m) * accs[h]
                for s, j in zip(ss, js):
                    acc = acc + jnp.dot(jnp.exp(s - m).astype(BF16), vblock(h, j), preferred_element_type=F32)
                new_accs.append(acc)
                new_maxes.append(m)
            return tuple(new_accs), tuple(new_maxes)
        return body

    carry = (tuple(accs), tuple(maxes))
    done = 0
    for n_blocks in ATTN_TRIPS:
        trips = (i * r - done) // n_blocks
        carry = lax.fori_loop(0, trips, make_body(n_blocks, done), carry)
        done = done + trips * n_blocks
    accs = carry[0]
    outs = [a / a[:, HEAD_DIM:HEAD_DIM + 1] for a in accs]
    o_ref[0] = jnp.concatenate([_pair_out(outs[k:k + 2]) for k in range(0, nh, 2)], axis=1).astype(BF16)


def _attention(kind, q, kt, v, consts):
    b, _, s, _ = q.shape
    nk, tk = kt.shape[2], kt.shape[4]
    tq = tk if kind == "sb" else min(FOX_Q_BLOCKS * tk, s)
    hps = SB_HEADS_PER_STEP if kind == "sb" else FOX_HEADS_PER_STEP
    head0 = 0 if kind == "sb" else N_HEADS // hps
    in_specs = [
        pl.BlockSpec((1, hps, tq, HEAD_SLAB), lambda bi, hp, i: (bi, hp + head0, i, 0)),
        pl.BlockSpec((1, hps, nk, HEAD_SLAB, tk), lambda bi, hp, i: (bi, hp + head0, 0, 0, 0)),
        pl.BlockSpec((1, hps, s, HEAD_SLAB), lambda bi, hp, i: (bi, hp + head0, 0, 0)),
    ]
    args = [q, kt, v]
    if kind == "sb":
        in_specs.append(pl.BlockSpec((tk, tk), lambda bi, hp, i: (0, 0)))
        args.append(consts["u_incl"])
        body = _sb_attn_kernel
    else:
        body = _fox_attn_kernel
    return pl.pallas_call(
        body,
        grid=(b, N_HEADS // hps, s // tq),
        in_specs=in_specs,
        out_specs=pl.BlockSpec((1, tq, hps * HEAD_DIM), lambda bi, hp, i: (bi, i, hp)),
        out_shape=jax.ShapeDtypeStruct((b, s, BRANCH_WIDTH), BF16),
        compiler_params=pltpu.CompilerParams(dimension_semantics=("arbitrary", "arbitrary", "arbitrary"),
                                             vmem_limit_bytes=VMEM_LIMIT),
        name=kind + "_attn",
    )(*args)


def _mixer_out_kernel(x_ref, ya_ref, ysb_ref, yfx_ref, yc_ref, g_ref, wg_ref, bg_ref, wb_ref, wo_ref,
                      fg_ref, wr_ref, tru_ref, xo_ref, xp_ref, rt_ref, rtt_ref, cnt_ref, carry):
    tm = x_ref.shape[0]

    @pl.when(pl.program_id(0) == 0)
    def _():
        carry[...] = jnp.zeros_like(carry)

    x = x_ref[...]
    xb = _rms(x, g_ref[...]).astype(BF16)
    h = None
    for g, y_ref in enumerate((ya_ref, ysb_ref, yfx_ref, yc_ref)):
        gate = _sigmoid(jnp.dot(xb, wg_ref[g], preferred_element_type=F32) + bg_ref[g])
        term = gate * jnp.dot(y_ref[...], wb_ref[g], preferred_element_type=F32)
        h = term if h is None else h + term
    xo = x + jnp.dot(h.astype(BF16), wo_ref[...], preferred_element_type=F32)
    xo_ref[...] = xo
    xn = _rms(xo, fg_ref[...])
    _store_planes(xp_ref, xn)

    xh = xn.astype(BF16)
    xl = (xn - xh.astype(F32)).astype(BF16)
    nt = (((1,), (1,)), ((), ()))
    logits = (lax.dot_general(wr_ref[0], xh, nt, preferred_element_type=F32)
              + lax.dot_general(wr_ref[0], xl, nt, preferred_element_type=F32)
              + lax.dot_general(wr_ref[1], xh, nt, preferred_element_type=F32))[0:ROUTE_ROWS, :]
    row = lax.broadcasted_iota(I32, (ROUTE_ROWS, tm), 0).astype(F32)
    ninf = -jnp.inf
    big = float(LANES)
    gl = jnp.where(row < N_GROUPS, logits, ninf)
    gmax = jnp.max(gl, axis=0, keepdims=True)
    gidx = jnp.min(jnp.where(gl == gmax, row, big), axis=0, keepdims=True)
    p_group = 1.0 / jnp.sum(jnp.exp(gl - gmax), axis=0, keepdims=True)
    first = N_GROUPS + EXPERTS_PER_GROUP * gidx
    el = jnp.where((row >= first) & (row < first + EXPERTS_PER_GROUP), logits, ninf)
    m1 = jnp.max(el, axis=0, keepdims=True)
    i1 = jnp.min(jnp.where(el == m1, row, big), axis=0, keepdims=True)
    el2 = jnp.where(row == i1, ninf, el)
    m2 = jnp.max(el2, axis=0, keepdims=True)
    i2 = jnp.min(jnp.where(el2 == m2, row, big), axis=0, keepdims=True)
    e2 = jnp.exp(m2 - m1)
    w1 = p_group / (1.0 + e2)
    w2 = w1 * e2

    sel1 = row == i1
    sel2 = row == i2
    onehot = jnp.where(sel1, 1.0, jnp.where(sel2, 1.0, 0.0))
    before = jnp.dot(onehot.astype(BF16), tru_ref[...], preferred_element_type=F32) + carry[:, 0:1]
    r1 = jnp.sum(jnp.where(sel1, before, 0.0), axis=0, keepdims=True)
    r2 = jnp.sum(jnp.where(sel2, before, 0.0), axis=0, keepdims=True)
    total = carry[...] + jnp.sum(onehot, axis=1, keepdims=True)
    carry[...] = total
    cnt_ref[...] = total

    row8 = lax.broadcasted_iota(I32, (8, tm), 0)
    fields = (i1 - N_GROUPS, i2 - N_GROUPS, w1, w2, r1, r2)
    rtt = jnp.zeros((8, tm), F32)
    for k, f in enumerate(fields):
        rtt = jnp.where(row8 == k, f, rtt)
    rtt_ref[...] = rtt
    rt_ref[...] = jnp.concatenate([rtt, jnp.zeros((LANES - 8, tm), F32)], axis=0).T


def _mixer_out(x2d, ya, ysb, yfx, yc, lw, consts, row0=0):
    t = ya.shape[0]
    tm = min(TOKEN_TILE, t)

    def full(a):
        return pl.BlockSpec(a.shape, lambda i, _n=a.ndim: (0,) * _n)

    tok = lambda w: pl.BlockSpec((tm, w), lambda i: (i, 0))
    weights = [lw["mix_g"], lw["w_gate"], lw["b_gate"], lw["w_branch"], lw["w_out"], lw["ffn_g"], lw["w_router"],
               consts["triu_strict"]]
    return pl.pallas_call(
        _mixer_out_kernel,
        grid=(t // tm,),
        in_specs=[pl.BlockSpec((tm, D_MODEL), lambda i: (i + row0 // tm, 0))] + [tok(BRANCH_WIDTH)] * 4
        + [full(a) for a in weights],
        out_specs=[tok(D_MODEL), pl.BlockSpec((SC_SPLIT, tm, PLANE_W), lambda i: (0, i, 0)), tok(LANES),
                   pl.BlockSpec((8, tm), lambda i: (0, i)), pl.BlockSpec((ROUTE_ROWS, LANES), lambda i: (0, 0))],
        out_shape=[jax.ShapeDtypeStruct((t, D_MODEL), F32),
                   jax.ShapeDtypeStruct((SC_SPLIT, t, PLANE_W), I32),
                   jax.ShapeDtypeStruct((t, LANES), F32),
                   jax.ShapeDtypeStruct((8, t), F32),
                   jax.ShapeDtypeStruct((ROUTE_ROWS, LANES), F32)],
        scratch_shapes=[pltpu.VMEM((ROUTE_ROWS, LANES), F32)],
        compiler_params=pltpu.CompilerParams(dimension_semantics=("arbitrary",),
                                             vmem_limit_bytes=VMEM_LIMIT),
        name="mixer_out",
    )(x2d, ya, ysb, yfx, yc, *weights)


def _sc_mesh():
    return plsc.VectorSubcoreMesh(core_axis_name="core", subcore_axis_name="subcore")


def _plane_index(idx, rows_per_plane):
    return jnp.concatenate([idx + k * rows_per_plane for k in range(SC_SPLIT)]).reshape(1, -1)


def _dispatch_rows(planes, pos0, pos1, n_out):
    w = planes.shape[2]
    rows = planes.reshape(-1, w)
    t = rows.shape[0]
    idx0 = _plane_index(pos0, n_out)
    idx1 = _plane_index(pos1, n_out)

    @functools.partial(pl.kernel, out_type=jax.ShapeDtypeStruct((SC_SPLIT * n_out, w), rows.dtype),
                       mesh=_sc_mesh(), scratch_types=[])
    def scatter_kernel(x_hbm, i0_hbm, i1_hbm, o_hbm):
        def body(x_vmem, i0_vmem, i1_vmem):
            pltpu.sync_copy(x_vmem, o_hbm.at[i0_vmem.at[0]])
            pltpu.sync_copy(x_vmem, o_hbm.at[i1_vmem.at[0]])

        pltpu.emit_pipeline(
            body,
            grid=(t // SC_WINDOW,),
            in_specs=[pl.BlockSpec((SC_WINDOW, w), lambda i: (i, 0)),
                      pl.BlockSpec((1, SC_WINDOW), lambda i: (0, i)),
                      pl.BlockSpec((1, SC_WINDOW), lambda i: (0, i))],
            out_specs=[],
            core_axis_name=("core", "subcore"),
            dimension_semantics=(pltpu.PARALLEL,),
        )(x_hbm, i0_hbm, i1_hbm)

    return scatter_kernel(rows, idx0, idx1).reshape(SC_SPLIT, n_out, w)


def _collect_rows(planes, idx):
    n, w = planes.shape[1:]
    table = planes.reshape(-1, w)
    idx2 = _plane_index(idx, n)
    m = idx2.shape[1]

    @functools.partial(pl.kernel, out_type=jax.ShapeDtypeStruct((m, w), table.dtype), mesh=_sc_mesh(),
                       scratch_types=[])
    def gather_kernel(x_hbm, i_hbm, o_hbm):
        def body(i_vmem, o_vmem):
            pltpu.sync_copy(x_hbm.at[i_vmem.at[0]], o_vmem)

        pltpu.emit_pipeline(
            body,
            grid=(m // SC_WINDOW,),
            in_specs=[pl.BlockSpec((1, SC_WINDOW), lambda i: (0, i))],
            out_specs=[pl.BlockSpec((SC_WINDOW, w), lambda i: (i, 0))],
            core_axis_name=("core", "subcore"),
            dimension_semantics=(pltpu.PARALLEL,),
        )(i_hbm, o_hbm)

    return gather_kernel(table, idx2).reshape(SC_SPLIT, -1, w)


def _moe_ffn_kernel(te_ref, nt_ref, xs_ref, wgu_ref, wd_ref, ys_ref):
    i = pl.program_id(0)

    @pl.when(i < nt_ref[0])
    def _():
        gu = None
        for k, part in enumerate(_load_planes(xs_ref)):
            term = jnp.dot(part.astype(BF16), wgu_ref[0, k * PLANE_W:(k + 1) * PLANE_W, :],
                           preferred_element_type=F32)
            gu = term if gu is None else gu + term
        gate = gu[:, :D_EXPERT]
        hid = gate * _sigmoid(gate) * gu[:, D_EXPERT:]
        _store_planes(ys_ref, jnp.dot(hid.astype(BF16), wd_ref[0], preferred_element_type=F32))

    @pl.when(i >= nt_ref[0])
    def _():
        ys_ref[...] = jnp.zeros_like(ys_ref)


def _moe_ffn(xs, tile_expert, n_tiles, lw):
    p = xs.shape[1]
    tg = GROUP_TILE
    rows = pl.BlockSpec((SC_SPLIT, tg, PLANE_W), lambda i, te, nt: (0, i, 0))
    grid_spec = pltpu.PrefetchScalarGridSpec(
        num_scalar_prefetch=2,
        grid=(p // tg,),
        in_specs=[rows,
                  pl.BlockSpec((1, D_MODEL, 2 * D_EXPERT), lambda i, te, nt: (te[i], 0, 0)),
                  pl.BlockSpec((1, D_EXPERT, D_MODEL), lambda i, te, nt: (te[i], 0, 0))],
        out_specs=rows,
    )
    return pl.pallas_call(
        _moe_ffn_kernel,
        grid_spec=grid_spec,
        out_shape=jax.ShapeDtypeStruct((SC_SPLIT, p, PLANE_W), I32),
        compiler_params=pltpu.CompilerParams(dimension_semantics=("arbitrary",),
                                             vmem_limit_bytes=VMEM_LIMIT),
        name="moe_ffn",
    )(tile_expert, n_tiles, xs, lw["w_gu"], lw["w_down"])


def _route_plan(route_t, cnt, t):
    tg = GROUP_TILE
    n_tiles_max = (2 * t) // tg + N_EXPERTS
    counts = cnt[N_GROUPS:N_GROUPS + N_EXPERTS, 0].astype(I32)
    padded = ((counts + tg - 1) // tg) * tg
    ends = jnp.cumsum(padded)
    offs = ends - padded
    experts = jnp.arange(N_EXPERTS, dtype=I32)[:, None]

    def first_row(e):
        return jnp.sum(jnp.where(e[None, :] == experts, offs[:, None], 0), axis=0)

    fields = route_t.astype(I32)
    pos0 = first_row(fields[0]) + fields[4]
    pos1 = first_row(fields[1]) + fields[5]
    tile_start = jnp.arange(n_tiles_max, dtype=I32) * tg
    n_tiles = ends[-1] // tg
    tile_clamped = jnp.minimum(tile_start, jnp.maximum(n_tiles - 1, 0) * tg)
    tile_expert = jnp.sum((ends[None, :] <= tile_clamped[:, None]).astype(I32), axis=1)
    tile_expert = jnp.minimum(tile_expert, N_EXPERTS - 1)
    return pos0, pos1, tile_expert, n_tiles.reshape(1), n_tiles_max * tg


def _moe(xp, route_t, cnt, lw):
    t = xp.shape[1]
    pos0, pos1, tile_expert, n_tiles, p_rows = _route_plan(route_t, cnt, t)
    xs = _dispatch_rows(xp, pos0, pos1, p_rows)
    ys = _moe_ffn(xs, tile_expert, n_tiles, lw)
    return _collect_rows(ys, jnp.concatenate([pos0, pos1])).reshape(SC_SPLIT, 2, t, PLANE_W)


def _final_kernel(x_ref, y0_ref, y1_ref, rt_ref, g_ref, *rest):
    o_ref = rest[-1]
    x = _combine(x_ref[...], y0_ref, y1_ref, rt_ref[...], lead=(0,))
    o_ref[...] = _rms(x, g_ref[...])


def _final(x2d, y01, route, g, t_total, row0, out_prev):
    t = x2d.shape[0]
    tm = min(TOKEN_TILE, t)
    tok = lambda w: pl.BlockSpec((tm, w), lambda i: (i, 0))
    choice = lambda c: pl.BlockSpec((SC_SPLIT, 1, tm, PLANE_W), lambda i: (0, c, i, 0))
    in_specs = [tok(D_MODEL), choice(0), choice(1), tok(LANES), pl.BlockSpec((1, D_MODEL), lambda i: (0, 0))]
    args = [x2d, y01, y01, route, g]
    aliases = {}
    if out_prev is not None:
        in_specs.append(pl.BlockSpec(memory_space=pl.ANY))
        args.append(out_prev)
        aliases = {len(args) - 1: 0}
    return pl.pallas_call(
        _final_kernel,
        grid=(t // tm,),
        in_specs=in_specs,
        out_specs=pl.BlockSpec((tm, D_MODEL), lambda i: (i + row0 // tm, 0)),
        out_shape=jax.ShapeDtypeStruct((t_total, D_MODEL), F32),
        input_output_aliases=aliases,
        compiler_params=pltpu.CompilerParams(dimension_semantics=("arbitrary",),
                                             vmem_limit_bytes=VMEM_LIMIT),
        name="final_norm",
    )(*args)


def _constants(tm, tk):
    r = jnp.arange(tm)
    tril = (r[None, :] <= r[:, None]).astype(BF16)
    triu_strict = (r[:, None] < r[None, :]).astype(BF16)
    rk = jnp.arange(tk)
    u_incl = (rk[:, None] >= rk[None, :]).astype(BF16)
    nh = N_HEADS
    pq = jnp.zeros((3, LANES, nh * HEAD_SLAB), F32)
    pk = jnp.zeros((3, nh * HEAD_DIM, LANES), F32)
    qc = jnp.zeros((1, nh * HEAD_SLAB), F32)
    kc = jnp.zeros((nh * HEAD_DIM, LANES), F32)
    for part in range(3):
        for h in range(nh):
            pq = pq.at[part, h, h * HEAD_SLAB + HEAD_DIM + part].set(1.0)
            qc = qc.at[0, h * HEAD_SLAB + HEAD_DIM + 3 + part].set(1.0)
            kc = kc.at[h * HEAD_DIM + part, :].set(1.0)
            pk = pk.at[part, h * HEAD_DIM + 3 + part, h].set(1.0)
    vc = jnp.zeros((1, HEAD_SLAB), F32).at[0, HEAD_DIM].set(1.0)
    return {"tril": tril, "triu_strict": triu_strict, "u_incl": u_incl, "pq": pq.astype(BF16),
            "pk": pk.astype(BF16), "qc": qc, "kc": kc, "vc": vc}


def _layer_weights(layer, mix_norm_g, w_in, b_forget, conv_a_w, conf_dw_w, conf_dw_b, conf_ln_g, conf_ln_b,
                   w_branch, w_gate, b_gate, w_out, ffn_norm_g, w_router_group, w_router_expert,
                   w_expert_gate, w_expert_up, w_expert_down):
    w = w_in[layer]
    bw = BRANCH_WIDTH
    a_x, a_b, a_c, sb_q, sb_k, sb_v, fx_q, fx_k, fx_v = [w[:, i * bw:(i + 1) * bw] for i in range(9)]
    fx_f = w[:, 9 * bw:9 * bw + N_HEADS]
    conf = w[:, 9 * bw + N_HEADS:]
    scale = HEAD_DIM ** -0.5
    f_pad = jnp.pad(fx_f, ((0, 0), (0, LANES - N_HEADS)))
    w_main = jnp.concatenate([conf, f_pad, sb_q * scale, fx_q * scale, sb_v, fx_v, a_x, a_b, a_c], axis=1)
    w_t = jnp.concatenate([sb_k, fx_k], axis=1).T
    w_router = jnp.concatenate([w_router_group[layer], w_router_expert[layer].reshape(D_MODEL, N_EXPERTS)], axis=1)
    w_router = jnp.pad(w_router, ((0, 0), (0, LANES - N_GROUPS - N_EXPERTS)))
    return {
        "mix_g": mix_norm_g[layer].reshape(1, D_MODEL),
        "w_main": w_main.astype(BF16),
        "w_t": w_t.astype(BF16),
        "bf": jnp.pad(b_forget[layer], (0, LANES - N_HEADS)).reshape(1, LANES),
        "caw": conv_a_w[layer],
        "cdw": jnp.pad(conf_dw_w[layer], ((0, 1), (0, 0))),
        "cdb": conf_dw_b[layer].reshape(1, bw),
        "lng": conf_ln_g[layer].reshape(1, bw),
        "lnb": conf_ln_b[layer].reshape(1, bw),
        "w_gate": w_gate[layer].astype(BF16),
        "b_gate": b_gate[layer].reshape(4, 1, D_MODEL),
        "w_branch": w_branch[layer].astype(BF16),
        "w_out": w_out[layer].astype(BF16),
        "ffn_g": ffn_norm_g[layer].reshape(1, D_MODEL),
        "w_router": jnp.stack([w_router.T.astype(BF16), (w_router - w_router.astype(BF16).astype(F32)).T.astype(BF16)]),
        "w_gu": jnp.concatenate([w_expert_gate[layer], w_expert_up[layer]], axis=2).astype(BF16),
        "w_down": w_expert_down[layer].astype(BF16),
    }


def kernel(x, mix_norm_g, w_in, b_forget, conv_a_w, conf_dw_w, conf_dw_b, conf_ln_g, conf_ln_b, w_branch, w_gate,
           b_gate, w_out, ffn_norm_g, w_router_group, w_router_expert, w_expert_gate, w_expert_up, w_expert_down,
           final_norm_g):
    b, s, _ = x.shape
    depth = w_in.shape[0]
    consts = _constants(min(TOKEN_TILE, s), min(ATTN_TILE, s))
    parts = BATCH_SPLIT if b % BATCH_SPLIT == 0 else 1
    bp = b // parts
    tp = bp * s
    xs = [x] * parts
    combs = [None] * parts
    for layer in range(depth):
        lw = _layer_weights(layer, mix_norm_g, w_in, b_forget, conv_a_w, conf_dw_w, conf_dw_b, conf_ln_g,
                            conf_ln_b, w_branch, w_gate, b_gate, w_out, ffn_norm_g, w_router_group,
                            w_router_expert, w_expert_gate, w_expert_up, w_expert_down)
        routed = []
        for k in range(parts):
            outs = _mixer_in(xs[k], combs[k], lw, consts, b0=k * bp if layer == 0 else 0, b=bp)
            if combs[k] is not None:
                xk, *outs = outs
            else:
                xk = None
            ya, yc, q, kt, v = outs
            ysb = _attention("sb", q, kt, v, consts)
            yfx = _attention("fox", q, kt, v, consts)
            if xk is None:
                x2d_in, row0 = x.reshape(b * s, D_MODEL), k * tp
            else:
                x2d_in, row0 = xk.reshape(tp, D_MODEL), 0
            routed.append(_mixer_out(x2d_in, ya.reshape(tp, -1), ysb.reshape(tp, -1), yfx.reshape(tp, -1),
                                     yc.reshape(tp, -1), lw, consts, row0=row0))
        for k in range(parts):
            x2d, xp, route, route_t, cnt = routed[k]
            y01 = _moe(xp, route_t, cnt, lw)
            xs[k] = x2d.reshape(bp, s, D_MODEL)
            combs[k] = (y01.reshape(SC_SPLIT, 2, bp, s, PLANE_W), route.reshape(bp, s, LANES))
    out = None
    for k in range(parts):
        out = _final(xs[k].reshape(tp, D_MODEL), combs[k][0].reshape(SC_SPLIT, 2, tp, PLANE_W),
                     combs[k][1].reshape(tp, LANES), final_norm_g.reshape(1, D_MODEL), b * s, k * tp, out)
    return out.reshape(b, s, D_MODEL)
```

```python
import functools

import jax
import jax.numpy as jnp
from jax import lax
from jax.experimental import pallas as pl
from jax.experimental.pallas import tpu as pltpu
from jax.experimental.pallas import tpu_sc as plsc

F32 = jnp.float32
BF16 = jnp.bfloat16
I32 = jnp.int32

D_MODEL = 1024
BRANCH_WIDTH = 256
HEAD_DIM = 64
N_HEADS = 4
HEAD_SLAB = 128
CONV_A_WIDTH = 3
CONF_WIDTH = 31
N_GROUPS = 4
EXPERTS_PER_GROUP = 8
N_EXPERTS = 32
D_EXPERT = 256
EPS = 1e-6
LOG2E = 1.4426950408889634
LANES = 128
SUBLANES = 8
A_HALO = SUBLANES
CONF_HALO = 32
ROUTE_ROWS = 48
HALF_D = D_MODEL // 2
SC_SPLIT = 2
PLANE_W = HALF_D // SC_SPLIT

TOKEN_TILE = 512
MIXER_OUT_TILE = 1024
ATTN_TILE = 256
FOX_Q_BLOCKS = 2
SB_HEADS_PER_STEP = 4
FOX_HEADS_PER_STEP = 4
ATTN_TRIPS = (8, 4, 2, 1)
SB_GROUP = 4
GROUP_TILE = 512
CONV_ROW_CHUNK = 128
SC_WINDOW = 128
BATCH_SPLIT = 2
VMEM_LIMIT = 56 * 1024 * 1024

_C_CU, _C_CG = 0, 256
_C_F = 512
_N_FIRST = 640
_C_Q = 640
_C_V = 1152
_C_AX, _C_AB, _C_AC = 1664, 1920, 2176
_N_MAIN = 2432


def _rms(x, g):
    return x * lax.rsqrt(jnp.mean(x * x, axis=-1, keepdims=True) + EPS) * g


def _softplus(z):
    return jnp.maximum(z, 0.0) + jnp.log1p(jnp.exp(-jnp.abs(z)))


def _sigmoid(z):
    return 1.0 / (1.0 + jnp.exp(-z))


def _split3(v):
    hi = v.astype(BF16)
    r = v - hi.astype(F32)
    mid = r.astype(BF16)
    lo = (r - mid.astype(F32)).astype(BF16)
    return hi, mid, lo


def _pack_rows(v):
    lo = pltpu.bitcast(v[:, :HALF_D].astype(BF16).astype(F32), jnp.uint32)
    hi = pltpu.bitcast(v[:, HALF_D:].astype(BF16).astype(F32), jnp.uint32)
    return pltpu.bitcast((lo >> 16) | hi, I32)


def _unpack_rows(w):
    u = pltpu.bitcast(w, jnp.uint32)
    lo = pltpu.bitcast(u << 16, F32)
    hi = pltpu.bitcast(u & jnp.uint32(0xFFFF0000), F32)
    return lo, hi


def _store_planes(ref, v, lead=()):
    packed = _pack_rows(v)
    for k in range(SC_SPLIT):
        ref[(k, *lead)] = packed[:, k * PLANE_W:(k + 1) * PLANE_W]


def _load_planes(ref, lead=()):
    los, his = zip(*[_unpack_rows(ref[(k, *lead)]) for k in range(SC_SPLIT)])
    return list(los) + list(his)


def _combine(x, y0_ref, y1_ref, rt, lead=()):
    w0 = rt[:, 2:3]
    w1 = rt[:, 3:4]
    parts = [w0 * a + w1 * b for a, b in zip(_load_planes(y0_ref, lead), _load_planes(y1_ref, lead))]
    return x + jnp.concatenate(parts, axis=1)


def _mixer_in_kernel(combine, *refs):
    if combine:
        (x_ref, y0_ref, y1_ref, rt_ref, *refs) = refs
    else:
        (x_ref, *refs) = refs
    (g_ref, wm_ref, wt_ref, bf_ref, caw_ref, cdw_ref, cdb_ref, lng_ref, lnb_ref, tril_ref,
     pq_ref, qc_ref, pk_ref, kc_ref, vc_ref, *refs) = refs
    if combine:
        (xo_ref, *refs) = refs
    (ya_ref, yc_ref, q_ref, kt_ref, v_ref, bufa, bufc, dcarry) = refs

    tm = x_ref.shape[1]
    tk = kt_ref.shape[-1]

    @pl.when(pl.program_id(1) == 0)
    def _():
        bufa[0:A_HALO, :] = jnp.zeros((A_HALO, BRANCH_WIDTH), F32)
        bufc[0, 0:CONF_HALO, :] = jnp.zeros((CONF_HALO, BRANCH_WIDTH), F32)
        dcarry[...] = jnp.zeros_like(dcarry)

    x = x_ref[0]
    if combine:
        x = _combine(x, y0_ref, y1_ref, rt_ref[0], lead=(0, 0))
        xo_ref[0] = x
    xb = _rms(x, g_ref[...]).astype(BF16)

    p = jnp.concatenate([jnp.dot(xb, wm_ref[:, :_N_FIRST], preferred_element_type=F32),
                         jnp.dot(xb, wm_ref[:, _N_FIRST:], preferred_element_type=F32)], axis=1)
    pt = lax.dot_general(wt_ref[...], xb, (((1,), (1,)), ((), ())),
                         preferred_element_type=F32)

    ca = p[:, _C_AC:_C_AC + 256] * p[:, _C_AX:_C_AX + 256]
    bufa[A_HALO:A_HALO + tm, :] = ca
    caw = caw_ref[...]
    conv = caw[CONV_A_WIDTH - 1:CONV_A_WIDTH] * ca
    for k in range(CONV_A_WIDTH - 1):
        off = A_HALO - (CONV_A_WIDTH - 1) + k
        conv = conv + caw[k:k + 1] * bufa[off:off + tm, :]
    ya_ref[0] = (p[:, _C_AB:_C_AB + 256] * conv).astype(BF16)
    bufa[0:A_HALO, :] = ca[tm - A_HALO:tm]

    u = p[:, _C_CU:_C_CU + 256] * _sigmoid(p[:, _C_CG:_C_CG + 256])
    bufc[0, CONF_HALO:CONF_HALO + tm, :] = u
    for r in range(1, SUBLANES):
        bufc[r, 0:tm + CONF_HALO - SUBLANES, :] = bufc[0, r:r + tm + CONF_HALO - SUBLANES, :]
    cdw = cdw_ref[...]
    cdb = cdb_ref[...]
    lng = lng_ref[...]
    lnb = lnb_ref[...]
    for c in range(tm // CONV_ROW_CHUNK):
        row0 = c * CONV_ROW_CHUNK
        acc = jnp.broadcast_to(cdb, (CONV_ROW_CHUNK, BRANCH_WIDTH))
        for k in range(CONF_WIDTH):
            off = CONF_HALO - (CONF_WIDTH - 1) + k
            start = row0 + off - off % SUBLANES
            acc = acc + cdw[k:k + 1] * bufc[off % SUBLANES, start:start + CONV_ROW_CHUNK, :]
        mu = jnp.mean(acc, axis=-1, keepdims=True)
        cen = acc - mu
        var = jnp.mean(cen * cen, axis=-1, keepdims=True)
        yn = cen * lax.rsqrt(var + EPS) * lng + lnb
        yc_ref[0, row0:row0 + CONV_ROW_CHUNK, :] = (yn * _sigmoid(yn)).astype(BF16)
    bufc[0, 0:CONF_HALO, :] = u[tm - CONF_HALO:tm]

    logf = -_softplus(-(p[:, _C_F:_C_F + LANES] + bf_ref[...]))
    tril = tril_ref[...]
    dcum = dcarry[0:1, :]
    for part in _split3(logf):
        dcum = dcum + jnp.dot(tril, part, preferred_element_type=F32)
    dcarry[0:1, :] = dcum[tm - 1:tm, :]
    dcum_t = dcum.T
    qh, qm, ql = _split3(dcum)
    kh, km, kl = _split3(-dcum_t)
    q_extra = (jnp.dot(qh, pq_ref[0], preferred_element_type=F32)
               + jnp.dot(qm, pq_ref[1], preferred_element_type=F32)
               + jnp.dot(ql, pq_ref[2], preferred_element_type=F32) + qc_ref[...])
    k_extra = (jnp.dot(pk_ref[0], kh, preferred_element_type=F32)
               + jnp.dot(pk_ref[1], km, preferred_element_type=F32)
               + jnp.dot(pk_ref[2], kl, preferred_element_type=F32)
               + jnp.concatenate([kc_ref[...]] * (tm // LANES), axis=1))

    lane = lax.broadcasted_iota(I32, (tm, HEAD_SLAB), 1)
    low = lane < HEAD_DIM
    vc = vc_ref[...]
    for hd in range(2 * N_HEADS):
        is_fox = hd >= N_HEADS
        pair = (hd // 2) * HEAD_SLAB
        qs = p[:, _C_Q + pair:_C_Q + pair + HEAD_SLAB]
        vs = p[:, _C_V + pair:_C_V + pair + HEAD_SLAB]
        if hd % 2:
            qs = pltpu.roll(qs, HEAD_DIM, axis=1)
            vs = pltpu.roll(vs, HEAD_DIM, axis=1)
        if is_fox:
            hf = hd - N_HEADS
            qx = q_extra[:, hf * HEAD_SLAB:(hf + 1) * HEAD_SLAB]
            kx = k_extra[hf * HEAD_DIM:(hf + 1) * HEAD_DIM, :]
            vx = vc
        else:
            qx = 0.0
            kx = jnp.zeros((HEAD_DIM, tm), F32)
            vx = 0.0
        q_ref[0, hd] = jnp.where(low, qs, qx).astype(BF16)
        v_ref[0, hd] = jnp.where(low, vs, vx).astype(BF16)
        kfull = jnp.concatenate([pt[hd * HEAD_DIM:(hd + 1) * HEAD_DIM, :], kx], axis=0).astype(BF16)
        for c in range(tm // tk):
            kt_ref[0, hd, c] = kfull[:, c * tk:(c + 1) * tk]


def _mixer_in(x, comb, lw, consts, b0=0, b=None):
    s = x.shape[1]
    b = x.shape[0] if b is None else b
    tm = min(TOKEN_TILE, s)
    tk = min(ATTN_TILE, s)
    nk = s // tk
    nh2 = 2 * N_HEADS
    combine = comb is not None

    def full(a):
        return pl.BlockSpec(a.shape, lambda bi, si, _n=a.ndim: (0,) * _n)

    tok = lambda w: pl.BlockSpec((1, tm, w), lambda bi, si: (bi, si, 0))
    in_arrays = [x]
    in_specs = [pl.BlockSpec((1, tm, D_MODEL), lambda bi, si: (bi + b0, si, 0))]
    if combine:
        y01, route = comb
        in_arrays += [y01, y01, route]
        in_specs += [pl.BlockSpec((SC_SPLIT, 1, 1, tm, PLANE_W), lambda bi, si, _c=c: (0, _c, bi, si, 0))
                     for c in range(2)] + [tok(LANES)]
    weights = [lw["mix_g"], lw["w_main"], lw["w_t"], lw["bf"], lw["caw"], lw["cdw"], lw["cdb"], lw["lng"],
               lw["lnb"], consts["tril"], consts["pq"], consts["qc"], consts["pk"], consts["kc"], consts["vc"]]
    in_arrays += weights
    in_specs += [full(a) for a in weights]

    out_shape = []
    out_specs = []
    if combine:
        out_shape.append(jax.ShapeDtypeStruct((b, s, D_MODEL), F32))
        out_specs.append(tok(D_MODEL))
    out_shape += [
        jax.ShapeDtypeStruct((b, s, BRANCH_WIDTH), BF16),
        jax.ShapeDtypeStruct((b, s, BRANCH_WIDTH), BF16),
        jax.ShapeDtypeStruct((b, nh2, s, HEAD_SLAB), BF16),
        jax.ShapeDtypeStruct((b, nh2, nk, HEAD_SLAB, tk), BF16),
        jax.ShapeDtypeStruct((b, nh2, s, HEAD_SLAB), BF16),
    ]
    out_specs += [
        tok(BRANCH_WIDTH), tok(BRANCH_WIDTH),
        pl.BlockSpec((1, nh2, tm, HEAD_SLAB), lambda bi, si: (bi, 0, si, 0)),
        pl.BlockSpec((1, nh2, tm // tk, HEAD_SLAB, tk), lambda bi, si: (bi, 0, si, 0, 0)),
        pl.BlockSpec((1, nh2, tm, HEAD_SLAB), lambda bi, si: (bi, 0, si, 0)),
    ]
    return pl.pallas_call(
        functools.partial(_mixer_in_kernel, combine),
        grid=(b, s // tm),
        in_specs=in_specs,
        out_specs=out_specs,
        out_shape=out_shape,
        scratch_shapes=[pltpu.VMEM((A_HALO + tm, BRANCH_WIDTH), F32),
                        pltpu.VMEM((SUBLANES, CONF_HALO + tm, BRANCH_WIDTH), F32),
                        pltpu.VMEM((SUBLANES, LANES), F32)],
        compiler_params=pltpu.CompilerParams(dimension_semantics=("arbitrary", "arbitrary"),
                                             vmem_limit_bytes=VMEM_LIMIT),
        name="mixer_in",
    )(*in_arrays)


def _pair_out(accs):
    lane = lax.broadcasted_iota(I32, accs[0].shape, 1)
    return jnp.where(lane < HEAD_DIM, accs[0], pltpu.roll(accs[1], HEAD_DIM, axis=1))


def _sb_attn_kernel(q_ref, kt_ref, v_ref, u_ref, o_ref):
    tq = q_ref.shape[2]
    nk, tk = kt_ref.shape[2], kt_ref.shape[-1]
    i = pl.program_id(2)
    umat = u_ref[...]
    nh = q_ref.shape[1]
    nb = SB_GROUP
    qs = [q_ref[0, h] for h in range(nh)]
    row = lax.broadcasted_iota(I32, (tq, tk), 0)
    col = lax.broadcasted_iota(I32, (tq, tk), 1)

    def weights(g, tails, masked):
        ws, new_tails = [], []
        for h in range(nh):
            tail = tails[h]
            wh = []
            for u in reversed(range(nb)):
                jr = g * nb + u
                z = jnp.dot(qs[h], kt_ref[0, h, jnp.minimum(jr, nk - 1)], preferred_element_type=F32)
                sp = jnp.maximum(z, 0.0) + jnp.log(1.0 + jnp.exp2(jnp.abs(z) * -LOG2E))
                if masked:
                    mask = col + (jr - i) * tk < row
                    sp = jnp.where(mask, sp, 0.0)
                later = jnp.dot(sp.astype(BF16), umat, preferred_element_type=F32)
                w = jnp.exp((z - later - tail).astype(BF16))
                if masked:
                    w = jnp.where(mask, w, jnp.zeros_like(w))
                wh.append(w)
                tail = tail + later[:, 0:1]
            ws.append(tuple(reversed(wh)))
            new_tails.append(tail)
        return tuple(ws), tuple(new_tails)

    def apply(g, accs, ws):
        out = []
        for h in range(nh):
            acc = accs[h]
            for u in range(nb):
                start = pl.multiple_of(jnp.minimum(g * nb + u, nk - 1) * tk, tk)
                acc = acc + jnp.dot(ws[h][u], v_ref[0, h, pl.ds(start, tk), :], preferred_element_type=F32)
            out.append(acc)
        return tuple(out)

    last = i // nb
    zero_t = tuple(jnp.zeros((tq, 1), F32) for _ in range(nh))
    accs = tuple(jnp.zeros((tq, HEAD_SLAB), F32) for _ in range(nh))
    ws, tails = weights(last, zero_t, True)

    def body(t, carry):
        accs, tails, ws = carry
        g = last - 1 - t
        accs = apply(g + 1, accs, ws)
        ws, tails = weights(g, tails, False)
        return accs, tails, ws

    accs, tails, ws = lax.fori_loop(0, last, body, (accs, tails, ws))
    accs = apply(0, accs, ws)
    o_ref[0] = jnp.concatenate([_pair_out(accs[k:k + 2]) for k in range(0, nh, 2)], axis=1).astype(BF16)


def _fox_attn_kernel(q_ref, kt_ref, v_ref, o_ref):
    tq = q_ref.shape[2]
    tk = kt_ref.shape[-1]
    i = pl.program_id(2)
    row = lax.broadcasted_iota(I32, (tq, tk), 0)
    col = lax.broadcasted_iota(I32, (tq, tk), 1)
    nh = q_ref.shape[1]
    qs = [q_ref[0, h] for h in range(nh)]

    def vblock(h, j):
        return v_ref[0, h, pl.ds(pl.multiple_of(j * tk, tk), tk), :]

    r = tq // tk
    accs, maxes = [], []
    for h in range(nh):
        ss = [jnp.where(col + d * tk <= row, jnp.dot(qs[h], kt_ref[0, h, i * r + d], preferred_element_type=F32),
                        -jnp.inf) for d in range(r)]
        m = jnp.max(ss[0], axis=-1, keepdims=True)
        for s in ss[1:]:
            m = jnp.maximum(m, jnp.max(s, axis=-1, keepdims=True))
        acc = None
        for d, s in enumerate(ss):
            term = jnp.dot(jnp.exp(s - m).astype(BF16), vblock(h, i * r + d), preferred_element_type=F32)
            acc = term if acc is None else acc + term
        accs.append(acc)
        maxes.append(m)

    def make_body(n_blocks, first):
        def body(jj, carry):
            accs, maxes = carry
            new_accs, new_maxes = [], []
            for h in range(nh):
                js = [first + n_blocks * jj + u for u in range(n_blocks)]
                ss = [jnp.dot(qs[h], kt_ref[0, h, j], preferred_element_type=F32) for j in js]
                m = maxes[h]
                for s in ss:
                    m = jnp.maximum(m, jnp.max(s, axis=-1, keepdims=True))
                acc = jnp.exp(maxes[h] - m) * accs[h]
                for s, j in zip(ss, js):
                    acc = acc + jnp.dot(jnp.exp(s - m).astype(BF16), vblock(h, j), preferred_element_type=F32)
                new_accs.append(acc)
                new_maxes.append(m)
            return tuple(new_accs), tuple(new_maxes)
        return body

    carry = (tuple(accs), tuple(maxes))
    done = 0
    for n_blocks in ATTN_TRIPS:
        trips = (i * r - done) // n_blocks
        carry = lax.fori_loop(0, trips, make_body(n_blocks, done), carry)
        done = done + trips * n_blocks
    accs = carry[0]
    outs = [a / a[:, HEAD_DIM:HEAD_DIM + 1] for a in accs]
    o_ref[0] = jnp.concatenate([_pair_out(outs[k:k + 2]) for k in range(0, nh, 2)], axis=1).astype(BF16)


def _attention(kind, q, kt, v, consts):
    b, _, s, _ = q.shape
    nk, tk = kt.shape[2], kt.shape[4]
    tq = tk if kind == "sb" else min(FOX_Q_BLOCKS * tk, s)
    hps = SB_HEADS_PER_STEP if kind == "sb" else FOX_HEADS_PER_STEP
    head0 = 0 if kind == "sb" else N_HEADS // hps
    in_specs = [
        pl.BlockSpec((1, hps, tq, HEAD_SLAB), lambda bi, hp, i: (bi, hp + head0, i, 0)),
        pl.BlockSpec((1, hps, nk, HEAD_SLAB, tk), lambda bi, hp, i: (bi, hp + head0, 0, 0, 0)),
        pl.BlockSpec((1, hps, s, HEAD_SLAB), lambda bi, hp, i: (bi, hp + head0, 0, 0)),
    ]
    args = [q, kt, v]
    if kind == "sb":
        in_specs.append(pl.BlockSpec((tk, tk), lambda bi, hp, i: (0, 0)))
        args.append(consts["u_incl"])
        body = _sb_attn_kernel
    else:
        body = _fox_attn_kernel
    return pl.pallas_call(
        body,
        grid=(b, N_HEADS // hps, s // tq),
        in_specs=in_specs,
        out_specs=pl.BlockSpec((1, tq, hps * HEAD_DIM), lambda bi, hp, i: (bi, i, hp)),
        out_shape=jax.ShapeDtypeStruct((b, s, BRANCH_WIDTH), BF16),
        compiler_params=pltpu.CompilerParams(dimension_semantics=("arbitrary", "arbitrary", "arbitrary"),
                                             vmem_limit_bytes=VMEM_LIMIT),
        name=kind + "_attn",
    )(*args)


def _mixer_out_kernel(x_ref, ya_ref, ysb_ref, yfx_ref, yc_ref, g_ref, wg_ref, bg_ref, wb_ref, wo_ref,
                      fg_ref, wr_ref, tru_ref, xo_ref, xp_ref, rt_ref, rtt_ref, cnt_ref, carry):
    tm = x_ref.shape[0]

    @pl.when(pl.program_id(0) == 0)
    def _():
        carry[...] = jnp.zeros_like(carry)

    x = x_ref[...]
    xb = _rms(x, g_ref[...]).astype(BF16)
    h = None
    for g, y_ref in enumerate((ya_ref, ysb_ref, yfx_ref, yc_ref)):
        gate = _sigmoid(jnp.dot(xb, wg_ref[g], preferred_element_type=F32) + bg_ref[g])
        term = gate * jnp.dot(y_ref[...], wb_ref[g], preferred_element_type=F32)
        h = term if h is None else h + term
    xo = x + jnp.dot(h.astype(BF16), wo_ref[...], preferred_element_type=F32)
    xo_ref[...] = xo
    xn = _rms(xo, fg_ref[...])
    _store_planes(xp_ref, xn)

    xh = xn.astype(BF16)
    xl = (xn - xh.astype(F32)).astype(BF16)
    nt = (((1,), (1,)), ((), ()))
    logits = (lax.dot_general(wr_ref[0], xh, nt, preferred_element_type=F32)
              + lax.dot_general(wr_ref[0], xl, nt, preferred_element_type=F32)
              + lax.dot_general(wr_ref[1], xh, nt, preferred_element_type=F32))[0:ROUTE_ROWS, :]
    row = lax.broadcasted_iota(I32, (ROUTE_ROWS, tm), 0).astype(F32)
    ninf = -jnp.inf
    big = float(LANES)
    gl = jnp.where(row < N_GROUPS, logits, ninf)
    gmax = jnp.max(gl, axis=0, keepdims=True)
    gidx = jnp.min(jnp.where(gl == gmax, row, big), axis=0, keepdims=True)
    p_group = 1.0 / jnp.sum(jnp.exp(gl - gmax), axis=0, keepdims=True)
    first = N_GROUPS + EXPERTS_PER_GROUP * gidx
    el = jnp.where((row >= first) & (row < first + EXPERTS_PER_GROUP), logits, ninf)
    m1 = jnp.max(el, axis=0, keepdims=True)
    i1 = jnp.min(jnp.where(el == m1, row, big), axis=0, keepdims=True)
    el2 = jnp.where(row == i1, ninf, el)
    m2 = jnp.max(el2, axis=0, keepdims=True)
    i2 = jnp.min(jnp.where(el2 == m2, row, big), axis=0, keepdims=True)
    e2 = jnp.exp(m2 - m1)
    w1 = p_group / (1.0 + e2)
    w2 = w1 * e2

    sel1 = row == i1
    sel2 = row == i2
    onehot = jnp.where(sel1, 1.0, jnp.where(sel2, 1.0, 0.0))
    before = jnp.dot(onehot.astype(BF16), tru_ref[...], preferred_element_type=F32) + carry[:, 0:1]
    r1 = jnp.sum(jnp.where(sel1, before, 0.0), axis=0, keepdims=True)
    r2 = jnp.sum(jnp.where(sel2, before, 0.0), axis=0, keepdims=True)
    total = carry[...] + jnp.sum(onehot, axis=1, keepdims=True)
    carry[...] = total
    cnt_ref[...] = total

    row8 = lax.broadcasted_iota(I32, (8, tm), 0)
    fields = (i1 - N_GROUPS, i2 - N_GROUPS, w1, w2, r1, r2)
    rtt = jnp.zeros((8, tm), F32)
    for k, f in enumerate(fields):
        rtt = jnp.where(row8 == k, f, rtt)
    rtt_ref[...] = rtt
    rt_ref[...] = jnp.concatenate([rtt, jnp.zeros((LANES - 8, tm), F32)], axis=0).T


def _mixer_out(x2d, ya, ysb, yfx, yc, lw, consts, row0=0):
    t = ya.shape[0]
    tm = min(MIXER_OUT_TILE, t)

    def full(a):
        return pl.BlockSpec(a.shape, lambda i, _n=a.ndim: (0,) * _n, pipeline_mode=pl.Buffered(1))

    tok = lambda w: pl.BlockSpec((tm, w), lambda i: (i, 0))
    weights = [lw["mix_g"], lw["w_gate"], lw["b_gate"], lw["w_branch"], lw["w_out"], lw["ffn_g"], lw["w_router"],
               consts["triu_strict"]]
    return pl.pallas_call(
        _mixer_out_kernel,
        grid=(t // tm,),
        in_specs=[pl.BlockSpec((tm, D_MODEL), lambda i: (i + row0 // tm, 0))] + [tok(BRANCH_WIDTH)] * 4
        + [full(a) for a in weights],
        out_specs=[tok(D_MODEL), pl.BlockSpec((SC_SPLIT, tm, PLANE_W), lambda i: (0, i, 0)), tok(LANES),
                   pl.BlockSpec((8, tm), lambda i: (0, i)), pl.BlockSpec((ROUTE_ROWS, LANES), lambda i: (0, 0))],
        out_shape=[jax.ShapeDtypeStruct((t, D_MODEL), F32),
                   jax.ShapeDtypeStruct((SC_SPLIT, t, PLANE_W), I32),
                   jax.ShapeDtypeStruct((t, LANES), F32),
                   jax.ShapeDtypeStruct((8, t), F32),
                   jax.ShapeDtypeStruct((ROUTE_ROWS, LANES), F32)],
        scratch_shapes=[pltpu.VMEM((ROUTE_ROWS, LANES), F32)],
        compiler_params=pltpu.CompilerParams(dimension_semantics=("arbitrary",),
                                             vmem_limit_bytes=VMEM_LIMIT),
        name="mixer_out",
    )(x2d, ya, ysb, yfx, yc, *weights)


def _sc_mesh():
    return plsc.VectorSubcoreMesh(core_axis_name="core", subcore_axis_name="subcore")


def _plane_index(idx, rows_per_plane):
    return jnp.concatenate([idx + k * rows_per_plane for k in range(SC_SPLIT)]).reshape(1, -1)


def _dispatch_rows(planes, pos0, pos1, n_out):
    w = planes.shape[2]
    rows = planes.reshape(-1, w)
    t = rows.shape[0]
    idx0 = _plane_index(pos0, n_out)
    idx1 = _plane_index(pos1, n_out)

    @functools.partial(pl.kernel, out_type=jax.ShapeDtypeStruct((SC_SPLIT * n_out, w), rows.dtype),
                       mesh=_sc_mesh(), scratch_types=[])
    def scatter_kernel(x_hbm, i0_hbm, i1_hbm, o_hbm):
        def body(x_vmem, i0_vmem, i1_vmem):
            pltpu.sync_copy(x_vmem, o_hbm.at[i0_vmem.at[0]])
            pltpu.sync_copy(x_vmem, o_hbm.at[i1_vmem.at[0]])

        pltpu.emit_pipeline(
            body,
            grid=(t // SC_WINDOW,),
            in_specs=[pl.BlockSpec((SC_WINDOW, w), lambda i: (i, 0)),
                      pl.BlockSpec((1, SC_WINDOW), lambda i: (0, i)),
                      pl.BlockSpec((1, SC_WINDOW), lambda i: (0, i))],
            out_specs=[],
            core_axis_name=("core", "subcore"),
            dimension_semantics=(pltpu.PARALLEL,),
        )(x_hbm, i0_hbm, i1_hbm)

    return scatter_kernel(rows, idx0, idx1).reshape(SC_SPLIT, n_out, w)


def _collect_rows(planes, idx):
    n, w = planes.shape[1:]
    table = planes.reshape(-1, w)
    idx2 = _plane_index(idx, n)
    m = idx2.shape[1]

    @functools.partial(pl.kernel, out_type=jax.ShapeDtypeStruct((m, w), table.dtype), mesh=_sc_mesh(),
                       scratch_types=[])
    def gather_kernel(x_hbm, i_hbm, o_hbm):
        def body(i_vmem, o_vmem):
            pltpu.sync_copy(x_hbm.at[i_vmem.at[0]], o_vmem)

        pltpu.emit_pipeline(
            body,
            grid=(m // SC_WINDOW,),
            in_specs=[pl.BlockSpec((1, SC_WINDOW), lambda i: (0, i))],
            out_specs=[pl.BlockSpec((SC_WINDOW, w), lambda i: (i, 0))],
            core_axis_name=("core", "subcore"),
            dimension_semantics=(pltpu.PARALLEL,),
        )(i_hbm, o_hbm)

    return gather_kernel(table, idx2).reshape(SC_SPLIT, -1, w)


def _moe_ffn_kernel(te_ref, nt_ref, xs_ref, wgu_ref, wd_ref, ys_ref):
    i = pl.program_id(0)

    @pl.when(i < nt_ref[0])
    def _():
        gu = None
        for k, part in enumerate(_load_planes(xs_ref)):
            term = jnp.dot(part.astype(BF16), wgu_ref[0, k * PLANE_W:(k + 1) * PLANE_W, :],
                           preferred_element_type=F32)
            gu = term if gu is None else gu + term
        gate = gu[:, :D_EXPERT]
        hid = gate * _sigmoid(gate) * gu[:, D_EXPERT:]
        _store_planes(ys_ref, jnp.dot(hid.astype(BF16), wd_ref[0], preferred_element_type=F32))

    @pl.when(i >= nt_ref[0])
    def _():
        ys_ref[...] = jnp.zeros_like(ys_ref)


def _moe_ffn(xs, tile_expert, n_tiles, lw):
    p = xs.shape[1]
    tg = GROUP_TILE
    rows = pl.BlockSpec((SC_SPLIT, tg, PLANE_W), lambda i, te, nt: (0, i, 0))
    grid_spec = pltpu.PrefetchScalarGridSpec(
        num_scalar_prefetch=2,
        grid=(p // tg,),
        in_specs=[rows,
                  pl.BlockSpec((1, D_MODEL, 2 * D_EXPERT), lambda i, te, nt: (te[i], 0, 0)),
                  pl.BlockSpec((1, D_EXPERT, D_MODEL), lambda i, te, nt: (te[i], 0, 0))],
        out_specs=rows,
    )
    return pl.pallas_call(
        _moe_ffn_kernel,
        grid_spec=grid_spec,
        out_shape=jax.ShapeDtypeStruct((SC_SPLIT, p, PLANE_W), I32),
        compiler_params=pltpu.CompilerParams(dimension_semantics=("arbitrary",),
                                             vmem_limit_bytes=VMEM_LIMIT),
        name="moe_ffn",
    )(tile_expert, n_tiles, xs, lw["w_gu"], lw["w_down"])


def _route_plan(route_t, cnt, t):
    tg = GROUP_TILE
    n_tiles_max = (2 * t) // tg + N_EXPERTS
    counts = cnt[N_GROUPS:N_GROUPS + N_EXPERTS, 0].astype(I32)
    padded = ((counts + tg - 1) // tg) * tg
    ends = jnp.cumsum(padded)
    offs = ends - padded
    experts = jnp.arange(N_EXPERTS, dtype=I32)[:, None]

    def first_row(e):
        return jnp.sum(jnp.where(e[None, :] == experts, offs[:, None], 0), axis=0)

    fields = route_t.astype(I32)
    pos0 = first_row(fields[0]) + fields[4]
    pos1 = first_row(fields[1]) + fields[5]
    tile_start = jnp.arange(n_tiles_max, dtype=I32) * tg
    n_tiles = ends[-1] // tg
    tile_clamped = jnp.minimum(tile_start, jnp.maximum(n_tiles - 1, 0) * tg)
    tile_expert = jnp.sum((ends[None, :] <= tile_clamped[:, None]).astype(I32), axis=1)
    tile_expert = jnp.minimum(tile_expert, N_EXPERTS - 1)
    return pos0, pos1, tile_expert, n_tiles.reshape(1), n_tiles_max * tg


def _moe(xp, route_t, cnt, lw):
    t = xp.shape[1]
    pos0, pos1, tile_expert, n_tiles, p_rows = _route_plan(route_t, cnt, t)
    xs = _dispatch_rows(xp, pos0, pos1, p_rows)
    ys = _moe_ffn(xs, tile_expert, n_tiles, lw)
    return _collect_rows(ys, jnp.concatenate([pos0, pos1])).reshape(SC_SPLIT, 2, t, PLANE_W)


def _final_kernel(x_ref, y0_ref, y1_ref, rt_ref, g_ref, *rest):
    o_ref = rest[-1]
    x = _combine(x_ref[...], y0_ref, y1_ref, rt_ref[...], lead=(0,))
    o_ref[...] = _rms(x, g_ref[...])


def _final(x2d, y01, route, g, t_total, row0, out_prev):
    t = x2d.shape[0]
    tm = min(TOKEN_TILE, t)
    tok = lambda w: pl.BlockSpec((tm, w), lambda i: (i, 0))
    choice = lambda c: pl.BlockSpec((SC_SPLIT, 1, tm, PLANE_W), lambda i: (0, c, i, 0))
    in_specs = [tok(D_MODEL), choice(0), choice(1), tok(LANES), pl.BlockSpec((1, D_MODEL), lambda i: (0, 0))]
    args = [x2d, y01, y01, route, g]
    aliases = {}
    if out_prev is not None:
        in_specs.append(pl.BlockSpec(memory_space=pl.ANY))
        args.append(out_prev)
        aliases = {len(args) - 1: 0}
    return pl.pallas_call(
        _final_kernel,
        grid=(t // tm,),
        in_specs=in_specs,
        out_specs=pl.BlockSpec((tm, D_MODEL), lambda i: (i + row0 // tm, 0)),
        out_shape=jax.ShapeDtypeStruct((t_total, D_MODEL), F32),
        input_output_aliases=aliases,
        compiler_params=pltpu.CompilerParams(dimension_semantics=("arbitrary",),
                                             vmem_limit_bytes=VMEM_LIMIT),
        name="final_norm",
    )(*args)


def _constants(tm, tk, tm_out):
    r = jnp.arange(tm)
    tril = (r[None, :] <= r[:, None]).astype(BF16)
    ro = jnp.arange(tm_out)
    triu_strict = (ro[:, None] < ro[None, :]).astype(BF16)
    rk = jnp.arange(tk)
    u_incl = (rk[:, None] >= rk[None, :]).astype(BF16)
    nh = N_HEADS
    pq = jnp.zeros((3, LANES, nh * HEAD_SLAB), F32)
    pk = jnp.zeros((3, nh * HEAD_DIM, LANES), F32)
    qc = jnp.zeros((1, nh * HEAD_SLAB), F32)
    kc = jnp.zeros((nh * HEAD_DIM, LANES), F32)
    for part in range(3):
        for h in range(nh):
            pq = pq.at[part, h, h * HEAD_SLAB + HEAD_DIM + part].set(1.0)
            qc = qc.at[0, h * HEAD_SLAB + HEAD_DIM + 3 + part].set(1.0)
            kc = kc.at[h * HEAD_DIM + part, :].set(1.0)
            pk = pk.at[part, h * HEAD_DIM + 3 + part, h].set(1.0)
    vc = jnp.zeros((1, HEAD_SLAB), F32).at[0, HEAD_DIM].set(1.0)
    return {"tril": tril, "triu_strict": triu_strict, "u_incl": u_incl, "pq": pq.astype(BF16),
            "pk": pk.astype(BF16), "qc": qc, "kc": kc, "vc": vc}


def _layer_weights(layer, mix_norm_g, w_in, b_forget, conv_a_w, conf_dw_w, conf_dw_b, conf_ln_g, conf_ln_b,
                   w_branch, w_gate, b_gate, w_out, ffn_norm_g, w_router_group, w_router_expert,
                   w_expert_gate, w_expert_up, w_expert_down):
    w = w_in[layer]
    bw = BRANCH_WIDTH
    a_x, a_b, a_c, sb_q, sb_k, sb_v, fx_q, fx_k, fx_v = [w[:, i * bw:(i + 1) * bw] for i in range(9)]
    fx_f = w[:, 9 * bw:9 * bw + N_HEADS]
    conf = w[:, 9 * bw + N_HEADS:]
    scale = HEAD_DIM ** -0.5
    f_pad = jnp.pad(fx_f, ((0, 0), (0, LANES - N_HEADS)))
    w_main = jnp.concatenate([conf, f_pad, sb_q * scale, fx_q * scale, sb_v, fx_v, a_x, a_b, a_c], axis=1)
    w_t = jnp.concatenate([sb_k, fx_k], axis=1).T
    w_router = jnp.concatenate([w_router_group[layer], w_router_expert[layer].reshape(D_MODEL, N_EXPERTS)], axis=1)
    w_router = jnp.pad(w_router, ((0, 0), (0, LANES - N_GROUPS - N_EXPERTS)))
    return {
        "mix_g": mix_norm_g[layer].reshape(1, D_MODEL),
        "w_main": w_main.astype(BF16),
        "w_t": w_t.astype(BF16),
        "bf": jnp.pad(b_forget[layer], (0, LANES - N_HEADS)).reshape(1, LANES),
        "caw": conv_a_w[layer],
        "cdw": jnp.pad(conf_dw_w[layer], ((0, 1), (0, 0))),
        "cdb": conf_dw_b[layer].reshape(1, bw),
        "lng": conf_ln_g[layer].reshape(1, bw),
        "lnb": conf_ln_b[layer].reshape(1, bw),
        "w_gate": w_gate[layer].astype(BF16),
        "b_gate": b_gate[layer].reshape(4, 1, D_MODEL),
        "w_branch": w_branch[layer].astype(BF16),
        "w_out": w_out[layer].astype(BF16),
        "ffn_g": ffn_norm_g[layer].reshape(1, D_MODEL),
        "w_router": jnp.stack([w_router.T.astype(BF16), (w_router - w_router.astype(BF16).astype(F32)).T.astype(BF16)]),
        "w_gu": jnp.concatenate([w_expert_gate[layer], w_expert_up[layer]], axis=2).astype(BF16),
        "w_down": w_expert_down[layer].astype(BF16),
    }


def kernel(x, mix_norm_g, w_in, b_forget, conv_a_w, conf_dw_w, conf_dw_b, conf_ln_g, conf_ln_b, w_branch, w_gate,
           b_gate, w_out, ffn_norm_g, w_router_group, w_router_expert, w_expert_gate, w_expert_up, w_expert_down,
           final_norm_g):
    b, s, _ = x.shape
    depth = w_in.shape[0]
    parts = BATCH_SPLIT if b % BATCH_SPLIT == 0 else 1
    bp = b // parts
    tp = bp * s
    consts = _constants(min(TOKEN_TILE, s), min(ATTN_TILE, s), min(MIXER_OUT_TILE, tp))
    xs = [x] * parts
    combs = [None] * parts
    for layer in range(depth):
        lw = _layer_weights(layer, mix_norm_g, w_in, b_forget, conv_a_w, conf_dw_w, conf_dw_b, conf_ln_g,
                            conf_ln_b, w_branch, w_gate, b_gate, w_out, ffn_norm_g, w_router_group,
                            w_router_expert, w_expert_gate, w_expert_up, w_expert_down)
        routed = []
        for k in range(parts):
            outs = _mixer_in(xs[k], combs[k], lw, consts, b0=k * bp if layer == 0 else 0, b=bp)
            if combs[k] is not None:
                xk, *outs = outs
            else:
                xk = None
            ya, yc, q, kt, v = outs
            ysb = _attention("sb", q, kt, v, consts)
            yfx = _attention("fox", q, kt, v, consts)
            if xk is None:
                x2d_in, row0 = x.reshape(b * s, D_MODEL), k * tp
            else:
                x2d_in, row0 = xk.reshape(tp, D_MODEL), 0
            routed.append(_mixer_out(x2d_in, ya.reshape(tp, -1), ysb.reshape(tp, -1), yfx.reshape(tp, -1),
                                     yc.reshape(tp, -1), lw, consts, row0=row0))
        for k in range(parts):
            x2d, xp, route, route_t, cnt = routed[k]
            y01 = _moe(xp, route_t, cnt, lw)
            xs[k] = x2d.reshape(bp, s, D_MODEL)
            combs[k] = (y01.reshape(SC_SPLIT, 2, bp, s, PLANE_W), route.reshape(bp, s, LANES))
    out = None
    for k in range(parts):
        out = _final(xs[k].reshape(tp, D_MODEL), combs[k][0].reshape(SC_SPLIT, 2, tp, PLANE_W),
                     combs[k][1].reshape(tp, LANES), final_norm_g.reshape(1, D_MODEL), b * s, k * tp, out)
    return out.reshape(b, s, D_MODEL)
```

```python
import functools

import jax
import jax.numpy as jnp
from jax import lax
from jax.experimental import pallas as pl
from jax.experimental.pallas import tpu as pltpu
from jax.experimental.pallas import tpu_sc as plsc

F32 = jnp.float32
BF16 = jnp.bfloat16
I32 = jnp.int32

D_MODEL = 1024
BRANCH_WIDTH = 256
HEAD_DIM = 64
N_HEADS = 4
HEAD_SLAB = 128
CONV_A_WIDTH = 3
CONF_WIDTH = 31
N_GROUPS = 4
EXPERTS_PER_GROUP = 8
N_EXPERTS = 32
D_EXPERT = 256
EPS = 1e-6
LOG2E = 1.4426950408889634
LANES = 128
SUBLANES = 8
A_HALO = SUBLANES
CONF_HALO = 32
ROUTE_ROWS = 48
HALF_D = D_MODEL // 2
SC_SPLIT = 2
PLANE_W = HALF_D // SC_SPLIT

TOKEN_TILE = 512
MIXER_OUT_TILE = 1024
ATTN_TILE = 256
FOX_Q_BLOCKS = 2
SB_HEADS_PER_STEP = 4
FOX_HEADS_PER_STEP = 4
ATTN_TRIPS = (8, 4, 2, 1)
SB_GROUP = 4
GROUP_TILE = 512
CONV_ROW_CHUNK = 128
SC_WINDOW = 128
BATCH_SPLIT = 2
VMEM_LIMIT = 56 * 1024 * 1024

_C_CU, _C_CG = 0, 256
_C_F = 512
_N_FIRST = 640
_C_Q = 640
_C_V = 1152
_C_AX, _C_AB, _C_AC = 1664, 1920, 2176
_N_MAIN = 2432


def _rms(x, g):
    return x * lax.rsqrt(jnp.mean(x * x, axis=-1, keepdims=True) + EPS) * g


def _softplus(z):
    return jnp.maximum(z, 0.0) + jnp.log1p(jnp.exp(-jnp.abs(z)))


def _sigmoid(z):
    return 1.0 / (1.0 + jnp.exp(-z))


def _split3(v):
    hi = v.astype(BF16)
    r = v - hi.astype(F32)
    mid = r.astype(BF16)
    lo = (r - mid.astype(F32)).astype(BF16)
    return hi, mid, lo


def _pack_rows(v):
    lo = pltpu.bitcast(v[:, :HALF_D].astype(BF16).astype(F32), jnp.uint32)
    hi = pltpu.bitcast(v[:, HALF_D:].astype(BF16).astype(F32), jnp.uint32)
    return pltpu.bitcast((lo >> 16) | hi, I32)


def _unpack_rows(w):
    u = pltpu.bitcast(w, jnp.uint32)
    lo = pltpu.bitcast(u << 16, F32)
    hi = pltpu.bitcast(u & jnp.uint32(0xFFFF0000), F32)
    return lo, hi


def _store_planes(ref, v, lead=()):
    packed = _pack_rows(v)
    for k in range(SC_SPLIT):
        ref[(k, *lead)] = packed[:, k * PLANE_W:(k + 1) * PLANE_W]


def _load_planes(ref, lead=()):
    los, his = zip(*[_unpack_rows(ref[(k, *lead)]) for k in range(SC_SPLIT)])
    return list(los) + list(his)


def _combine(x, y0_ref, y1_ref, rt, lead=()):
    w0 = rt[:, 2:3]
    w1 = rt[:, 3:4]
    parts = [w0 * a + w1 * b for a, b in zip(_load_planes(y0_ref, lead), _load_planes(y1_ref, lead))]
    return x + jnp.concatenate(parts, axis=1)


def _mixer_in_kernel(combine, *refs):
    if combine:
        (x_ref, y0_ref, y1_ref, rt_ref, *refs) = refs
    else:
        (x_ref, *refs) = refs
    (g_ref, wm_ref, wt_ref, bf_ref, caw_ref, cdw_ref, cdb_ref, lng_ref, lnb_ref, tril_ref,
     pq_ref, qc_ref, pk_ref, kc_ref, vc_ref, *refs) = refs
    if combine:
        (xo_ref, *refs) = refs
    (ya_ref, yc_ref, q_ref, kt_ref, v_ref, bufa, bufc, dcarry) = refs

    tm = x_ref.shape[1]
    tk = kt_ref.shape[-1]

    @pl.when(pl.program_id(1) == 0)
    def _():
        bufa[0:A_HALO, :] = jnp.zeros((A_HALO, BRANCH_WIDTH), F32)
        bufc[0, 0:CONF_HALO, :] = jnp.zeros((CONF_HALO, BRANCH_WIDTH), F32)
        dcarry[...] = jnp.zeros_like(dcarry)

    x = x_ref[0]
    if combine:
        x = _combine(x, y0_ref, y1_ref, rt_ref[0], lead=(0, 0))
        xo_ref[0] = x
    xb = _rms(x, g_ref[...]).astype(BF16)

    p = jnp.concatenate([jnp.dot(xb, wm_ref[:, :_N_FIRST], preferred_element_type=F32),
                         jnp.dot(xb, wm_ref[:, _N_FIRST:], preferred_element_type=F32)], axis=1)
    pt = lax.dot_general(wt_ref[...], xb, (((1,), (1,)), ((), ())),
                         preferred_element_type=F32)

    ca = p[:, _C_AC:_C_AC + 256] * p[:, _C_AX:_C_AX + 256]
    bufa[A_HALO:A_HALO + tm, :] = ca
    caw = caw_ref[...]
    conv = caw[CONV_A_WIDTH - 1:CONV_A_WIDTH] * ca
    for k in range(CONV_A_WIDTH - 1):
        off = A_HALO - (CONV_A_WIDTH - 1) + k
        conv = conv + caw[k:k + 1] * bufa[off:off + tm, :]
    ya_ref[0] = (p[:, _C_AB:_C_AB + 256] * conv).astype(BF16)
    bufa[0:A_HALO, :] = ca[tm - A_HALO:tm]

    u = p[:, _C_CU:_C_CU + 256] * _sigmoid(p[:, _C_CG:_C_CG + 256])
    bufc[0, CONF_HALO:CONF_HALO + tm, :] = u
    for r in range(1, SUBLANES):
        bufc[r, 0:tm + CONF_HALO - SUBLANES, :] = bufc[0, r:r + tm + CONF_HALO - SUBLANES, :]
    cdw = cdw_ref[...]
    cdb = cdb_ref[...]
    lng = lng_ref[...]
    lnb = lnb_ref[...]
    for c in range(tm // CONV_ROW_CHUNK):
        row0 = c * CONV_ROW_CHUNK
        acc = jnp.broadcast_to(cdb, (CONV_ROW_CHUNK, BRANCH_WIDTH))
        for k in range(CONF_WIDTH):
            off = CONF_HALO - (CONF_WIDTH - 1) + k
            start = row0 + off - off % SUBLANES
            acc = acc + cdw[k:k + 1] * bufc[off % SUBLANES, start:start + CONV_ROW_CHUNK, :]
        mu = jnp.mean(acc, axis=-1, keepdims=True)
        cen = acc - mu
        var = jnp.mean(cen * cen, axis=-1, keepdims=True)
        yn = cen * lax.rsqrt(var + EPS) * lng + lnb
        yc_ref[0, row0:row0 + CONV_ROW_CHUNK, :] = (yn * _sigmoid(yn)).astype(BF16)
    bufc[0, 0:CONF_HALO, :] = u[tm - CONF_HALO:tm]

    logf = -_softplus(-(p[:, _C_F:_C_F + LANES] + bf_ref[...]))
    tril = tril_ref[...]
    dcum = dcarry[0:1, :]
    for part in _split3(logf):
        dcum = dcum + jnp.dot(tril, part, preferred_element_type=F32)
    dcarry[0:1, :] = dcum[tm - 1:tm, :]
    dcum_t = dcum.T
    qh, qm, ql = _split3(dcum)
    kh, km, kl = _split3(-dcum_t)
    q_extra = (jnp.dot(qh, pq_ref[0], preferred_element_type=F32)
               + jnp.dot(qm, pq_ref[1], preferred_element_type=F32)
               + jnp.dot(ql, pq_ref[2], preferred_element_type=F32) + qc_ref[...])
    k_extra = (jnp.dot(pk_ref[0], kh, preferred_element_type=F32)
               + jnp.dot(pk_ref[1], km, preferred_element_type=F32)
               + jnp.dot(pk_ref[2], kl, preferred_element_type=F32)
               + jnp.concatenate([kc_ref[...]] * (tm // LANES), axis=1))

    lane = lax.broadcasted_iota(I32, (tm, HEAD_SLAB), 1)
    low = lane < HEAD_DIM
    vc = vc_ref[...]
    for hd in range(2 * N_HEADS):
        is_fox = hd >= N_HEADS
        pair = (hd // 2) * HEAD_SLAB
        qs = p[:, _C_Q + pair:_C_Q + pair + HEAD_SLAB]
        vs = p[:, _C_V + pair:_C_V + pair + HEAD_SLAB]
        if hd % 2:
            qs = pltpu.roll(qs, HEAD_DIM, axis=1)
            vs = pltpu.roll(vs, HEAD_DIM, axis=1)
        if is_fox:
            hf = hd - N_HEADS
            qx = q_extra[:, hf * HEAD_SLAB:(hf + 1) * HEAD_SLAB]
            kx = k_extra[hf * HEAD_DIM:(hf + 1) * HEAD_DIM, :]
            vx = vc
        else:
            qx = 0.0
            kx = jnp.zeros((HEAD_DIM, tm), F32)
            vx = 0.0
        q_ref[0, hd] = jnp.where(low, qs, qx).astype(BF16)
        v_ref[0, hd] = jnp.where(low, vs, vx).astype(BF16)
        kfull = jnp.concatenate([pt[hd * HEAD_DIM:(hd + 1) * HEAD_DIM, :], kx], axis=0).astype(BF16)
        for c in range(tm // tk):
            kt_ref[0, hd, c] = kfull[:, c * tk:(c + 1) * tk]


def _mixer_in(x, comb, lw, consts, b0=0, b=None):
    s = x.shape[1]
    b = x.shape[0] if b is None else b
    tm = min(TOKEN_TILE, s)
    tk = min(ATTN_TILE, s)
    nk = s // tk
    nh2 = 2 * N_HEADS
    combine = comb is not None

    def full(a):
        return pl.BlockSpec(a.shape, lambda bi, si, _n=a.ndim: (0,) * _n, pipeline_mode=pl.Buffered(1))

    tok = lambda w: pl.BlockSpec((1, tm, w), lambda bi, si: (bi, si, 0))
    in_arrays = [x]
    in_specs = [pl.BlockSpec((1, tm, D_MODEL), lambda bi, si: (bi + b0, si, 0))]
    if combine:
        y01, route = comb
        in_arrays += [y01, y01, route]
        in_specs += [pl.BlockSpec((SC_SPLIT, 1, 1, tm, PLANE_W), lambda bi, si, _c=c: (0, _c, bi, si, 0))
                     for c in range(2)] + [tok(LANES)]
    weights = [lw["mix_g"], lw["w_main"], lw["w_t"], lw["bf"], lw["caw"], lw["cdw"], lw["cdb"], lw["lng"],
               lw["lnb"], consts["tril"], consts["pq"], consts["qc"], consts["pk"], consts["kc"], consts["vc"]]
    in_arrays += weights
    in_specs += [full(a) for a in weights]

    out_shape = []
    out_specs = []
    if combine:
        out_shape.append(jax.ShapeDtypeStruct((b, s, D_MODEL), F32))
        out_specs.append(tok(D_MODEL))
    out_shape += [
        jax.ShapeDtypeStruct((b, s, BRANCH_WIDTH), BF16),
        jax.ShapeDtypeStruct((b, s, BRANCH_WIDTH), BF16),
        jax.ShapeDtypeStruct((b, nh2, s, HEAD_SLAB), BF16),
        jax.ShapeDtypeStruct((b, nh2, nk, HEAD_SLAB, tk), BF16),
        jax.ShapeDtypeStruct((b, nh2, s, HEAD_SLAB), BF16),
    ]
    out_specs += [
        tok(BRANCH_WIDTH), tok(BRANCH_WIDTH),
        pl.BlockSpec((1, nh2, tm, HEAD_SLAB), lambda bi, si: (bi, 0, si, 0)),
        pl.BlockSpec((1, nh2, tm // tk, HEAD_SLAB, tk), lambda bi, si: (bi, 0, si, 0, 0)),
        pl.BlockSpec((1, nh2, tm, HEAD_SLAB), lambda bi, si: (bi, 0, si, 0)),
    ]
    return pl.pallas_call(
        functools.partial(_mixer_in_kernel, combine),
        grid=(b, s // tm),
        in_specs=in_specs,
        out_specs=out_specs,
        out_shape=out_shape,
        scratch_shapes=[pltpu.VMEM((A_HALO + tm, BRANCH_WIDTH), F32),
                        pltpu.VMEM((SUBLANES, CONF_HALO + tm, BRANCH_WIDTH), F32),
                        pltpu.VMEM((SUBLANES, LANES), F32)],
        compiler_params=pltpu.CompilerParams(dimension_semantics=("arbitrary", "arbitrary"),
                                             vmem_limit_bytes=VMEM_LIMIT),
        name="mixer_in",
    )(*in_arrays)


def _pair_out(accs):
    lane = lax.broadcasted_iota(I32, accs[0].shape, 1)
    return jnp.where(lane < HEAD_DIM, accs[0], pltpu.roll(accs[1], HEAD_DIM, axis=1))


def _sb_attn_kernel(q_ref, kt_ref, v_ref, u_ref, o_ref):
    tq = q_ref.shape[2]
    nk, tk = kt_ref.shape[2], kt_ref.shape[-1]
    i = pl.program_id(2)
    umat = u_ref[...]
    nh = q_ref.shape[1]
    nb = SB_GROUP
    qs = [q_ref[0, h] for h in range(nh)]
    row = lax.broadcasted_iota(I32, (tq, tk), 0)
    col = lax.broadcasted_iota(I32, (tq, tk), 1)

    def weights(g, tails, masked):
        ws, new_tails = [], []
        for h in range(nh):
            tail = tails[h]
            wh = []
            for u in reversed(range(nb)):
                jr = g * nb + u
                z = jnp.dot(qs[h], kt_ref[0, h, jnp.minimum(jr, nk - 1)], preferred_element_type=F32)
                sp = jnp.maximum(z, 0.0) + jnp.log(1.0 + jnp.exp2(jnp.abs(z) * -LOG2E))
                if masked:
                    mask = col + (jr - i) * tk < row
                    sp = jnp.where(mask, sp, 0.0)
                later = jnp.dot(sp.astype(BF16), umat, preferred_element_type=F32)
                w = jnp.exp((z - later - tail).astype(BF16))
                if masked:
                    w = jnp.where(mask, w, jnp.zeros_like(w))
                wh.append(w)
                tail = tail + later[:, 0:1]
            ws.append(tuple(reversed(wh)))
            new_tails.append(tail)
        return tuple(ws), tuple(new_tails)

    def apply(g, accs, ws):
        out = []
        for h in range(nh):
            acc = accs[h]
            for u in range(nb):
                start = pl.multiple_of(jnp.minimum(g * nb + u, nk - 1) * tk, tk)
                acc = acc + jnp.dot(ws[h][u], v_ref[0, h, pl.ds(start, tk), :], preferred_element_type=F32)
            out.append(acc)
        return tuple(out)

    last = i // nb
    zero_t = tuple(jnp.zeros((tq, 1), F32) for _ in range(nh))
    accs = tuple(jnp.zeros((tq, HEAD_SLAB), F32) for _ in range(nh))
    ws, tails = weights(last, zero_t, True)

    def body(t, carry):
        accs, tails, ws = carry
        g = last - 1 - t
        accs = apply(g + 1, accs, ws)
        ws, tails = weights(g, tails, False)
        return accs, tails, ws

    accs, tails, ws = lax.fori_loop(0, last, body, (accs, tails, ws))
    accs = apply(0, accs, ws)
    o_ref[0] = jnp.concatenate([_pair_out(accs[k:k + 2]) for k in range(0, nh, 2)], axis=1).astype(BF16)


def _fox_attn_kernel(q_ref, kt_ref, v_ref, o_ref):
    tq = q_ref.shape[2]
    tk = kt_ref.shape[-1]
    i = pl.program_id(2)
    row = lax.broadcasted_iota(I32, (tq, tk), 0)
    col = lax.broadcasted_iota(I32, (tq, tk), 1)
    nh = q_ref.shape[1]
    qs = [q_ref[0, h] for h in range(nh)]

    def vblock(h, j):
        return v_ref[0, h, pl.ds(pl.multiple_of(j * tk, tk), tk), :]

    r = tq // tk
    accs, maxes = [], []
    for h in range(nh):
        ss = [jnp.where(col + d * tk <= row, jnp.dot(qs[h], kt_ref[0, h, i * r + d], preferred_element_type=F32),
                        -jnp.inf) for d in range(r)]
        m = jnp.max(ss[0], axis=-1, keepdims=True)
        for s in ss[1:]:
            m = jnp.maximum(m, jnp.max(s, axis=-1, keepdims=True))
        acc = None
        for d, s in enumerate(ss):
            term = jnp.dot(jnp.exp(s - m).astype(BF16), vblock(h, i * r + d), preferred_element_type=F32)
            acc = term if acc is None else acc + term
        accs.append(acc)
        maxes.append(m)

    def make_body(n_blocks, first):
        def body(jj, carry):
            accs, maxes = carry
            new_accs, new_maxes = [], []
            for h in range(nh):
                js = [first + n_blocks * jj + u for u in range(n_blocks)]
                ss = [jnp.dot(qs[h], kt_ref[0, h, j], preferred_element_type=F32) for j in js]
                m = maxes[h]
                for s in ss:
                    m = jnp.maximum(m, jnp.max(s, axis=-1, keepdims=True))
                acc = jnp.exp(maxes[h] - m) * accs[h]
                for s, j in zip(ss, js):
                    acc = acc + jnp.dot(jnp.exp(s - m).astype(BF16), vblock(h, j), preferred_element_type=F32)
                new_accs.append(acc)
                new_maxes.append(m)
            return tuple(new_accs), tuple(new_maxes)
        return body

    carry = (tuple(accs), tuple(maxes))
    done = 0
    for n_blocks in ATTN_TRIPS:
        trips = (i * r - done) // n_blocks
        carry = lax.fori_loop(0, trips, make_body(n_blocks, done), carry)
        done = done + trips * n_blocks
    accs = carry[0]
    outs = [a / a[:, HEAD_DIM:HEAD_DIM + 1] for a in accs]
    o_ref[0] = jnp.concatenate([_pair_out(outs[k:k + 2]) for k in range(0, nh, 2)], axis=1).astype(BF16)


def _attention(kind, q, kt, v, consts):
    b, _, s, _ = q.shape
    nk, tk = kt.shape[2], kt.shape[4]
    tq = tk if kind == "sb" else min(FOX_Q_BLOCKS * tk, s)
    hps = SB_HEADS_PER_STEP if kind == "sb" else FOX_HEADS_PER_STEP
    head0 = 0 if kind == "sb" else N_HEADS // hps
    in_specs = [
        pl.BlockSpec((1, hps, tq, HEAD_SLAB), lambda bi, hp, i: (bi, hp + head0, i, 0)),
        pl.BlockSpec((1, hps, nk, HEAD_SLAB, tk), lambda bi, hp, i: (bi, hp + head0, 0, 0, 0)),
        pl.BlockSpec((1, hps, s, HEAD_SLAB), lambda bi, hp, i: (bi, hp + head0, 0, 0)),
    ]
    args = [q, kt, v]
    if kind == "sb":
        in_specs.append(pl.BlockSpec((tk, tk), lambda bi, hp, i: (0, 0)))
        args.append(consts["u_incl"])
        body = _sb_attn_kernel
    else:
        body = _fox_attn_kernel
    return pl.pallas_call(
        body,
        grid=(b, N_HEADS // hps, s // tq),
        in_specs=in_specs,
        out_specs=pl.BlockSpec((1, tq, hps * HEAD_DIM), lambda bi, hp, i: (bi, i, hp)),
        out_shape=jax.ShapeDtypeStruct((b, s, BRANCH_WIDTH), BF16),
        compiler_params=pltpu.CompilerParams(dimension_semantics=("arbitrary", "arbitrary", "arbitrary"),
                                             vmem_limit_bytes=VMEM_LIMIT),
        name=kind + "_attn",
    )(*args)


def _mixer_out_kernel(x_ref, ya_ref, ysb_ref, yfx_ref, yc_ref, g_ref, wg_ref, bg_ref, wb_ref, wo_ref,
                      fg_ref, wr_ref, tru_ref, xo_ref, xp_ref, rt_ref, rtt_ref, cnt_ref, carry):
    tm = x_ref.shape[0]

    @pl.when(pl.program_id(0) == 0)
    def _():
        carry[...] = jnp.zeros_like(carry)

    x = x_ref[...]
    xb = _rms(x, g_ref[...]).astype(BF16)
    h = None
    for g, y_ref in enumerate((ya_ref, ysb_ref, yfx_ref, yc_ref)):
        gate = _sigmoid(jnp.dot(xb, wg_ref[g], preferred_element_type=F32) + bg_ref[g])
        term = gate * jnp.dot(y_ref[...], wb_ref[g], preferred_element_type=F32)
        h = term if h is None else h + term
    xo = x + jnp.dot(h.astype(BF16), wo_ref[...], preferred_element_type=F32)
    xo_ref[...] = xo
    xn = _rms(xo, fg_ref[...])
    _store_planes(xp_ref, xn)

    xh = xn.astype(BF16)
    xl = (xn - xh.astype(F32)).astype(BF16)
    nt = (((1,), (1,)), ((), ()))
    logits = (lax.dot_general(wr_ref[0], xh, nt, preferred_element_type=F32)
              + lax.dot_general(wr_ref[0], xl, nt, preferred_element_type=F32)
              + lax.dot_general(wr_ref[1], xh, nt, preferred_element_type=F32))[0:ROUTE_ROWS, :]
    row = lax.broadcasted_iota(I32, (ROUTE_ROWS, tm), 0).astype(F32)
    ninf = -jnp.inf
    big = float(LANES)
    gl = jnp.where(row < N_GROUPS, logits, ninf)
    gmax = jnp.max(gl, axis=0, keepdims=True)
    gidx = jnp.min(jnp.where(gl == gmax, row, big), axis=0, keepdims=True)
    p_group = 1.0 / jnp.sum(jnp.exp(gl - gmax), axis=0, keepdims=True)
    first = N_GROUPS + EXPERTS_PER_GROUP * gidx
    el = jnp.where((row >= first) & (row < first + EXPERTS_PER_GROUP), logits, ninf)
    m1 = jnp.max(el, axis=0, keepdims=True)
    i1 = jnp.min(jnp.where(el == m1, row, big), axis=0, keepdims=True)
    el2 = jnp.where(row == i1, ninf, el)
    m2 = jnp.max(el2, axis=0, keepdims=True)
    i2 = jnp.min(jnp.where(el2 == m2, row, big), axis=0, keepdims=True)
    e2 = jnp.exp(m2 - m1)
    w1 = p_group / (1.0 + e2)
    w2 = w1 * e2

    sel1 = row == i1
    sel2 = row == i2
    onehot = jnp.where(sel1, 1.0, jnp.where(sel2, 1.0, 0.0))
    before = jnp.dot(onehot.astype(BF16), tru_ref[...], preferred_element_type=F32) + carry[:, 0:1]
    r1 = jnp.sum(jnp.where(sel1, before, 0.0), axis=0, keepdims=True)
    r2 = jnp.sum(jnp.where(sel2, before, 0.0), axis=0, keepdims=True)
    total = carry[...] + jnp.sum(onehot, axis=1, keepdims=True)
    carry[...] = total
    cnt_ref[...] = total

    row8 = lax.broadcasted_iota(I32, (8, tm), 0)
    fields = (i1 - N_GROUPS, i2 - N_GROUPS, w1, w2, r1, r2)
    rtt = jnp.zeros((8, tm), F32)
    for k, f in enumerate(fields):
        rtt = jnp.where(row8 == k, f, rtt)
    rtt_ref[...] = rtt
    rt_ref[...] = jnp.concatenate([rtt, jnp.zeros((LANES - 8, tm), F32)], axis=0).T


def _mixer_out(x2d, ya, ysb, yfx, yc, lw, consts, row0=0):
    t = ya.shape[0]
    tm = min(MIXER_OUT_TILE, t)

    def full(a):
        return pl.BlockSpec(a.shape, lambda i, _n=a.ndim: (0,) * _n, pipeline_mode=pl.Buffered(1))

    tok = lambda w: pl.BlockSpec((tm, w), lambda i: (i, 0))
    weights = [lw["mix_g"], lw["w_gate"], lw["b_gate"], lw["w_branch"], lw["w_out"], lw["ffn_g"], lw["w_router"],
               consts["triu_strict"]]
    return pl.pallas_call(
        _mixer_out_kernel,
        grid=(t // tm,),
        in_specs=[pl.BlockSpec((tm, D_MODEL), lambda i: (i + row0 // tm, 0))] + [tok(BRANCH_WIDTH)] * 4
        + [full(a) for a in weights],
        out_specs=[tok(D_MODEL), pl.BlockSpec((SC_SPLIT, tm, PLANE_W), lambda i: (0, i, 0)), tok(LANES),
                   pl.BlockSpec((8, tm), lambda i: (0, i)), pl.BlockSpec((ROUTE_ROWS, LANES), lambda i: (0, 0))],
        out_shape=[jax.ShapeDtypeStruct((t, D_MODEL), F32),
                   jax.ShapeDtypeStruct((SC_SPLIT, t, PLANE_W), I32),
                   jax.ShapeDtypeStruct((t, LANES), F32),
                   jax.ShapeDtypeStruct((8, t), F32),
                   jax.ShapeDtypeStruct((ROUTE_ROWS, LANES), F32)],
        scratch_shapes=[pltpu.VMEM((ROUTE_ROWS, LANES), F32)],
        compiler_params=pltpu.CompilerParams(dimension_semantics=("arbitrary",),
                                             vmem_limit_bytes=VMEM_LIMIT),
        name="mixer_out",
    )(x2d, ya, ysb, yfx, yc, *weights)


def _sc_mesh():
    return plsc.VectorSubcoreMesh(core_axis_name="core", subcore_axis_name="subcore")


def _plane_index(idx, rows_per_plane):
    return jnp.concatenate([idx + k * rows_per_plane for k in range(SC_SPLIT)]).reshape(1, -1)


def _dispatch_rows(planes, pos0, pos1, n_out):
    w = planes.shape[2]
    rows = planes.reshape(-1, w)
    t = rows.shape[0]
    idx0 = _plane_index(pos0, n_out)
    idx1 = _plane_index(pos1, n_out)

    @functools.partial(pl.kernel, out_type=jax.ShapeDtypeStruct((SC_SPLIT * n_out, w), rows.dtype),
                       mesh=_sc_mesh(), scratch_types=[])
    def scatter_kernel(x_hbm, i0_hbm, i1_hbm, o_hbm):
        def body(x_vmem, i0_vmem, i1_vmem):
            pltpu.sync_copy(x_vmem, o_hbm.at[i0_vmem.at[0]])
            pltpu.sync_copy(x_vmem, o_hbm.at[i1_vmem.at[0]])

        pltpu.emit_pipeline(
            body,
            grid=(t // SC_WINDOW,),
            in_specs=[pl.BlockSpec((SC_WINDOW, w), lambda i: (i, 0)),
                      pl.BlockSpec((1, SC_WINDOW), lambda i: (0, i)),
                      pl.BlockSpec((1, SC_WINDOW), lambda i: (0, i))],
            out_specs=[],
            core_axis_name=("core", "subcore"),
            dimension_semantics=(pltpu.PARALLEL,),
        )(x_hbm, i0_hbm, i1_hbm)

    return scatter_kernel(rows, idx0, idx1).reshape(SC_SPLIT, n_out, w)


def _collect_rows(planes, idx):
    n, w = planes.shape[1:]
    table = planes.reshape(-1, w)
    idx2 = _plane_index(idx, n)
    m = idx2.shape[1]

    @functools.partial(pl.kernel, out_type=jax.ShapeDtypeStruct((m, w), table.dtype), mesh=_sc_mesh(),
                       scratch_types=[])
    def gather_kernel(x_hbm, i_hbm, o_hbm):
        def body(i_vmem, o_vmem):
            pltpu.sync_copy(x_hbm.at[i_vmem.at[0]], o_vmem)

        pltpu.emit_pipeline(
            body,
            grid=(m // SC_WINDOW,),
            in_specs=[pl.BlockSpec((1, SC_WINDOW), lambda i: (0, i))],
            out_specs=[pl.BlockSpec((SC_WINDOW, w), lambda i: (i, 0))],
            core_axis_name=("core", "subcore"),
            dimension_semantics=(pltpu.PARALLEL,),
        )(i_hbm, o_hbm)

    return gather_kernel(table, idx2).reshape(SC_SPLIT, -1, w)


def _moe_ffn_kernel(te_ref, nt_ref, xs_ref, wgu_ref, wd_ref, ys_ref):
    i = pl.program_id(0)

    @pl.when(i < nt_ref[0])
    def _():
        gu = None
        for k, part in enumerate(_load_planes(xs_ref)):
            term = jnp.dot(part.astype(BF16), wgu_ref[0, k * PLANE_W:(k + 1) * PLANE_W, :],
                           preferred_element_type=F32)
            gu = term if gu is None else gu + term
        gate = gu[:, :D_EXPERT]
        hid = gate * _sigmoid(gate) * gu[:, D_EXPERT:]
        _store_planes(ys_ref, jnp.dot(hid.astype(BF16), wd_ref[0], preferred_element_type=F32))

    @pl.when(i >= nt_ref[0])
    def _():
        ys_ref[...] = jnp.zeros_like(ys_ref)


def _moe_ffn(xs, tile_expert, n_tiles, lw):
    p = xs.shape[1]
    tg = GROUP_TILE
    rows = pl.BlockSpec((SC_SPLIT, tg, PLANE_W), lambda i, te, nt: (0, i, 0))
    grid_spec = pltpu.PrefetchScalarGridSpec(
        num_scalar_prefetch=2,
        grid=(p // tg,),
        in_specs=[rows,
                  pl.BlockSpec((1, D_MODEL, 2 * D_EXPERT), lambda i, te, nt: (te[i], 0, 0)),
                  pl.BlockSpec((1, D_EXPERT, D_MODEL), lambda i, te, nt: (te[i], 0, 0))],
        out_specs=rows,
    )
    return pl.pallas_call(
        _moe_ffn_kernel,
        grid_spec=grid_spec,
        out_shape=jax.ShapeDtypeStruct((SC_SPLIT, p, PLANE_W), I32),
        compiler_params=pltpu.CompilerParams(dimension_semantics=("arbitrary",),
                                             vmem_limit_bytes=VMEM_LIMIT),
        name="moe_ffn",
    )(tile_expert, n_tiles, xs, lw["w_gu"], lw["w_down"])


def _route_plan(route_t, cnt, t):
    tg = GROUP_TILE
    n_tiles_max = (2 * t) // tg + N_EXPERTS
    counts = cnt[N_GROUPS:N_GROUPS + N_EXPERTS, 0].astype(I32)
    padded = ((counts + tg - 1) // tg) * tg
    ends = jnp.cumsum(padded)
    offs = ends - padded
    experts = jnp.arange(N_EXPERTS, dtype=I32)[:, None]

    def first_row(e):
        return jnp.sum(jnp.where(e[None, :] == experts, offs[:, None], 0), axis=0)

    fields = route_t.astype(I32)
    pos0 = first_row(fields[0]) + fields[4]
    pos1 = first_row(fields[1]) + fields[5]
    tile_start = jnp.arange(n_tiles_max, dtype=I32) * tg
    n_tiles = ends[-1] // tg
    tile_clamped = jnp.minimum(tile_start, jnp.maximum(n_tiles - 1, 0) * tg)
    tile_expert = jnp.sum((ends[None, :] <= tile_clamped[:, None]).astype(I32), axis=1)
    tile_expert = jnp.minimum(tile_expert, N_EXPERTS - 1)
    return pos0, pos1, tile_expert, n_tiles.reshape(1), n_tiles_max * tg


def _moe(xp, route_t, cnt, lw):
    t = xp.shape[1]
    pos0, pos1, tile_expert, n_tiles, p_rows = _route_plan(route_t, cnt, t)
    xs = _dispatch_rows(xp, pos0, pos1, p_rows)
    ys = _moe_ffn(xs, tile_expert, n_tiles, lw)
    return _collect_rows(ys, jnp.concatenate([pos0, pos1])).reshape(SC_SPLIT, 2, t, PLANE_W)


def _final_kernel(x_ref, y0_ref, y1_ref, rt_ref, g_ref, *rest):
    o_ref = rest[-1]
    x = _combine(x_ref[...], y0_ref, y1_ref, rt_ref[...], lead=(0,))
    o_ref[...] = _rms(x, g_ref[...])


def _final(x2d, y01, route, g, t_total, row0, out_prev):
    t = x2d.shape[0]
    tm = min(TOKEN_TILE, t)
    tok = lambda w: pl.BlockSpec((tm, w), lambda i: (i, 0))
    choice = lambda c: pl.BlockSpec((SC_SPLIT, 1, tm, PLANE_W), lambda i: (0, c, i, 0))
    in_specs = [tok(D_MODEL), choice(0), choice(1), tok(LANES), pl.BlockSpec((1, D_MODEL), lambda i: (0, 0))]
    args = [x2d, y01, y01, route, g]
    aliases = {}
    if out_prev is not None:
        in_specs.append(pl.BlockSpec(memory_space=pl.ANY))
        args.append(out_prev)
        aliases = {len(args) - 1: 0}
    return pl.pallas_call(
        _final_kernel,
        grid=(t // tm,),
        in_specs=in_specs,
        out_specs=pl.BlockSpec((tm, D_MODEL), lambda i: (i + row0 // tm, 0)),
        out_shape=jax.ShapeDtypeStruct((t_total, D_MODEL), F32),
        input_output_aliases=aliases,
        compiler_params=pltpu.CompilerParams(dimension_semantics=("arbitrary",),
                                             vmem_limit_bytes=VMEM_LIMIT),
        name="final_norm",
    )(*args)


def _constants(tm, tk, tm_out):
    r = jnp.arange(tm)
    tril = (r[None, :] <= r[:, None]).astype(BF16)
    ro = jnp.arange(tm_out)
    triu_strict = (ro[:, None] < ro[None, :]).astype(BF16)
    rk = jnp.arange(tk)
    u_incl = (rk[:, None] >= rk[None, :]).astype(BF16)
    nh = N_HEADS
    pq = jnp.zeros((3, LANES, nh * HEAD_SLAB), F32)
    pk = jnp.zeros((3, nh * HEAD_DIM, LANES), F32)
    qc = jnp.zeros((1, nh * HEAD_SLAB), F32)
    kc = jnp.zeros((nh * HEAD_DIM, LANES), F32)
    for part in range(3):
        for h in range(nh):
            pq = pq.at[part, h, h * HEAD_SLAB + HEAD_DIM + part].set(1.0)
            qc = qc.at[0, h * HEAD_SLAB + HEAD_DIM + 3 + part].set(1.0)
            kc = kc.at[h * HEAD_DIM + part, :].set(1.0)
            pk = pk.at[part, h * HEAD_DIM + 3 + part, h].set(1.0)
    vc = jnp.zeros((1, HEAD_SLAB), F32).at[0, HEAD_DIM].set(1.0)
    return {"tril": tril, "triu_strict": triu_strict, "u_incl": u_incl, "pq": pq.astype(BF16),
            "pk": pk.astype(BF16), "qc": qc, "kc": kc, "vc": vc}


def _layer_weights(layer, mix_norm_g, w_in, b_forget, conv_a_w, conf_dw_w, conf_dw_b, conf_ln_g, conf_ln_b,
                   w_branch, w_gate, b_gate, w_out, ffn_norm_g, w_router_group, w_router_expert,
                   w_expert_gate, w_expert_up, w_expert_down):
    w = w_in[layer]
    bw = BRANCH_WIDTH
    a_x, a_b, a_c, sb_q, sb_k, sb_v, fx_q, fx_k, fx_v = [w[:, i * bw:(i + 1) * bw] for i in range(9)]
    fx_f = w[:, 9 * bw:9 * bw + N_HEADS]
    conf = w[:, 9 * bw + N_HEADS:]
    scale = HEAD_DIM ** -0.5
    f_pad = jnp.pad(fx_f, ((0, 0), (0, LANES - N_HEADS)))
    w_main = jnp.concatenate([conf, f_pad, sb_q * scale, fx_q * scale, sb_v, fx_v, a_x, a_b, a_c], axis=1)
    w_t = jnp.concatenate([sb_k, fx_k], axis=1).T
    w_router = jnp.concatenate([w_router_group[layer], w_router_expert[layer].reshape(D_MODEL, N_EXPERTS)], axis=1)
    w_router = jnp.pad(w_router, ((0, 0), (0, LANES - N_GROUPS - N_EXPERTS)))
    return {
        "mix_g": mix_norm_g[layer].reshape(1, D_MODEL),
        "w_main": w_main.astype(BF16),
        "w_t": w_t.astype(BF16),
        "bf": jnp.pad(b_forget[layer], (0, LANES - N_HEADS)).reshape(1, LANES),
        "caw": conv_a_w[layer],
        "cdw": jnp.pad(conf_dw_w[layer], ((0, 1), (0, 0))),
        "cdb": conf_dw_b[layer].reshape(1, bw),
        "lng": conf_ln_g[layer].reshape(1, bw),
        "lnb": conf_ln_b[layer].reshape(1, bw),
        "w_gate": w_gate[layer].astype(BF16),
        "b_gate": b_gate[layer].reshape(4, 1, D_MODEL),
        "w_branch": w_branch[layer].astype(BF16),
        "w_out": w_out[layer].astype(BF16),
        "ffn_g": ffn_norm_g[layer].reshape(1, D_MODEL),
        "w_router": jnp.stack([w_router.T.astype(BF16), (w_router - w_router.astype(BF16).astype(F32)).T.astype(BF16)]),
        "w_gu": jnp.concatenate([w_expert_gate[layer], w_expert_up[layer]], axis=2).astype(BF16),
        "w_down": w_expert_down[layer].astype(BF16),
    }


def kernel(x, mix_norm_g, w_in, b_forget, conv_a_w, conf_dw_w, conf_dw_b, conf_ln_g, conf_ln_b, w_branch, w_gate,
           b_gate, w_out, ffn_norm_g, w_router_group, w_router_expert, w_expert_gate, w_expert_up, w_expert_down,
           final_norm_g):
    b, s, _ = x.shape
    depth = w_in.shape[0]
    parts = BATCH_SPLIT if b % BATCH_SPLIT == 0 else 1
    bp = b // parts
    tp = bp * s
    consts = _constants(min(TOKEN_TILE, s), min(ATTN_TILE, s), min(MIXER_OUT_TILE, tp))
    xs = [x] * parts
    combs = [None] * parts
    for layer in range(depth):
        lw = _layer_weights(layer, mix_norm_g, w_in, b_forget, conv_a_w, conf_dw_w, conf_dw_b, conf_ln_g,
                            conf_ln_b, w_branch, w_gate, b_gate, w_out, ffn_norm_g, w_router_group,
                            w_router_expert, w_expert_gate, w_expert_up, w_expert_down)
        routed = []
        for k in range(parts):
            outs = _mixer_in(xs[k], combs[k], lw, consts, b0=k * bp if layer == 0 else 0, b=bp)
            if combs[k] is not None:
                xk, *outs = outs
            else:
                xk = None
            ya, yc, q, kt, v = outs
            ysb = _attention("sb", q, kt, v, consts)
            yfx = _attention("fox", q, kt, v, consts)
            if xk is None:
                x2d_in, row0 = x.reshape(b * s, D_MODEL), k * tp
            else:
                x2d_in, row0 = xk.reshape(tp, D_MODEL), 0
            routed.append(_mixer_out(x2d_in, ya.reshape(tp, -1), ysb.reshape(tp, -1), yfx.reshape(tp, -1),
                                     yc.reshape(tp, -1), lw, consts, row0=row0))
        for k in range(parts):
            x2d, xp, route, route_t, cnt = routed[k]
            y01 = _moe(xp, route_t, cnt, lw)
            xs[k] = x2d.reshape(bp, s, D_MODEL)
            combs[k] = (y01.reshape(SC_SPLIT, 2, bp, s, PLANE_W), route.reshape(bp, s, LANES))
    out = None
    for k in range(parts):
        out = _final(xs[k].reshape(tp, D_MODEL), combs[k][0].reshape(SC_SPLIT, 2, tp, PLANE_W),
                     combs[k][1].reshape(tp, LANES), final_norm_g.reshape(1, D_MODEL), b * s, k * tp, out)
    return out.reshape(b, s, D_MODEL)
```
